```python
import jax, jax.numpy as jnp
from jax import lax
import numpy as np

D_MODEL = 2048
BATCH = 8
SEQ = 4096
DEPTH = 1
DEC_BATCH = 16
DEC_SEQ = 16
PAST_LEN = 4096

CHUNK = 64
WINDOW = 128
WIN_CHUNKS = WINDOW // CHUNK
ATT_WIDTH = D_MODEL // 2
GLA_WIDTH = D_MODEL - ATT_WIDTH
HEAD_DIM = 64
N_HEADS = ATT_WIDTH // HEAD_DIM
N_KV = 4
GQA_REP = N_HEADS // N_KV
N_GLA = 4
GLA_DV = GLA_WIDTH // N_GLA
GLA_DK = GLA_DV // 2
GLA_KW = N_GLA * GLA_DK
GATE_RANK = 16
GATE_TAU = 16.0
EPS = 1e-6
ATT_SCALE = HEAD_DIM ** -0.5
ATT_CACHE = min(WINDOW, PAST_LEN)
IN_SIZES = (ATT_WIDTH, N_KV * HEAD_DIM, N_KV * HEAD_DIM, ATT_WIDTH,
            GLA_KW, GLA_KW, GLA_WIDTH, GLA_WIDTH, GATE_RANK)
IN_TOTAL = sum(IN_SIZES)
SPLITS = tuple(int(s) for s in np.cumsum(IN_SIZES)[:-1])

kernel_name = "hymba_swa_sink_gla_streaming_step"


def rmsnorm(x, g):
    xf = x.astype(jnp.float32)
    y = xf * lax.rsqrt(jnp.mean(xf * xf, axis=-1, keepdims=True) + EPS)
    return (y * g.astype(jnp.float32)).astype(x.dtype)


def sink_softmax(s, sink):
    m = jnp.maximum(jnp.max(s, axis=-1, keepdims=True), sink)
    e = jnp.exp(s - m)
    return e / (jnp.sum(e, axis=-1, keepdims=True) + jnp.exp(sink - m))


def branch_inputs(x, norm_g, w_in, w_gate_up, b_gate, q_norm_g, k_norm_g):
    B, T, _ = x.shape
    xn = rmsnorm(x, norm_g)
    h = xn @ w_in
    qa, ka, va, ga, qg, kg, vg, gg, lr = jnp.split(h, SPLITS, axis=-1)
    qa = rmsnorm(qa.reshape(B, T, N_HEADS, HEAD_DIM), q_norm_g)
    ka = rmsnorm(ka.reshape(B, T, N_KV, HEAD_DIM), k_norm_g)
    va = va.reshape(B, T, N_KV, HEAD_DIM)
    qg = qg.reshape(B, T, N_GLA, GLA_DK) * (GLA_DK ** -0.5)
    kg = kg.reshape(B, T, N_GLA, GLA_DK)
    vg = vg.reshape(B, T, N_GLA, GLA_DV)
    log_a = jax.nn.log_sigmoid((lr @ w_gate_up + b_gate).astype(jnp.float32)) / GATE_TAU
    log_a = log_a.reshape(B, T, N_GLA, GLA_DK)
    return qa, ka, va, ga, qg, kg, vg, gg, log_a


def swa_prompt(q, k, v, sinks):
    B, S = q.shape[:2]
    n = S // CHUNK
    L = (WIN_CHUNKS + 1) * CHUNK
    qb = q.reshape(B, n, CHUNK, N_KV, GQA_REP, HEAD_DIM).astype(jnp.float32)
    pad = ((0, 0), (WINDOW, 0), (0, 0), (0, 0))
    kp = jnp.pad(k, pad).reshape(B, n + WIN_CHUNKS, CHUNK, N_KV, HEAD_DIM)
    vp = jnp.pad(v, pad).reshape(B, n + WIN_CHUNKS, CHUNK, N_KV, HEAD_DIM)
    kband = jnp.concatenate([kp[:, i:i + n] for i in range(WIN_CHUNKS + 1)], axis=2)
    vband = jnp.concatenate([vp[:, i:i + n] for i in range(WIN_CHUNKS + 1)], axis=2)
    key_pos = jnp.arange(n)[:, None] * CHUNK - WINDOW + jnp.arange(L)[None, :]
    valid = key_pos >= 0
    s = jnp.einsum('bnqgrd,bnkgd->bngrqk', qb, kband.astype(jnp.float32)) * ATT_SCALE
    s = jnp.where(valid[None, :, None, None, None, :], s, -jnp.inf)
    sink = sinks.astype(jnp.float32).reshape(N_KV, GQA_REP)[None, None, :, :, None, None]
    p = sink_softmax(s, sink)
    o = jnp.einsum('bngrqk,bnkgd->bnqgrd', p.astype(v.dtype), vband)
    return o.reshape(B, S, ATT_WIDTH)


def swa_sample(q, k_new, v_new, cache_k, cache_v, sinks):
    B, T = q.shape[:2]
    keys = jnp.concatenate([cache_k.astype(k_new.dtype), k_new], axis=1)
    vals = jnp.concatenate([cache_v.astype(v_new.dtype), v_new], axis=1)
    qg = q.reshape(B, T, N_KV, GQA_REP, HEAD_DIM).astype(jnp.float32)
    s = jnp.einsum('btgrd,bsgd->bgrts', qg, keys.astype(jnp.float32)) * ATT_SCALE
    sink = sinks.astype(jnp.float32).reshape(N_KV, GQA_REP)[None, :, :, None, None]
    p = sink_softmax(s, sink)
    o = jnp.einsum('bgrts,bsgd->btgrd', p.astype(vals.dtype), vals).reshape(B, T, ATT_WIDTH)
    return o, keys[:, -ATT_CACHE:], vals[:, -ATT_CACHE:]


def gla_scan(q, k, v, log_a, s0):
    B, T, H, _ = q.shape
    blk = min(CHUNK, T)
    n = T // blk

    def blocks(a):
        return jnp.moveaxis(a.astype(jnp.float32).reshape(B, n, blk, H, a.shape[-1]), 1, 0)

    causal = jnp.tril(jnp.ones((blk, blk), dtype=bool))

    def step(S, inp):
        qc, kc, vc, ac = inp
        bcum = jnp.cumsum(ac, axis=1)
        diff = bcum[:, :, None] - bcum[:, None, :]
        decay = jnp.exp(jnp.where(causal[None, :, :, None, None], diff, -jnp.inf))
        att = jnp.einsum('bijhd,bjhd->bhij', qc[:, :, None] * decay, kc)
        o = (jnp.einsum('bhij,bjhe->bihe', att, vc)
             + jnp.einsum('bihd,bhde->bihe', qc * jnp.exp(bcum), S))
        blast = bcum[:, -1]
        S = (jnp.exp(blast)[..., None] * S
             + jnp.einsum('bjhd,bjhe->bhde', kc * jnp.exp(blast[:, None] - bcum), vc))
        return S, o

    S, o = lax.scan(step, s0.astype(jnp.float32), (blocks(q), blocks(k), blocks(v), blocks(log_a)))
    o = jnp.moveaxis(o, 0, 1).reshape(B, T, H, v.shape[-1])
    return o, S


def merge(x, o_attn, ga, o_gla, gg, gla_norm_g, w_out):
    B, T, _ = x.shape
    o_gla = rmsnorm(o_gla.astype(x.dtype), gla_norm_g).reshape(B, T, GLA_WIDTH)
    mix = jnp.concatenate([o_attn * jax.nn.silu(ga), o_gla * jax.nn.silu(gg)], axis=-1)
    return x + mix @ w_out


def setup_inputs(seed: int = 0) -> dict:
    key = jax.random.key(seed)
    ks = jax.random.split(key, 16)
    f32 = jnp.float32
    nrm = jax.random.normal
    return {
        "x_prompt": nrm(ks[0], (BATCH, SEQ, D_MODEL), f32),
        "x_sample": nrm(ks[1], (DEC_BATCH, DEC_SEQ, D_MODEL), f32),
        "cache_k": nrm(ks[2], (DEPTH, DEC_BATCH, ATT_CACHE, N_KV, HEAD_DIM), f32),
        "cache_v": nrm(ks[3], (DEPTH, DEC_BATCH, ATT_CACHE, N_KV, HEAD_DIM), f32),
        "state_gla": nrm(ks[4], (DEPTH, DEC_BATCH, N_GLA, GLA_DK, GLA_DV), f32) * 0.5,
        "norm_g": 1.0 + 0.02 * nrm(ks[5], (DEPTH, D_MODEL), f32),
        "w_in": nrm(ks[6], (DEPTH, D_MODEL, IN_TOTAL), f32) * D_MODEL ** -0.5,
        "w_gate_up": nrm(ks[7], (DEPTH, GATE_RANK, GLA_KW), f32) * GATE_RANK ** -0.5,
        "b_gate": 0.1 * nrm(ks[8], (DEPTH, GLA_KW), f32),
        "q_norm_g": 1.0 + 0.02 * nrm(ks[9], (DEPTH, HEAD_DIM), f32),
        "k_norm_g": 1.0 + 0.02 * nrm(ks[10], (DEPTH, HEAD_DIM), f32),
        "sinks": 0.5 * nrm(ks[11], (DEPTH, N_HEADS), f32),
        "gla_norm_g": 1.0 + 0.02 * nrm(ks[12], (DEPTH, GLA_DV), f32),
        "w_out": nrm(ks[13], (DEPTH, ATT_WIDTH + GLA_WIDTH, D_MODEL), f32) * (ATT_WIDTH + GLA_WIDTH) ** -0.5,
    }


def reference(x_prompt, x_sample, cache_k, cache_v, state_gla, norm_g, w_in, w_gate_up, b_gate,
              q_norm_g, k_norm_g, sinks, gla_norm_g, w_out):
    h_p, h_s = x_prompt, x_sample
    pk, pv, ps, sk, sv, ss = [], [], [], [], [], []
    for l in range(DEPTH):
        qa, ka, va, ga, qg, kg, vg, gg, la = branch_inputs(
            h_p, norm_g[l], w_in[l], w_gate_up[l], b_gate[l], q_norm_g[l], k_norm_g[l])
        oa = swa_prompt(qa, ka, va, sinks[l])
        s0 = jnp.zeros((h_p.shape[0], N_GLA, GLA_DK, GLA_DV), jnp.float32)
        og, sg = gla_scan(qg, kg, vg, la, s0)
        h_p = merge(h_p, oa, ga, og, gg, gla_norm_g[l], w_out[l])
        pk.append(ka[:, -ATT_CACHE:])
        pv.append(va[:, -ATT_CACHE:])
        ps.append(sg.astype(state_gla.dtype))
        qa, ka, va, ga, qg, kg, vg, gg, la = branch_inputs(
            h_s, norm_g[l], w_in[l], w_gate_up[l], b_gate[l], q_norm_g[l], k_norm_g[l])
        oa, nk, nv = swa_sample(qa, ka, va, cache_k[l], cache_v[l], sinks[l])
        og, sg = gla_scan(qg, kg, vg, la, state_gla[l])
        h_s = merge(h_s, oa, ga, og, gg, gla_norm_g[l], w_out[l])
        sk.append(nk)
        sv.append(nv)
        ss.append(sg.astype(state_gla.dtype))
    return (h_p, h_s, jnp.stack(pk), jnp.stack(pv), jnp.stack(ps), jnp.stack(sk), jnp.stack(sv), jnp.stack(ss))
```

```python
import functools

import jax
import jax.numpy as jnp
import numpy as np
from jax import lax
from jax.experimental import pallas as pl
from jax.experimental.pallas import tpu as pltpu

D_MODEL = 2048
CHUNK = 64
WINDOW = 128
HEAD_DIM = 64
N_HEADS = 16
N_KV = 4
GQA_REP = N_HEADS // N_KV
ATT_WIDTH = N_HEADS * HEAD_DIM
KV_WIDTH = N_KV * HEAD_DIM
N_GLA = 4
GLA_DK = 128
GLA_DV = 256
GLA_KW = N_GLA * GLA_DK
GLA_WIDTH = N_GLA * GLA_DV
GATE_RANK = 16
GATE_TAU = 16.0
EPS = 1e-6
ATT_SCALE = HEAD_DIM ** -0.5
WIN_KEYS = WINDOW + CHUNK

LANES = 128
QK_WIDTH = ATT_WIDTH + KV_WIDTH
N_QK_GROUPS = QK_WIDTH // HEAD_DIM
OFF_Q, OFF_K, OFF_V = 0, ATT_WIDTH, ATT_WIDTH + KV_WIDTH
OFF_GA = OFF_V + KV_WIDTH
OFF_QG = OFF_GA + ATT_WIDTH
OFF_KG = OFF_QG + GLA_KW
OFF_VG = OFF_KG + GLA_KW
OFF_GG = OFF_VG + GLA_WIDTH
MAIN_COLS = OFF_GG + GLA_WIDTH

VMEM_LIMIT = 56 * 1024 * 1024

BF16 = jnp.bfloat16
F32 = jnp.float32


def _dot(a, b):
    return jnp.dot(a, b, preferred_element_type=F32)


def _dot_t(a, b):
    return lax.dot_general(a, b, (((1,), (1,)), ((), ())), preferred_element_type=F32)


def _dot_ta(a, b):
    return lax.dot_general(a, b, (((0,), (0,)), ((), ())), preferred_element_type=F32)


def _silu(x):
    return x / (1.0 + jnp.exp(-x))


def _const_spec(shape):
    nd = len(shape)
    return pl.BlockSpec(shape, lambda *_: (0,) * nd, pipeline_mode=pl.Buffered(1))


def _inproj_kernel(x_ref, ng_ref, w_ref, wlr_ref, wgu_ref, bg_ref, gqk_ref, gsum_ref, gexp_ref,
                   q_ref, k_ref, v_ref, ga_ref, qg_ref, kg_ref, vg_ref, gg_ref, la_ref):
    x = x_ref[...]
    ms = jnp.mean(x * x, axis=-1, keepdims=True)
    xn = (x * lax.rsqrt(ms + EPS) * ng_ref[...]).astype(BF16)

    qk = _dot(xn, w_ref[:, OFF_Q:OFF_V])
    ss = _dot((qk * qk).astype(BF16), gsum_ref[...])
    r = lax.rsqrt(ss * (1.0 / HEAD_DIM) + EPS)
    r_hi = r.astype(BF16)
    r_lo = (r - r_hi.astype(F32)).astype(BF16)
    r_full = _dot(jnp.concatenate([r_hi, r_lo], axis=1), gexp_ref[...])
    qkn = qk * r_full * gqk_ref[...]
    q_ref[...] = qkn[:, :ATT_WIDTH].astype(BF16)
    k_ref[...] = qkn[:, ATT_WIDTH:]

    v_ref[...] = _dot(xn, w_ref[:, OFF_V:OFF_GA])
    ga_ref[...] = _silu(_dot(xn, w_ref[:, OFF_GA:OFF_QG])).astype(BF16)
    qg_ref[...] = (_dot(xn, w_ref[:, OFF_QG:OFF_KG]) * (GLA_DK ** -0.5)).astype(BF16)
    kg_ref[...] = _dot(xn, w_ref[:, OFF_KG:OFF_VG]).astype(BF16)
    vg_ref[...] = _dot(xn, w_ref[:, OFF_VG:OFF_GG]).astype(BF16)
    gg_ref[...] = _silu(_dot(xn, w_ref[:, OFF_GG:MAIN_COLS])).astype(BF16)

    lr = _dot(xn, wlr_ref[...]).astype(BF16)
    z = _dot(lr, wgu_ref[...]) + bg_ref[...]
    log_sig = jnp.minimum(z, 0.0) - jnp.log(1.0 + jnp.exp(-jnp.abs(z)))
    la_ref[...] = log_sig * (1.0 / GATE_TAU)


def _inproj(x2d, p, tm):
    m = x2d.shape[0]
    row = lambda w: pl.BlockSpec((tm, w), lambda i: (i, 0))
    outs = [(ATT_WIDTH, BF16), (KV_WIDTH, F32), (KV_WIDTH, F32), (ATT_WIDTH, BF16),
            (GLA_KW, BF16), (GLA_KW, BF16), (GLA_WIDTH, BF16), (GLA_WIDTH, BF16), (GLA_KW, F32)]
    consts = [p["norm_g"], p["w_main"], p["w_lr"], p["w_gu"], p["b_gate"], p["g_qk"],
              p["g_sum"], p["g_exp"]]
    return pl.pallas_call(
        _inproj_kernel,
        grid=(m // tm,),
        in_specs=[row(D_MODEL)] + [_const_spec(c.shape) for c in consts],
        out_specs=[row(w) for w, _ in outs],
        out_shape=[jax.ShapeDtypeStruct((m, w), dt) for w, dt in outs],
        compiler_params=pltpu.CompilerParams(
            dimension_semantics=("arbitrary",), vmem_limit_bytes=VMEM_LIMIT),
        name="inproj",
    )(x2d, *consts)


def _attn_prompt_kernel(q_ref, kp_ref, kc_ref, vp_ref, vc_ref, ga_ref, sink_ref,
                        o_ref, kw_ref, vw_ref, *, tq):
    i = pl.program_id(1)
    kw_ref[0:WINDOW, :] = kp_ref[0].astype(BF16)
    kw_ref[WINDOW:, :] = kc_ref[0].astype(BF16)
    vw_ref[0:WINDOW, :] = vp_ref[0].astype(BF16)
    vw_ref[WINDOW:, :] = vc_ref[0].astype(BF16)

    lane = lax.broadcasted_iota(jnp.int32, (1, LANES), 1)
    lo = lane < HEAD_DIM
    col_group = lax.broadcasted_iota(jnp.int32, (1, KV_WIDTH), 1) // HEAD_DIM
    col = lax.broadcasted_iota(jnp.int32, (1, N_KV * WIN_KEYS), 1)
    key_in_seg = col - WIN_KEYS * sum((col >= g * WIN_KEYS).astype(jnp.int32) for g in range(1, N_KV))
    ones_bd = jnp.concatenate(
        [jnp.broadcast_to(jnp.where(col_group == g, 1.0, 0.0).astype(BF16), (WIN_KEYS, KV_WIDTH))
         for g in range(N_KV)], axis=0)
    neg_inf = jnp.float32(-jnp.inf)

    def chunk(j, carry):
        r0 = pl.multiple_of(j * CHUNK, CHUNK)
        kwin = kw_ref[pl.ds(r0, WIN_KEYS), :]
        vwin = vw_ref[pl.ds(r0, WIN_KEYS), :]
        zero = jnp.zeros_like(kwin)
        k_bd = jnp.concatenate([jnp.where(col_group == g, kwin, zero) for g in range(N_KV)], axis=0)
        v_bd = jnp.concatenate([jnp.where(col_group == g, vwin, zero) for g in range(N_KV)], axis=0)
        qs = jnp.concatenate(
            [q_ref[pl.ds(r0, CHUNK), r * KV_WIDTH:(r + 1) * KV_WIDTH] for r in range(GQA_REP)], axis=0)
        s = _dot_t(qs, k_bd)
        first_valid = (2 - (i * (tq // CHUNK) + j)) * CHUNK
        s = jnp.where(key_in_seg >= first_valid, s, neg_inf)

        c = [s[:, LANES * n:LANES * (n + 1)] for n in range(6)]
        seg_max = [
            jnp.maximum(c[0], jnp.where(lo, c[1], neg_inf)),
            jnp.maximum(jnp.where(lo, neg_inf, c[1]), c[2]),
            jnp.maximum(c[3], jnp.where(lo, c[4], neg_inf)),
            jnp.maximum(jnp.where(lo, neg_inf, c[4]), c[5]),
        ]
        m = [jnp.maximum(jnp.max(seg_max[g], axis=-1, keepdims=True), sink_ref[g])
             for g in range(N_KV)]
        shift = [m[0], jnp.where(lo, m[0], m[1]), m[1], m[2], jnp.where(lo, m[2], m[3]), m[3]]
        p = jnp.concatenate([jnp.exp(c[n] - shift[n]) for n in range(6)], axis=1).astype(BF16)

        ov = _dot(p, jnp.concatenate([v_bd, ones_bd], axis=1))
        es = [jnp.exp(sink_ref[g] - m[g]) for g in range(N_KV)]
        e_sink = jnp.concatenate([jnp.where(lo, es[0], es[1]), jnp.where(lo, es[2], es[3])], axis=1)
        o = ov[:, :KV_WIDTH] / (ov[:, KV_WIDTH:] + e_sink)
        for r in range(GQA_REP):
            cols = slice(r * KV_WIDTH, (r + 1) * KV_WIDTH)
            gate = ga_ref[pl.ds(r0, CHUNK), cols].astype(F32)
            o_ref[pl.ds(r0, CHUNK), cols] = (o[r * CHUNK:(r + 1) * CHUNK, :] * gate).astype(BF16)
        return carry

    lax.fori_loop(0, tq // CHUNK, chunk, 0)


def _attn_prompt(q, k, v, ga, sink_rows, batch, seq, tq):
    nblk = seq // tq
    per_win = tq // WINDOW
    rows = lambda w: pl.BlockSpec((tq, w), lambda b, i: (b * nblk + i, 0))
    cur = pl.BlockSpec((1, tq, KV_WIDTH), lambda b, i: (b, i, 0))
    prev = pl.BlockSpec((1, WINDOW, KV_WIDTH), lambda b, i: (b, jnp.maximum(i * per_win - 1, 0), 0))
    k3 = k.reshape(batch, seq, KV_WIDTH)
    v3 = v.reshape(batch, seq, KV_WIDTH)
    return pl.pallas_call(
        functools.partial(_attn_prompt_kernel, tq=tq),
        grid=(batch, nblk),
        in_specs=[rows(ATT_WIDTH), prev, cur, prev, cur, rows(ATT_WIDTH), _const_spec(sink_rows.shape)],
        out_specs=rows(ATT_WIDTH),
        out_shape=jax.ShapeDtypeStruct((batch * seq, ATT_WIDTH), BF16),
        scratch_shapes=[pltpu.VMEM((WINDOW + tq, KV_WIDTH), BF16),
                        pltpu.VMEM((WINDOW + tq, KV_WIDTH), BF16)],
        compiler_params=pltpu.CompilerParams(
            dimension_semantics=("arbitrary", "arbitrary"), vmem_limit_bytes=VMEM_LIMIT),
        name="attn_prompt",
    )(q, k3, k3, v3, v3, ga, sink_rows)


def _attn_sample_kernel(q_ref, kn_ref, vn_ref, ck_ref, cv_ref, ga_ref, sink_ref,
                        o_ref, wk_ref, wv_ref):
    keys = jnp.concatenate([ck_ref[0], kn_ref[...]], axis=0)
    vals = jnp.concatenate([cv_ref[0], vn_ref[...]], axis=0)
    n_new = kn_ref.shape[0]
    wk_ref[0] = keys[n_new:, :]
    wv_ref[0] = vals[n_new:, :]
    keys_b = keys.astype(BF16)
    vals_b = vals.astype(BF16)
    for r in range(GQA_REP):
        for g in range(N_KV):
            c0 = r * KV_WIDTH + g * HEAD_DIM
            kv = slice(g * HEAD_DIM, (g + 1) * HEAD_DIM)
            s = _dot_t(q_ref[:, c0:c0 + HEAD_DIM], keys_b[:, kv])
            sink = sink_ref[g, r * CHUNK:r * CHUNK + n_new, 0:1]
            m = jnp.maximum(jnp.max(s, axis=-1, keepdims=True), sink)
            e = jnp.exp(s - m)
            p = e / (jnp.sum(e, axis=-1, keepdims=True) + jnp.exp(sink - m))
            o = _dot(p.astype(BF16), vals_b[:, kv])
            gate = ga_ref[:, c0:c0 + HEAD_DIM].astype(F32)
            o_ref[:, c0:c0 + HEAD_DIM] = (o * gate).astype(BF16)


def _attn_sample(q, k, v, cache_k, cache_v, ga, sink_rows, batch, t):
    n_cache = cache_k.shape[1]
    rows = lambda w: pl.BlockSpec((t, w), lambda b: (b, 0))
    cache = pl.BlockSpec((1, n_cache, KV_WIDTH), lambda b: (b, 0, 0))
    return pl.pallas_call(
        _attn_sample_kernel,
        grid=(batch,),
        in_specs=[rows(ATT_WIDTH), rows(KV_WIDTH), rows(KV_WIDTH), cache, cache, rows(ATT_WIDTH),
                  _const_spec(sink_rows.shape)],
        out_specs=[rows(ATT_WIDTH), cache, cache],
        out_shape=[jax.ShapeDtypeStruct((batch * t, ATT_WIDTH), BF16),
                   jax.ShapeDtypeStruct((batch, n_cache, KV_WIDTH), F32),
                   jax.ShapeDtypeStruct((batch, n_cache, KV_WIDTH), F32)],
        compiler_params=pltpu.CompilerParams(
            dimension_semantics=("arbitrary",), vmem_limit_bytes=VMEM_LIMIT),
        name="attn_sample",
    )(q, k, v, cache_k, cache_v, ga, sink_rows)


def _gla_kernel(qg_ref, kg_ref, vg_ref, gg_ref, la_ref, s0_ref, ng_ref, tril_ref,
                o_ref, sout_ref, st_ref, *, blk, tg):
    i = pl.program_id(1)

    @pl.when(i == 0)
    def _():
        for h in range(N_GLA):
            st_ref[h] = s0_ref[0, h].T

    rowi = lax.broadcasted_iota(jnp.int32, (blk, blk), 0)
    coli = lax.broadcasted_iota(jnp.int32, (blk, blk), 1)
    causal = rowi >= coli

    def chunk(j, carry):
        r0 = pl.multiple_of(j * blk, blk)
        la = la_ref[pl.ds(r0, blk), :]
        la_hi = la.astype(BF16)
        la_lo = (la - la_hi.astype(F32)).astype(BF16)
        bcum = _dot(tril_ref[...], jnp.concatenate([la_hi, la_lo], axis=0))
        for h in range(N_GLA):
            kc = slice(h * GLA_DK, (h + 1) * GLA_DK)
            vc = slice(h * GLA_DV, (h + 1) * GLA_DV)
            b = bcum[:, kc]
            blast = b[blk - 1:blk, :]
            q = qg_ref[pl.ds(r0, blk), kc].astype(F32)
            k = kg_ref[pl.ds(r0, blk), kc].astype(F32)
            v = vg_ref[pl.ds(r0, blk), vc]
            q_dec = (q * jnp.exp(b)).astype(BF16)
            k_inv = (k * jnp.exp(-b)).astype(BF16)
            k_end = (k * jnp.exp(blast - b)).astype(BF16)
            att = jnp.where(causal, _dot_t(q_dec, k_inv), 0.0).astype(BF16)
            st = st_ref[h]
            o = _dot(att, v) + _dot_t(q_dec, st.astype(BF16))
            st_ref[h] = st * jnp.exp(blast) + _dot_ta(v, k_end)
            ms = jnp.mean(o * o, axis=-1, keepdims=True)
            on = o * lax.rsqrt(ms + EPS) * ng_ref[...]
            gate = gg_ref[pl.ds(r0, blk), vc].astype(F32)
            o_ref[pl.ds(r0, blk), vc] = (on * gate).astype(BF16)
        return carry

    lax.fori_loop(0, tg // blk, chunk, 0)

    @pl.when(i == pl.num_programs(1) - 1)
    def _():
        for h in range(N_GLA):
            sout_ref[0, h] = st_ref[h].T


def _gla(qg, kg, vg, gg, la, s0, norm_g, tril2, batch, seq, blk, tg):
    nblk = seq // tg
    rows = lambda w: pl.BlockSpec((tg, w), lambda b, i: (b * nblk + i, 0))
    state = pl.BlockSpec((1, N_GLA, GLA_DK, GLA_DV), lambda b, i: (b, 0, 0, 0))
    return pl.pallas_call(
        functools.partial(_gla_kernel, blk=blk, tg=tg),
        grid=(batch, nblk),
        in_specs=[rows(GLA_KW), rows(GLA_KW), rows(GLA_WIDTH), rows(GLA_WIDTH), rows(GLA_KW), state,
                  _const_spec(norm_g.shape), _const_spec(tril2.shape)],
        out_specs=[rows(GLA_WIDTH), state],
        out_shape=[jax.ShapeDtypeStruct((batch * seq, GLA_WIDTH), BF16),
                   jax.ShapeDtypeStruct((batch, N_GLA, GLA_DK, GLA_DV), F32)],
        scratch_shapes=[pltpu.VMEM((N_GLA, GLA_DV, GLA_DK), F32)],
        compiler_params=pltpu.CompilerParams(
            dimension_semantics=("arbitrary", "arbitrary"), vmem_limit_bytes=VMEM_LIMIT),
        name="gla",
    )(qg, kg, vg, gg, la, s0, norm_g, tril2)


def _outproj_kernel(x_ref, ma_ref, mg_ref, wa_ref, wg_ref, y_ref):
    y_ref[...] = x_ref[...] + _dot(ma_ref[...], wa_ref[...]) + _dot(mg_ref[...], wg_ref[...])


def _outproj(x2d, mix_a, mix_g, w_a, w_g, tm):
    m = x2d.shape[0]
    row = lambda w: pl.BlockSpec((tm, w), lambda i: (i, 0))
    return pl.pallas_call(
        _outproj_kernel,
        grid=(m // tm,),
        in_specs=[row(D_MODEL), row(ATT_WIDTH), row(GLA_WIDTH), _const_spec(w_a.shape), _const_spec(w_g.shape)],
        out_specs=row(D_MODEL),
        out_shape=jax.ShapeDtypeStruct((m, D_MODEL), F32),
        compiler_params=pltpu.CompilerParams(
            dimension_semantics=("arbitrary",), vmem_limit_bytes=VMEM_LIMIT),
        name="outproj",
    )(x2d, mix_a, mix_g, w_a, w_g)


def _tril2(blk):
    t = np.tril(np.ones((blk, blk), np.float32))
    return jnp.asarray(np.concatenate([t, t], axis=1), BF16)


def _prep(norm_g, w_in, w_gate_up, b_gate, q_norm_g, k_norm_g, sinks, gla_norm_g, w_out):
    def regroup_cols(w):
        return w.reshape(-1, N_KV, GQA_REP, HEAD_DIM).transpose(0, 2, 1, 3).reshape(-1, ATT_WIDTH)

    s = np.cumsum((ATT_WIDTH, KV_WIDTH, KV_WIDTH, ATT_WIDTH, GLA_KW, GLA_KW, GLA_WIDTH, GLA_WIDTH))
    wq, wk, wv, wga, wqg, wkg, wvg, wgg, wlr = jnp.split(w_in, [int(v) for v in s], axis=1)
    w_main = jnp.concatenate([regroup_cols(wq), wk, wv, regroup_cols(wga), wqg, wkg, wvg, wgg],
                             axis=1).astype(BF16)
    w_lr = jnp.pad(wlr, ((0, 0), (0, LANES - GATE_RANK))).astype(BF16)
    w_gu = jnp.pad(w_gate_up, ((0, LANES - GATE_RANK), (0, 0))).astype(BF16)
    g_qk = jnp.concatenate([jnp.tile(q_norm_g, N_HEADS) * ATT_SCALE, jnp.tile(k_norm_g, N_KV)])[None, :]

    grp = np.arange(QK_WIDTH) // HEAD_DIM
    ind = (grp[:, None] == np.arange(LANES)[None, :]).astype(np.float32)
    g_sum = jnp.asarray(ind, BF16)
    g_exp = jnp.asarray(np.concatenate([ind.T, ind.T], axis=0), BF16)

    w_out_a = (w_out[:ATT_WIDTH].reshape(N_KV, GQA_REP, HEAD_DIM, D_MODEL)
               .transpose(1, 0, 2, 3).reshape(ATT_WIDTH, D_MODEL).astype(BF16))
    w_out_g = w_out[ATT_WIDTH:].astype(BF16)

    sk = sinks.astype(F32).reshape(N_KV, GQA_REP)
    sink_rows = jnp.broadcast_to(sk[:, :, None, None], (N_KV, GQA_REP, CHUNK, LANES)).reshape(
        N_KV, GQA_REP * CHUNK, LANES)
    return dict(norm_g=norm_g[None, :], w_main=w_main, w_lr=w_lr, w_gu=w_gu, b_gate=b_gate[None, :],
                g_qk=g_qk, g_sum=g_sum, g_exp=g_exp, w_out_a=w_out_a, w_out_g=w_out_g,
                sink_rows=sink_rows, gla_norm_g=gla_norm_g[None, :])


def kernel(x_prompt, x_sample, cache_k, cache_v, state_gla, norm_g, w_in, w_gate_up, b_gate,
           q_norm_g, k_norm_g, sinks, gla_norm_g, w_out):
    depth = norm_g.shape[0]
    assert depth == 1
    B, S, _ = x_prompt.shape
    DB, T, _ = x_sample.shape
    n_cache = cache_k.shape[2]
    p = _prep(norm_g[0], w_in[0], w_gate_up[0], b_gate[0], q_norm_g[0], k_norm_g[0], sinks[0],
              gla_norm_g[0], w_out[0])

    xp = x_prompt.reshape(B * S, D_MODEL)
    q, k, v, ga, qg, kg, vg, gg, la = _inproj(xp, p, tm=256)
    mix_a = _attn_prompt(q, k, v, ga, p["sink_rows"], B, S, tq=512)
    s0 = jnp.zeros((B, N_GLA, GLA_DK, GLA_DV), F32)
    mix_g, gla_p = _gla(qg, kg, vg, gg, la, s0, p["gla_norm_g"], _tril2(CHUNK), B, S, blk=CHUNK, tg=512)
    y_p = _outproj(xp, mix_a, mix_g, p["w_out_a"], p["w_out_g"], tm=512).reshape(B, S, D_MODEL)
    win_k_p = k.reshape(B, S, N_KV, HEAD_DIM)[:, S - n_cache:]
    win_v_p = v.reshape(B, S, N_KV, HEAD_DIM)[:, S - n_cache:]

    xs = x_sample.reshape(DB * T, D_MODEL)
    q, k, v, ga, qg, kg, vg, gg, la = _inproj(xs, p, tm=DB * T)
    ck = cache_k[0].reshape(DB, n_cache, KV_WIDTH)
    cv = cache_v[0].reshape(DB, n_cache, KV_WIDTH)
    mix_a, win_k_s, win_v_s = _attn_sample(q, k, v, ck, cv, ga, p["sink_rows"], DB, T)
    blk = min(CHUNK, T)
    mix_g, gla_s = _gla(qg, kg, vg, gg, la, state_gla[0], p["gla_norm_g"], _tril2(blk), DB, T,
                        blk=blk, tg=T)
    y_s = _outproj(xs, mix_a, mix_g, p["w_out_a"], p["w_out_g"], tm=DB * T).reshape(DB, T, D_MODEL)

    shape5 = lambda a: a.reshape(1, a.shape[0], n_cache, N_KV, HEAD_DIM)
    return (y_p, y_s, win_k_p[None], win_v_p[None], gla_p[None],
            shape5(win_k_s), shape5(win_v_s), gla_s[None])
```

```python
import functools

import jax
import jax.numpy as jnp
import numpy as np
from jax import lax
from jax.experimental import pallas as pl
from jax.experimental.pallas import tpu as pltpu

D_MODEL = 2048
CHUNK = 64
WINDOW = 128
HEAD_DIM = 64
N_HEADS = 16
N_KV = 4
GQA_REP = N_HEADS // N_KV
ATT_WIDTH = N_HEADS * HEAD_DIM
KV_WIDTH = N_KV * HEAD_DIM
N_GLA = 4
GLA_DK = 128
GLA_DV = 256
GLA_KW = N_GLA * GLA_DK
GLA_WIDTH = N_GLA * GLA_DV
GATE_RANK = 16
GATE_TAU = 16.0
EPS = 1e-6
ATT_SCALE = HEAD_DIM ** -0.5
LOG2E = float(np.log2(np.e))
WIN_CHUNKS = WINDOW // CHUNK
WIN_KEYS = WINDOW + CHUNK

LANES = 128
QK_WIDTH = ATT_WIDTH + KV_WIDTH
N_QK_GROUPS = QK_WIDTH // HEAD_DIM
OFF_Q, OFF_K, OFF_V = 0, ATT_WIDTH, ATT_WIDTH + KV_WIDTH
OFF_GA = OFF_V + KV_WIDTH
OFF_QG = OFF_GA + ATT_WIDTH
OFF_KG = OFF_QG + GLA_KW
OFF_VG = OFF_KG + GLA_KW
OFF_GG = OFF_VG + GLA_WIDTH
MAIN_COLS = OFF_GG + GLA_WIDTH

VMEM_LIMIT = 56 * 1024 * 1024

BF16 = jnp.bfloat16
F32 = jnp.float32


def _dot(a, b):
    return jnp.dot(a, b, preferred_element_type=F32)


def _dot_t(a, b):
    return lax.dot_general(a, b, (((1,), (1,)), ((), ())), preferred_element_type=F32)


def _dot_ta(a, b):
    return lax.dot_general(a, b, (((0,), (0,)), ((), ())), preferred_element_type=F32)


def _silu(x):
    return x / (1.0 + jnp.exp(-x))


def _const_spec(shape):
    nd = len(shape)
    return pl.BlockSpec(shape, lambda *_: (0,) * nd, pipeline_mode=pl.Buffered(1))


def _inproj_kernel(x_ref, ng_ref, w_ref, wlr_ref, wgu_ref, bg_ref, gqk_ref, gsum_ref, gexp_ref,
                   q_ref, k_ref, v_ref, ga_ref, qg_ref, kg_ref, vg_ref, gg_ref, la_ref):
    x = x_ref[...]
    ms = jnp.mean(x * x, axis=-1, keepdims=True)
    xn = (x * lax.rsqrt(ms + EPS) * ng_ref[...]).astype(BF16)

    qk = _dot(xn, w_ref[:, OFF_Q:OFF_V])
    ss = _dot((qk * qk).astype(BF16), gsum_ref[...])
    r = lax.rsqrt(ss * (1.0 / HEAD_DIM) + EPS)
    r_hi = r.astype(BF16)
    r_lo = (r - r_hi.astype(F32)).astype(BF16)
    r_full = _dot(jnp.concatenate([r_hi, r_lo], axis=1), gexp_ref[...])
    qkn = qk * r_full * gqk_ref[...]
    q_ref[...] = qkn[:, :ATT_WIDTH].astype(BF16)
    k_ref[...] = qkn[:, ATT_WIDTH:]

    v_ref[...] = _dot(xn, w_ref[:, OFF_V:OFF_GA])
    ga_ref[...] = _silu(_dot(xn, w_ref[:, OFF_GA:OFF_QG])).astype(BF16)
    qg_ref[...] = (_dot(xn, w_ref[:, OFF_QG:OFF_KG]) * (GLA_DK ** -0.5)).astype(BF16)
    kg_ref[...] = _dot(xn, w_ref[:, OFF_KG:OFF_VG]).astype(BF16)
    vg_ref[...] = _dot(xn, w_ref[:, OFF_VG:OFF_GG]).astype(BF16)
    gg_ref[...] = _silu(_dot(xn, w_ref[:, OFF_GG:MAIN_COLS])).astype(BF16)

    lr = _dot(xn, wlr_ref[...]).astype(BF16)
    z = _dot(lr, wgu_ref[...]) + bg_ref[...]
    log_sig = jnp.minimum(z, 0.0) - jnp.log(1.0 + jnp.exp(-jnp.abs(z)))
    la_ref[...] = log_sig * (1.0 / GATE_TAU)


def _inproj(x2d, p, tm):
    m = x2d.shape[0]
    row = lambda w: pl.BlockSpec((tm, w), lambda i: (i, 0))
    outs = [(ATT_WIDTH, BF16), (KV_WIDTH, F32), (KV_WIDTH, F32), (ATT_WIDTH, BF16),
            (GLA_KW, BF16), (GLA_KW, BF16), (GLA_WIDTH, BF16), (GLA_WIDTH, BF16), (GLA_KW, F32)]
    consts = [p["norm_g"], p["w_main"], p["w_lr"], p["w_gu"], p["b_gate"], p["g_qk"],
              p["g_sum"], p["g_exp"]]
    return pl.pallas_call(
        _inproj_kernel,
        grid=(m // tm,),
        in_specs=[row(D_MODEL)] + [_const_spec(c.shape) for c in consts],
        out_specs=[row(w) for w, _ in outs],
        out_shape=[jax.ShapeDtypeStruct((m, w), dt) for w, dt in outs],
        compiler_params=pltpu.CompilerParams(
            dimension_semantics=("arbitrary",), vmem_limit_bytes=VMEM_LIMIT),
        name="inproj",
    )(x2d, *consts)


def _attn_prompt_kernel(q_ref, kp_ref, kc_ref, vp_ref, vc_ref, ga_ref, sink_ref,
                        o_ref, kw_ref, vw_ref, *, tq, unroll):
    assert unroll >= WIN_CHUNKS and tq % (CHUNK * unroll) == 0
    i = pl.program_id(1)
    kw_ref[0:WINDOW, :] = kp_ref[0].astype(BF16)
    kw_ref[WINDOW:, :] = kc_ref[0].astype(BF16)
    vw_ref[0:WINDOW, :] = vp_ref[0].astype(BF16)
    vw_ref[WINDOW:, :] = vc_ref[0].astype(BF16)

    lane = lax.broadcasted_iota(jnp.int32, (1, LANES), 1)
    lo = lane < HEAD_DIM
    col_group = lax.broadcasted_iota(jnp.int32, (1, KV_WIDTH), 1) // HEAD_DIM
    col = lax.broadcasted_iota(jnp.int32, (1, N_KV * WIN_KEYS), 1)
    key_in_seg = col - WIN_KEYS * sum((col >= g * WIN_KEYS).astype(jnp.int32) for g in range(1, N_KV))
    ones_bd = jnp.concatenate(
        [jnp.broadcast_to(jnp.where(col_group == g, 1.0, 0.0).astype(BF16), (WIN_KEYS, KV_WIDTH))
         for g in range(N_KV)], axis=0)
    neg_inf = jnp.float32(-jnp.inf)

    def chunk_group(j, n_masked):
        rows = [pl.ds(pl.multiple_of((j * unroll + u) * CHUNK, CHUNK), CHUNK) for u in range(unroll)]
        wins = [pl.ds(pl.multiple_of((j * unroll + u) * CHUNK, CHUNK), WIN_KEYS) for u in range(unroll)]
        s, v_all = [], []
        for u in range(unroll):
            kwin = kw_ref[wins[u], :]
            vwin = vw_ref[wins[u], :]
            zero = jnp.zeros_like(kwin)
            k_bd = jnp.concatenate([jnp.where(col_group == g, kwin, zero) for g in range(N_KV)], axis=0)
            v_bd = jnp.concatenate([jnp.where(col_group == g, vwin, zero) for g in range(N_KV)], axis=0)
            v_all.append(jnp.concatenate([v_bd, ones_bd], axis=1))
            qs = jnp.concatenate(
                [q_ref[rows[u], r * KV_WIDTH:(r + 1) * KV_WIDTH] for r in range(GQA_REP)], axis=0)
            s.append(_dot_t(qs, k_bd))
        p, m = [], []
        for u in range(unroll):
            su = s[u]
            if u < n_masked:
                su = jnp.where(key_in_seg >= (WIN_CHUNKS - u) * CHUNK, su, neg_inf)
            c = [su[:, LANES * n:LANES * (n + 1)] for n in range(6)]
            seg_max = [
                jnp.maximum(c[0], jnp.where(lo, c[1], neg_inf)),
                jnp.maximum(jnp.where(lo, neg_inf, c[1]), c[2]),
                jnp.maximum(c[3], jnp.where(lo, c[4], neg_inf)),
                jnp.maximum(jnp.where(lo, neg_inf, c[4]), c[5]),
            ]
            mu = [jnp.maximum(jnp.max(seg_max[g], axis=-1, keepdims=True), sink_ref[g])
                  for g in range(N_KV)]
            shift = [mu[0], jnp.where(lo, mu[0], mu[1]), mu[1], mu[2], jnp.where(lo, mu[2], mu[3]), mu[3]]
            p.append(jnp.concatenate([jnp.exp2(c[n] - shift[n]) for n in range(6)], axis=1).astype(BF16))
            m.append(mu)
        ov = [_dot(p[u], v_all[u]) for u in range(unroll)]
        for u in range(unroll):
            es = [jnp.exp2(sink_ref[g] - m[u][g]) for g in range(N_KV)]
            e_sink = jnp.concatenate([jnp.where(lo, es[0], es[1]), jnp.where(lo, es[2], es[3])], axis=1)
            o = ov[u][:, :KV_WIDTH] / (ov[u][:, KV_WIDTH:] + e_sink)
            for r in range(GQA_REP):
                cols = slice(r * KV_WIDTH, (r + 1) * KV_WIDTH)
                gate = ga_ref[rows[u], cols].astype(F32)
                o_ref[rows[u], cols] = (o[r * CHUNK:(r + 1) * CHUNK, :] * gate).astype(BF16)

    n_masked = min(WIN_CHUNKS, unroll)

    @pl.when(i == 0)
    def _():
        chunk_group(0, n_masked)

    @pl.when(i != 0)
    def _():
        chunk_group(0, 0)

    def body(j, carry):
        chunk_group(j, 0)
        return carry

    lax.fori_loop(1, tq // (CHUNK * unroll), body, 0)


def _attn_prompt(q, k, v, ga, sink_rows, batch, seq, tq, unroll):
    nblk = seq // tq
    per_win = tq // WINDOW
    rows = lambda w: pl.BlockSpec((tq, w), lambda b, i: (b * nblk + i, 0))
    cur = pl.BlockSpec((1, tq, KV_WIDTH), lambda b, i: (b, i, 0))
    prev = pl.BlockSpec((1, WINDOW, KV_WIDTH), lambda b, i: (b, jnp.maximum(i * per_win - 1, 0), 0))
    k3 = k.reshape(batch, seq, KV_WIDTH)
    v3 = v.reshape(batch, seq, KV_WIDTH)
    return pl.pallas_call(
        functools.partial(_attn_prompt_kernel, tq=tq, unroll=unroll),
        grid=(batch, nblk),
        in_specs=[rows(ATT_WIDTH), prev, cur, prev, cur, rows(ATT_WIDTH), _const_spec(sink_rows.shape)],
        out_specs=rows(ATT_WIDTH),
        out_shape=jax.ShapeDtypeStruct((batch * seq, ATT_WIDTH), BF16),
        scratch_shapes=[pltpu.VMEM((WINDOW + tq, KV_WIDTH), BF16),
                        pltpu.VMEM((WINDOW + tq, KV_WIDTH), BF16)],
        compiler_params=pltpu.CompilerParams(
            dimension_semantics=("arbitrary", "arbitrary"), vmem_limit_bytes=VMEM_LIMIT),
        name="attn_prompt",
    )(q, k3, k3, v3, v3, ga, sink_rows)


def _attn_sample_kernel(q_ref, kn_ref, vn_ref, ck_ref, cv_ref, ga_ref, sink_ref,
                        o_ref, wk_ref, wv_ref, *, t, bb):
    n_cache = ck_ref.shape[1]
    assert n_cache == LANES and t <= LANES
    lane = lax.broadcasted_iota(jnp.int32, (1, LANES), 1)
    lo = lane < HEAD_DIM
    is_new = lane < t
    col_group = lax.broadcasted_iota(jnp.int32, (1, KV_WIDTH), 1) // HEAD_DIM
    seg = 2 * LANES
    ones_bd = jnp.concatenate(
        [jnp.broadcast_to(jnp.where(col_group == g, 1.0, 0.0).astype(BF16), (seg, KV_WIDTH))
         for g in range(N_KV)], axis=0)
    pad = jnp.zeros((LANES - t, KV_WIDTH), BF16)
    neg_inf = jnp.float32(-jnp.inf)

    s, v_all = [], []
    for b in range(bb):
        rows = slice(b * t, (b + 1) * t)
        kn, vn, ck, cv = kn_ref[rows, :], vn_ref[rows, :], ck_ref[b], cv_ref[b]
        wk_ref[b] = jnp.concatenate([ck[t:, :], kn], axis=0)
        wv_ref[b] = jnp.concatenate([cv[t:, :], vn], axis=0)
        kcat = jnp.concatenate([ck.astype(BF16), kn.astype(BF16), pad], axis=0)
        vcat = jnp.concatenate([cv.astype(BF16), vn.astype(BF16), pad], axis=0)
        zero = jnp.zeros_like(kcat)
        k_bd = jnp.concatenate([jnp.where(col_group == g, kcat, zero) for g in range(N_KV)], axis=0)
        v_bd = jnp.concatenate([jnp.where(col_group == g, vcat, zero) for g in range(N_KV)], axis=0)
        v_all.append(jnp.concatenate([v_bd, ones_bd], axis=1))
        qs = jnp.concatenate(
            [q_ref[rows, r * KV_WIDTH:(r + 1) * KV_WIDTH] for r in range(GQA_REP)], axis=0)
        s.append(_dot_t(qs, k_bd))
    p, m = [], []
    for b in range(bb):
        c = [s[b][:, LANES * n:LANES * (n + 1)] for n in range(2 * N_KV)]
        c = [cn if n % 2 == 0 else jnp.where(is_new, cn, neg_inf) for n, cn in enumerate(c)]
        mb = [jnp.maximum(jnp.max(jnp.maximum(c[2 * g], c[2 * g + 1]), axis=-1, keepdims=True),
                          sink_ref[g]) for g in range(N_KV)]
        p.append(jnp.concatenate([jnp.exp2(c[n] - mb[n // 2]) for n in range(2 * N_KV)],
                                 axis=1).astype(BF16))
        m.append(mb)
    ov = [_dot(p[b], v_all[b]) for b in range(bb)]
    for b in range(bb):
        rows = slice(b * t, (b + 1) * t)
        es = [jnp.exp2(sink_ref[g] - m[b][g]) for g in range(N_KV)]
        e_sink = jnp.concatenate([jnp.where(lo, es[0], es[1]), jnp.where(lo, es[2], es[3])], axis=1)
        o = ov[b][:, :KV_WIDTH] / (ov[b][:, KV_WIDTH:] + e_sink)
        for r in range(GQA_REP):
            cols = slice(r * KV_WIDTH, (r + 1) * KV_WIDTH)
            gate = ga_ref[rows, cols].astype(F32)
            o_ref[rows, cols] = (o[r * t:(r + 1) * t, :] * gate).astype(BF16)


def _attn_sample(q, k, v, cache_k, cache_v, ga, sink_rows, batch, t, bb):
    n_cache = cache_k.shape[1]
    rows = lambda w: pl.BlockSpec((bb * t, w), lambda b: (b, 0))
    cache = pl.BlockSpec((bb, n_cache, KV_WIDTH), lambda b: (b, 0, 0))
    return pl.pallas_call(
        functools.partial(_attn_sample_kernel, t=t, bb=bb),
        grid=(batch // bb,),
        in_specs=[rows(ATT_WIDTH), rows(KV_WIDTH), rows(KV_WIDTH), cache, cache, rows(ATT_WIDTH),
                  _const_spec(sink_rows.shape)],
        out_specs=[rows(ATT_WIDTH), cache, cache],
        out_shape=[jax.ShapeDtypeStruct((batch * t, ATT_WIDTH), BF16),
                   jax.ShapeDtypeStruct((batch, n_cache, KV_WIDTH), F32),
                   jax.ShapeDtypeStruct((batch, n_cache, KV_WIDTH), F32)],
        compiler_params=pltpu.CompilerParams(
            dimension_semantics=("arbitrary",), vmem_limit_bytes=VMEM_LIMIT),
        name="attn_sample",
    )(q, k, v, cache_k, cache_v, ga, sink_rows)


def _gla_kernel(qg_ref, kg_ref, vg_ref, gg_ref, la_ref, s0_ref, ng_ref, tril_ref,
                o_ref, sout_ref, st_ref, *, blk, tg, unroll):
    i = pl.program_id(1)

    @pl.when(i == 0)
    def _():
        st_ref[...] = s0_ref[0]

    rowi = lax.broadcasted_iota(jnp.int32, (blk, blk), 0)
    coli = lax.broadcasted_iota(jnp.int32, (blk, blk), 1)
    causal = rowi >= coli
    eye = (lax.broadcasted_iota(jnp.int32, (GLA_DK, GLA_DK), 0)
           == lax.broadcasted_iota(jnp.int32, (GLA_DK, GLA_DK), 1))

    heads = range(N_GLA)
    kcol = [slice(h * GLA_DK, (h + 1) * GLA_DK) for h in heads]
    vcol = [slice(h * GLA_DV, (h + 1) * GLA_DV) for h in heads]

    def chunk(j, carry):
        rows = [pl.ds(pl.multiple_of((j * unroll + u) * blk, blk), blk) for u in range(unroll)]
        bcum = []
        for r in rows:
            la = la_ref[r, :]
            la_hi = la.astype(BF16)
            la_lo = (la - la_hi.astype(F32)).astype(BF16)
            bcum.append(_dot(tril_ref[...], jnp.concatenate([la_hi, la_lo], axis=0)))
        q_dec, k_inv, k_end, dec_col = {}, {}, {}, {}
        for u, r in enumerate(rows):
            for h in heads:
                b = bcum[u][:, kcol[h]]
                blast = b[blk - 1:blk, :]
                q = qg_ref[r, kcol[h]].astype(F32)
                k = kg_ref[r, kcol[h]].astype(F32)
                q_dec[u, h] = (q * jnp.exp(b)).astype(BF16)
                k_inv[u, h] = (k * jnp.exp(-b)).astype(BF16)
                k_end[u, h] = (k * jnp.exp(blast - b)).astype(BF16)
                dec_row = jnp.broadcast_to(jnp.exp(blast), (GLA_DK, GLA_DK))
                dec_col[u, h] = jnp.sum(jnp.where(eye, dec_row, 0.0), axis=-1, keepdims=True)
        att = {(u, h): _dot_t(q_dec[u, h], k_inv[u, h]) for u in range(unroll) for h in heads}
        d_st = {(u, h): _dot_ta(k_end[u, h], vg_ref[rows[u], vcol[h]])
                for u in range(unroll) for h in heads}
        att = {key: jnp.where(causal, a, 0.0).astype(BF16) for key, a in att.items()}
        st = [st_ref[h] for h in heads]
        o = {}
        for u, r in enumerate(rows):
            for h in heads:
                o[u, h] = _dot(att[u, h], vg_ref[r, vcol[h]]) + _dot(q_dec[u, h], st[h].astype(BF16))
                st[h] = st[h] * dec_col[u, h] + d_st[u, h]
        for h in heads:
            st_ref[h] = st[h]
        for u, r in enumerate(rows):
            for h in heads:
                ov = o[u, h]
                ms = jnp.mean(ov * ov, axis=-1, keepdims=True)
                on = ov * lax.rsqrt(ms + EPS) * ng_ref[...]
                gate = gg_ref[r, vcol[h]].astype(F32)
                o_ref[r, vcol[h]] = (on * gate).astype(BF16)
        return carry

    lax.fori_loop(0, tg // (blk * unroll), chunk, 0)

    @pl.when(i == pl.num_programs(1) - 1)
    def _():
        sout_ref[0] = st_ref[...]


def _gla(qg, kg, vg, gg, la, s0, norm_g, tril2, batch, seq, blk, tg, unroll=1):
    nblk = seq // tg
    rows = lambda w: pl.BlockSpec((tg, w), lambda b, i: (b * nblk + i, 0))
    state = pl.BlockSpec((1, N_GLA, GLA_DK, GLA_DV), lambda b, i: (b, 0, 0, 0))
    return pl.pallas_call(
        functools.partial(_gla_kernel, blk=blk, tg=tg, unroll=unroll),
        grid=(batch, nblk),
        in_specs=[rows(GLA_KW), rows(GLA_KW), rows(GLA_WIDTH), rows(GLA_WIDTH), rows(GLA_KW), state,
                  _const_spec(norm_g.shape), _const_spec(tril2.shape)],
        out_specs=[rows(GLA_WIDTH), state],
        out_shape=[jax.ShapeDtypeStruct((batch * seq, GLA_WIDTH), BF16),
                   jax.ShapeDtypeStruct((batch, N_GLA, GLA_DK, GLA_DV), F32)],
        scratch_shapes=[pltpu.VMEM((N_GLA, GLA_DK, GLA_DV), F32)],
        compiler_params=pltpu.CompilerParams(
            dimension_semantics=("arbitrary", "arbitrary"), vmem_limit_bytes=VMEM_LIMIT),
        name="gla",
    )(qg, kg, vg, gg, la, s0, norm_g, tril2)


def _outproj_kernel(x_ref, ma_ref, mg_ref, wa_ref, wg_ref, y_ref):
    y_ref[...] = x_ref[...] + _dot(ma_ref[...], wa_ref[...]) + _dot(mg_ref[...], wg_ref[...])


def _outproj(x2d, mix_a, mix_g, w_a, w_g, tm):
    m = x2d.shape[0]
    row = lambda w: pl.BlockSpec((tm, w), lambda i: (i, 0))
    return pl.pallas_call(
        _outproj_kernel,
        grid=(m // tm,),
        in_specs=[row(D_MODEL), row(ATT_WIDTH), row(GLA_WIDTH), _const_spec(w_a.shape), _const_spec(w_g.shape)],
        out_specs=row(D_MODEL),
        out_shape=jax.ShapeDtypeStruct((m, D_MODEL), F32),
        compiler_params=pltpu.CompilerParams(
            dimension_semantics=("arbitrary",), vmem_limit_bytes=VMEM_LIMIT),
        name="outproj",
    )(x2d, mix_a, mix_g, w_a, w_g)


def _tril2(blk):
    t = np.tril(np.ones((blk, blk), np.float32))
    return jnp.asarray(np.concatenate([t, t], axis=1), BF16)


def _prep(norm_g, w_in, w_gate_up, b_gate, q_norm_g, k_norm_g, sinks, gla_norm_g, w_out):
    def regroup_cols(w):
        return w.reshape(-1, N_KV, GQA_REP, HEAD_DIM).transpose(0, 2, 1, 3).reshape(-1, ATT_WIDTH)

    s = np.cumsum((ATT_WIDTH, KV_WIDTH, KV_WIDTH, ATT_WIDTH, GLA_KW, GLA_KW, GLA_WIDTH, GLA_WIDTH))
    wq, wk, wv, wga, wqg, wkg, wvg, wgg, wlr = jnp.split(w_in, [int(v) for v in s], axis=1)
    w_main = jnp.concatenate([regroup_cols(wq), wk, wv, regroup_cols(wga), wqg, wkg, wvg, wgg],
                             axis=1).astype(BF16)
    w_lr = jnp.pad(wlr, ((0, 0), (0, LANES - GATE_RANK))).astype(BF16)
    w_gu = jnp.pad(w_gate_up, ((0, LANES - GATE_RANK), (0, 0))).astype(BF16)
    g_qk = jnp.concatenate([jnp.tile(q_norm_g, N_HEADS) * (ATT_SCALE * LOG2E),
                            jnp.tile(k_norm_g, N_KV)])[None, :]

    grp = np.arange(QK_WIDTH) // HEAD_DIM
    ind = (grp[:, None] == np.arange(LANES)[None, :]).astype(np.float32)
    g_sum = jnp.asarray(ind, BF16)
    g_exp = jnp.asarray(np.concatenate([ind.T, ind.T], axis=0), BF16)

    w_out_a = (w_out[:ATT_WIDTH].reshape(N_KV, GQA_REP, HEAD_DIM, D_MODEL)
               .transpose(1, 0, 2, 3).reshape(ATT_WIDTH, D_MODEL).astype(BF16))
    w_out_g = w_out[ATT_WIDTH:].astype(BF16)

    def sink_rows(t):
        sk = sinks.astype(F32).reshape(N_KV, GQA_REP) * LOG2E
        return jnp.broadcast_to(sk[:, :, None, None], (N_KV, GQA_REP, t, LANES)).reshape(
            N_KV, GQA_REP * t, LANES)

    return dict(norm_g=norm_g[None, :], w_main=w_main, w_lr=w_lr, w_gu=w_gu, b_gate=b_gate[None, :],
                g_qk=g_qk, g_sum=g_sum, g_exp=g_exp, w_out_a=w_out_a, w_out_g=w_out_g,
                sink_rows=sink_rows, gla_norm_g=gla_norm_g[None, :])


def kernel(x_prompt, x_sample, cache_k, cache_v, state_gla, norm_g, w_in, w_gate_up, b_gate,
           q_norm_g, k_norm_g, sinks, gla_norm_g, w_out):
    depth = norm_g.shape[0]
    assert depth == 1
    B, S, _ = x_prompt.shape
    DB, T, _ = x_sample.shape
    n_cache = cache_k.shape[2]
    p = _prep(norm_g[0], w_in[0], w_gate_up[0], b_gate[0], q_norm_g[0], k_norm_g[0], sinks[0],
              gla_norm_g[0], w_out[0])

    xp = x_prompt.reshape(B * S, D_MODEL)
    q, k, v, ga, qg, kg, vg, gg, la = _inproj(xp, p, tm=256)
    mix_a = _attn_prompt(q, k, v, ga, p["sink_rows"](CHUNK), B, S, tq=512, unroll=2)
    s0 = jnp.zeros((B, N_GLA, GLA_DK, GLA_DV), F32)
    mix_g, gla_p = _gla(qg, kg, vg, gg, la, s0, p["gla_norm_g"], _tril2(CHUNK), B, S, blk=CHUNK, tg=512,
                      unroll=4)
    y_p = _outproj(xp, mix_a, mix_g, p["w_out_a"], p["w_out_g"], tm=512).reshape(B, S, D_MODEL)
    win_k_p = k.reshape(B, S, N_KV, HEAD_DIM)[:, S - n_cache:]
    win_v_p = v.reshape(B, S, N_KV, HEAD_DIM)[:, S - n_cache:]

    xs = x_sample.reshape(DB * T, D_MODEL)
    q, k, v, ga, qg, kg, vg, gg, la = _inproj(xs, p, tm=DB * T)
    ck = cache_k[0].reshape(DB, n_cache, KV_WIDTH)
    cv = cache_v[0].reshape(DB, n_cache, KV_WIDTH)
    mix_a, win_k_s, win_v_s = _attn_sample(q, k, v, ck, cv, ga, p["sink_rows"](T), DB, T, bb=4)
    blk = min(CHUNK, T)
    mix_g, gla_s = _gla(qg, kg, vg, gg, la, state_gla[0], p["gla_norm_g"], _tril2(blk), DB, T,
                        blk=blk, tg=T)
    y_s = _outproj(xs, mix_a, mix_g, p["w_out_a"], p["w_out_g"], tm=DB * T).reshape(DB, T, D_MODEL)

    shape5 = lambda a: a.reshape(1, a.shape[0], n_cache, N_KV, HEAD_DIM)
    return (y_p, y_s, win_k_p[None], win_v_p[None], gla_p[None],
            shape5(win_k_s), shape5(win_v_s), gla_s[None])
```

```python
import functools

import jax
import jax.numpy as jnp
import numpy as np
from jax import lax
from jax.experimental import pallas as pl
from jax.experimental.pallas import tpu as pltpu

D_MODEL = 2048
CHUNK = 64
WINDOW = 128
HEAD_DIM = 64
N_HEADS = 16
N_KV = 4
GQA_REP = N_HEADS // N_KV
ATT_WIDTH = N_HEADS * HEAD_DIM
KV_WIDTH = N_KV * HEAD_DIM
N_GLA = 4
GLA_DK = 128
GLA_DV = 256
GLA_KW = N_GLA * GLA_DK
GLA_WIDTH = N_GLA * GLA_DV
GATE_RANK = 16
GATE_TAU = 16.0
EPS = 1e-6
ATT_SCALE = HEAD_DIM ** -0.5
LOG2E = float(np.log2(np.e))
WIN_CHUNKS = WINDOW // CHUNK
WIN_KEYS = WINDOW + CHUNK

LANES = 128
QK_WIDTH = ATT_WIDTH + KV_WIDTH
N_QK_GROUPS = QK_WIDTH // HEAD_DIM
OFF_Q, OFF_K, OFF_V = 0, ATT_WIDTH, ATT_WIDTH + KV_WIDTH
OFF_GA = OFF_V + KV_WIDTH
OFF_QG = OFF_GA + ATT_WIDTH
OFF_KG = OFF_QG + GLA_KW
OFF_VG = OFF_KG + GLA_KW
OFF_GG = OFF_VG + GLA_WIDTH
MAIN_COLS = OFF_GG + GLA_WIDTH

VMEM_LIMIT = 56 * 1024 * 1024

BF16 = jnp.bfloat16
F32 = jnp.float32


def _dot(a, b):
    return jnp.dot(a, b, preferred_element_type=F32)


def _dot_t(a, b):
    return lax.dot_general(a, b, (((1,), (1,)), ((), ())), preferred_element_type=F32)


def _dot_ta(a, b):
    return lax.dot_general(a, b, (((0,), (0,)), ((), ())), preferred_element_type=F32)


def _silu(x):
    return x / (1.0 + jnp.exp(-x))


def _const_spec(shape):
    nd = len(shape)
    return pl.BlockSpec(shape, lambda *_: (0,) * nd, pipeline_mode=pl.Buffered(1))


def _inproj_kernel(x_ref, ng_ref, w_ref, wlr_ref, wgu_ref, bg_ref, gqk_ref, gsum_ref, gexp_ref,
                   q_ref, k_ref, v_ref, ga_ref, qg_ref, kg_ref, vg_ref, gg_ref, la_ref, *, sub):
    for s in range(x_ref.shape[0] // sub):
        rows = slice(s * sub, (s + 1) * sub)
        x = x_ref[rows, :]
        ms = jnp.mean(x * x, axis=-1, keepdims=True)
        xn = (x * lax.rsqrt(ms + EPS) * ng_ref[...]).astype(BF16)

        qk = _dot(xn, w_ref[:, OFF_Q:OFF_V])
        ga_ref[rows, :] = _silu(_dot(xn, w_ref[:, OFF_GA:OFF_QG])).astype(BF16)
        ss = _dot((qk * qk).astype(BF16), gsum_ref[...])
        lr = _dot(xn, wlr_ref[...]).astype(BF16)
        qg_ref[rows, :] = (_dot(xn, w_ref[:, OFF_QG:OFF_KG]) * (GLA_DK ** -0.5)).astype(BF16)
        kg_ref[rows, :] = _dot(xn, w_ref[:, OFF_KG:OFF_VG]).astype(BF16)
        r = lax.rsqrt(ss * (1.0 / HEAD_DIM) + EPS)
        r_hi = r.astype(BF16)
        r_lo = (r - r_hi.astype(F32)).astype(BF16)
        r_full = _dot(jnp.concatenate([r_hi, r_lo], axis=1), gexp_ref[...])
        z = _dot(lr, wgu_ref[...]) + bg_ref[...]
        gg_ref[rows, :] = _silu(_dot(xn, w_ref[:, OFF_GG:MAIN_COLS])).astype(BF16)
        vg_ref[rows, :] = _dot(xn, w_ref[:, OFF_VG:OFF_GG]).astype(BF16)
        v_ref[rows, :] = _dot(xn, w_ref[:, OFF_V:OFF_GA])

        qkn = qk * r_full * gqk_ref[...]
        q_ref[rows, :] = qkn[:, :ATT_WIDTH].astype(BF16)
        k_ref[rows, :] = qkn[:, ATT_WIDTH:]
        log_sig = jnp.minimum(z, 0.0) - jnp.log(1.0 + jnp.exp(-jnp.abs(z)))
        la_ref[rows, :] = log_sig * (1.0 / GATE_TAU)


def _inproj(x2d, p, tm):
    m = x2d.shape[0]
    row = lambda w: pl.BlockSpec((tm, w), lambda i: (i, 0))
    outs = [(ATT_WIDTH, BF16), (KV_WIDTH, F32), (KV_WIDTH, F32), (ATT_WIDTH, BF16),
            (GLA_KW, BF16), (GLA_KW, BF16), (GLA_WIDTH, BF16), (GLA_WIDTH, BF16), (GLA_KW, F32)]
    consts = [p["norm_g"], p["w_main"], p["w_lr"], p["w_gu"], p["b_gate"], p["g_qk"],
              p["g_sum"], p["g_exp"]]
    return pl.pallas_call(
        functools.partial(_inproj_kernel, sub=min(tm, 256)),
        grid=(m // tm,),
        in_specs=[row(D_MODEL)] + [_const_spec(c.shape) for c in consts],
        out_specs=[row(w) for w, _ in outs],
        out_shape=[jax.ShapeDtypeStruct((m, w), dt) for w, dt in outs],
        compiler_params=pltpu.CompilerParams(
            dimension_semantics=("arbitrary",), vmem_limit_bytes=VMEM_LIMIT),
        name="inproj",
    )(x2d, *consts)


def _attn_prompt_kernel(q_ref, kp_ref, kc_ref, vp_ref, vc_ref, ga_ref, sink_ref,
                        o_ref, wk_ref, wv_ref, kw_ref, vw_ref, *, tq, unroll):
    assert unroll >= WIN_CHUNKS and tq % (CHUNK * unroll) == 0 and tq >= WINDOW
    i = pl.program_id(1)
    kw_ref[0:WINDOW, :] = kp_ref[0].astype(BF16)
    kw_ref[WINDOW:, :] = kc_ref[0].astype(BF16)
    vw_ref[0:WINDOW, :] = vp_ref[0].astype(BF16)
    vw_ref[WINDOW:, :] = vc_ref[0].astype(BF16)

    @pl.when(i == pl.num_programs(1) - 1)
    def _():
        wk_ref[0] = kc_ref[0, tq - WINDOW:, :]
        wv_ref[0] = vc_ref[0, tq - WINDOW:, :]

    lane = lax.broadcasted_iota(jnp.int32, (1, LANES), 1)
    lo = lane < HEAD_DIM
    col_group = lax.broadcasted_iota(jnp.int32, (1, KV_WIDTH), 1) // HEAD_DIM
    col = lax.broadcasted_iota(jnp.int32, (1, N_KV * WIN_KEYS), 1)
    key_in_seg = col - WIN_KEYS * sum((col >= g * WIN_KEYS).astype(jnp.int32) for g in range(1, N_KV))
    ones_bd = jnp.concatenate(
        [jnp.broadcast_to(jnp.where(col_group == g, 1.0, 0.0).astype(BF16), (WIN_KEYS, KV_WIDTH))
         for g in range(N_KV)], axis=0)
    neg_inf = jnp.float32(-jnp.inf)

    def chunk_group(j, n_masked):
        rows = [pl.ds(pl.multiple_of((j * unroll + u) * CHUNK, CHUNK), CHUNK) for u in range(unroll)]
        wins = [pl.ds(pl.multiple_of((j * unroll + u) * CHUNK, CHUNK), WIN_KEYS) for u in range(unroll)]
        s, v_all = [], []
        for u in range(unroll):
            kwin = kw_ref[wins[u], :]
            vwin = vw_ref[wins[u], :]
            zero = jnp.zeros_like(kwin)
            k_bd = jnp.concatenate([jnp.where(col_group == g, kwin, zero) for g in range(N_KV)], axis=0)
            v_bd = jnp.concatenate([jnp.where(col_group == g, vwin, zero) for g in range(N_KV)], axis=0)
            v_all.append(jnp.concatenate([v_bd, ones_bd], axis=1))
            qs = jnp.concatenate(
                [q_ref[rows[u], r * KV_WIDTH:(r + 1) * KV_WIDTH] for r in range(GQA_REP)], axis=0)
            s.append(_dot_t(qs, k_bd))
        p, m = [], []
        for u in range(unroll):
            su = s[u]
            if u < n_masked:
                su = jnp.where(key_in_seg >= (WIN_CHUNKS - u) * CHUNK, su, neg_inf)
            c = [su[:, LANES * n:LANES * (n + 1)] for n in range(6)]
            seg_max = [
                jnp.maximum(c[0], jnp.where(lo, c[1], neg_inf)),
                jnp.maximum(jnp.where(lo, neg_inf, c[1]), c[2]),
                jnp.maximum(c[3], jnp.where(lo, c[4], neg_inf)),
                jnp.maximum(jnp.where(lo, neg_inf, c[4]), c[5]),
            ]
            mu = [jnp.maximum(jnp.max(seg_max[g], axis=-1, keepdims=True), sink_ref[g])
                  for g in range(N_KV)]
            shift = [mu[0], jnp.where(lo, mu[0], mu[1]), mu[1], mu[2], jnp.where(lo, mu[2], mu[3]), mu[3]]
            p.append(jnp.concatenate([jnp.exp2(c[n] - shift[n]) for n in range(6)], axis=1).astype(BF16))
            m.append(mu)
        ov = [_dot(p[u], v_all[u]) for u in range(unroll)]
        for u in range(unroll):
            es = [jnp.exp2(sink_ref[g] - m[u][g]) for g in range(N_KV)]
            e_sink = jnp.concatenate([jnp.where(lo, es[0], es[1]), jnp.where(lo, es[2], es[3])], axis=1)
            o = ov[u][:, :KV_WIDTH] / (ov[u][:, KV_WIDTH:] + e_sink)
            for r in range(GQA_REP):
                cols = slice(r * KV_WIDTH, (r + 1) * KV_WIDTH)
                gate = ga_ref[rows[u], cols].astype(F32)
                o_ref[rows[u], cols] = (o[r * CHUNK:(r + 1) * CHUNK, :] * gate).astype(BF16)

    n_masked = min(WIN_CHUNKS, unroll)

    @pl.when(i == 0)
    def _():
        chunk_group(0, n_masked)

    @pl.when(i != 0)
    def _():
        chunk_group(0, 0)

    def body(j, carry):
        chunk_group(j, 0)
        return carry

    lax.fori_loop(1, tq // (CHUNK * unroll), body, 0)


def _attn_prompt(q, k, v, ga, sink_rows, batch, seq, tq, unroll):
    nblk = seq // tq
    per_win = tq // WINDOW
    rows = lambda w: pl.BlockSpec((tq, w), lambda b, i: (b * nblk + i, 0))
    cur = pl.BlockSpec((1, tq, KV_WIDTH), lambda b, i: (b, i, 0))
    prev = pl.BlockSpec((1, WINDOW, KV_WIDTH), lambda b, i: (b, jnp.maximum(i * per_win - 1, 0), 0))
    win = pl.BlockSpec((1, WINDOW, KV_WIDTH), lambda b, i: (b, 0, 0))
    k3 = k.reshape(batch, seq, KV_WIDTH)
    v3 = v.reshape(batch, seq, KV_WIDTH)
    return pl.pallas_call(
        functools.partial(_attn_prompt_kernel, tq=tq, unroll=unroll),
        grid=(batch, nblk),
        in_specs=[rows(ATT_WIDTH), prev, cur, prev, cur, rows(ATT_WIDTH), _const_spec(sink_rows.shape)],
        out_specs=[rows(ATT_WIDTH), win, win],
        out_shape=[jax.ShapeDtypeStruct((batch * seq, ATT_WIDTH), BF16),
                   jax.ShapeDtypeStruct((batch, WINDOW, KV_WIDTH), F32),
                   jax.ShapeDtypeStruct((batch, WINDOW, KV_WIDTH), F32)],
        scratch_shapes=[pltpu.VMEM((WINDOW + tq, KV_WIDTH), BF16),
                        pltpu.VMEM((WINDOW + tq, KV_WIDTH), BF16)],
        compiler_params=pltpu.CompilerParams(
            dimension_semantics=("arbitrary", "arbitrary"), vmem_limit_bytes=VMEM_LIMIT),
        name="attn_prompt",
    )(q, k3, k3, v3, v3, ga, sink_rows)


def _attn_sample_kernel(q_ref, kn_ref, vn_ref, ck_ref, cv_ref, ga_ref, sink_ref,
                        o_ref, wk_ref, wv_ref, *, t, bb):
    n_cache = ck_ref.shape[1]
    assert n_cache == LANES and t <= LANES
    lane = lax.broadcasted_iota(jnp.int32, (1, LANES), 1)
    lo = lane < HEAD_DIM
    is_new = lane < t
    col_group = lax.broadcasted_iota(jnp.int32, (1, KV_WIDTH), 1) // HEAD_DIM
    seg = 2 * LANES
    ones_bd = jnp.concatenate(
        [jnp.broadcast_to(jnp.where(col_group == g, 1.0, 0.0).astype(BF16), (seg, KV_WIDTH))
         for g in range(N_KV)], axis=0)
    pad = jnp.zeros((LANES - t, KV_WIDTH), BF16)
    neg_inf = jnp.float32(-jnp.inf)

    s, v_all = [], []
    for b in range(bb):
        rows = slice(b * t, (b + 1) * t)
        kn, vn, ck, cv = kn_ref[rows, :], vn_ref[rows, :], ck_ref[b], cv_ref[b]
        wk_ref[b] = jnp.concatenate([ck[t:, :], kn], axis=0)
        wv_ref[b] = jnp.concatenate([cv[t:, :], vn], axis=0)
        kcat = jnp.concatenate([ck.astype(BF16), kn.astype(BF16), pad], axis=0)
        vcat = jnp.concatenate([cv.astype(BF16), vn.astype(BF16), pad], axis=0)
        zero = jnp.zeros_like(kcat)
        k_bd = jnp.concatenate([jnp.where(col_group == g, kcat, zero) for g in range(N_KV)], axis=0)
        v_bd = jnp.concatenate([jnp.where(col_group == g, vcat, zero) for g in range(N_KV)], axis=0)
        v_all.append(jnp.concatenate([v_bd, ones_bd], axis=1))
        qs = jnp.concatenate(
            [q_ref[rows, r * KV_WIDTH:(r + 1) * KV_WIDTH] for r in range(GQA_REP)], axis=0)
        s.append(_dot_t(qs, k_bd))
    p, m = [], []
    for b in range(bb):
        c = [s[b][:, LANES * n:LANES * (n + 1)] for n in range(2 * N_KV)]
        c = [cn if n % 2 == 0 else jnp.where(is_new, cn, neg_inf) for n, cn in enumerate(c)]
        mb = [jnp.maximum(jnp.max(jnp.maximum(c[2 * g], c[2 * g + 1]), axis=-1, keepdims=True),
                          sink_ref[g]) for g in range(N_KV)]
        p.append(jnp.concatenate([jnp.exp2(c[n] - mb[n // 2]) for n in range(2 * N_KV)],
                                 axis=1).astype(BF16))
        m.append(mb)
    ov = [_dot(p[b], v_all[b]) for b in range(bb)]
    for b in range(bb):
        rows = slice(b * t, (b + 1) * t)
        es = [jnp.exp2(sink_ref[g] - m[b][g]) for g in range(N_KV)]
        e_sink = jnp.concatenate([jnp.where(lo, es[0], es[1]), jnp.where(lo, es[2], es[3])], axis=1)
        o = ov[b][:, :KV_WIDTH] / (ov[b][:, KV_WIDTH:] + e_sink)
        for r in range(GQA_REP):
            cols = slice(r * KV_WIDTH, (r + 1) * KV_WIDTH)
            gate = ga_ref[rows, cols].astype(F32)
            o_ref[rows, cols] = (o[r * t:(r + 1) * t, :] * gate).astype(BF16)


def _attn_sample(q, k, v, cache_k, cache_v, ga, sink_rows, batch, t, bb):
    n_cache = cache_k.shape[1]
    rows = lambda w: pl.BlockSpec((bb * t, w), lambda b: (b, 0))
    cache = pl.BlockSpec((bb, n_cache, KV_WIDTH), lambda b: (b, 0, 0))
    return pl.pallas_call(
        functools.partial(_attn_sample_kernel, t=t, bb=bb),
        grid=(batch // bb,),
        in_specs=[rows(ATT_WIDTH), rows(KV_WIDTH), rows(KV_WIDTH), cache, cache, rows(ATT_WIDTH),
                  _const_spec(sink_rows.shape)],
        out_specs=[rows(ATT_WIDTH), cache, cache],
        out_shape=[jax.ShapeDtypeStruct((batch * t, ATT_WIDTH), BF16),
                   jax.ShapeDtypeStruct((batch, n_cache, KV_WIDTH), F32),
                   jax.ShapeDtypeStruct((batch, n_cache, KV_WIDTH), F32)],
        compiler_params=pltpu.CompilerParams(
            dimension_semantics=("arbitrary",), vmem_limit_bytes=VMEM_LIMIT),
        name="attn_sample",
    )(q, k, v, cache_k, cache_v, ga, sink_rows)


def _gla_kernel(qg_ref, kg_ref, vg_ref, gg_ref, la_ref, s0_ref, ng_ref, tril_ref,
                o_ref, sout_ref, st_ref, *, blk, tg, unroll):
    i = pl.program_id(1)

    @pl.when(i == 0)
    def _():
        st_ref[...] = s0_ref[0]

    rowi = lax.broadcasted_iota(jnp.int32, (blk, blk), 0)
    coli = lax.broadcasted_iota(jnp.int32, (blk, blk), 1)
    causal = rowi >= coli
    eye = (lax.broadcasted_iota(jnp.int32, (GLA_DK, GLA_DK), 0)
           == lax.broadcasted_iota(jnp.int32, (GLA_DK, GLA_DK), 1))

    heads = range(N_GLA)
    kcol = [slice(h * GLA_DK, (h + 1) * GLA_DK) for h in heads]
    vcol = [slice(h * GLA_DV, (h + 1) * GLA_DV) for h in heads]

    def chunk(j, carry):
        rows = [pl.ds(pl.multiple_of((j * unroll + u) * blk, blk), blk) for u in range(unroll)]
        bcum = []
        for r in rows:
            la = la_ref[r, :]
            la_hi = la.astype(BF16)
            la_lo = (la - la_hi.astype(F32)).astype(BF16)
            bcum.append(_dot(tril_ref[...], jnp.concatenate([la_hi, la_lo], axis=0)))
        q_dec, k_inv, k_end, dec_col = {}, {}, {}, {}
        for u, r in enumerate(rows):
            for h in heads:
                b = bcum[u][:, kcol[h]]
                blast = b[blk - 1:blk, :]
                q = qg_ref[r, kcol[h]].astype(F32)
                k = kg_ref[r, kcol[h]].astype(F32)
                q_dec[u, h] = (q * jnp.exp(b)).astype(BF16)
                k_inv[u, h] = (k * jnp.exp(-b)).astype(BF16)
                k_end[u, h] = (k * jnp.exp(blast - b)).astype(BF16)
                dec_row = jnp.broadcast_to(jnp.exp(blast), (GLA_DK, GLA_DK))
                dec_col[u, h] = jnp.sum(jnp.where(eye, dec_row, 0.0), axis=-1, keepdims=True)
        att = {(u, h): _dot_t(q_dec[u, h], k_inv[u, h]) for u in range(unroll) for h in heads}
        d_st = {(u, h): _dot_ta(k_end[u, h], vg_ref[rows[u], vcol[h]])
                for u in range(unroll) for h in heads}
        att = {key: jnp.where(causal, a, 0.0).astype(BF16) for key, a in att.items()}
        st = [st_ref[h] for h in heads]
        o = {}
        for u, r in enumerate(rows):
            for h in heads:
                o[u, h] = _dot(att[u, h], vg_ref[r, vcol[h]]) + _dot(q_dec[u, h], st[h].astype(BF16))
                st[h] = st[h] * dec_col[u, h] + d_st[u, h]
        for h in heads:
            st_ref[h] = st[h]
        for u, r in enumerate(rows):
            for h in heads:
                ov = o[u, h]
                ms = jnp.mean(ov * ov, axis=-1, keepdims=True)
                on = ov * lax.rsqrt(ms + EPS) * ng_ref[...]
                gate = gg_ref[r, vcol[h]].astype(F32)
                o_ref[r, vcol[h]] = (on * gate).astype(BF16)
        return carry

    lax.fori_loop(0, tg // (blk * unroll), chunk, 0)

    @pl.when(i == pl.num_programs(1) - 1)
    def _():
        sout_ref[0] = st_ref[...]


def _gla(qg, kg, vg, gg, la, s0, norm_g, tril2, batch, seq, blk, tg, unroll=1):
    nblk = seq // tg
    rows = lambda w: pl.BlockSpec((tg, w), lambda b, i: (b * nblk + i, 0))
    state = pl.BlockSpec((1, N_GLA, GLA_DK, GLA_DV), lambda b, i: (b, 0, 0, 0))
    return pl.pallas_call(
        functools.partial(_gla_kernel, blk=blk, tg=tg, unroll=unroll),
        grid=(batch, nblk),
        in_specs=[rows(GLA_KW), rows(GLA_KW), rows(GLA_WIDTH), rows(GLA_WIDTH), rows(GLA_KW), state,
                  _const_spec(norm_g.shape), _const_spec(tril2.shape)],
        out_specs=[rows(GLA_WIDTH), state],
        out_shape=[jax.ShapeDtypeStruct((batch * seq, GLA_WIDTH), BF16),
                   jax.ShapeDtypeStruct((batch, N_GLA, GLA_DK, GLA_DV), F32)],
        scratch_shapes=[pltpu.VMEM((N_GLA, GLA_DK, GLA_DV), F32)],
        compiler_params=pltpu.CompilerParams(
            dimension_semantics=("arbitrary", "arbitrary"), vmem_limit_bytes=VMEM_LIMIT),
        name="gla",
    )(qg, kg, vg, gg, la, s0, norm_g, tril2)


def _outproj_kernel(x_ref, ma_ref, mg_ref, w_ref, y_ref):
    y_ref[...] = (x_ref[...] + _dot(ma_ref[...], w_ref[:ATT_WIDTH, :])
                  + _dot(mg_ref[...], w_ref[ATT_WIDTH:, :]))


def _outproj(x2d, mix_a, mix_g, w, tm):
    m = x2d.shape[0]
    row = lambda w: pl.BlockSpec((tm, w), lambda i: (i, 0))
    return pl.pallas_call(
        _outproj_kernel,
        grid=(m // tm,),
        in_specs=[row(D_MODEL), row(ATT_WIDTH), row(GLA_WIDTH), _const_spec(w.shape)],
        out_specs=row(D_MODEL),
        out_shape=jax.ShapeDtypeStruct((m, D_MODEL), F32),
        compiler_params=pltpu.CompilerParams(
            dimension_semantics=("arbitrary",), vmem_limit_bytes=VMEM_LIMIT),
        name="outproj",
    )(x2d, mix_a, mix_g, w)


def _tril2(blk):
    t = np.tril(np.ones((blk, blk), np.float32))
    return jnp.asarray(np.concatenate([t, t], axis=1), BF16)


def _prep_w_in_kernel(w_ref, main_ref, lr_ref):
    def piece(base, g, r):
        c0 = base + (g * GQA_REP + r) * HEAD_DIM
        return w_ref[:, c0:c0 + HEAD_DIM]

    for base in (OFF_Q, OFF_GA):
        for r in range(GQA_REP):
            for g in range(0, N_KV, 2):
                c0 = base + (r * N_KV + g) * HEAD_DIM
                pair = jnp.concatenate([piece(base, g, r), piece(base, g + 1, r)], axis=1)
                main_ref[:, c0:c0 + 2 * HEAD_DIM] = pair.astype(BF16)
    main_ref[:, OFF_K:OFF_GA] = w_ref[:, OFF_K:OFF_GA].astype(BF16)
    main_ref[:, OFF_QG:MAIN_COLS] = w_ref[:, OFF_QG:MAIN_COLS].astype(BF16)
    lr = w_ref[:, MAIN_COLS:MAIN_COLS + GATE_RANK]
    lr_ref[...] = jnp.concatenate(
        [lr, jnp.zeros((lr.shape[0], LANES - GATE_RANK), F32)], axis=1).astype(BF16)


def _prep_w_in(w_in, tr=256):
    rows, cols = w_in.shape
    return pl.pallas_call(
        _prep_w_in_kernel,
        grid=(rows // tr,),
        in_specs=[pl.BlockSpec((tr, cols), lambda i: (i, 0))],
        out_specs=[pl.BlockSpec((tr, MAIN_COLS), lambda i: (i, 0)),
                   pl.BlockSpec((tr, LANES), lambda i: (i, 0))],
        out_shape=[jax.ShapeDtypeStruct((rows, MAIN_COLS), BF16),
                   jax.ShapeDtypeStruct((rows, LANES), BF16)],
        compiler_params=pltpu.CompilerParams(
            dimension_semantics=("arbitrary",), vmem_limit_bytes=VMEM_LIMIT),
        name="prep_w_in",
    )(w_in)


def _prep_w_out_kernel(w_ref, o_ref):
    o_ref[...] = w_ref[...].astype(BF16)


def _prep_w_out(w_out):
    n_att = ATT_WIDTH // HEAD_DIM

    def src_block(j):
        regrouped = (j % N_KV) * GQA_REP + j // N_KV
        return jnp.where(j < n_att, regrouped, j)

    rows, cols = w_out.shape
    return pl.pallas_call(
        _prep_w_out_kernel,
        grid=(rows // HEAD_DIM,),
        in_specs=[pl.BlockSpec((HEAD_DIM, cols), lambda j: (src_block(j), 0))],
        out_specs=pl.BlockSpec((HEAD_DIM, cols), lambda j: (j, 0)),
        out_shape=jax.ShapeDtypeStruct((rows, cols), BF16),
        compiler_params=pltpu.CompilerParams(dimension_semantics=("arbitrary",)),
        name="prep_w_out",
    )(w_out)


def _prep(norm_g, w_in, w_gate_up, b_gate, q_norm_g, k_norm_g, sinks, gla_norm_g, w_out):
    w_main, w_lr = _prep_w_in(w_in)
    w_gu = jnp.pad(w_gate_up, ((0, LANES - GATE_RANK), (0, 0))).astype(BF16)
    g_qk = jnp.concatenate([jnp.tile(q_norm_g, N_HEADS) * (ATT_SCALE * LOG2E),
                            jnp.tile(k_norm_g, N_KV)])[None, :]

    grp = np.arange(QK_WIDTH) // HEAD_DIM
    ind = (grp[:, None] == np.arange(LANES)[None, :]).astype(np.float32)
    g_sum = jnp.asarray(ind, BF16)
    g_exp = jnp.asarray(np.concatenate([ind.T, ind.T], axis=0), BF16)

    w_out_b = _prep_w_out(w_out)

    def sink_rows(t):
        sk = sinks.astype(F32).reshape(N_KV, GQA_REP) * LOG2E
        return jnp.broadcast_to(sk[:, :, None, None], (N_KV, GQA_REP, t, LANES)).reshape(
            N_KV, GQA_REP * t, LANES)

    return dict(norm_g=norm_g[None, :], w_main=w_main, w_lr=w_lr, w_gu=w_gu, b_gate=b_gate[None, :],
                g_qk=g_qk, g_sum=g_sum, g_exp=g_exp, w_out=w_out_b,
                sink_rows=sink_rows, gla_norm_g=gla_norm_g[None, :])


def kernel(x_prompt, x_sample, cache_k, cache_v, state_gla, norm_g, w_in, w_gate_up, b_gate,
           q_norm_g, k_norm_g, sinks, gla_norm_g, w_out):
    depth = norm_g.shape[0]
    assert depth == 1
    B, S, _ = x_prompt.shape
    DB, T, _ = x_sample.shape
    n_cache = cache_k.shape[2]
    p = _prep(norm_g[0], w_in[0], w_gate_up[0], b_gate[0], q_norm_g[0], k_norm_g[0], sinks[0],
              gla_norm_g[0], w_out[0])

    xp = x_prompt.reshape(B * S, D_MODEL)
    q, k, v, ga, qg, kg, vg, gg, la = _inproj(xp, p, tm=512)
    assert n_cache == WINDOW
    mix_a, win_k_p, win_v_p = _attn_prompt(q, k, v, ga, p["sink_rows"](CHUNK), B, S, tq=512, unroll=2)
    s0 = jnp.zeros((B, N_GLA, GLA_DK, GLA_DV), F32)
    mix_g, gla_p = _gla(qg, kg, vg, gg, la, s0, p["gla_norm_g"], _tril2(CHUNK), B, S, blk=CHUNK, tg=512,
                      unroll=4)
    y_p = _outproj(xp, mix_a, mix_g, p["w_out"], tm=512).reshape(B, S, D_MODEL)

    xs = x_sample.reshape(DB * T, D_MODEL)
    q, k, v, ga, qg, kg, vg, gg, la = _inproj(xs, p, tm=DB * T)
    ck = cache_k[0].reshape(DB, n_cache, KV_WIDTH)
    cv = cache_v[0].reshape(DB, n_cache, KV_WIDTH)
    mix_a, win_k_s, win_v_s = _attn_sample(q, k, v, ck, cv, ga, p["sink_rows"](T), DB, T, bb=4)
    blk = min(CHUNK, T)
    mix_g, gla_s = _gla(qg, kg, vg, gg, la, state_gla[0], p["gla_norm_g"], _tril2(blk), DB, T,
                        blk=blk, tg=T)
    y_s = _outproj(xs, mix_a, mix_g, p["w_out"], tm=DB * T).reshape(DB, T, D_MODEL)

    shape5 = lambda a: a.reshape(1, a.shape[0], n_cache, N_KV, HEAD_DIM)
    return (y_p, y_s, shape5(win_k_p), shape5(win_v_p), gla_p[None],
            shape5(win_k_s), shape5(win_v_s), gla_s[None])
```

```python
import functools

import jax
import jax.numpy as jnp
import numpy as np
from jax import lax
from jax.experimental import pallas as pl
from jax.experimental.pallas import tpu as pltpu

D_MODEL = 2048
CHUNK = 64
WINDOW = 128
HEAD_DIM = 64
N_HEADS = 16
N_KV = 4
GQA_REP = N_HEADS // N_KV
ATT_WIDTH = N_HEADS * HEAD_DIM
KV_WIDTH = N_KV * HEAD_DIM
N_GLA = 4
GLA_DK = 128
GLA_DV = 256
GLA_KW = N_GLA * GLA_DK
GLA_WIDTH = N_GLA * GLA_DV
GATE_RANK = 16
GATE_TAU = 16.0
EPS = 1e-6
ATT_SCALE = HEAD_DIM ** -0.5
LOG2E = float(np.log2(np.e))
WIN_CHUNKS = WINDOW // CHUNK
WIN_KEYS = WINDOW + CHUNK

LANES = 128
QK_WIDTH = ATT_WIDTH + KV_WIDTH
N_QK_GROUPS = QK_WIDTH // HEAD_DIM
OFF_Q, OFF_K, OFF_V = 0, ATT_WIDTH, ATT_WIDTH + KV_WIDTH
OFF_GA = OFF_V + KV_WIDTH
OFF_QG = OFF_GA + ATT_WIDTH
OFF_KG = OFF_QG + GLA_KW
OFF_VG = OFF_KG + GLA_KW
OFF_GG = OFF_VG + GLA_WIDTH
MAIN_COLS = OFF_GG + GLA_WIDTH

VMEM_LIMIT = 56 * 1024 * 1024

BF16 = jnp.bfloat16
F32 = jnp.float32


def _dot(a, b):
    return jnp.dot(a, b, preferred_element_type=F32)


def _dot_t(a, b):
    return lax.dot_general(a, b, (((1,), (1,)), ((), ())), preferred_element_type=F32)


def _dot_ta(a, b):
    return lax.dot_general(a, b, (((0,), (0,)), ((), ())), preferred_element_type=F32)


def _silu(x):
    return x / (1.0 + jnp.exp(-x))


def _const_spec(shape):
    nd = len(shape)
    return pl.BlockSpec(shape, lambda *_: (0,) * nd, pipeline_mode=pl.Buffered(1))


def _inproj_kernel(x_ref, ng_ref, w_ref, wlr_ref, wgu_ref, bg_ref, gqk_ref, gng_ref, gsum_ref, gexp_ref,
                   q_ref, k_ref, v_ref, ga_ref, qg_ref, kg_ref, vg_ref, gg_ref, lah_ref, lal_ref,
                   *, sub):
    for s in range(x_ref.shape[0] // sub):
        rows = slice(s * sub, (s + 1) * sub)
        x = x_ref[rows, :]
        ms = jnp.mean(x * x, axis=-1, keepdims=True)
        xn = (x * lax.rsqrt(ms + EPS) * ng_ref[...]).astype(BF16)

        qk = _dot(xn, w_ref[:, OFF_Q:OFF_V])
        ga_ref[rows, :] = _silu(_dot(xn, w_ref[:, OFF_GA:OFF_QG])).astype(BF16)
        ss = _dot((qk * qk).astype(BF16), gsum_ref[...])
        lr = _dot(xn, wlr_ref[...]).astype(BF16)
        qg_ref[rows, :] = (_dot(xn, w_ref[:, OFF_QG:OFF_KG]) * (GLA_DK ** -0.5)).astype(BF16)
        kg_ref[rows, :] = _dot(xn, w_ref[:, OFF_KG:OFF_VG]).astype(BF16)
        r = lax.rsqrt(ss * (1.0 / HEAD_DIM) + EPS)
        r_hi = r.astype(BF16)
        r_lo = (r - r_hi.astype(F32)).astype(BF16)
        r_full = _dot(jnp.concatenate([r_hi, r_lo], axis=1), gexp_ref[...])
        z = _dot(lr, wgu_ref[...]) + bg_ref[...]
        gg_ref[rows, :] = (_silu(_dot(xn, w_ref[:, OFF_GG:MAIN_COLS])) * gng_ref[...]).astype(BF16)
        vg_ref[rows, :] = _dot(xn, w_ref[:, OFF_VG:OFF_GG]).astype(BF16)
        v_ref[rows, :] = _dot(xn, w_ref[:, OFF_V:OFF_GA])

        qkn = qk * r_full * gqk_ref[...]
        q_ref[rows, :] = qkn[:, :ATT_WIDTH].astype(BF16)
        k_ref[rows, :] = qkn[:, ATT_WIDTH:]
        log_sig = jnp.minimum(z, 0.0) - jnp.log(1.0 + jnp.exp(-jnp.abs(z)))
        la = log_sig * (LOG2E / GATE_TAU)
        la_hi = la.astype(BF16)
        lah_ref[rows, :] = la_hi
        lal_ref[rows, :] = (la - la_hi.astype(F32)).astype(BF16)


def _inproj(x2d, p, tm):
    m = x2d.shape[0]
    row = lambda w: pl.BlockSpec((tm, w), lambda i: (i, 0))
    outs = [(ATT_WIDTH, BF16), (KV_WIDTH, F32), (KV_WIDTH, F32), (ATT_WIDTH, BF16),
            (GLA_KW, BF16), (GLA_KW, BF16), (GLA_WIDTH, BF16), (GLA_WIDTH, BF16),
            (GLA_KW, BF16), (GLA_KW, BF16)]
    consts = [p["norm_g"], p["w_main"], p["w_lr"], p["w_gu"], p["b_gate"], p["g_qk"], p["g_gla"],
              p["g_sum"], p["g_exp"]]
    return pl.pallas_call(
        functools.partial(_inproj_kernel, sub=min(tm, 256)),
        grid=(m // tm,),
        in_specs=[row(D_MODEL)] + [_const_spec(c.shape) for c in consts],
        out_specs=[row(w) for w, _ in outs],
        out_shape=[jax.ShapeDtypeStruct((m, w), dt) for w, dt in outs],
        compiler_params=pltpu.CompilerParams(
            dimension_semantics=("arbitrary",), vmem_limit_bytes=VMEM_LIMIT),
        name="inproj",
    )(x2d, *consts)


def _attn_prompt_kernel(q_ref, kp_ref, kc_ref, vp_ref, vc_ref, ga_ref, sink_ref,
                        o_ref, wk_ref, wv_ref, kw_ref, vw_ref, *, tq, unroll):
    assert unroll >= WIN_CHUNKS and tq % (CHUNK * unroll) == 0 and tq >= WINDOW
    i = pl.program_id(1)
    kw_ref[0:WINDOW, :] = kp_ref[0].astype(BF16)
    kw_ref[WINDOW:, :] = kc_ref[0].astype(BF16)
    vw_ref[0:WINDOW, :] = vp_ref[0].astype(BF16)
    vw_ref[WINDOW:, :] = vc_ref[0].astype(BF16)

    @pl.when(i == pl.num_programs(1) - 1)
    def _():
        wk_ref[0] = kc_ref[0, tq - WINDOW:, :]
        wv_ref[0] = vc_ref[0, tq - WINDOW:, :]

    lane = lax.broadcasted_iota(jnp.int32, (1, LANES), 1)
    lo = lane < HEAD_DIM
    col_group = lax.broadcasted_iota(jnp.int32, (1, KV_WIDTH), 1) // HEAD_DIM
    col = lax.broadcasted_iota(jnp.int32, (1, N_KV * WIN_KEYS), 1)
    key_in_seg = col - WIN_KEYS * sum((col >= g * WIN_KEYS).astype(jnp.int32) for g in range(1, N_KV))
    ones_bd = jnp.concatenate(
        [jnp.broadcast_to(jnp.where(col_group == g, 1.0, 0.0).astype(BF16), (WIN_KEYS, KV_WIDTH))
         for g in range(N_KV)], axis=0)
    neg_inf = jnp.float32(-jnp.inf)

    def chunk_group(j, n_masked):
        rows = [pl.ds(pl.multiple_of((j * unroll + u) * CHUNK, CHUNK), CHUNK) for u in range(unroll)]
        wins = [pl.ds(pl.multiple_of((j * unroll + u) * CHUNK, CHUNK), WIN_KEYS) for u in range(unroll)]
        s, v_all = [], []
        for u in range(unroll):
            kwin = kw_ref[wins[u], :]
            vwin = vw_ref[wins[u], :]
            zero = jnp.zeros_like(kwin)
            k_bd = jnp.concatenate([jnp.where(col_group == g, kwin, zero) for g in range(N_KV)], axis=0)
            v_bd = jnp.concatenate([jnp.where(col_group == g, vwin, zero) for g in range(N_KV)], axis=0)
            v_all.append(jnp.concatenate([v_bd, ones_bd], axis=1))
            qs = jnp.concatenate(
                [q_ref[rows[u], r * KV_WIDTH:(r + 1) * KV_WIDTH] for r in range(GQA_REP)], axis=0)
            s.append(_dot_t(qs, k_bd))
        p, m = [], []
        for u in range(unroll):
            su = s[u]
            if u < n_masked:
                su = jnp.where(key_in_seg >= (WIN_CHUNKS - u) * CHUNK, su, neg_inf)
            c = [su[:, LANES * n:LANES * (n + 1)] for n in range(6)]
            seg_max = [
                jnp.maximum(c[0], jnp.where(lo, c[1], neg_inf)),
                jnp.maximum(jnp.where(lo, neg_inf, c[1]), c[2]),
                jnp.maximum(c[3], jnp.where(lo, c[4], neg_inf)),
                jnp.maximum(jnp.where(lo, neg_inf, c[4]), c[5]),
            ]
            mu = [jnp.maximum(jnp.max(seg_max[g], axis=-1, keepdims=True), sink_ref[g])
                  for g in range(N_KV)]
            shift = [mu[0], jnp.where(lo, mu[0], mu[1]), mu[1], mu[2], jnp.where(lo, mu[2], mu[3]), mu[3]]
            p.append(jnp.concatenate([jnp.exp2(c[n] - shift[n]) for n in range(6)], axis=1).astype(BF16))
            m.append(mu)
        ov = [_dot(p[u], v_all[u]) for u in range(unroll)]
        for u in range(unroll):
            es = [jnp.exp2(sink_ref[g] - m[u][g]) for g in range(N_KV)]
            e_sink = jnp.concatenate([jnp.where(lo, es[0], es[1]), jnp.where(lo, es[2], es[3])], axis=1)
            o = ov[u][:, :KV_WIDTH] / (ov[u][:, KV_WIDTH:] + e_sink)
            for r in range(GQA_REP):
                cols = slice(r * KV_WIDTH, (r + 1) * KV_WIDTH)
                gate = ga_ref[rows[u], cols].astype(F32)
                o_ref[rows[u], cols] = (o[r * CHUNK:(r + 1) * CHUNK, :] * gate).astype(BF16)

    n_masked = min(WIN_CHUNKS, unroll)

    @pl.when(i == 0)
    def _():
        chunk_group(0, n_masked)

    @pl.when(i != 0)
    def _():
        chunk_group(0, 0)

    def body(j, carry):
        chunk_group(j, 0)
        return carry

    lax.fori_loop(1, tq // (CHUNK * unroll), body, 0)


def _attn_prompt(q, k, v, ga, sink_rows, batch, seq, tq, unroll):
    nblk = seq // tq
    per_win = tq // WINDOW
    rows = lambda w: pl.BlockSpec((tq, w), lambda b, i: (b * nblk + i, 0))
    cur = pl.BlockSpec((1, tq, KV_WIDTH), lambda b, i: (b, i, 0))
    prev = pl.BlockSpec((1, WINDOW, KV_WIDTH), lambda b, i: (b, jnp.maximum(i * per_win - 1, 0), 0))
    win = pl.BlockSpec((1, WINDOW, KV_WIDTH), lambda b, i: (b, 0, 0))
    k3 = k.reshape(batch, seq, KV_WIDTH)
    v3 = v.reshape(batch, seq, KV_WIDTH)
    return pl.pallas_call(
        functools.partial(_attn_prompt_kernel, tq=tq, unroll=unroll),
        grid=(batch, nblk),
        in_specs=[rows(ATT_WIDTH), prev, cur, prev, cur, rows(ATT_WIDTH), _const_spec(sink_rows.shape)],
        out_specs=[rows(ATT_WIDTH), win, win],
        out_shape=[jax.ShapeDtypeStruct((batch * seq, ATT_WIDTH), BF16),
                   jax.ShapeDtypeStruct((batch, WINDOW, KV_WIDTH), F32),
                   jax.ShapeDtypeStruct((batch, WINDOW, KV_WIDTH), F32)],
        scratch_shapes=[pltpu.VMEM((WINDOW + tq, KV_WIDTH), BF16),
                        pltpu.VMEM((WINDOW + tq, KV_WIDTH), BF16)],
        compiler_params=pltpu.CompilerParams(
            dimension_semantics=("arbitrary", "arbitrary"), vmem_limit_bytes=VMEM_LIMIT),
        name="attn_prompt",
    )(q, k3, k3, v3, v3, ga, sink_rows)


def _attn_sample_kernel(q_ref, kn_ref, vn_ref, ck_ref, cv_ref, ga_ref, sink_ref,
                        o_ref, wk_ref, wv_ref, *, t, bb):
    n_cache = ck_ref.shape[1]
    assert n_cache == LANES and t <= LANES
    lane = lax.broadcasted_iota(jnp.int32, (1, LANES), 1)
    lo = lane < HEAD_DIM
    is_new = lane < t
    col_group = lax.broadcasted_iota(jnp.int32, (1, KV_WIDTH), 1) // HEAD_DIM
    seg = 2 * LANES
    ones_bd = jnp.concatenate(
        [jnp.broadcast_to(jnp.where(col_group == g, 1.0, 0.0).astype(BF16), (seg, KV_WIDTH))
         for g in range(N_KV)], axis=0)
    pad = jnp.zeros((LANES - t, KV_WIDTH), BF16)
    neg_inf = jnp.float32(-jnp.inf)

    s, v_all = [], []
    for b in range(bb):
        rows = slice(b * t, (b + 1) * t)
        kn, vn, ck, cv = kn_ref[rows, :], vn_ref[rows, :], ck_ref[b], cv_ref[b]
        wk_ref[b] = jnp.concatenate([ck[t:, :], kn], axis=0)
        wv_ref[b] = jnp.concatenate([cv[t:, :], vn], axis=0)
        kcat = jnp.concatenate([ck.astype(BF16), kn.astype(BF16), pad], axis=0)
        vcat = jnp.concatenate([cv.astype(BF16), vn.astype(BF16), pad], axis=0)
        zero = jnp.zeros_like(kcat)
        k_bd = jnp.concatenate([jnp.where(col_group == g, kcat, zero) for g in range(N_KV)], axis=0)
        v_bd = jnp.concatenate([jnp.where(col_group == g, vcat, zero) for g in range(N_KV)], axis=0)
        v_all.append(jnp.concatenate([v_bd, ones_bd], axis=1))
        qs = jnp.concatenate(
            [q_ref[rows, r * KV_WIDTH:(r + 1) * KV_WIDTH] for r in range(GQA_REP)], axis=0)
        s.append(_dot_t(qs, k_bd))
    p, m = [], []
    for b in range(bb):
        c = [s[b][:, LANES * n:LANES * (n + 1)] for n in range(2 * N_KV)]
        c = [cn if n % 2 == 0 else jnp.where(is_new, cn, neg_inf) for n, cn in enumerate(c)]
        mb = [jnp.maximum(jnp.max(jnp.maximum(c[2 * g], c[2 * g + 1]), axis=-1, keepdims=True),
                          sink_ref[g]) for g in range(N_KV)]
        p.append(jnp.concatenate([jnp.exp2(c[n] - mb[n // 2]) for n in range(2 * N_KV)],
                                 axis=1).astype(BF16))
        m.append(mb)
    ov = [_dot(p[b], v_all[b]) for b in range(bb)]
    for b in range(bb):
        rows = slice(b * t, (b + 1) * t)
        es = [jnp.exp2(sink_ref[g] - m[b][g]) for g in range(N_KV)]
        e_sink = jnp.concatenate([jnp.where(lo, es[0], es[1]), jnp.where(lo, es[2], es[3])], axis=1)
        o = ov[b][:, :KV_WIDTH] / (ov[b][:, KV_WIDTH:] + e_sink)
        for r in range(GQA_REP):
            cols = slice(r * KV_WIDTH, (r + 1) * KV_WIDTH)
            gate = ga_ref[rows, cols].astype(F32)
            o_ref[rows, cols] = (o[r * t:(r + 1) * t, :] * gate).astype(BF16)


def _attn_sample(q, k, v, cache_k, cache_v, ga, sink_rows, batch, t, bb):
    n_cache = cache_k.shape[1]
    rows = lambda w: pl.BlockSpec((bb * t, w), lambda b: (b, 0))
    cache = pl.BlockSpec((bb, n_cache, KV_WIDTH), lambda b: (b, 0, 0))
    return pl.pallas_call(
        functools.partial(_attn_sample_kernel, t=t, bb=bb),
        grid=(batch // bb,),
        in_specs=[rows(ATT_WIDTH), rows(KV_WIDTH), rows(KV_WIDTH), cache, cache, rows(ATT_WIDTH),
                  _const_spec(sink_rows.shape)],
        out_specs=[rows(ATT_WIDTH), cache, cache],
        out_shape=[jax.ShapeDtypeStruct((batch * t, ATT_WIDTH), BF16),
                   jax.ShapeDtypeStruct((batch, n_cache, KV_WIDTH), F32),
                   jax.ShapeDtypeStruct((batch, n_cache, KV_WIDTH), F32)],
        compiler_params=pltpu.CompilerParams(
            dimension_semantics=("arbitrary",), vmem_limit_bytes=VMEM_LIMIT),
        name="attn_sample",
    )(q, k, v, cache_k, cache_v, ga, sink_rows)


def _gla_kernel(qg_ref, kg_ref, vg_ref, gg_ref, lah_ref, lal_ref, s0_ref, tril_ref,
                o_ref, sout_ref, st_ref, *, blk, tg, unroll):
    i = pl.program_id(1)

    @pl.when(i == 0)
    def _():
        st_ref[...] = s0_ref[0]

    rowi = lax.broadcasted_iota(jnp.int32, (blk, blk), 0)
    coli = lax.broadcasted_iota(jnp.int32, (blk, blk), 1)
    causal = rowi >= coli
    eye = (lax.broadcasted_iota(jnp.int32, (GLA_DK, GLA_DK), 0)
           == lax.broadcasted_iota(jnp.int32, (GLA_DK, GLA_DK), 1))

    heads = range(N_GLA)
    kcol = [slice(h * GLA_DK, (h + 1) * GLA_DK) for h in heads]
    vcol = [slice(h * GLA_DV, (h + 1) * GLA_DV) for h in heads]

    def chunk(j, carry):
        rows = [pl.ds(pl.multiple_of((j * unroll + u) * blk, blk), blk) for u in range(unroll)]
        bcum = [_dot(tril_ref[...], jnp.concatenate([lah_ref[r, :], lal_ref[r, :]], axis=0))
                for r in rows]
        q_dec, k_inv, k_end, dec_col = {}, {}, {}, {}
        for u, r in enumerate(rows):
            for h in heads:
                b = bcum[u][:, kcol[h]]
                blast = b[blk - 1:blk, :]
                q = qg_ref[r, kcol[h]].astype(F32)
                k = kg_ref[r, kcol[h]].astype(F32)
                decay = jnp.exp2(b)
                q_dec[u, h] = (q * decay).astype(BF16)
                k_inv[u, h] = (k * (1.0 / decay)).astype(BF16)
                k_end[u, h] = (k * jnp.exp2(blast - b)).astype(BF16)
                dec_row = jnp.broadcast_to(jnp.exp2(blast), (GLA_DK, GLA_DK))
                dec_col[u, h] = jnp.sum(jnp.where(eye, dec_row, 0.0), axis=-1, keepdims=True)
        att = {(u, h): _dot_t(q_dec[u, h], k_inv[u, h]) for u in range(unroll) for h in heads}
        d_st = {(u, h): _dot_ta(k_end[u, h], vg_ref[rows[u], vcol[h]])
                for u in range(unroll) for h in heads}
        att = {key: jnp.where(causal, a, 0.0).astype(BF16) for key, a in att.items()}
        st = [st_ref[h] for h in heads]
        o = {}
        for u, r in enumerate(rows):
            for h in heads:
                o[u, h] = _dot(att[u, h], vg_ref[r, vcol[h]]) + _dot(q_dec[u, h], st[h].astype(BF16))
                st[h] = st[h] * dec_col[u, h] + d_st[u, h]
        for h in heads:
            st_ref[h] = st[h]
        for u, r in enumerate(rows):
            for h in heads:
                ov = o[u, h]
                ms = jnp.mean(ov * ov, axis=-1, keepdims=True)
                gate = gg_ref[r, vcol[h]].astype(F32)
                o_ref[r, vcol[h]] = (ov * lax.rsqrt(ms + EPS) * gate).astype(BF16)
        return carry

    lax.fori_loop(0, tg // (blk * unroll), chunk, 0)

    @pl.when(i == pl.num_programs(1) - 1)
    def _():
        sout_ref[0] = st_ref[...]


def _gla(qg, kg, vg, gg, la_hi, la_lo, s0, tril2, batch, seq, blk, tg, unroll=1):
    nblk = seq // tg
    rows = lambda w: pl.BlockSpec((tg, w), lambda b, i: (b * nblk + i, 0))
    state = pl.BlockSpec((1, N_GLA, GLA_DK, GLA_DV), lambda b, i: (b, 0, 0, 0))
    return pl.pallas_call(
        functools.partial(_gla_kernel, blk=blk, tg=tg, unroll=unroll),
        grid=(batch, nblk),
        in_specs=[rows(GLA_KW), rows(GLA_KW), rows(GLA_WIDTH), rows(GLA_WIDTH), rows(GLA_KW),
                  rows(GLA_KW), state, _const_spec(tril2.shape)],
        out_specs=[rows(GLA_WIDTH), state],
        out_shape=[jax.ShapeDtypeStruct((batch * seq, GLA_WIDTH), BF16),
                   jax.ShapeDtypeStruct((batch, N_GLA, GLA_DK, GLA_DV), F32)],
        scratch_shapes=[pltpu.VMEM((N_GLA, GLA_DK, GLA_DV), F32)],
        compiler_params=pltpu.CompilerParams(
            dimension_semantics=("arbitrary", "arbitrary"), vmem_limit_bytes=VMEM_LIMIT),
        name="gla",
    )(qg, kg, vg, gg, la_hi, la_lo, s0, tril2)


def _outproj_kernel(x_ref, ma_ref, mg_ref, w_ref, y_ref):
    y_ref[...] = (x_ref[...] + _dot(ma_ref[...], w_ref[:ATT_WIDTH, :])
                  + _dot(mg_ref[...], w_ref[ATT_WIDTH:, :]))


def _outproj(x2d, mix_a, mix_g, w, tm):
    m = x2d.shape[0]
    row = lambda w: pl.BlockSpec((tm, w), lambda i: (i, 0))
    return pl.pallas_call(
        _outproj_kernel,
        grid=(m // tm,),
        in_specs=[row(D_MODEL), row(ATT_WIDTH), row(GLA_WIDTH), _const_spec(w.shape)],
        out_specs=row(D_MODEL),
        out_shape=jax.ShapeDtypeStruct((m, D_MODEL), F32),
        compiler_params=pltpu.CompilerParams(
            dimension_semantics=("arbitrary",), vmem_limit_bytes=VMEM_LIMIT),
        name="outproj",
    )(x2d, mix_a, mix_g, w)


def _tril2(blk):
    t = np.tril(np.ones((blk, blk), np.float32))
    return jnp.asarray(np.concatenate([t, t], axis=1), BF16)


def _prep_w_in_kernel(u0_ref, u1_ref, u2_ref, u3_ref, lr_ref, main_ref, wlr_ref):
    units = jnp.concatenate([u0_ref[...], u1_ref[...], u2_ref[...], u3_ref[...]], axis=0)
    main_ref[...] = units.T.astype(BF16)

    @pl.when(pl.program_id(0) == 0)
    def _():
        lr = lr_ref[...]
        padded = jnp.concatenate([lr, jnp.zeros((LANES - lr.shape[0], lr.shape[1]), F32)], axis=0)
        wlr_ref[...] = padded.T.astype(BF16)


def _prep_w_in(w_in_t):
    n_cols, k = w_in_t.shape
    units_per_block = KV_WIDTH // HEAD_DIM

    def src_unit(j, p):
        u = j * units_per_block + p
        def regroup(v):
            return (v % N_KV) * GQA_REP + v // N_KV
        q0, ga0 = OFF_Q // HEAD_DIM, OFF_GA // HEAD_DIM
        n_att = ATT_WIDTH // HEAD_DIM
        u = jnp.where((u >= q0) & (u < q0 + n_att), q0 + regroup(u - q0), u)
        return jnp.where((u >= ga0) & (u < ga0 + n_att), ga0 + regroup(u - ga0), u)

    unit = lambda p: pl.BlockSpec((HEAD_DIM, k), lambda j: (src_unit(j, p), 0))
    return pl.pallas_call(
        _prep_w_in_kernel,
        grid=(MAIN_COLS // KV_WIDTH,),
        in_specs=[unit(0), unit(1), unit(2), unit(3),
                  pl.BlockSpec((GATE_RANK, k), lambda j: (MAIN_COLS // GATE_RANK, 0))],
        out_specs=[pl.BlockSpec((k, KV_WIDTH), lambda j: (0, j)),
                   pl.BlockSpec((k, LANES), lambda j: (0, 0))],
        out_shape=[jax.ShapeDtypeStruct((k, MAIN_COLS), BF16),
                   jax.ShapeDtypeStruct((k, LANES), BF16)],
        compiler_params=pltpu.CompilerParams(
            dimension_semantics=("arbitrary",), vmem_limit_bytes=VMEM_LIMIT),
        name="prep_w_in",
    )(w_in_t, w_in_t, w_in_t, w_in_t, w_in_t)


def _prep_w_out_kernel(w_ref, o_ref):
    o_ref[...] = w_ref[...].astype(BF16)


def _prep_w_out(w_out):
    n_att = ATT_WIDTH // HEAD_DIM

    def src_block(j):
        regrouped = (j % N_KV) * GQA_REP + j // N_KV
        return jnp.where(j < n_att, regrouped, j)

    rows, cols = w_out.shape
    return pl.pallas_call(
        _prep_w_out_kernel,
        grid=(rows // HEAD_DIM,),
        in_specs=[pl.BlockSpec((HEAD_DIM, cols), lambda j: (src_block(j), 0))],
        out_specs=pl.BlockSpec((HEAD_DIM, cols), lambda j: (j, 0)),
        out_shape=jax.ShapeDtypeStruct((rows, cols), BF16),
        compiler_params=pltpu.CompilerParams(dimension_semantics=("arbitrary",)),
        name="prep_w_out",
    )(w_out)


def _prep(norm_g, w_in, w_gate_up, b_gate, q_norm_g, k_norm_g, sinks, gla_norm_g, w_out):
    w_main, w_lr = _prep_w_in(w_in.T)
    w_gu = jnp.pad(w_gate_up, ((0, LANES - GATE_RANK), (0, 0))).astype(BF16)
    g_qk = jnp.concatenate([jnp.tile(q_norm_g, N_HEADS) * (ATT_SCALE * LOG2E),
                            jnp.tile(k_norm_g, N_KV)])[None, :]

    grp = np.arange(QK_WIDTH) // HEAD_DIM
    ind = (grp[:, None] == np.arange(LANES)[None, :]).astype(np.float32)
    g_sum = jnp.asarray(ind, BF16)
    g_exp = jnp.asarray(np.concatenate([ind.T, ind.T], axis=0), BF16)

    w_out_b = _prep_w_out(w_out)

    def sink_rows(t):
        sk = sinks.astype(F32).reshape(N_KV, GQA_REP) * LOG2E
        return jnp.broadcast_to(sk[:, :, None, None], (N_KV, GQA_REP, t, LANES)).reshape(
            N_KV, GQA_REP * t, LANES)

    return dict(norm_g=norm_g[None, :], w_main=w_main, w_lr=w_lr, w_gu=w_gu, b_gate=b_gate[None, :],
                g_qk=g_qk, g_gla=jnp.tile(gla_norm_g, N_GLA)[None, :], g_sum=g_sum, g_exp=g_exp,
                w_out=w_out_b, sink_rows=sink_rows)


def kernel(x_prompt, x_sample, cache_k, cache_v, state_gla, norm_g, w_in, w_gate_up, b_gate,
           q_norm_g, k_norm_g, sinks, gla_norm_g, w_out):
    depth = norm_g.shape[0]
    assert depth == 1
    B, S, _ = x_prompt.shape
    DB, T, _ = x_sample.shape
    n_cache = cache_k.shape[2]
    p = _prep(norm_g[0], w_in[0], w_gate_up[0], b_gate[0], q_norm_g[0], k_norm_g[0], sinks[0],
              gla_norm_g[0], w_out[0])

    xp = x_prompt.reshape(B * S, D_MODEL)
    q, k, v, ga, qg, kg, vg, gg, la_hi, la_lo = _inproj(xp, p, tm=512)
    assert n_cache == WINDOW
    mix_a, win_k_p, win_v_p = _attn_prompt(q, k, v, ga, p["sink_rows"](CHUNK), B, S, tq=512, unroll=2)
    s0 = jnp.zeros((B, N_GLA, GLA_DK, GLA_DV), F32)
    mix_g, gla_p = _gla(qg, kg, vg, gg, la_hi, la_lo, s0, _tril2(CHUNK), B, S, blk=CHUNK, tg=512,
                      unroll=4)
    y_p = _outproj(xp, mix_a, mix_g, p["w_out"], tm=512).reshape(B, S, D_MODEL)

    xs = x_sample.reshape(DB * T, D_MODEL)
    q, k, v, ga, qg, kg, vg, gg, la_hi, la_lo = _inproj(xs, p, tm=DB * T)
    ck = cache_k[0].reshape(DB, n_cache, KV_WIDTH)
    cv = cache_v[0].reshape(DB, n_cache, KV_WIDTH)
    mix_a, win_k_s, win_v_s = _attn_sample(q, k, v, ck, cv, ga, p["sink_rows"](T), DB, T, bb=4)
    blk = min(CHUNK, T)
    mix_g, gla_s = _gla(qg, kg, vg, gg, la_hi, la_lo, state_gla[0], _tril2(blk), DB, T,
                        blk=blk, tg=T)
    y_s = _outproj(xs, mix_a, mix_g, p["w_out"], tm=DB * T).reshape(DB, T, D_MODEL)

    shape5 = lambda a: a.reshape(1, a.shape[0], n_cache, N_KV, HEAD_DIM)
    return (y_p, y_s, shape5(win_k_p), shape5(win_v_p), gla_p[None],
            shape5(win_k_s), shape5(win_v_s), gla_s[None])
```

```python
import functools

import jax
import jax.numpy as jnp
import numpy as np
from jax import lax
from jax.experimental import pallas as pl
from jax.experimental.pallas import tpu as pltpu

D_MODEL = 2048
CHUNK = 64
WINDOW = 128
HEAD_DIM = 64
N_HEADS = 16
N_KV = 4
GQA_REP = N_HEADS // N_KV
ATT_WIDTH = N_HEADS * HEAD_DIM
KV_WIDTH = N_KV * HEAD_DIM
N_GLA = 4
GLA_DK = 128
GLA_DV = 256
GLA_KW = N_GLA * GLA_DK
GLA_WIDTH = N_GLA * GLA_DV
GATE_RANK = 16
GATE_TAU = 16.0
EPS = 1e-6
ATT_SCALE = HEAD_DIM ** -0.5
LOG2E = float(np.log2(np.e))
LOG2_DECAY_LIMIT = 100.0
WIN_CHUNKS = WINDOW // CHUNK
WIN_KEYS = WINDOW + CHUNK

LANES = 128
QK_WIDTH = ATT_WIDTH + KV_WIDTH
N_QK_GROUPS = QK_WIDTH // HEAD_DIM
OFF_Q, OFF_K, OFF_V = 0, ATT_WIDTH, ATT_WIDTH + KV_WIDTH
OFF_GA = OFF_V + KV_WIDTH
OFF_QG = OFF_GA + ATT_WIDTH
OFF_KG = OFF_QG + GLA_KW
OFF_VG = OFF_KG + GLA_KW
OFF_GG = OFF_VG + GLA_WIDTH
MAIN_COLS = OFF_GG + GLA_WIDTH

VMEM_LIMIT = 56 * 1024 * 1024

BF16 = jnp.bfloat16
F32 = jnp.float32


def _dot(a, b):
    return jnp.dot(a, b, preferred_element_type=F32)


def _dot_t(a, b):
    return lax.dot_general(a, b, (((1,), (1,)), ((), ())), preferred_element_type=F32)


def _dot_ta(a, b):
    return lax.dot_general(a, b, (((0,), (0,)), ((), ())), preferred_element_type=F32)


def _silu(x):
    return x / (1.0 + jnp.exp(-x))


def _const_spec(shape):
    nd = len(shape)
    return pl.BlockSpec(shape, lambda *_: (0,) * nd, pipeline_mode=pl.Buffered(1))


def _inproj_kernel(x_ref, ng_ref, w_ref, wlr_ref, wgu_ref, bg_ref, gqk_ref, gng_ref, gsum_ref, gexp_ref,
                   q_ref, k_ref, v_ref, ga_ref, qg_ref, kg_ref, vg_ref, gg_ref, lah_ref, lal_ref,
                   *, sub):
    for s in range(x_ref.shape[0] // sub):
        rows = slice(s * sub, (s + 1) * sub)
        x = x_ref[rows, :]
        ms = jnp.mean(x * x, axis=-1, keepdims=True)
        xn = (x * lax.rsqrt(ms + EPS) * ng_ref[...]).astype(BF16)

        qk = _dot(xn, w_ref[:, OFF_Q:OFF_V])
        ga_ref[rows, :] = _silu(_dot(xn, w_ref[:, OFF_GA:OFF_QG])).astype(BF16)
        ss = _dot((qk * qk).astype(BF16), gsum_ref[...])
        lr = _dot(xn, wlr_ref[...]).astype(BF16)
        qg_ref[rows, :] = (_dot(xn, w_ref[:, OFF_QG:OFF_KG]) * (GLA_DK ** -0.5)).astype(BF16)
        kg_ref[rows, :] = _dot(xn, w_ref[:, OFF_KG:OFF_VG]).astype(BF16)
        r = lax.rsqrt(ss * (1.0 / HEAD_DIM) + EPS)
        r_hi = r.astype(BF16)
        r_lo = (r - r_hi.astype(F32)).astype(BF16)
        r_full = _dot(jnp.concatenate([r_hi, r_lo], axis=1), gexp_ref[...])
        z = _dot(lr, wgu_ref[...]) + bg_ref[...]
        gg_ref[rows, :] = (_silu(_dot(xn, w_ref[:, OFF_GG:MAIN_COLS])) * gng_ref[...]).astype(BF16)
        vg_ref[rows, :] = _dot(xn, w_ref[:, OFF_VG:OFF_GG]).astype(BF16)
        v_ref[rows, :] = _dot(xn, w_ref[:, OFF_V:OFF_GA])

        qkn = qk * r_full * gqk_ref[...]
        q_ref[rows, :] = qkn[:, :ATT_WIDTH].astype(BF16)
        k_ref[rows, :] = qkn[:, ATT_WIDTH:]
        log_sig = jnp.minimum(z, 0.0) - jnp.log(1.0 + jnp.exp(-jnp.abs(z)))
        la = log_sig * (LOG2E / GATE_TAU)
        la_hi = la.astype(BF16)
        lah_ref[rows, :] = la_hi
        lal_ref[rows, :] = (la - la_hi.astype(F32)).astype(BF16)


def _inproj(x2d, p, tm):
    m = x2d.shape[0]
    row = lambda w: pl.BlockSpec((tm, w), lambda i: (i, 0))
    outs = [(ATT_WIDTH, BF16), (KV_WIDTH, F32), (KV_WIDTH, F32), (ATT_WIDTH, BF16),
            (GLA_KW, BF16), (GLA_KW, BF16), (GLA_WIDTH, BF16), (GLA_WIDTH, BF16),
            (GLA_KW, BF16), (GLA_KW, BF16)]
    consts = [p["norm_g"], p["w_main"], p["w_lr"], p["w_gu"], p["b_gate"], p["g_qk"], p["g_gla"],
              p["g_sum"], p["g_exp"]]
    return pl.pallas_call(
        functools.partial(_inproj_kernel, sub=min(tm, 256)),
        grid=(m // tm,),
        in_specs=[row(D_MODEL)] + [_const_spec(c.shape) for c in consts],
        out_specs=[row(w) for w, _ in outs],
        out_shape=[jax.ShapeDtypeStruct((m, w), dt) for w, dt in outs],
        compiler_params=pltpu.CompilerParams(
            dimension_semantics=("arbitrary",), vmem_limit_bytes=VMEM_LIMIT),
        name="inproj",
    )(x2d, *consts)


def _attn_prompt_kernel(q_ref, kp_ref, kc_ref, vp_ref, vc_ref, ga_ref, sink_ref,
                        o_ref, wk_ref, wv_ref, kw_ref, vw_ref, *, tq, unroll):
    assert unroll >= WIN_CHUNKS and tq % (CHUNK * unroll) == 0 and tq >= WINDOW
    i = pl.program_id(1)
    kw_ref[0:WINDOW, :] = kp_ref[0].astype(BF16)
    kw_ref[WINDOW:, :] = kc_ref[0].astype(BF16)
    vw_ref[0:WINDOW, :] = vp_ref[0].astype(BF16)
    vw_ref[WINDOW:, :] = vc_ref[0].astype(BF16)

    @pl.when(i == pl.num_programs(1) - 1)
    def _():
        wk_ref[0] = kc_ref[0, tq - WINDOW:, :]
        wv_ref[0] = vc_ref[0, tq - WINDOW:, :]

    lane = lax.broadcasted_iota(jnp.int32, (1, LANES), 1)
    lo = lane < HEAD_DIM
    col_group = lax.broadcasted_iota(jnp.int32, (1, KV_WIDTH), 1) // HEAD_DIM
    col = lax.broadcasted_iota(jnp.int32, (1, N_KV * WIN_KEYS), 1)
    key_in_seg = col - WIN_KEYS * sum((col >= g * WIN_KEYS).astype(jnp.int32) for g in range(1, N_KV))
    ones_bd = jnp.concatenate(
        [jnp.broadcast_to(jnp.where(col_group == g, 1.0, 0.0).astype(BF16), (WIN_KEYS, KV_WIDTH))
         for g in range(N_KV)], axis=0)
    neg_inf = jnp.float32(-jnp.inf)

    def chunk_group(j, n_masked):
        rows = [pl.ds(pl.multiple_of((j * unroll + u) * CHUNK, CHUNK), CHUNK) for u in range(unroll)]
        wins = [pl.ds(pl.multiple_of((j * unroll + u) * CHUNK, CHUNK), WIN_KEYS) for u in range(unroll)]
        s, v_all = [], []
        for u in range(unroll):
            kwin = kw_ref[wins[u], :]
            vwin = vw_ref[wins[u], :]
            zero = jnp.zeros_like(kwin)
            k_bd = jnp.concatenate([jnp.where(col_group == g, kwin, zero) for g in range(N_KV)], axis=0)
            v_bd = jnp.concatenate([jnp.where(col_group == g, vwin, zero) for g in range(N_KV)], axis=0)
            v_all.append(jnp.concatenate([v_bd, ones_bd], axis=1))
            qs = jnp.concatenate(
                [q_ref[rows[u], r * KV_WIDTH:(r + 1) * KV_WIDTH] for r in range(GQA_REP)], axis=0)
            s.append(_dot_t(qs, k_bd))
        p, m = [], []
        for u in range(unroll):
            su = s[u]
            if u < n_masked:
                su = jnp.where(key_in_seg >= (WIN_CHUNKS - u) * CHUNK, su, neg_inf)
            c = [su[:, LANES * n:LANES * (n + 1)] for n in range(6)]
            seg_max = [
                jnp.maximum(c[0], jnp.where(lo, c[1], neg_inf)),
                jnp.maximum(jnp.where(lo, neg_inf, c[1]), c[2]),
                jnp.maximum(c[3], jnp.where(lo, c[4], neg_inf)),
                jnp.maximum(jnp.where(lo, neg_inf, c[4]), c[5]),
            ]
            mu = [jnp.maximum(jnp.max(seg_max[g], axis=-1, keepdims=True), sink_ref[g])
                  for g in range(N_KV)]
            shift = [mu[0], jnp.where(lo, mu[0], mu[1]), mu[1], mu[2], jnp.where(lo, mu[2], mu[3]), mu[3]]
            p.append(jnp.concatenate([jnp.exp2(c[n] - shift[n]) for n in range(6)], axis=1).astype(BF16))
            m.append(mu)
        ov = [_dot(p[u], v_all[u]) for u in range(unroll)]
        for u in range(unroll):
            es = [jnp.exp2(sink_ref[g] - m[u][g]) for g in range(N_KV)]
            e_sink = jnp.concatenate([jnp.where(lo, es[0], es[1]), jnp.where(lo, es[2], es[3])], axis=1)
            o = ov[u][:, :KV_WIDTH] / (ov[u][:, KV_WIDTH:] + e_sink)
            for r in range(GQA_REP):
                cols = slice(r * KV_WIDTH, (r + 1) * KV_WIDTH)
                gate = ga_ref[rows[u], cols].astype(F32)
                o_ref[rows[u], cols] = (o[r * CHUNK:(r + 1) * CHUNK, :] * gate).astype(BF16)

    n_masked = min(WIN_CHUNKS, unroll)

    @pl.when(i == 0)
    def _():
        chunk_group(0, n_masked)

    @pl.when(i != 0)
    def _():
        chunk_group(0, 0)

    def body(j, carry):
        chunk_group(j, 0)
        return carry

    lax.fori_loop(1, tq // (CHUNK * unroll), body, 0)


def _attn_prompt(q, k, v, ga, sink_rows, batch, seq, tq, unroll):
    nblk = seq // tq
    per_win = tq // WINDOW
    rows = lambda w: pl.BlockSpec((tq, w), lambda b, i: (b * nblk + i, 0))
    cur = pl.BlockSpec((1, tq, KV_WIDTH), lambda b, i: (b, i, 0))
    prev = pl.BlockSpec((1, WINDOW, KV_WIDTH), lambda b, i: (b, jnp.maximum(i * per_win - 1, 0), 0))
    win = pl.BlockSpec((1, WINDOW, KV_WIDTH), lambda b, i: (b, 0, 0))
    k3 = k.reshape(batch, seq, KV_WIDTH)
    v3 = v.reshape(batch, seq, KV_WIDTH)
    return pl.pallas_call(
        functools.partial(_attn_prompt_kernel, tq=tq, unroll=unroll),
        grid=(batch, nblk),
        in_specs=[rows(ATT_WIDTH), prev, cur, prev, cur, rows(ATT_WIDTH), _const_spec(sink_rows.shape)],
        out_specs=[rows(ATT_WIDTH), win, win],
        out_shape=[jax.ShapeDtypeStruct((batch * seq, ATT_WIDTH), BF16),
                   jax.ShapeDtypeStruct((batch, WINDOW, KV_WIDTH), F32),
                   jax.ShapeDtypeStruct((batch, WINDOW, KV_WIDTH), F32)],
        scratch_shapes=[pltpu.VMEM((WINDOW + tq, KV_WIDTH), BF16),
                        pltpu.VMEM((WINDOW + tq, KV_WIDTH), BF16)],
        compiler_params=pltpu.CompilerParams(
            dimension_semantics=("arbitrary", "arbitrary"), vmem_limit_bytes=VMEM_LIMIT),
        name="attn_prompt",
    )(q, k3, k3, v3, v3, ga, sink_rows)


def _attn_sample_kernel(q_ref, kn_ref, vn_ref, ck_ref, cv_ref, ga_ref, sink_ref,
                        o_ref, wk_ref, wv_ref, *, t, bb):
    n_cache = ck_ref.shape[1]
    assert n_cache == LANES and t <= LANES
    lane = lax.broadcasted_iota(jnp.int32, (1, LANES), 1)
    lo = lane < HEAD_DIM
    is_new = lane < t
    col_group = lax.broadcasted_iota(jnp.int32, (1, KV_WIDTH), 1) // HEAD_DIM
    seg = 2 * LANES
    ones_bd = jnp.concatenate(
        [jnp.broadcast_to(jnp.where(col_group == g, 1.0, 0.0).astype(BF16), (seg, KV_WIDTH))
         for g in range(N_KV)], axis=0)
    pad = jnp.zeros((LANES - t, KV_WIDTH), BF16)
    neg_inf = jnp.float32(-jnp.inf)

    s, v_all = [], []
    for b in range(bb):
        rows = slice(b * t, (b + 1) * t)
        kn, vn, ck, cv = kn_ref[rows, :], vn_ref[rows, :], ck_ref[b], cv_ref[b]
        wk_ref[b] = jnp.concatenate([ck[t:, :], kn], axis=0)
        wv_ref[b] = jnp.concatenate([cv[t:, :], vn], axis=0)
        kcat = jnp.concatenate([ck.astype(BF16), kn.astype(BF16), pad], axis=0)
        vcat = jnp.concatenate([cv.astype(BF16), vn.astype(BF16), pad], axis=0)
        zero = jnp.zeros_like(kcat)
        k_bd = jnp.concatenate([jnp.where(col_group == g, kcat, zero) for g in range(N_KV)], axis=0)
        v_bd = jnp.concatenate([jnp.where(col_group == g, vcat, zero) for g in range(N_KV)], axis=0)
        v_all.append(jnp.concatenate([v_bd, ones_bd], axis=1))
        qs = jnp.concatenate(
            [q_ref[rows, r * KV_WIDTH:(r + 1) * KV_WIDTH] for r in range(GQA_REP)], axis=0)
        s.append(_dot_t(qs, k_bd))
    p, m = [], []
    for b in range(bb):
        c = [s[b][:, LANES * n:LANES * (n + 1)] for n in range(2 * N_KV)]
        c = [cn if n % 2 == 0 else jnp.where(is_new, cn, neg_inf) for n, cn in enumerate(c)]
        mb = [jnp.maximum(jnp.max(jnp.maximum(c[2 * g], c[2 * g + 1]), axis=-1, keepdims=True),
                          sink_ref[g]) for g in range(N_KV)]
        p.append(jnp.concatenate([jnp.exp2(c[n] - mb[n // 2]) for n in range(2 * N_KV)],
                                 axis=1).astype(BF16))
        m.append(mb)
    ov = [_dot(p[b], v_all[b]) for b in range(bb)]
    for b in range(bb):
        rows = slice(b * t, (b + 1) * t)
        es = [jnp.exp2(sink_ref[g] - m[b][g]) for g in range(N_KV)]
        e_sink = jnp.concatenate([jnp.where(lo, es[0], es[1]), jnp.where(lo, es[2], es[3])], axis=1)
        o = ov[b][:, :KV_WIDTH] / (ov[b][:, KV_WIDTH:] + e_sink)
        for r in range(GQA_REP):
            cols = slice(r * KV_WIDTH, (r + 1) * KV_WIDTH)
            gate = ga_ref[rows, cols].astype(F32)
            o_ref[rows, cols] = (o[r * t:(r + 1) * t, :] * gate).astype(BF16)


def _attn_sample(q, k, v, cache_k, cache_v, ga, sink_rows, batch, t, bb):
    n_cache = cache_k.shape[1]
    rows = lambda w: pl.BlockSpec((bb * t, w), lambda b: (b, 0))
    cache = pl.BlockSpec((bb, n_cache, KV_WIDTH), lambda b: (b, 0, 0))
    return pl.pallas_call(
        functools.partial(_attn_sample_kernel, t=t, bb=bb),
        grid=(batch // bb,),
        in_specs=[rows(ATT_WIDTH), rows(KV_WIDTH), rows(KV_WIDTH), cache, cache, rows(ATT_WIDTH),
                  _const_spec(sink_rows.shape)],
        out_specs=[rows(ATT_WIDTH), cache, cache],
        out_shape=[jax.ShapeDtypeStruct((batch * t, ATT_WIDTH), BF16),
                   jax.ShapeDtypeStruct((batch, n_cache, KV_WIDTH), F32),
                   jax.ShapeDtypeStruct((batch, n_cache, KV_WIDTH), F32)],
        compiler_params=pltpu.CompilerParams(
            dimension_semantics=("arbitrary",), vmem_limit_bytes=VMEM_LIMIT),
        name="attn_sample",
    )(q, k, v, cache_k, cache_v, ga, sink_rows)


def _gla_kernel(qg_ref, kg_ref, vg_ref, gg_ref, lah_ref, lal_ref, s0_ref, tril_ref,
                o_ref, sout_ref, st_ref, b_ref, q_ref, *, blk, tg, unroll):
    i = pl.program_id(1)
    n_groups = tg // (blk * unroll)

    @pl.when(i == 0)
    def _():
        st_ref[0] = s0_ref[0]

    rowi = lax.broadcasted_iota(jnp.int32, (blk, blk), 0)
    coli = lax.broadcasted_iota(jnp.int32, (blk, blk), 1)
    causal = rowi >= coli
    eye = (lax.broadcasted_iota(jnp.int32, (GLA_DK, GLA_DK), 0)
           == lax.broadcasted_iota(jnp.int32, (GLA_DK, GLA_DK), 1))

    heads = range(N_GLA)
    kcol = [slice(h * GLA_DK, (h + 1) * GLA_DK) for h in heads]
    vcol = [slice(h * GLA_DV, (h + 1) * GLA_DV) for h in heads]

    def cum_log2_decay(r):
        return _dot(tril_ref[...], jnp.concatenate([lah_ref[r, :], lal_ref[r, :]], axis=0))

    def decay_column(blast):
        dec_row = jnp.broadcast_to(jnp.exp2(blast), (GLA_DK, GLA_DK))
        return jnp.sum(jnp.where(eye, dec_row, 0.0), axis=-1, keepdims=True)

    def finish(ov, r, h):
        ms = jnp.mean(ov * ov, axis=-1, keepdims=True)
        gate = gg_ref[r, vcol[h]].astype(F32)
        o_ref[r, vcol[h]] = (ov * lax.rsqrt(ms + EPS) * gate).astype(BF16)

    def exact_group(rows, cur, nxt):
        st = [st_ref[cur, h] for h in heads]
        for r in rows:
            bcum = cum_log2_decay(r)
            for h in heads:
                b = bcum[:, kcol[h]]
                blast = b[blk - 1:blk, :]
                q = qg_ref[r, kcol[h]].astype(F32)
                k = kg_ref[r, kcol[h]].astype(F32)
                v = vg_ref[r, vcol[h]]
                b_ref[0:blk, :] = b
                q_ref[0:blk, :] = q

                def score_row(t, att_t):
                    w = jnp.where(rowi[:, 0:1] <= t, jnp.exp2(b_ref[pl.ds(t, 1), :] - b), 0.0)
                    col = jnp.sum(q_ref[pl.ds(t, 1), :] * k * w, axis=-1, keepdims=True)
                    return att_t + jnp.where(coli == t, col, 0.0)

                att_t = lax.fori_loop(0, blk, score_row, jnp.zeros((blk, blk), F32))
                q_dec = (q * jnp.exp2(b)).astype(BF16)
                k_end = (k * jnp.exp2(blast - b)).astype(BF16)
                ov = _dot_ta(att_t.astype(BF16), v) + _dot(q_dec, st[h].astype(BF16))
                st[h] = st[h] * decay_column(blast) + _dot_ta(k_end, v)
                finish(ov, r, h)
        for h in heads:
            st_ref[nxt, h] = st[h]

    def group(j, carry):
        rows = [pl.ds(pl.multiple_of((j * unroll + u) * blk, blk), blk) for u in range(unroll)]
        cur = (i * n_groups + j) % 2
        nxt = 1 - cur
        bcum = [cum_log2_decay(r) for r in rows]
        q_dec, k_inv, k_end, dec_col = {}, {}, {}, {}
        for u, r in enumerate(rows):
            for h in heads:
                b = bcum[u][:, kcol[h]]
                blast = b[blk - 1:blk, :]
                q = qg_ref[r, kcol[h]].astype(F32)
                k = kg_ref[r, kcol[h]].astype(F32)
                decay = jnp.exp2(b)
                q_dec[u, h] = (q * decay).astype(BF16)
                k_inv[u, h] = (k * (1.0 / decay)).astype(BF16)
                k_end[u, h] = (k * jnp.exp2(blast - b)).astype(BF16)
                dec_col[u, h] = decay_column(blast)
        att = {(u, h): _dot_t(q_dec[u, h], k_inv[u, h]) for u in range(unroll) for h in heads}
        d_st = {(u, h): _dot_ta(k_end[u, h], vg_ref[rows[u], vcol[h]])
                for u in range(unroll) for h in heads}
        att = {key: jnp.where(causal, a, 0.0).astype(BF16) for key, a in att.items()}
        st = [st_ref[cur, h] for h in heads]
        o = {}
        for u, r in enumerate(rows):
            for h in heads:
                o[u, h] = _dot(att[u, h], vg_ref[r, vcol[h]]) + _dot(q_dec[u, h], st[h].astype(BF16))
                st[h] = st[h] * dec_col[u, h] + d_st[u, h]
        for h in heads:
            st_ref[nxt, h] = st[h]
        for u, r in enumerate(rows):
            for h in heads:
                finish(o[u, h], r, h)

        total = functools.reduce(jnp.minimum, [bc[blk - 1:blk, :] for bc in bcum])

        @pl.when(jnp.min(total) < -LOG2_DECAY_LIMIT)
        def _():
            exact_group(rows, cur, nxt)
        return carry

    lax.fori_loop(0, n_groups, group, 0)

    @pl.when(i == pl.num_programs(1) - 1)
    def _():
        sout_ref[0] = st_ref[(pl.num_programs(1) * n_groups) % 2]


def _gla(qg, kg, vg, gg, la_hi, la_lo, s0, tril2, batch, seq, blk, tg, unroll=1):
    nblk = seq // tg
    rows = lambda w: pl.BlockSpec((tg, w), lambda b, i: (b * nblk + i, 0))
    state = pl.BlockSpec((1, N_GLA, GLA_DK, GLA_DV), lambda b, i: (b, 0, 0, 0))
    return pl.pallas_call(
        functools.partial(_gla_kernel, blk=blk, tg=tg, unroll=unroll),
        grid=(batch, nblk),
        in_specs=[rows(GLA_KW), rows(GLA_KW), rows(GLA_WIDTH), rows(GLA_WIDTH), rows(GLA_KW),
                  rows(GLA_KW), state, _const_spec(tril2.shape)],
        out_specs=[rows(GLA_WIDTH), state],
        out_shape=[jax.ShapeDtypeStruct((batch * seq, GLA_WIDTH), BF16),
                   jax.ShapeDtypeStruct((batch, N_GLA, GLA_DK, GLA_DV), F32)],
        scratch_shapes=[pltpu.VMEM((2, N_GLA, GLA_DK, GLA_DV), F32),
                        pltpu.VMEM((blk, GLA_DK), F32),
                        pltpu.VMEM((blk, GLA_DK), F32)],
        compiler_params=pltpu.CompilerParams(
            dimension_semantics=("arbitrary", "arbitrary"), vmem_limit_bytes=VMEM_LIMIT),
        name="gla",
    )(qg, kg, vg, gg, la_hi, la_lo, s0, tril2)


def _outproj_kernel(x_ref, ma_ref, mg_ref, w_ref, y_ref):
    y_ref[...] = (x_ref[...] + _dot(ma_ref[...], w_ref[:ATT_WIDTH, :])
                  + _dot(mg_ref[...], w_ref[ATT_WIDTH:, :]))


def _outproj(x2d, mix_a, mix_g, w, tm):
    m = x2d.shape[0]
    row = lambda w: pl.BlockSpec((tm, w), lambda i: (i, 0))
    return pl.pallas_call(
        _outproj_kernel,
        grid=(m // tm,),
        in_specs=[row(D_MODEL), row(ATT_WIDTH), row(GLA_WIDTH), _const_spec(w.shape)],
        out_specs=row(D_MODEL),
        out_shape=jax.ShapeDtypeStruct((m, D_MODEL), F32),
        compiler_params=pltpu.CompilerParams(
            dimension_semantics=("arbitrary",), vmem_limit_bytes=VMEM_LIMIT),
        name="outproj",
    )(x2d, mix_a, mix_g, w)


def _tril2(blk):
    t = np.tril(np.ones((blk, blk), np.float32))
    return jnp.asarray(np.concatenate([t, t], axis=1), BF16)


def _prep_w_in_kernel(u0_ref, u1_ref, u2_ref, u3_ref, lr_ref, main_ref, wlr_ref):
    units = jnp.concatenate([u0_ref[...], u1_ref[...], u2_ref[...], u3_ref[...]], axis=0)
    main_ref[...] = units.T.astype(BF16)

    @pl.when(pl.program_id(0) == 0)
    def _():
        lr = lr_ref[...]
        padded = jnp.concatenate([lr, jnp.zeros((LANES - lr.shape[0], lr.shape[1]), F32)], axis=0)
        wlr_ref[...] = padded.T.astype(BF16)


def _prep_w_in(w_in_t):
    n_cols, k = w_in_t.shape
    units_per_block = KV_WIDTH // HEAD_DIM

    def src_unit(j, p):
        u = j * units_per_block + p
        def regroup(v):
            return (v % N_KV) * GQA_REP + v // N_KV
        q0, ga0 = OFF_Q // HEAD_DIM, OFF_GA // HEAD_DIM
        n_att = ATT_WIDTH // HEAD_DIM
        u = jnp.where((u >= q0) & (u < q0 + n_att), q0 + regroup(u - q0), u)
        return jnp.where((u >= ga0) & (u < ga0 + n_att), ga0 + regroup(u - ga0), u)

    unit = lambda p: pl.BlockSpec((HEAD_DIM, k), lambda j: (src_unit(j, p), 0))
    return pl.pallas_call(
        _prep_w_in_kernel,
        grid=(MAIN_COLS // KV_WIDTH,),
        in_specs=[unit(0), unit(1), unit(2), unit(3),
                  pl.BlockSpec((GATE_RANK, k), lambda j: (MAIN_COLS // GATE_RANK, 0))],
        out_specs=[pl.BlockSpec((k, KV_WIDTH), lambda j: (0, j)),
                   pl.BlockSpec((k, LANES), lambda j: (0, 0))],
        out_shape=[jax.ShapeDtypeStruct((k, MAIN_COLS), BF16),
                   jax.ShapeDtypeStruct((k, LANES), BF16)],
        compiler_params=pltpu.CompilerParams(
            dimension_semantics=("arbitrary",), vmem_limit_bytes=VMEM_LIMIT),
        name="prep_w_in",
    )(w_in_t, w_in_t, w_in_t, w_in_t, w_in_t)


def _prep_w_out_kernel(w_ref, o_ref):
    o_ref[...] = w_ref[...].astype(BF16)


def _prep_w_out(w_out):
    n_att = ATT_WIDTH // HEAD_DIM

    def src_block(j):
        regrouped = (j % N_KV) * GQA_REP + j // N_KV
        return jnp.where(j < n_att, regrouped, j)

    rows, cols = w_out.shape
    return pl.pallas_call(
        _prep_w_out_kernel,
        grid=(rows // HEAD_DIM,),
        in_specs=[pl.BlockSpec((HEAD_DIM, cols), lambda j: (src_block(j), 0))],
        out_specs=pl.BlockSpec((HEAD_DIM, cols), lambda j: (j, 0)),
        out_shape=jax.ShapeDtypeStruct((rows, cols), BF16),
        compiler_params=pltpu.CompilerParams(dimension_semantics=("arbitrary",)),
        name="prep_w_out",
    )(w_out)


def _prep(norm_g, w_in, w_gate_up, b_gate, q_norm_g, k_norm_g, sinks, gla_norm_g, w_out):
    w_main, w_lr = _prep_w_in(w_in.T)
    w_gu = jnp.pad(w_gate_up, ((0, LANES - GATE_RANK), (0, 0))).astype(BF16)
    g_qk = jnp.concatenate([jnp.tile(q_norm_g, N_HEADS) * (ATT_SCALE * LOG2E),
                            jnp.tile(k_norm_g, N_KV)])[None, :]

    grp = np.arange(QK_WIDTH) // HEAD_DIM
    ind = (grp[:, None] == np.arange(LANES)[None, :]).astype(np.float32)
    g_sum = jnp.asarray(ind, BF16)
    g_exp = jnp.asarray(np.concatenate([ind.T, ind.T], axis=0), BF16)

    w_out_b = _prep_w_out(w_out)

    def sink_rows(t):
        sk = sinks.astype(F32).reshape(N_KV, GQA_REP) * LOG2E
        return jnp.broadcast_to(sk[:, :, None, None], (N_KV, GQA_REP, t, LANES)).reshape(
            N_KV, GQA_REP * t, LANES)

    return dict(norm_g=norm_g[None, :], w_main=w_main, w_lr=w_lr, w_gu=w_gu, b_gate=b_gate[None, :],
                g_qk=g_qk, g_gla=jnp.tile(gla_norm_g, N_GLA)[None, :], g_sum=g_sum, g_exp=g_exp,
                w_out=w_out_b, sink_rows=sink_rows)


def kernel(x_prompt, x_sample, cache_k, cache_v, state_gla, norm_g, w_in, w_gate_up, b_gate,
           q_norm_g, k_norm_g, sinks, gla_norm_g, w_out):
    depth = norm_g.shape[0]
    assert depth == 1
    B, S, _ = x_prompt.shape
    DB, T, _ = x_sample.shape
    n_cache = cache_k.shape[2]
    p = _prep(norm_g[0], w_in[0], w_gate_up[0], b_gate[0], q_norm_g[0], k_norm_g[0], sinks[0],
              gla_norm_g[0], w_out[0])

    xp = x_prompt.reshape(B * S, D_MODEL)
    q, k, v, ga, qg, kg, vg, gg, la_hi, la_lo = _inproj(xp, p, tm=512)
    assert n_cache == WINDOW
    mix_a, win_k_p, win_v_p = _attn_prompt(q, k, v, ga, p["sink_rows"](CHUNK), B, S, tq=512, unroll=2)
    s0 = jnp.zeros((B, N_GLA, GLA_DK, GLA_DV), F32)
    mix_g, gla_p = _gla(qg, kg, vg, gg, la_hi, la_lo, s0, _tril2(CHUNK), B, S, blk=CHUNK, tg=512,
                      unroll=4)
    y_p = _outproj(xp, mix_a, mix_g, p["w_out"], tm=512).reshape(B, S, D_MODEL)

    xs = x_sample.reshape(DB * T, D_MODEL)
    q, k, v, ga, qg, kg, vg, gg, la_hi, la_lo = _inproj(xs, p, tm=DB * T)
    ck = cache_k[0].reshape(DB, n_cache, KV_WIDTH)
    cv = cache_v[0].reshape(DB, n_cache, KV_WIDTH)
    mix_a, win_k_s, win_v_s = _attn_sample(q, k, v, ck, cv, ga, p["sink_rows"](T), DB, T, bb=4)
    blk = min(CHUNK, T)
    mix_g, gla_s = _gla(qg, kg, vg, gg, la_hi, la_lo, state_gla[0], _tril2(blk), DB, T,
                        blk=blk, tg=T)
    y_s = _outproj(xs, mix_a, mix_g, p["w_out"], tm=DB * T).reshape(DB, T, D_MODEL)

    shape5 = lambda a: a.reshape(1, a.shape[0], n_cache, N_KV, HEAD_DIM)
    return (y_p, y_s, shape5(win_k_p), shape5(win_v_p), gla_p[None],
            shape5(win_k_s), shape5(win_v_s), gla_s[None])
```

```python
import functools

import jax
import jax.numpy as jnp
import numpy as np
from jax import lax
from jax.experimental import pallas as pl
from jax.experimental.pallas import tpu as pltpu

D_MODEL = 2048
CHUNK = 64
WINDOW = 128
HEAD_DIM = 64
N_HEADS = 16
N_KV = 4
GQA_REP = N_HEADS // N_KV
ATT_WIDTH = N_HEADS * HEAD_DIM
KV_WIDTH = N_KV * HEAD_DIM
N_GLA = 4
GLA_DK = 128
GLA_DV = 256
GLA_KW = N_GLA * GLA_DK
GLA_WIDTH = N_GLA * GLA_DV
GATE_RANK = 16
GATE_TAU = 16.0
EPS = 1e-6
ATT_SCALE = HEAD_DIM ** -0.5
LOG2E = float(np.log2(np.e))
LOG2_DECAY_LIMIT = 100.0
WIN_CHUNKS = WINDOW // CHUNK
WIN_KEYS = WINDOW + CHUNK

LANES = 128
QK_WIDTH = ATT_WIDTH + KV_WIDTH
N_QK_GROUPS = QK_WIDTH // HEAD_DIM
OFF_Q, OFF_K, OFF_V = 0, ATT_WIDTH, ATT_WIDTH + KV_WIDTH
OFF_GA = OFF_V + KV_WIDTH
OFF_QG = OFF_GA + ATT_WIDTH
OFF_KG = OFF_QG + GLA_KW
OFF_VG = OFF_KG + GLA_KW
OFF_GG = OFF_VG + GLA_WIDTH
MAIN_COLS = OFF_GG + GLA_WIDTH

VMEM_LIMIT = 56 * 1024 * 1024

BF16 = jnp.bfloat16
F32 = jnp.float32


def _dot(a, b):
    return jnp.dot(a, b, preferred_element_type=F32)


def _dot_t(a, b):
    return lax.dot_general(a, b, (((1,), (1,)), ((), ())), preferred_element_type=F32)


def _dot_ta(a, b):
    return lax.dot_general(a, b, (((0,), (0,)), ((), ())), preferred_element_type=F32)


def _silu(x):
    return x / (1.0 + jnp.exp(-x))


def _const_spec(shape):
    nd = len(shape)
    return pl.BlockSpec(shape, lambda *_: (0,) * nd, pipeline_mode=pl.Buffered(1))


def _inproj_kernel(x_ref, ng_ref, w_ref, wlr_ref, wgu_ref, bg_ref, gqk_ref, gng_ref, gsum_ref, gexp_ref,
                   q_ref, k_ref, v_ref, ga_ref, qg_ref, kg_ref, vg_ref, gg_ref, lah_ref, lal_ref,
                   *, sub):
    for s in range(x_ref.shape[0] // sub):
        rows = slice(s * sub, (s + 1) * sub)
        x = x_ref[rows, :]
        ms = jnp.mean(x * x, axis=-1, keepdims=True)
        xn = (x * lax.rsqrt(ms + EPS) * ng_ref[...]).astype(BF16)

        qk = _dot(xn, w_ref[:, OFF_Q:OFF_V])
        ga_ref[rows, :] = _silu(_dot(xn, w_ref[:, OFF_GA:OFF_QG])).astype(BF16)
        ss = _dot((qk * qk).astype(BF16), gsum_ref[...])
        lr = _dot(xn, wlr_ref[...]).astype(BF16)
        qg_ref[rows, :] = (_dot(xn, w_ref[:, OFF_QG:OFF_KG]) * (GLA_DK ** -0.5)).astype(BF16)
        kg_ref[rows, :] = _dot(xn, w_ref[:, OFF_KG:OFF_VG]).astype(BF16)
        r = lax.rsqrt(ss * (1.0 / HEAD_DIM) + EPS)
        r_hi = r.astype(BF16)
        r_lo = (r - r_hi.astype(F32)).astype(BF16)
        r_full = _dot(jnp.concatenate([r_hi, r_lo], axis=1), gexp_ref[...])
        z = _dot(lr, wgu_ref[...]) + bg_ref[...]
        gg_ref[rows, :] = (_silu(_dot(xn, w_ref[:, OFF_GG:MAIN_COLS])) * gng_ref[...]).astype(BF16)
        vg_ref[rows, :] = _dot(xn, w_ref[:, OFF_VG:OFF_GG]).astype(BF16)
        v_ref[rows, :] = _dot(xn, w_ref[:, OFF_V:OFF_GA])

        qkn = qk * r_full * gqk_ref[...]
        q_ref[rows, :] = qkn[:, :ATT_WIDTH].astype(BF16)
        k_ref[rows, :] = qkn[:, ATT_WIDTH:]
        log_sig = jnp.minimum(z, 0.0) - jnp.log(1.0 + jnp.exp(-jnp.abs(z)))
        la = log_sig * (LOG2E / GATE_TAU)
        la_hi = la.astype(BF16)
        lah_ref[rows, :] = la_hi
        lal_ref[rows, :] = (la - la_hi.astype(F32)).astype(BF16)


def _inproj(x2d, p, tm):
    m = x2d.shape[0]
    row = lambda w: pl.BlockSpec((tm, w), lambda i: (i, 0))
    outs = [(ATT_WIDTH, BF16), (KV_WIDTH, F32), (KV_WIDTH, F32), (ATT_WIDTH, BF16),
            (GLA_KW, BF16), (GLA_KW, BF16), (GLA_WIDTH, BF16), (GLA_WIDTH, BF16),
            (GLA_KW, BF16), (GLA_KW, BF16)]
    consts = [p["norm_g"], p["w_main"], p["w_lr"], p["w_gu"], p["b_gate"], p["g_qk"], p["g_gla"],
              p["g_sum"], p["g_exp"]]
    return pl.pallas_call(
        functools.partial(_inproj_kernel, sub=min(tm, 256)),
        grid=(m // tm,),
        in_specs=[row(D_MODEL)] + [_const_spec(c.shape) for c in consts],
        out_specs=[row(w) for w, _ in outs],
        out_shape=[jax.ShapeDtypeStruct((m, w), dt) for w, dt in outs],
        compiler_params=pltpu.CompilerParams(
            dimension_semantics=("arbitrary",), vmem_limit_bytes=VMEM_LIMIT),
        name="inproj",
    )(x2d, *consts)


def _attn_prompt_kernel(q_ref, kp_ref, kc_ref, vp_ref, vc_ref, ga_ref, sink_ref,
                        o_ref, wk_ref, wv_ref, kw_ref, vw_ref, *, tq, unroll):
    assert unroll >= WIN_CHUNKS and tq % (CHUNK * unroll) == 0 and tq >= WINDOW
    i = pl.program_id(1)
    kw_ref[0:WINDOW, :] = kp_ref[0].astype(BF16)
    kw_ref[WINDOW:, :] = kc_ref[0].astype(BF16)
    vw_ref[0:WINDOW, :] = vp_ref[0].astype(BF16)
    vw_ref[WINDOW:, :] = vc_ref[0].astype(BF16)

    @pl.when(i == pl.num_programs(1) - 1)
    def _():
        wk_ref[0] = kc_ref[0, tq - WINDOW:, :]
        wv_ref[0] = vc_ref[0, tq - WINDOW:, :]

    lane = lax.broadcasted_iota(jnp.int32, (1, LANES), 1)
    lo = lane < HEAD_DIM
    col_group = lax.broadcasted_iota(jnp.int32, (1, KV_WIDTH), 1) // HEAD_DIM
    col = lax.broadcasted_iota(jnp.int32, (1, N_KV * WIN_KEYS), 1)
    key_in_seg = col - WIN_KEYS * sum((col >= g * WIN_KEYS).astype(jnp.int32) for g in range(1, N_KV))
    ones_bd = jnp.concatenate(
        [jnp.broadcast_to(jnp.where(col_group == g, 1.0, 0.0).astype(BF16), (WIN_KEYS, KV_WIDTH))
         for g in range(N_KV)], axis=0)
    neg_inf = jnp.float32(-jnp.inf)

    def chunk_group(j, n_masked):
        rows = [pl.ds(pl.multiple_of((j * unroll + u) * CHUNK, CHUNK), CHUNK) for u in range(unroll)]
        wins = [pl.ds(pl.multiple_of((j * unroll + u) * CHUNK, CHUNK), WIN_KEYS) for u in range(unroll)]
        s, v_all = [], []
        for u in range(unroll):
            kwin = kw_ref[wins[u], :]
            vwin = vw_ref[wins[u], :]
            zero = jnp.zeros_like(kwin)
            k_bd = jnp.concatenate([jnp.where(col_group == g, kwin, zero) for g in range(N_KV)], axis=0)
            v_bd = jnp.concatenate([jnp.where(col_group == g, vwin, zero) for g in range(N_KV)], axis=0)
            v_all.append(jnp.concatenate([v_bd, ones_bd], axis=1))
            qs = jnp.concatenate(
                [q_ref[rows[u], r * KV_WIDTH:(r + 1) * KV_WIDTH] for r in range(GQA_REP)], axis=0)
            s.append(_dot_t(qs, k_bd))
        p, m = [], []
        for u in range(unroll):
            su = s[u]
            if u < n_masked:
                su = jnp.where(key_in_seg >= (WIN_CHUNKS - u) * CHUNK, su, neg_inf)
            c = [su[:, LANES * n:LANES * (n + 1)] for n in range(6)]
            seg_max = [
                jnp.maximum(c[0], jnp.where(lo, c[1], neg_inf)),
                jnp.maximum(jnp.where(lo, neg_inf, c[1]), c[2]),
                jnp.maximum(c[3], jnp.where(lo, c[4], neg_inf)),
                jnp.maximum(jnp.where(lo, neg_inf, c[4]), c[5]),
            ]
            mu = [jnp.maximum(jnp.max(seg_max[g], axis=-1, keepdims=True), sink_ref[g])
                  for g in range(N_KV)]
            shift = [mu[0], jnp.where(lo, mu[0], mu[1]), mu[1], mu[2], jnp.where(lo, mu[2], mu[3]), mu[3]]
            p.append(jnp.concatenate([jnp.exp2(c[n] - shift[n]) for n in range(6)], axis=1).astype(BF16))
            m.append(mu)
        ov = [_dot(p[u], v_all[u]) for u in range(unroll)]
        for u in range(unroll):
            es = [jnp.exp2(sink_ref[g] - m[u][g]) for g in range(N_KV)]
            e_sink = jnp.concatenate([jnp.where(lo, es[0], es[1]), jnp.where(lo, es[2], es[3])], axis=1)
            o = ov[u][:, :KV_WIDTH] / (ov[u][:, KV_WIDTH:] + e_sink)
            for r in range(GQA_REP):
                cols = slice(r * KV_WIDTH, (r + 1) * KV_WIDTH)
                gate = ga_ref[rows[u], cols].astype(F32)
                o_ref[rows[u], cols] = (o[r * CHUNK:(r + 1) * CHUNK, :] * gate).astype(BF16)

    n_masked = min(WIN_CHUNKS, unroll)

    @pl.when(i == 0)
    def _():
        chunk_group(0, n_masked)

    @pl.when(i != 0)
    def _():
        chunk_group(0, 0)

    def body(j, carry):
        chunk_group(j, 0)
        return carry

    lax.fori_loop(1, tq // (CHUNK * unroll), body, 0)


def _attn_prompt(q, k, v, ga, sink_rows, batch, seq, tq, unroll):
    nblk = seq // tq
    per_win = tq // WINDOW
    rows = lambda w: pl.BlockSpec((tq, w), lambda b, i: (b * nblk + i, 0))
    cur = pl.BlockSpec((1, tq, KV_WIDTH), lambda b, i: (b, i, 0))
    prev = pl.BlockSpec((1, WINDOW, KV_WIDTH), lambda b, i: (b, jnp.maximum(i * per_win - 1, 0), 0))
    win = pl.BlockSpec((1, WINDOW, KV_WIDTH), lambda b, i: (b, 0, 0))
    k3 = k.reshape(batch, seq, KV_WIDTH)
    v3 = v.reshape(batch, seq, KV_WIDTH)
    return pl.pallas_call(
        functools.partial(_attn_prompt_kernel, tq=tq, unroll=unroll),
        grid=(batch, nblk),
        in_specs=[rows(ATT_WIDTH), prev, cur, prev, cur, rows(ATT_WIDTH), _const_spec(sink_rows.shape)],
        out_specs=[rows(ATT_WIDTH), win, win],
        out_shape=[jax.ShapeDtypeStruct((batch * seq, ATT_WIDTH), BF16),
                   jax.ShapeDtypeStruct((batch, WINDOW, KV_WIDTH), F32),
                   jax.ShapeDtypeStruct((batch, WINDOW, KV_WIDTH), F32)],
        scratch_shapes=[pltpu.VMEM((WINDOW + tq, KV_WIDTH), BF16),
                        pltpu.VMEM((WINDOW + tq, KV_WIDTH), BF16)],
        compiler_params=pltpu.CompilerParams(
            dimension_semantics=("arbitrary", "arbitrary"), vmem_limit_bytes=VMEM_LIMIT),
        name="attn_prompt",
    )(q, k3, k3, v3, v3, ga, sink_rows)


def _mix_prompt_kernel(q_ref, kp_ref, kc_ref, vp_ref, vc_ref, ga_ref, sink_ref,
                       qg_ref, kg_ref, vg_ref, gg_ref, lah_ref, lal_ref, tril_ref,
                       oa_ref, og_ref, wk_ref, wv_ref, sout_ref,
                       kw_ref, vw_ref, st_ref, b_ref, qx_ref, tot_ref, *, tq, group):
    n_chunks = tq // CHUNK
    assert n_chunks % group == 0 and group >= WIN_CHUNKS and tq >= WINDOW
    i = pl.program_id(1)
    last = pl.num_programs(1) - 1
    cur = i % 2
    nxt = 1 - cur

    kw_ref[0:WINDOW, :] = kp_ref[0].astype(BF16)
    kw_ref[WINDOW:, :] = kc_ref[0].astype(BF16)
    vw_ref[0:WINDOW, :] = vp_ref[0].astype(BF16)
    vw_ref[WINDOW:, :] = vc_ref[0].astype(BF16)

    @pl.when(i == 0)
    def _():
        st_ref[0] = jnp.zeros(st_ref.shape[1:], F32)

    @pl.when(i == last)
    def _():
        wk_ref[0] = kc_ref[0, tq - WINDOW:, :]
        wv_ref[0] = vc_ref[0, tq - WINDOW:, :]

    lane = lax.broadcasted_iota(jnp.int32, (1, LANES), 1)
    lo = lane < HEAD_DIM
    col_group = lax.broadcasted_iota(jnp.int32, (1, KV_WIDTH), 1) // HEAD_DIM
    col = lax.broadcasted_iota(jnp.int32, (1, N_KV * WIN_KEYS), 1)
    key_in_seg = col - WIN_KEYS * sum((col >= g * WIN_KEYS).astype(jnp.int32) for g in range(1, N_KV))
    ones_bd = jnp.concatenate(
        [jnp.broadcast_to(jnp.where(col_group == g, 1.0, 0.0).astype(BF16), (WIN_KEYS, KV_WIDTH))
         for g in range(N_KV)], axis=0)
    neg_inf = jnp.float32(-jnp.inf)
    rows = [slice(c * CHUNK, (c + 1) * CHUNK) for c in range(n_chunks)]
    a_s, a_v, a_p, a_m, a_ov = {}, {}, {}, {}, {}

    def attn_scores(c):
        win = slice(c * CHUNK, c * CHUNK + WIN_KEYS)
        kwin, vwin = kw_ref[win, :], vw_ref[win, :]
        zero = jnp.zeros_like(kwin)
        k_bd = jnp.concatenate([jnp.where(col_group == g, kwin, zero) for g in range(N_KV)], axis=0)
        v_bd = jnp.concatenate([jnp.where(col_group == g, vwin, zero) for g in range(N_KV)], axis=0)
        a_v[c] = jnp.concatenate([v_bd, ones_bd], axis=1)
        qs = jnp.concatenate(
            [q_ref[rows[c], r * KV_WIDTH:(r + 1) * KV_WIDTH] for r in range(GQA_REP)], axis=0)
        a_s[c] = _dot_t(qs, k_bd)

    def attn_softmax(c):
        su = a_s.pop(c)
        if c < WIN_CHUNKS:
            first_valid = jnp.where(i == 0, (WIN_CHUNKS - c) * CHUNK, 0)
            su = jnp.where(key_in_seg >= first_valid, su, neg_inf)
        cs = [su[:, LANES * n:LANES * (n + 1)] for n in range(6)]
        seg_max = [
            jnp.maximum(cs[0], jnp.where(lo, cs[1], neg_inf)),
            jnp.maximum(jnp.where(lo, neg_inf, cs[1]), cs[2]),
            jnp.maximum(cs[3], jnp.where(lo, cs[4], neg_inf)),
            jnp.maximum(jnp.where(lo, neg_inf, cs[4]), cs[5]),
        ]
        mu = [jnp.maximum(jnp.max(seg_max[g], axis=-1, keepdims=True), sink_ref[g])
              for g in range(N_KV)]
        shift = [mu[0], jnp.where(lo, mu[0], mu[1]), mu[1], mu[2], jnp.where(lo, mu[2], mu[3]), mu[3]]
        a_p[c] = jnp.concatenate([jnp.exp2(cs[n] - shift[n]) for n in range(6)], axis=1).astype(BF16)
        a_m[c] = mu

    def attn_pv(c):
        a_ov[c] = _dot(a_p.pop(c), a_v.pop(c))

    def attn_finish(c):
        ov, mu = a_ov.pop(c), a_m.pop(c)
        es = [jnp.exp2(sink_ref[g] - mu[g]) for g in range(N_KV)]
        e_sink = jnp.concatenate([jnp.where(lo, es[0], es[1]), jnp.where(lo, es[2], es[3])], axis=1)
        o = ov[:, :KV_WIDTH] / (ov[:, KV_WIDTH:] + e_sink)
        for r in range(GQA_REP):
            cols = slice(r * KV_WIDTH, (r + 1) * KV_WIDTH)
            gate = ga_ref[rows[c], cols].astype(F32)
            oa_ref[rows[c], cols] = (o[r * CHUNK:(r + 1) * CHUNK, :] * gate).astype(BF16)

    rowi = lax.broadcasted_iota(jnp.int32, (CHUNK, CHUNK), 0)
    coli = lax.broadcasted_iota(jnp.int32, (CHUNK, CHUNK), 1)
    causal = rowi >= coli
    eye = (lax.broadcasted_iota(jnp.int32, (GLA_DK, GLA_DK), 0)
           == lax.broadcasted_iota(jnp.int32, (GLA_DK, GLA_DK), 1))
    heads = range(N_GLA)
    kcol = [slice(h * GLA_DK, (h + 1) * GLA_DK) for h in heads]
    vcol = [slice(h * GLA_DV, (h + 1) * GLA_DV) for h in heads]
    g_b, g_qd, g_ki, g_ke, g_dc, g_att, g_dst, g_o = {}, {}, {}, {}, {}, {}, {}, {}
    state = {}

    def cum_log2_decay(r):
        return _dot(tril_ref[...], jnp.concatenate([lah_ref[r, :], lal_ref[r, :]], axis=0))

    def decay_column(blast):
        dec_row = jnp.broadcast_to(jnp.exp2(blast), (GLA_DK, GLA_DK))
        return jnp.sum(jnp.where(eye, dec_row, 0.0), axis=-1, keepdims=True)

    def gla_finish_rows(ov, r, h):
        ms = jnp.mean(ov * ov, axis=-1, keepdims=True)
        gate = gg_ref[r, vcol[h]].astype(F32)
        og_ref[r, vcol[h]] = (ov * lax.rsqrt(ms + EPS) * gate).astype(BF16)

    def gla_cumsum(c):
        g_b[c] = cum_log2_decay(rows[c])

    def gla_prepare(c):
        for h in heads:
            b = g_b[c][:, kcol[h]]
            blast = b[CHUNK - 1:CHUNK, :]
            q = qg_ref[rows[c], kcol[h]].astype(F32)
            k = kg_ref[rows[c], kcol[h]].astype(F32)
            decay = jnp.exp2(b)
            g_qd[c, h] = (q * decay).astype(BF16)
            g_ki[c, h] = (k * (1.0 / decay)).astype(BF16)
            g_ke[c, h] = (k * jnp.exp2(blast - b)).astype(BF16)
            g_dc[c, h] = decay_column(blast)

    def gla_scores(c):
        for h in heads:
            g_att[c, h] = _dot_t(g_qd[c, h], g_ki.pop((c, h)))
            g_dst[c, h] = _dot_ta(g_ke.pop((c, h)), vg_ref[rows[c], vcol[h]])

    def gla_outputs(c):
        for h in heads:
            att = jnp.where(causal, g_att.pop((c, h)), 0.0).astype(BF16)
            g_o[c, h] = (_dot(att, vg_ref[rows[c], vcol[h]])
                         + _dot(g_qd.pop((c, h)), state[h].astype(BF16)))
            state[h] = state[h] * g_dc.pop((c, h)) + g_dst.pop((c, h))

    def gla_finish(c):
        for h in heads:
            gla_finish_rows(g_o.pop((c, h)), rows[c], h)

    def gla_exact_step():
        def chunk(c, carry):
            r = pl.ds(pl.multiple_of(c * CHUNK, CHUNK), CHUNK)
            bcum = cum_log2_decay(r)
            for h in heads:
                b = bcum[:, kcol[h]]
                blast = b[CHUNK - 1:CHUNK, :]
                q = qg_ref[r, kcol[h]].astype(F32)
                k = kg_ref[r, kcol[h]].astype(F32)
                v = vg_ref[r, vcol[h]]
                b_ref[...] = b
                qx_ref[...] = q

                def score_row(t, att_t):
                    w = jnp.where(rowi[:, 0:1] <= t, jnp.exp2(b_ref[pl.ds(t, 1), :] - b), 0.0)
                    colv = jnp.sum(qx_ref[pl.ds(t, 1), :] * k * w, axis=-1, keepdims=True)
                    return att_t + jnp.where(coli == t, colv, 0.0)

                att_t = lax.fori_loop(0, CHUNK, score_row, jnp.zeros((CHUNK, CHUNK), F32))
                q_dec = (q * jnp.exp2(b)).astype(BF16)
                k_end = (k * jnp.exp2(blast - b)).astype(BF16)
                st = st_ref[nxt, h]
                ov = _dot_ta(att_t.astype(BF16), v) + _dot(q_dec, st.astype(BF16))
                st_ref[nxt, h] = st * decay_column(blast) + _dot_ta(k_end, v)
                gla_finish_rows(ov, r, h)
            return carry

        st_ref[nxt] = st_ref[cur]
        lax.fori_loop(0, n_chunks, chunk, 0)

    def run_group(g0):
        cs = list(range(g0, g0 + group))
        half = group // 2
        src = cur if g0 == 0 else nxt
        for h in heads:
            state[h] = st_ref[src, h]
        for c in cs:
            gla_cumsum(c)
        for c in cs:
            attn_scores(c)
        for c in cs:
            gla_prepare(c)
        for c in cs:
            gla_scores(c)
        for c in cs[:half]:
            attn_softmax(c)
        for c in cs[:half]:
            attn_pv(c)
        for c in cs:
            gla_outputs(c)
        for c in cs[half:]:
            attn_softmax(c)
        for c in cs[half:]:
            attn_pv(c)
        for c in cs:
            gla_finish(c)
        for c in cs:
            attn_finish(c)
        for h in heads:
            st_ref[nxt, h] = state[h]
        total = functools.reduce(jnp.minimum, [g_b.pop(c)[CHUNK - 1:CHUNK, :] for c in cs])
        tot_ref[...] = total if g0 == 0 else jnp.minimum(tot_ref[...], total)

    for g0 in range(0, n_chunks, group):
        pl.when(i >= 0)(functools.partial(run_group, g0))

    @pl.when(jnp.min(tot_ref[...]) < -LOG2_DECAY_LIMIT)
    def _():
        gla_exact_step()

    @pl.when(i == last)
    def _():
        sout_ref[0] = st_ref[nxt]


def _mix_prompt(q, k, v, ga, sink_rows, qg, kg, vg, gg, la_hi, la_lo, tril2, batch, seq, tq, group):
    nblk = seq // tq
    per_win = tq // WINDOW
    rows = lambda w: pl.BlockSpec((tq, w), lambda b, i: (b * nblk + i, 0))
    cur = pl.BlockSpec((1, tq, KV_WIDTH), lambda b, i: (b, i, 0))
    prev = pl.BlockSpec((1, WINDOW, KV_WIDTH), lambda b, i: (b, jnp.maximum(i * per_win - 1, 0), 0))
    win = pl.BlockSpec((1, WINDOW, KV_WIDTH), lambda b, i: (b, 0, 0))
    state = pl.BlockSpec((1, N_GLA, GLA_DK, GLA_DV), lambda b, i: (b, 0, 0, 0))
    k3 = k.reshape(batch, seq, KV_WIDTH)
    v3 = v.reshape(batch, seq, KV_WIDTH)
    return pl.pallas_call(
        functools.partial(_mix_prompt_kernel, tq=tq, group=group),
        grid=(batch, nblk),
        in_specs=[rows(ATT_WIDTH), prev, cur, prev, cur, rows(ATT_WIDTH), _const_spec(sink_rows.shape),
                  rows(GLA_KW), rows(GLA_KW), rows(GLA_WIDTH), rows(GLA_WIDTH), rows(GLA_KW),
                  rows(GLA_KW), _const_spec(tril2.shape)],
        out_specs=[rows(ATT_WIDTH), rows(GLA_WIDTH), win, win, state],
        out_shape=[jax.ShapeDtypeStruct((batch * seq, ATT_WIDTH), BF16),
                   jax.ShapeDtypeStruct((batch * seq, GLA_WIDTH), BF16),
                   jax.ShapeDtypeStruct((batch, WINDOW, KV_WIDTH), F32),
                   jax.ShapeDtypeStruct((batch, WINDOW, KV_WIDTH), F32),
                   jax.ShapeDtypeStruct((batch, N_GLA, GLA_DK, GLA_DV), F32)],
        scratch_shapes=[pltpu.VMEM((WINDOW + tq, KV_WIDTH), BF16),
                        pltpu.VMEM((WINDOW + tq, KV_WIDTH), BF16),
                        pltpu.VMEM((2, N_GLA, GLA_DK, GLA_DV), F32),
                        pltpu.VMEM((CHUNK, GLA_DK), F32),
                        pltpu.VMEM((CHUNK, GLA_DK), F32),
                        pltpu.VMEM((1, GLA_KW), F32)],
        compiler_params=pltpu.CompilerParams(
            dimension_semantics=("arbitrary", "arbitrary"), vmem_limit_bytes=VMEM_LIMIT),
        name="mix_prompt",
    )(q, k3, k3, v3, v3, ga, sink_rows, qg, kg, vg, gg, la_hi, la_lo, tril2)


def _attn_sample_kernel(q_ref, kn_ref, vn_ref, ck_ref, cv_ref, ga_ref, sink_ref,
                        o_ref, wk_ref, wv_ref, *, t, bb):
    n_cache = ck_ref.shape[1]
    assert n_cache == LANES and t <= LANES
    lane = lax.broadcasted_iota(jnp.int32, (1, LANES), 1)
    lo = lane < HEAD_DIM
    is_new = lane < t
    col_group = lax.broadcasted_iota(jnp.int32, (1, KV_WIDTH), 1) // HEAD_DIM
    seg = 2 * LANES
    ones_bd = jnp.concatenate(
        [jnp.broadcast_to(jnp.where(col_group == g, 1.0, 0.0).astype(BF16), (seg, KV_WIDTH))
         for g in range(N_KV)], axis=0)
    pad = jnp.zeros((LANES - t, KV_WIDTH), BF16)
    neg_inf = jnp.float32(-jnp.inf)

    s, v_all = [], []
    for b in range(bb):
        rows = slice(b * t, (b + 1) * t)
        kn, vn, ck, cv = kn_ref[rows, :], vn_ref[rows, :], ck_ref[b], cv_ref[b]
        wk_ref[b] = jnp.concatenate([ck[t:, :], kn], axis=0)
        wv_ref[b] = jnp.concatenate([cv[t:, :], vn], axis=0)
        kcat = jnp.concatenate([ck.astype(BF16), kn.astype(BF16), pad], axis=0)
        vcat = jnp.concatenate([cv.astype(BF16), vn.astype(BF16), pad], axis=0)
        zero = jnp.zeros_like(kcat)
        k_bd = jnp.concatenate([jnp.where(col_group == g, kcat, zero) for g in range(N_KV)], axis=0)
        v_bd = jnp.concatenate([jnp.where(col_group == g, vcat, zero) for g in range(N_KV)], axis=0)
        v_all.append(jnp.concatenate([v_bd, ones_bd], axis=1))
        qs = jnp.concatenate(
            [q_ref[rows, r * KV_WIDTH:(r + 1) * KV_WIDTH] for r in range(GQA_REP)], axis=0)
        s.append(_dot_t(qs, k_bd))
    p, m = [], []
    for b in range(bb):
        c = [s[b][:, LANES * n:LANES * (n + 1)] for n in range(2 * N_KV)]
        c = [cn if n % 2 == 0 else jnp.where(is_new, cn, neg_inf) for n, cn in enumerate(c)]
        mb = [jnp.maximum(jnp.max(jnp.maximum(c[2 * g], c[2 * g + 1]), axis=-1, keepdims=True),
                          sink_ref[g]) for g in range(N_KV)]
        p.append(jnp.concatenate([jnp.exp2(c[n] - mb[n // 2]) for n in range(2 * N_KV)],
                                 axis=1).astype(BF16))
        m.append(mb)
    ov = [_dot(p[b], v_all[b]) for b in range(bb)]
    for b in range(bb):
        rows = slice(b * t, (b + 1) * t)
        es = [jnp.exp2(sink_ref[g] - m[b][g]) for g in range(N_KV)]
        e_sink = jnp.concatenate([jnp.where(lo, es[0], es[1]), jnp.where(lo, es[2], es[3])], axis=1)
        o = ov[b][:, :KV_WIDTH] / (ov[b][:, KV_WIDTH:] + e_sink)
        for r in range(GQA_REP):
            cols = slice(r * KV_WIDTH, (r + 1) * KV_WIDTH)
            gate = ga_ref[rows, cols].astype(F32)
            o_ref[rows, cols] = (o[r * t:(r + 1) * t, :] * gate).astype(BF16)


def _attn_sample(q, k, v, cache_k, cache_v, ga, sink_rows, batch, t, bb):
    n_cache = cache_k.shape[1]
    rows = lambda w: pl.BlockSpec((bb * t, w), lambda b: (b, 0))
    cache = pl.BlockSpec((bb, n_cache, KV_WIDTH), lambda b: (b, 0, 0))
    return pl.pallas_call(
        functools.partial(_attn_sample_kernel, t=t, bb=bb),
        grid=(batch // bb,),
        in_specs=[rows(ATT_WIDTH), rows(KV_WIDTH), rows(KV_WIDTH), cache, cache, rows(ATT_WIDTH),
                  _const_spec(sink_rows.shape)],
        out_specs=[rows(ATT_WIDTH), cache, cache],
        out_shape=[jax.ShapeDtypeStruct((batch * t, ATT_WIDTH), BF16),
                   jax.ShapeDtypeStruct((batch, n_cache, KV_WIDTH), F32),
                   jax.ShapeDtypeStruct((batch, n_cache, KV_WIDTH), F32)],
        compiler_params=pltpu.CompilerParams(
            dimension_semantics=("arbitrary",), vmem_limit_bytes=VMEM_LIMIT),
        name="attn_sample",
    )(q, k, v, cache_k, cache_v, ga, sink_rows)


def _gla_kernel(qg_ref, kg_ref, vg_ref, gg_ref, lah_ref, lal_ref, s0_ref, tril_ref,
                o_ref, sout_ref, st_ref, b_ref, q_ref, *, blk, tg, unroll):
    i = pl.program_id(1)
    n_groups = tg // (blk * unroll)

    @pl.when(i == 0)
    def _():
        st_ref[0] = s0_ref[0]

    rowi = lax.broadcasted_iota(jnp.int32, (blk, blk), 0)
    coli = lax.broadcasted_iota(jnp.int32, (blk, blk), 1)
    causal = rowi >= coli
    eye = (lax.broadcasted_iota(jnp.int32, (GLA_DK, GLA_DK), 0)
           == lax.broadcasted_iota(jnp.int32, (GLA_DK, GLA_DK), 1))

    heads = range(N_GLA)
    kcol = [slice(h * GLA_DK, (h + 1) * GLA_DK) for h in heads]
    vcol = [slice(h * GLA_DV, (h + 1) * GLA_DV) for h in heads]

    def cum_log2_decay(r):
        return _dot(tril_ref[...], jnp.concatenate([lah_ref[r, :], lal_ref[r, :]], axis=0))

    def decay_column(blast):
        dec_row = jnp.broadcast_to(jnp.exp2(blast), (GLA_DK, GLA_DK))
        return jnp.sum(jnp.where(eye, dec_row, 0.0), axis=-1, keepdims=True)

    def finish(ov, r, h):
        ms = jnp.mean(ov * ov, axis=-1, keepdims=True)
        gate = gg_ref[r, vcol[h]].astype(F32)
        o_ref[r, vcol[h]] = (ov * lax.rsqrt(ms + EPS) * gate).astype(BF16)

    def exact_group(rows, cur, nxt):
        st = [st_ref[cur, h] for h in heads]
        for r in rows:
            bcum = cum_log2_decay(r)
            for h in heads:
                b = bcum[:, kcol[h]]
                blast = b[blk - 1:blk, :]
                q = qg_ref[r, kcol[h]].astype(F32)
                k = kg_ref[r, kcol[h]].astype(F32)
                v = vg_ref[r, vcol[h]]
                b_ref[0:blk, :] = b
                q_ref[0:blk, :] = q

                def score_row(t, att_t):
                    w = jnp.where(rowi[:, 0:1] <= t, jnp.exp2(b_ref[pl.ds(t, 1), :] - b), 0.0)
                    col = jnp.sum(q_ref[pl.ds(t, 1), :] * k * w, axis=-1, keepdims=True)
                    return att_t + jnp.where(coli == t, col, 0.0)

                att_t = lax.fori_loop(0, blk, score_row, jnp.zeros((blk, blk), F32))
                q_dec = (q * jnp.exp2(b)).astype(BF16)
                k_end = (k * jnp.exp2(blast - b)).astype(BF16)
                ov = _dot_ta(att_t.astype(BF16), v) + _dot(q_dec, st[h].astype(BF16))
                st[h] = st[h] * decay_column(blast) + _dot_ta(k_end, v)
                finish(ov, r, h)
        for h in heads:
            st_ref[nxt, h] = st[h]

    def group(j, carry):
        rows = [pl.ds(pl.multiple_of((j * unroll + u) * blk, blk), blk) for u in range(unroll)]
        cur = (i * n_groups + j) % 2
        nxt = 1 - cur
        bcum = [cum_log2_decay(r) for r in rows]
        q_dec, k_inv, k_end, dec_col = {}, {}, {}, {}
        for u, r in enumerate(rows):
            for h in heads:
                b = bcum[u][:, kcol[h]]
                blast = b[blk - 1:blk, :]
                q = qg_ref[r, kcol[h]].astype(F32)
                k = kg_ref[r, kcol[h]].astype(F32)
                decay = jnp.exp2(b)
                q_dec[u, h] = (q * decay).astype(BF16)
                k_inv[u, h] = (k * (1.0 / decay)).astype(BF16)
                k_end[u, h] = (k * jnp.exp2(blast - b)).astype(BF16)
                dec_col[u, h] = decay_column(blast)
        att = {(u, h): _dot_t(q_dec[u, h], k_inv[u, h]) for u in range(unroll) for h in heads}
        d_st = {(u, h): _dot_ta(k_end[u, h], vg_ref[rows[u], vcol[h]])
                for u in range(unroll) for h in heads}
        att = {key: jnp.where(causal, a, 0.0).astype(BF16) for key, a in att.items()}
        st = [st_ref[cur, h] for h in heads]
        o = {}
        for u, r in enumerate(rows):
            for h in heads:
                o[u, h] = _dot(att[u, h], vg_ref[r, vcol[h]]) + _dot(q_dec[u, h], st[h].astype(BF16))
                st[h] = st[h] * dec_col[u, h] + d_st[u, h]
        for h in heads:
            st_ref[nxt, h] = st[h]
        for u, r in enumerate(rows):
            for h in heads:
                finish(o[u, h], r, h)

        total = functools.reduce(jnp.minimum, [bc[blk - 1:blk, :] for bc in bcum])

        @pl.when(jnp.min(total) < -LOG2_DECAY_LIMIT)
        def _():
            exact_group(rows, cur, nxt)
        return carry

    lax.fori_loop(0, n_groups, group, 0)

    @pl.when(i == pl.num_programs(1) - 1)
    def _():
        sout_ref[0] = st_ref[(pl.num_programs(1) * n_groups) % 2]


def _gla(qg, kg, vg, gg, la_hi, la_lo, s0, tril2, batch, seq, blk, tg, unroll=1):
    nblk = seq // tg
    rows = lambda w: pl.BlockSpec((tg, w), lambda b, i: (b * nblk + i, 0))
    state = pl.BlockSpec((1, N_GLA, GLA_DK, GLA_DV), lambda b, i: (b, 0, 0, 0))
    return pl.pallas_call(
        functools.partial(_gla_kernel, blk=blk, tg=tg, unroll=unroll),
        grid=(batch, nblk),
        in_specs=[rows(GLA_KW), rows(GLA_KW), rows(GLA_WIDTH), rows(GLA_WIDTH), rows(GLA_KW),
                  rows(GLA_KW), state, _const_spec(tril2.shape)],
        out_specs=[rows(GLA_WIDTH), state],
        out_shape=[jax.ShapeDtypeStruct((batch * seq, GLA_WIDTH), BF16),
                   jax.ShapeDtypeStruct((batch, N_GLA, GLA_DK, GLA_DV), F32)],
        scratch_shapes=[pltpu.VMEM((2, N_GLA, GLA_DK, GLA_DV), F32),
                        pltpu.VMEM((blk, GLA_DK), F32),
                        pltpu.VMEM((blk, GLA_DK), F32)],
        compiler_params=pltpu.CompilerParams(
            dimension_semantics=("arbitrary", "arbitrary"), vmem_limit_bytes=VMEM_LIMIT),
        name="gla",
    )(qg, kg, vg, gg, la_hi, la_lo, s0, tril2)


def _outproj_kernel(x_ref, ma_ref, mg_ref, w_ref, y_ref):
    y_ref[...] = (x_ref[...] + _dot(ma_ref[...], w_ref[:ATT_WIDTH, :])
                  + _dot(mg_ref[...], w_ref[ATT_WIDTH:, :]))


def _outproj(x2d, mix_a, mix_g, w, tm):
    m = x2d.shape[0]
    row = lambda w: pl.BlockSpec((tm, w), lambda i: (i, 0))
    return pl.pallas_call(
        _outproj_kernel,
        grid=(m // tm,),
        in_specs=[row(D_MODEL), row(ATT_WIDTH), row(GLA_WIDTH), _const_spec(w.shape)],
        out_specs=row(D_MODEL),
        out_shape=jax.ShapeDtypeStruct((m, D_MODEL), F32),
        compiler_params=pltpu.CompilerParams(
            dimension_semantics=("arbitrary",), vmem_limit_bytes=VMEM_LIMIT),
        name="outproj",
    )(x2d, mix_a, mix_g, w)


def _tril2(blk):
    t = np.tril(np.ones((blk, blk), np.float32))
    return jnp.asarray(np.concatenate([t, t], axis=1), BF16)


def _prep_w_in_kernel(u0_ref, u1_ref, u2_ref, u3_ref, lr_ref, main_ref, wlr_ref):
    units = jnp.concatenate([u0_ref[...], u1_ref[...], u2_ref[...], u3_ref[...]], axis=0)
    main_ref[...] = units.T.astype(BF16)

    @pl.when(pl.program_id(0) == 0)
    def _():
        lr = lr_ref[...]
        padded = jnp.concatenate([lr, jnp.zeros((LANES - lr.shape[0], lr.shape[1]), F32)], axis=0)
        wlr_ref[...] = padded.T.astype(BF16)


def _prep_w_in(w_in_t):
    n_cols, k = w_in_t.shape
    units_per_block = KV_WIDTH // HEAD_DIM

    def src_unit(j, p):
        u = j * units_per_block + p
        def regroup(v):
            return (v % N_KV) * GQA_REP + v // N_KV
        q0, ga0 = OFF_Q // HEAD_DIM, OFF_GA // HEAD_DIM
        n_att = ATT_WIDTH // HEAD_DIM
        u = jnp.where((u >= q0) & (u < q0 + n_att), q0 + regroup(u - q0), u)
        return jnp.where((u >= ga0) & (u < ga0 + n_att), ga0 + regroup(u - ga0), u)

    unit = lambda p: pl.BlockSpec((HEAD_DIM, k), lambda j: (src_unit(j, p), 0))
    return pl.pallas_call(
        _prep_w_in_kernel,
        grid=(MAIN_COLS // KV_WIDTH,),
        in_specs=[unit(0), unit(1), unit(2), unit(3),
                  pl.BlockSpec((GATE_RANK, k), lambda j: (MAIN_COLS // GATE_RANK, 0))],
        out_specs=[pl.BlockSpec((k, KV_WIDTH), lambda j: (0, j)),
                   pl.BlockSpec((k, LANES), lambda j: (0, 0))],
        out_shape=[jax.ShapeDtypeStruct((k, MAIN_COLS), BF16),
                   jax.ShapeDtypeStruct((k, LANES), BF16)],
        compiler_params=pltpu.CompilerParams(
            dimension_semantics=("arbitrary",), vmem_limit_bytes=VMEM_LIMIT),
        name="prep_w_in",
    )(w_in_t, w_in_t, w_in_t, w_in_t, w_in_t)


def _prep_w_out_kernel(w_ref, o_ref):
    o_ref[...] = w_ref[...].astype(BF16)


def _prep_w_out(w_out):
    n_att = ATT_WIDTH // HEAD_DIM

    def src_block(j):
        regrouped = (j % N_KV) * GQA_REP + j // N_KV
        return jnp.where(j < n_att, regrouped, j)

    rows, cols = w_out.shape
    return pl.pallas_call(
        _prep_w_out_kernel,
        grid=(rows // HEAD_DIM,),
        in_specs=[pl.BlockSpec((HEAD_DIM, cols), lambda j: (src_block(j), 0))],
        out_specs=pl.BlockSpec((HEAD_DIM, cols), lambda j: (j, 0)),
        out_shape=jax.ShapeDtypeStruct((rows, cols), BF16),
        compiler_params=pltpu.CompilerParams(dimension_semantics=("arbitrary",)),
        name="prep_w_out",
    )(w_out)


def _prep(norm_g, w_in, w_gate_up, b_gate, q_norm_g, k_norm_g, sinks, gla_norm_g, w_out):
    w_main, w_lr = _prep_w_in(w_in.T)
    w_gu = jnp.pad(w_gate_up, ((0, LANES - GATE_RANK), (0, 0))).astype(BF16)
    g_qk = jnp.concatenate([jnp.tile(q_norm_g, N_HEADS) * (ATT_SCALE * LOG2E),
                            jnp.tile(k_norm_g, N_KV)])[None, :]

    grp = np.arange(QK_WIDTH) // HEAD_DIM
    ind = (grp[:, None] == np.arange(LANES)[None, :]).astype(np.float32)
    g_sum = jnp.asarray(ind, BF16)
    g_exp = jnp.asarray(np.concatenate([ind.T, ind.T], axis=0), BF16)

    w_out_b = _prep_w_out(w_out)

    def sink_rows(t):
        sk = sinks.astype(F32).reshape(N_KV, GQA_REP) * LOG2E
        return jnp.broadcast_to(sk[:, :, None, None], (N_KV, GQA_REP, t, LANES)).reshape(
            N_KV, GQA_REP * t, LANES)

    return dict(norm_g=norm_g[None, :], w_main=w_main, w_lr=w_lr, w_gu=w_gu, b_gate=b_gate[None, :],
                g_qk=g_qk, g_gla=jnp.tile(gla_norm_g, N_GLA)[None, :], g_sum=g_sum, g_exp=g_exp,
                w_out=w_out_b, sink_rows=sink_rows)


def kernel(x_prompt, x_sample, cache_k, cache_v, state_gla, norm_g, w_in, w_gate_up, b_gate,
           q_norm_g, k_norm_g, sinks, gla_norm_g, w_out):
    depth = norm_g.shape[0]
    assert depth == 1
    B, S, _ = x_prompt.shape
    DB, T, _ = x_sample.shape
    n_cache = cache_k.shape[2]
    p = _prep(norm_g[0], w_in[0], w_gate_up[0], b_gate[0], q_norm_g[0], k_norm_g[0], sinks[0],
              gla_norm_g[0], w_out[0])

    xp = x_prompt.reshape(B * S, D_MODEL)
    q, k, v, ga, qg, kg, vg, gg, la_hi, la_lo = _inproj(xp, p, tm=512)
    assert n_cache == WINDOW
    mix_a, mix_g, win_k_p, win_v_p, gla_p = _mix_prompt(
        q, k, v, ga, p["sink_rows"](CHUNK), qg, kg, vg, gg, la_hi, la_lo, _tril2(CHUNK), B, S,
        tq=512, group=4)
    y_p = _outproj(xp, mix_a, mix_g, p["w_out"], tm=512).reshape(B, S, D_MODEL)

    xs = x_sample.reshape(DB * T, D_MODEL)
    q, k, v, ga, qg, kg, vg, gg, la_hi, la_lo = _inproj(xs, p, tm=DB * T)
    ck = cache_k[0].reshape(DB, n_cache, KV_WIDTH)
    cv = cache_v[0].reshape(DB, n_cache, KV_WIDTH)
    mix_a, win_k_s, win_v_s = _attn_sample(q, k, v, ck, cv, ga, p["sink_rows"](T), DB, T, bb=4)
    blk = min(CHUNK, T)
    mix_g, gla_s = _gla(qg, kg, vg, gg, la_hi, la_lo, state_gla[0], _tril2(blk), DB, T,
                        blk=blk, tg=T)
    y_s = _outproj(xs, mix_a, mix_g, p["w_out"], tm=DB * T).reshape(DB, T, D_MODEL)

    shape5 = lambda a: a.reshape(1, a.shape[0], n_cache, N_KV, HEAD_DIM)
    return (y_p, y_s, shape5(win_k_p), shape5(win_v_p), gla_p[None],
            shape5(win_k_s), shape5(win_v_s), gla_s[None])
```

```python
import functools

import jax
import jax.numpy as jnp
import numpy as np
from jax import lax
from jax.experimental import pallas as pl
from jax.experimental.pallas import tpu as pltpu

D_MODEL = 2048
CHUNK = 64
WINDOW = 128
HEAD_DIM = 64
N_HEADS = 16
N_KV = 4
GQA_REP = N_HEADS // N_KV
ATT_WIDTH = N_HEADS * HEAD_DIM
KV_WIDTH = N_KV * HEAD_DIM
N_GLA = 4
GLA_DK = 128
GLA_DV = 256
GLA_KW = N_GLA * GLA_DK
GLA_WIDTH = N_GLA * GLA_DV
GATE_RANK = 16
GATE_TAU = 16.0
EPS = 1e-6
ATT_SCALE = HEAD_DIM ** -0.5
LOG2E = float(np.log2(np.e))
LOG2_DECAY_LIMIT = 100.0
WIN_CHUNKS = WINDOW // CHUNK
WIN_KEYS = WINDOW + CHUNK

LANES = 128
QK_WIDTH = ATT_WIDTH + KV_WIDTH
N_QK_GROUPS = QK_WIDTH // HEAD_DIM
OFF_Q, OFF_K, OFF_V = 0, ATT_WIDTH, ATT_WIDTH + KV_WIDTH
OFF_GA = OFF_V + KV_WIDTH
OFF_QG = OFF_GA + ATT_WIDTH
OFF_KG = OFF_QG + GLA_KW
OFF_VG = OFF_KG + GLA_KW
OFF_GG = OFF_VG + GLA_WIDTH
MAIN_COLS = OFF_GG + GLA_WIDTH

VMEM_LIMIT = 56 * 1024 * 1024

BF16 = jnp.bfloat16
F32 = jnp.float32


def _dot(a, b):
    return jnp.dot(a, b, preferred_element_type=F32)


def _dot_t(a, b):
    return lax.dot_general(a, b, (((1,), (1,)), ((), ())), preferred_element_type=F32)


def _dot_ta(a, b):
    return lax.dot_general(a, b, (((0,), (0,)), ((), ())), preferred_element_type=F32)


def _silu(x):
    return x / (1.0 + jnp.exp(-x))


def _const_spec(shape):
    nd = len(shape)
    return pl.BlockSpec(shape, lambda *_: (0,) * nd, pipeline_mode=pl.Buffered(1))


def _inproj_kernel(x_ref, ng_ref, w_ref, wlr_ref, wgu_ref, bg_ref, gqk_ref, gng_ref, gsum_ref, gexp_ref,
                   q_ref, k_ref, v_ref, ga_ref, qg_ref, kg_ref, vg_ref, gg_ref, lah_ref, lal_ref,
                   *, sub):
    for s in range(x_ref.shape[0] // sub):
        rows = slice(s * sub, (s + 1) * sub)
        x = x_ref[rows, :]
        ms = jnp.mean(x * x, axis=-1, keepdims=True)
        xn = (x * lax.rsqrt(ms + EPS) * ng_ref[...]).astype(BF16)

        qk = _dot(xn, w_ref[:, OFF_Q:OFF_V])
        ga_ref[rows, :] = _silu(_dot(xn, w_ref[:, OFF_GA:OFF_QG])).astype(BF16)
        ss = _dot((qk * qk).astype(BF16), gsum_ref[...])
        lr = _dot(xn, wlr_ref[...]).astype(BF16)
        qg_ref[rows, :] = (_dot(xn, w_ref[:, OFF_QG:OFF_KG]) * (GLA_DK ** -0.5)).astype(BF16)
        kg_ref[rows, :] = _dot(xn, w_ref[:, OFF_KG:OFF_VG]).astype(BF16)
        r = lax.rsqrt(ss * (1.0 / HEAD_DIM) + EPS)
        r_hi = r.astype(BF16)
        r_lo = (r - r_hi.astype(F32)).astype(BF16)
        r_full = _dot(jnp.concatenate([r_hi, r_lo], axis=1), gexp_ref[...])
        z = _dot(lr, wgu_ref[...]) + bg_ref[...]
        gg_ref[rows, :] = (_silu(_dot(xn, w_ref[:, OFF_GG:MAIN_COLS])) * gng_ref[...]).astype(BF16)
        vg_ref[rows, :] = _dot(xn, w_ref[:, OFF_VG:OFF_GG]).astype(BF16)
        v_ref[rows, :] = _dot(xn, w_ref[:, OFF_V:OFF_GA])

        qkn = qk * r_full * gqk_ref[...]
        q_ref[rows, :] = qkn[:, :ATT_WIDTH].astype(BF16)
        k_ref[rows, :] = qkn[:, ATT_WIDTH:]
        log_sig = jnp.minimum(z, 0.0) - jnp.log(1.0 + jnp.exp(-jnp.abs(z)))
        la = log_sig * (LOG2E / GATE_TAU)
        la_hi = la.astype(BF16)
        lah_ref[rows, :] = la_hi
        lal_ref[rows, :] = (la - la_hi.astype(F32)).astype(BF16)


def _inproj(x2d, p, tm):
    m = x2d.shape[0]
    row = lambda w: pl.BlockSpec((tm, w), lambda i: (i, 0))
    outs = [(ATT_WIDTH, BF16), (KV_WIDTH, F32), (KV_WIDTH, F32), (ATT_WIDTH, BF16),
            (GLA_KW, BF16), (GLA_KW, BF16), (GLA_WIDTH, BF16), (GLA_WIDTH, BF16),
            (GLA_KW, BF16), (GLA_KW, BF16)]
    consts = [p["norm_g"], p["w_main"], p["w_lr"], p["w_gu"], p["b_gate"], p["g_qk"], p["g_gla"],
              p["g_sum"], p["g_exp"]]
    return pl.pallas_call(
        functools.partial(_inproj_kernel, sub=min(tm, 256)),
        grid=(m // tm,),
        in_specs=[row(D_MODEL)] + [_const_spec(c.shape) for c in consts],
        out_specs=[row(w) for w, _ in outs],
        out_shape=[jax.ShapeDtypeStruct((m, w), dt) for w, dt in outs],
        compiler_params=pltpu.CompilerParams(
            dimension_semantics=("arbitrary",), vmem_limit_bytes=VMEM_LIMIT),
        name="inproj",
    )(x2d, *consts)


def _attn_prompt_kernel(q_ref, kp_ref, kc_ref, vp_ref, vc_ref, ga_ref, sink_ref,
                        o_ref, wk_ref, wv_ref, kw_ref, vw_ref, *, tq, unroll):
    assert unroll >= WIN_CHUNKS and tq % (CHUNK * unroll) == 0 and tq >= WINDOW
    i = pl.program_id(1)
    kw_ref[0:WINDOW, :] = kp_ref[0].astype(BF16)
    kw_ref[WINDOW:, :] = kc_ref[0].astype(BF16)
    vw_ref[0:WINDOW, :] = vp_ref[0].astype(BF16)
    vw_ref[WINDOW:, :] = vc_ref[0].astype(BF16)

    @pl.when(i == pl.num_programs(1) - 1)
    def _():
        wk_ref[0] = kc_ref[0, tq - WINDOW:, :]
        wv_ref[0] = vc_ref[0, tq - WINDOW:, :]

    lane = lax.broadcasted_iota(jnp.int32, (1, LANES), 1)
    lo = lane < HEAD_DIM
    col_group = lax.broadcasted_iota(jnp.int32, (1, KV_WIDTH), 1) // HEAD_DIM
    col = lax.broadcasted_iota(jnp.int32, (1, N_KV * WIN_KEYS), 1)
    key_in_seg = col - WIN_KEYS * sum((col >= g * WIN_KEYS).astype(jnp.int32) for g in range(1, N_KV))
    ones_bd = jnp.concatenate(
        [jnp.broadcast_to(jnp.where(col_group == g, 1.0, 0.0).astype(BF16), (WIN_KEYS, KV_WIDTH))
         for g in range(N_KV)], axis=0)
    neg_inf = jnp.float32(-jnp.inf)

    def chunk_group(j, n_masked):
        rows = [pl.ds(pl.multiple_of((j * unroll + u) * CHUNK, CHUNK), CHUNK) for u in range(unroll)]
        wins = [pl.ds(pl.multiple_of((j * unroll + u) * CHUNK, CHUNK), WIN_KEYS) for u in range(unroll)]
        s, v_all = [], []
        for u in range(unroll):
            kwin = kw_ref[wins[u], :]
            vwin = vw_ref[wins[u], :]
            zero = jnp.zeros_like(kwin)
            k_bd = jnp.concatenate([jnp.where(col_group == g, kwin, zero) for g in range(N_KV)], axis=0)
            v_bd = jnp.concatenate([jnp.where(col_group == g, vwin, zero) for g in range(N_KV)], axis=0)
            v_all.append(jnp.concatenate([v_bd, ones_bd], axis=1))
            qs = jnp.concatenate(
                [q_ref[rows[u], r * KV_WIDTH:(r + 1) * KV_WIDTH] for r in range(GQA_REP)], axis=0)
            s.append(_dot_t(qs, k_bd))
        p, m = [], []
        for u in range(unroll):
            su = s[u]
            if u < n_masked:
                su = jnp.where(key_in_seg >= (WIN_CHUNKS - u) * CHUNK, su, neg_inf)
            c = [su[:, LANES * n:LANES * (n + 1)] for n in range(6)]
            seg_max = [
                jnp.maximum(c[0], jnp.where(lo, c[1], neg_inf)),
                jnp.maximum(jnp.where(lo, neg_inf, c[1]), c[2]),
                jnp.maximum(c[3], jnp.where(lo, c[4], neg_inf)),
                jnp.maximum(jnp.where(lo, neg_inf, c[4]), c[5]),
            ]
            mu = [jnp.maximum(jnp.max(seg_max[g], axis=-1, keepdims=True), sink_ref[g])
                  for g in range(N_KV)]
            shift = [mu[0], jnp.where(lo, mu[0], mu[1]), mu[1], mu[2], jnp.where(lo, mu[2], mu[3]), mu[3]]
            p.append(jnp.concatenate([jnp.exp2(c[n] - shift[n]) for n in range(6)], axis=1).astype(BF16))
            m.append(mu)
        ov = [_dot(p[u], v_all[u]) for u in range(unroll)]
        for u in range(unroll):
            es = [jnp.exp2(sink_ref[g] - m[u][g]) for g in range(N_KV)]
            e_sink = jnp.concatenate([jnp.where(lo, es[0], es[1]), jnp.where(lo, es[2], es[3])], axis=1)
            o = ov[u][:, :KV_WIDTH] / (ov[u][:, KV_WIDTH:] + e_sink)
            for r in range(GQA_REP):
                cols = slice(r * KV_WIDTH, (r + 1) * KV_WIDTH)
                gate = ga_ref[rows[u], cols].astype(F32)
                o_ref[rows[u], cols] = (o[r * CHUNK:(r + 1) * CHUNK, :] * gate).astype(BF16)

    n_masked = min(WIN_CHUNKS, unroll)

    @pl.when(i == 0)
    def _():
        chunk_group(0, n_masked)

    @pl.when(i != 0)
    def _():
        chunk_group(0, 0)

    def body(j, carry):
        chunk_group(j, 0)
        return carry

    lax.fori_loop(1, tq // (CHUNK * unroll), body, 0)


def _attn_prompt(q, k, v, ga, sink_rows, batch, seq, tq, unroll):
    nblk = seq // tq
    per_win = tq // WINDOW
    rows = lambda w: pl.BlockSpec((tq, w), lambda b, i: (b * nblk + i, 0))
    cur = pl.BlockSpec((1, tq, KV_WIDTH), lambda b, i: (b, i, 0))
    prev = pl.BlockSpec((1, WINDOW, KV_WIDTH), lambda b, i: (b, jnp.maximum(i * per_win - 1, 0), 0))
    win = pl.BlockSpec((1, WINDOW, KV_WIDTH), lambda b, i: (b, 0, 0))
    k3 = k.reshape(batch, seq, KV_WIDTH)
    v3 = v.reshape(batch, seq, KV_WIDTH)
    return pl.pallas_call(
        functools.partial(_attn_prompt_kernel, tq=tq, unroll=unroll),
        grid=(batch, nblk),
        in_specs=[rows(ATT_WIDTH), prev, cur, prev, cur, rows(ATT_WIDTH), _const_spec(sink_rows.shape)],
        out_specs=[rows(ATT_WIDTH), win, win],
        out_shape=[jax.ShapeDtypeStruct((batch * seq, ATT_WIDTH), BF16),
                   jax.ShapeDtypeStruct((batch, WINDOW, KV_WIDTH), F32),
                   jax.ShapeDtypeStruct((batch, WINDOW, KV_WIDTH), F32)],
        scratch_shapes=[pltpu.VMEM((WINDOW + tq, KV_WIDTH), BF16),
                        pltpu.VMEM((WINDOW + tq, KV_WIDTH), BF16)],
        compiler_params=pltpu.CompilerParams(
            dimension_semantics=("arbitrary", "arbitrary"), vmem_limit_bytes=VMEM_LIMIT),
        name="attn_prompt",
    )(q, k3, k3, v3, v3, ga, sink_rows)


def _mix_prompt_kernel(q_ref, kp_ref, kc_ref, vp_ref, vc_ref, ga_ref, sink_ref, ones_ref,
                       qg_ref, kg_ref, vg_ref, gg_ref, lah_ref, lal_ref, tril_ref,
                       oa_ref, og_ref, wk_ref, wv_ref, sout_ref,
                       kw_ref, vw_ref, st_ref, b_ref, qx_ref, tot_ref, *, tq, group):
    n_chunks = tq // CHUNK
    assert n_chunks % group == 0 and group >= WIN_CHUNKS and tq >= WINDOW
    i = pl.program_id(1)
    last = pl.num_programs(1) - 1
    cur = i % 2
    nxt = 1 - cur

    kw_ref[0:WINDOW, :] = kp_ref[0].astype(BF16)
    kw_ref[WINDOW:, :] = kc_ref[0].astype(BF16)
    vw_ref[0:WINDOW, :] = vp_ref[0].astype(BF16)
    vw_ref[WINDOW:, :] = vc_ref[0].astype(BF16)

    @pl.when(i == 0)
    def _():
        st_ref[0] = jnp.zeros(st_ref.shape[1:], F32)

    @pl.when(i == last)
    def _():
        wk_ref[0] = kc_ref[0, tq - WINDOW:, :]
        wv_ref[0] = vc_ref[0, tq - WINDOW:, :]

    lane = lax.broadcasted_iota(jnp.int32, (1, LANES), 1)
    lo = lane < HEAD_DIM
    col_group = lax.broadcasted_iota(jnp.int32, (1, KV_WIDTH), 1) // HEAD_DIM
    col = lax.broadcasted_iota(jnp.int32, (1, N_KV * WIN_KEYS), 1)
    key_in_seg = col - WIN_KEYS * sum((col >= g * WIN_KEYS).astype(jnp.int32) for g in range(1, N_KV))
    neg_inf = jnp.float32(-jnp.inf)
    rows = [slice(c * CHUNK, (c + 1) * CHUNK) for c in range(n_chunks)]
    a_s, a_v, a_p, a_m, a_ov = {}, {}, {}, {}, {}

    def block_diag(win):
        zero = jnp.zeros_like(win)
        return jnp.concatenate([jnp.where(col_group == g, win, zero) for g in range(N_KV)], axis=0)

    def attn_scores(c):
        k_bd = block_diag(kw_ref[c * CHUNK:c * CHUNK + WIN_KEYS, :])
        v_bd = block_diag(vw_ref[c * CHUNK:c * CHUNK + WIN_KEYS, :])
        a_v[c] = jnp.concatenate([v_bd, ones_ref[...]], axis=1)
        qs = jnp.concatenate(
            [q_ref[rows[c], r * KV_WIDTH:(r + 1) * KV_WIDTH] for r in range(GQA_REP)], axis=0)
        a_s[c] = _dot_t(qs, k_bd)

    def attn_softmax(c):
        su = a_s.pop(c)
        if c < WIN_CHUNKS:
            first_valid = jnp.where(i == 0, (WIN_CHUNKS - c) * CHUNK, 0)
            su = jnp.where(key_in_seg >= first_valid, su, neg_inf)
        cs = [su[:, LANES * n:LANES * (n + 1)] for n in range(6)]
        seg_max = [
            jnp.maximum(cs[0], jnp.where(lo, cs[1], neg_inf)),
            jnp.maximum(jnp.where(lo, neg_inf, cs[1]), cs[2]),
            jnp.maximum(cs[3], jnp.where(lo, cs[4], neg_inf)),
            jnp.maximum(jnp.where(lo, neg_inf, cs[4]), cs[5]),
        ]
        mu = [jnp.maximum(jnp.max(seg_max[g], axis=-1, keepdims=True), sink_ref[g])
              for g in range(N_KV)]
        shift = [mu[0], jnp.where(lo, mu[0], mu[1]), mu[1], mu[2], jnp.where(lo, mu[2], mu[3]), mu[3]]
        a_p[c] = jnp.concatenate([jnp.exp2(cs[n] - shift[n]) for n in range(6)], axis=1).astype(BF16)
        a_m[c] = mu

    def attn_pv(c):
        a_ov[c] = _dot(a_p.pop(c), a_v.pop(c))

    def attn_finish(c):
        ov, mu = a_ov.pop(c), a_m.pop(c)
        es = [jnp.exp2(sink_ref[g] - mu[g]) for g in range(N_KV)]
        e_sink = jnp.concatenate([jnp.where(lo, es[0], es[1]), jnp.where(lo, es[2], es[3])], axis=1)
        o = ov[:, :KV_WIDTH] / (ov[:, KV_WIDTH:] + e_sink)
        for r in range(GQA_REP):
            cols = slice(r * KV_WIDTH, (r + 1) * KV_WIDTH)
            gate = ga_ref[rows[c], cols].astype(F32)
            oa_ref[rows[c], cols] = (o[r * CHUNK:(r + 1) * CHUNK, :] * gate).astype(BF16)

    rowi = lax.broadcasted_iota(jnp.int32, (CHUNK, CHUNK), 0)
    coli = lax.broadcasted_iota(jnp.int32, (CHUNK, CHUNK), 1)
    causal = rowi >= coli
    eye = (lax.broadcasted_iota(jnp.int32, (GLA_DK, GLA_DK), 0)
           == lax.broadcasted_iota(jnp.int32, (GLA_DK, GLA_DK), 1))
    heads = range(N_GLA)
    kcol = [slice(h * GLA_DK, (h + 1) * GLA_DK) for h in heads]
    vcol = [slice(h * GLA_DV, (h + 1) * GLA_DV) for h in heads]
    g_b, g_qd, g_ki, g_ke, g_dc, g_att, g_dst, g_o = {}, {}, {}, {}, {}, {}, {}, {}
    state = {}

    def cum_log2_decay(r):
        return _dot(tril_ref[...], jnp.concatenate([lah_ref[r, :], lal_ref[r, :]], axis=0))

    def decay_column(blast):
        dec_row = jnp.broadcast_to(jnp.exp2(blast), (GLA_DK, GLA_DK))
        return jnp.sum(jnp.where(eye, dec_row, 0.0), axis=-1, keepdims=True)

    def gla_finish_rows(ov, r, h):
        ms = jnp.mean(ov * ov, axis=-1, keepdims=True)
        gate = gg_ref[r, vcol[h]].astype(F32)
        og_ref[r, vcol[h]] = (ov * lax.rsqrt(ms + EPS) * gate).astype(BF16)

    def gla_cumsum(c):
        g_b[c] = cum_log2_decay(rows[c])

    def gla_prepare(c):
        for h in heads:
            b = g_b[c][:, kcol[h]]
            blast = b[CHUNK - 1:CHUNK, :]
            q = qg_ref[rows[c], kcol[h]].astype(F32)
            k = kg_ref[rows[c], kcol[h]].astype(F32)
            decay = jnp.exp2(b)
            g_qd[c, h] = (q * decay).astype(BF16)
            g_ki[c, h] = (k * (1.0 / decay)).astype(BF16)
            g_ke[c, h] = (k * jnp.exp2(blast - b)).astype(BF16)
            g_dc[c, h] = decay_column(blast)

    def gla_scores(c):
        for h in heads:
            g_att[c, h] = _dot_t(g_qd[c, h], g_ki.pop((c, h)))
            g_dst[c, h] = _dot_ta(g_ke.pop((c, h)), vg_ref[rows[c], vcol[h]])

    def gla_outputs(c):
        for h in heads:
            att = jnp.where(causal, g_att.pop((c, h)), 0.0).astype(BF16)
            g_o[c, h] = (_dot(att, vg_ref[rows[c], vcol[h]])
                         + _dot(g_qd.pop((c, h)), state[h].astype(BF16)))
            state[h] = state[h] * g_dc.pop((c, h)) + g_dst.pop((c, h))

    def gla_finish(c):
        for h in heads:
            gla_finish_rows(g_o.pop((c, h)), rows[c], h)

    def gla_exact_step():
        def chunk(c, carry):
            r = pl.ds(pl.multiple_of(c * CHUNK, CHUNK), CHUNK)
            bcum = cum_log2_decay(r)
            for h in heads:
                b = bcum[:, kcol[h]]
                blast = b[CHUNK - 1:CHUNK, :]
                q = qg_ref[r, kcol[h]].astype(F32)
                k = kg_ref[r, kcol[h]].astype(F32)
                v = vg_ref[r, vcol[h]]
                b_ref[...] = b
                qx_ref[...] = q

                def score_row(t, att_t):
                    w = jnp.where(rowi[:, 0:1] <= t, jnp.exp2(b_ref[pl.ds(t, 1), :] - b), 0.0)
                    colv = jnp.sum(qx_ref[pl.ds(t, 1), :] * k * w, axis=-1, keepdims=True)
                    return att_t + jnp.where(coli == t, colv, 0.0)

                att_t = lax.fori_loop(0, CHUNK, score_row, jnp.zeros((CHUNK, CHUNK), F32))
                q_dec = (q * jnp.exp2(b)).astype(BF16)
                k_end = (k * jnp.exp2(blast - b)).astype(BF16)
                st = st_ref[nxt, h]
                ov = _dot_ta(att_t.astype(BF16), v) + _dot(q_dec, st.astype(BF16))
                st_ref[nxt, h] = st * decay_column(blast) + _dot_ta(k_end, v)
                gla_finish_rows(ov, r, h)
            return carry

        st_ref[nxt] = st_ref[cur]
        lax.fori_loop(0, n_chunks, chunk, 0)

    for h in heads:
        state[h] = st_ref[cur, h]

    def run_group(g0):
        cs = list(range(g0, g0 + group))
        half = group // 2
        for c in cs:
            gla_cumsum(c)
        for c in cs:
            attn_scores(c)
        for c in cs:
            gla_prepare(c)
        for c in cs:
            gla_scores(c)
        for c in cs[:half]:
            attn_softmax(c)
        for c in cs[:half]:
            attn_pv(c)
        for c in cs:
            gla_outputs(c)
        for c in cs[half:]:
            attn_softmax(c)
        for c in cs[half:]:
            attn_pv(c)
        for c in cs:
            gla_finish(c)
        for c in cs:
            attn_finish(c)

    for g0 in range(0, n_chunks, group):
        run_group(g0)
    for h in heads:
        st_ref[nxt, h] = state[h]
    total = functools.reduce(jnp.minimum, [g_b[c][CHUNK - 1:CHUNK, :] for c in range(n_chunks)])

    @pl.when(jnp.min(total) < -LOG2_DECAY_LIMIT)
    def _():
        gla_exact_step()

    @pl.when(i == last)
    def _():
        sout_ref[0] = st_ref[nxt]


def _mix_prompt(q, k, v, ga, sink_rows, qg, kg, vg, gg, la_hi, la_lo, tril2, batch, seq, tq, group):
    nblk = seq // tq
    per_win = tq // WINDOW
    rows = lambda w: pl.BlockSpec((tq, w), lambda b, i: (b * nblk + i, 0))
    cur = pl.BlockSpec((1, tq, KV_WIDTH), lambda b, i: (b, i, 0))
    prev = pl.BlockSpec((1, WINDOW, KV_WIDTH), lambda b, i: (b, jnp.maximum(i * per_win - 1, 0), 0))
    win = pl.BlockSpec((1, WINDOW, KV_WIDTH), lambda b, i: (b, 0, 0))
    state = pl.BlockSpec((1, N_GLA, GLA_DK, GLA_DV), lambda b, i: (b, 0, 0, 0))
    k3 = k.reshape(batch, seq, KV_WIDTH)
    v3 = v.reshape(batch, seq, KV_WIDTH)
    ones_bd = jnp.asarray(np.kron(np.eye(N_KV), np.ones((WIN_KEYS, HEAD_DIM))), BF16)
    return pl.pallas_call(
        functools.partial(_mix_prompt_kernel, tq=tq, group=group),
        grid=(batch, nblk),
        in_specs=[rows(ATT_WIDTH), prev, cur, prev, cur, rows(ATT_WIDTH), _const_spec(sink_rows.shape),
                  _const_spec(ones_bd.shape),
                  rows(GLA_KW), rows(GLA_KW), rows(GLA_WIDTH), rows(GLA_WIDTH), rows(GLA_KW),
                  rows(GLA_KW), _const_spec(tril2.shape)],
        out_specs=[rows(ATT_WIDTH), rows(GLA_WIDTH), win, win, state],
        out_shape=[jax.ShapeDtypeStruct((batch * seq, ATT_WIDTH), BF16),
                   jax.ShapeDtypeStruct((batch * seq, GLA_WIDTH), BF16),
                   jax.ShapeDtypeStruct((batch, WINDOW, KV_WIDTH), F32),
                   jax.ShapeDtypeStruct((batch, WINDOW, KV_WIDTH), F32),
                   jax.ShapeDtypeStruct((batch, N_GLA, GLA_DK, GLA_DV), F32)],
        scratch_shapes=[pltpu.VMEM((WINDOW + tq, KV_WIDTH), BF16),
                        pltpu.VMEM((WINDOW + tq, KV_WIDTH), BF16),
                        pltpu.VMEM((2, N_GLA, GLA_DK, GLA_DV), F32),
                        pltpu.VMEM((CHUNK, GLA_DK), F32),
                        pltpu.VMEM((CHUNK, GLA_DK), F32),
                        pltpu.VMEM((1, GLA_KW), F32)],
        compiler_params=pltpu.CompilerParams(
            dimension_semantics=("arbitrary", "arbitrary"), vmem_limit_bytes=VMEM_LIMIT),
        name="mix_prompt",
    )(q, k3, k3, v3, v3, ga, sink_rows, ones_bd, qg, kg, vg, gg, la_hi, la_lo, tril2)


def _layer_prompt_kernel(q_ref, kp_ref, kc_ref, vp_ref, vc_ref, ga_ref, sink_ref, ones_ref,
                         qg_ref, kg_ref, vg_ref, gg_ref, lah_ref, lal_ref, tril_ref, x_ref, w_ref,
                         y_ref, wk_ref, wv_ref, sout_ref,
                         kw_ref, vw_ref, st_ref, b_ref, qx_ref, mix_ref, *, tq, group, nblk, n_steps):
    n_chunks = tq // CHUNK
    assert n_chunks % group == 0 and group >= WIN_CHUNKS and tq >= WINDOW
    t = pl.program_id(0)
    i = lax.rem(jnp.minimum(t, n_steps - 1), nblk)
    last = nblk - 1
    cur = i % 2
    nxt = 1 - cur
    mix_slot = t % 2
    prev_slot = 1 - mix_slot

    out_rows, out_cols = 256, 512
    out_pieces = [(slice(r, r + out_rows), slice(c, c + out_cols))
                  for r in range(0, tq, out_rows) for c in range(0, D_MODEL, out_cols)]

    def out_piece(n):
        r, c = out_pieces[n]
        y_ref[r, c] = x_ref[r, c] + _dot(mix_ref[prev_slot, r, :], w_ref[:, c])

    def mixers(do_out):
        kw_ref[0:WINDOW, :] = kp_ref[0].astype(BF16)
        kw_ref[WINDOW:, :] = kc_ref[0].astype(BF16)
        vw_ref[0:WINDOW, :] = vp_ref[0].astype(BF16)
        vw_ref[WINDOW:, :] = vc_ref[0].astype(BF16)

        @pl.when(i == 0)
        def _():
            st_ref[0] = jnp.zeros(st_ref.shape[1:], F32)

        @pl.when(i == last)
        def _():
            wk_ref[0] = kc_ref[0, tq - WINDOW:, :]
            wv_ref[0] = vc_ref[0, tq - WINDOW:, :]

        lane = lax.broadcasted_iota(jnp.int32, (1, LANES), 1)
        lo = lane < HEAD_DIM
        col_group = lax.broadcasted_iota(jnp.int32, (1, KV_WIDTH), 1) // HEAD_DIM
        col = lax.broadcasted_iota(jnp.int32, (1, N_KV * WIN_KEYS), 1)
        key_in_seg = col - WIN_KEYS * sum((col >= g * WIN_KEYS).astype(jnp.int32) for g in range(1, N_KV))
        neg_inf = jnp.float32(-jnp.inf)
        rows = [slice(c * CHUNK, (c + 1) * CHUNK) for c in range(n_chunks)]
        a_s, a_v, a_p, a_m, a_ov = {}, {}, {}, {}, {}

        def block_diag(win):
            zero = jnp.zeros_like(win)
            return jnp.concatenate([jnp.where(col_group == g, win, zero) for g in range(N_KV)], axis=0)

        def attn_scores(c):
            k_bd = block_diag(kw_ref[c * CHUNK:c * CHUNK + WIN_KEYS, :])
            v_bd = block_diag(vw_ref[c * CHUNK:c * CHUNK + WIN_KEYS, :])
            a_v[c] = jnp.concatenate([v_bd, ones_ref[...]], axis=1)
            qs = jnp.concatenate(
                [q_ref[rows[c], r * KV_WIDTH:(r + 1) * KV_WIDTH] for r in range(GQA_REP)], axis=0)
            a_s[c] = _dot_t(qs, k_bd)

        def attn_softmax(c):
            su = a_s.pop(c)
            if c < WIN_CHUNKS:
                first_valid = jnp.where(i == 0, (WIN_CHUNKS - c) * CHUNK, 0)
                su = jnp.where(key_in_seg >= first_valid, su, neg_inf)
            cs = [su[:, LANES * n:LANES * (n + 1)] for n in range(6)]
            seg_max = [
                jnp.maximum(cs[0], jnp.where(lo, cs[1], neg_inf)),
                jnp.maximum(jnp.where(lo, neg_inf, cs[1]), cs[2]),
                jnp.maximum(cs[3], jnp.where(lo, cs[4], neg_inf)),
                jnp.maximum(jnp.where(lo, neg_inf, cs[4]), cs[5]),
            ]
            mu = [jnp.maximum(jnp.max(seg_max[g], axis=-1, keepdims=True), sink_ref[g])
                  for g in range(N_KV)]
            shift = [mu[0], jnp.where(lo, mu[0], mu[1]), mu[1], mu[2], jnp.where(lo, mu[2], mu[3]), mu[3]]
            a_p[c] = jnp.concatenate([jnp.exp2(cs[n] - shift[n]) for n in range(6)], axis=1).astype(BF16)
            a_m[c] = mu

        def attn_pv(c):
            a_ov[c] = _dot(a_p.pop(c), a_v.pop(c))

        def attn_finish(c):
            ov, mu = a_ov.pop(c), a_m.pop(c)
            es = [jnp.exp2(sink_ref[g] - mu[g]) for g in range(N_KV)]
            e_sink = jnp.concatenate([jnp.where(lo, es[0], es[1]), jnp.where(lo, es[2], es[3])], axis=1)
            o = ov[:, :KV_WIDTH] / (ov[:, KV_WIDTH:] + e_sink)
            for r in range(GQA_REP):
                cols = slice(r * KV_WIDTH, (r + 1) * KV_WIDTH)
                gate = ga_ref[rows[c], cols].astype(F32)
                mix_ref[mix_slot, rows[c], cols] = (o[r * CHUNK:(r + 1) * CHUNK, :] * gate).astype(BF16)

        rowi = lax.broadcasted_iota(jnp.int32, (CHUNK, CHUNK), 0)
        coli = lax.broadcasted_iota(jnp.int32, (CHUNK, CHUNK), 1)
        causal = rowi >= coli
        eye = (lax.broadcasted_iota(jnp.int32, (GLA_DK, GLA_DK), 0)
               == lax.broadcasted_iota(jnp.int32, (GLA_DK, GLA_DK), 1))
        heads = range(N_GLA)
        kcol = [slice(h * GLA_DK, (h + 1) * GLA_DK) for h in heads]
        vcol = [slice(h * GLA_DV, (h + 1) * GLA_DV) for h in heads]
        mcol = [slice(ATT_WIDTH + h * GLA_DV, ATT_WIDTH + (h + 1) * GLA_DV) for h in heads]
        g_b, g_qd, g_ki, g_ke, g_dc, g_att, g_dst, g_o = {}, {}, {}, {}, {}, {}, {}, {}
        state = {}

        def cum_log2_decay(r):
            return _dot(tril_ref[...], jnp.concatenate([lah_ref[r, :], lal_ref[r, :]], axis=0))

        def decay_column(blast):
            dec_row = jnp.broadcast_to(jnp.exp2(blast), (GLA_DK, GLA_DK))
            return jnp.sum(jnp.where(eye, dec_row, 0.0), axis=-1, keepdims=True)

        def gla_finish_rows(ov, r, h):
            ms = jnp.mean(ov * ov, axis=-1, keepdims=True)
            gate = gg_ref[r, vcol[h]].astype(F32)
            mix_ref[mix_slot, r, mcol[h]] = (ov * lax.rsqrt(ms + EPS) * gate).astype(BF16)

        def gla_cumsum(c):
            g_b[c] = cum_log2_decay(rows[c])

        def gla_prepare(c):
            for h in heads:
                b = g_b[c][:, kcol[h]]
                blast = b[CHUNK - 1:CHUNK, :]
                q = qg_ref[rows[c], kcol[h]].astype(F32)
                k = kg_ref[rows[c], kcol[h]].astype(F32)
                decay = jnp.exp2(b)
                g_qd[c, h] = (q * decay).astype(BF16)
                g_ki[c, h] = (k * (1.0 / decay)).astype(BF16)
                g_ke[c, h] = (k * jnp.exp2(blast - b)).astype(BF16)
                g_dc[c, h] = decay_column(blast)

        def gla_scores(c):
            for h in heads:
                g_att[c, h] = _dot_t(g_qd[c, h], g_ki.pop((c, h)))
                g_dst[c, h] = _dot_ta(g_ke.pop((c, h)), vg_ref[rows[c], vcol[h]])

        def gla_outputs(c):
            for h in heads:
                att = jnp.where(causal, g_att.pop((c, h)), 0.0).astype(BF16)
                g_o[c, h] = (_dot(att, vg_ref[rows[c], vcol[h]])
                             + _dot(g_qd.pop((c, h)), state[h].astype(BF16)))
                state[h] = state[h] * g_dc.pop((c, h)) + g_dst.pop((c, h))

        def gla_finish(c):
            for h in heads:
                gla_finish_rows(g_o.pop((c, h)), rows[c], h)

        def gla_exact_step():
            def chunk(c, carry):
                r = pl.ds(pl.multiple_of(c * CHUNK, CHUNK), CHUNK)
                bcum = cum_log2_decay(r)
                for h in heads:
                    b = bcum[:, kcol[h]]
                    blast = b[CHUNK - 1:CHUNK, :]
                    q = qg_ref[r, kcol[h]].astype(F32)
                    k = kg_ref[r, kcol[h]].astype(F32)
                    v = vg_ref[r, vcol[h]]
                    b_ref[...] = b
                    qx_ref[...] = q

                    def score_row(tok, att_t):
                        w = jnp.where(rowi[:, 0:1] <= tok, jnp.exp2(b_ref[pl.ds(tok, 1), :] - b), 0.0)
                        colv = jnp.sum(qx_ref[pl.ds(tok, 1), :] * k * w, axis=-1, keepdims=True)
                        return att_t + jnp.where(coli == tok, colv, 0.0)

                    att_t = lax.fori_loop(0, CHUNK, score_row, jnp.zeros((CHUNK, CHUNK), F32))
                    q_dec = (q * jnp.exp2(b)).astype(BF16)
                    k_end = (k * jnp.exp2(blast - b)).astype(BF16)
                    st = st_ref[nxt, h]
                    ov = _dot_ta(att_t.astype(BF16), v) + _dot(q_dec, st.astype(BF16))
                    st_ref[nxt, h] = st * decay_column(blast) + _dot_ta(k_end, v)
                    gla_finish_rows(ov, r, h)
                return carry

            st_ref[nxt] = st_ref[cur]
            lax.fori_loop(0, n_chunks, chunk, 0)

        pending = list(range(len(out_pieces))) if do_out else []

        def out_some(n):
            for _ in range(min(n, len(pending))):
                out_piece(pending.pop(0))

        per_slot = -(-len(out_pieces) // (4 * (n_chunks // group)))
        for h in heads:
            state[h] = st_ref[cur, h]
        for g0 in range(0, n_chunks, group):
            cs = list(range(g0, g0 + group))
            half = group // 2
            for c in cs:
                gla_cumsum(c)
            out_some(per_slot)
            for c in cs:
                attn_scores(c)
            out_some(per_slot)
            for c in cs:
                gla_prepare(c)
            for c in cs:
                gla_scores(c)
            out_some(per_slot)
            for c in cs[:half]:
                attn_softmax(c)
            for c in cs[:half]:
                attn_pv(c)
            out_some(per_slot)
            for c in cs:
                gla_outputs(c)
            for c in cs[half:]:
                attn_softmax(c)
            for c in cs[half:]:
                attn_pv(c)
            for c in cs:
                gla_finish(c)
            for c in cs:
                attn_finish(c)
        out_some(len(pending))
        for h in heads:
            st_ref[nxt, h] = state[h]
        total = functools.reduce(jnp.minimum, [g_b[c][CHUNK - 1:CHUNK, :] for c in range(n_chunks)])

        @pl.when(jnp.min(total) < -LOG2_DECAY_LIMIT)
        def _():
            gla_exact_step()

        @pl.when(i == last)
        def _():
            sout_ref[0] = st_ref[nxt]

    @pl.when(t == 0)
    def _():
        mixers(do_out=False)

    @pl.when((t > 0) & (t < n_steps))
    def _():
        mixers(do_out=True)

    @pl.when(t == n_steps)
    def _():
        for n in range(len(out_pieces)):
            out_piece(n)


def _layer_prompt(x2d, q, k, v, ga, sink_rows, qg, kg, vg, gg, la_hi, la_lo, tril2, w_out, batch, seq,
                  tq, group):
    nblk = seq // tq
    n_steps = batch * nblk
    per_win = tq // WINDOW
    blk_of = lambda t: jnp.minimum(t, n_steps - 1)
    rows = lambda w: pl.BlockSpec((tq, w), lambda t: (blk_of(t), 0))
    late = lambda w: pl.BlockSpec((tq, w), lambda t: (jnp.maximum(t - 1, 0), 0))
    cur = pl.BlockSpec((1, tq, KV_WIDTH), lambda t: (blk_of(t) // nblk, blk_of(t) % nblk, 0))
    prev = pl.BlockSpec((1, WINDOW, KV_WIDTH),
                        lambda t: (blk_of(t) // nblk, jnp.maximum((blk_of(t) % nblk) * per_win - 1, 0), 0))
    win = pl.BlockSpec((1, WINDOW, KV_WIDTH), lambda t: (blk_of(t) // nblk, 0, 0))
    state = pl.BlockSpec((1, N_GLA, GLA_DK, GLA_DV), lambda t: (blk_of(t) // nblk, 0, 0, 0))
    k3 = k.reshape(batch, seq, KV_WIDTH)
    v3 = v.reshape(batch, seq, KV_WIDTH)
    ones_bd = jnp.asarray(np.kron(np.eye(N_KV), np.ones((WIN_KEYS, HEAD_DIM))), BF16)
    return pl.pallas_call(
        functools.partial(_layer_prompt_kernel, tq=tq, group=group, nblk=nblk, n_steps=n_steps),
        grid=(n_steps + 1,),
        in_specs=[rows(ATT_WIDTH), prev, cur, prev, cur, rows(ATT_WIDTH), _const_spec(sink_rows.shape),
                  _const_spec(ones_bd.shape),
                  rows(GLA_KW), rows(GLA_KW), rows(GLA_WIDTH), rows(GLA_WIDTH), rows(GLA_KW),
                  rows(GLA_KW), _const_spec(tril2.shape), late(D_MODEL), _const_spec(w_out.shape)],
        out_specs=[late(D_MODEL), win, win, state],
        out_shape=[jax.ShapeDtypeStruct((batch * seq, D_MODEL), F32),
                   jax.ShapeDtypeStruct((batch, WINDOW, KV_WIDTH), F32),
                   jax.ShapeDtypeStruct((batch, WINDOW, KV_WIDTH), F32),
                   jax.ShapeDtypeStruct((batch, N_GLA, GLA_DK, GLA_DV), F32)],
        scratch_shapes=[pltpu.VMEM((WINDOW + tq, KV_WIDTH), BF16),
                        pltpu.VMEM((WINDOW + tq, KV_WIDTH), BF16),
                        pltpu.VMEM((2, N_GLA, GLA_DK, GLA_DV), F32),
                        pltpu.VMEM((CHUNK, GLA_DK), F32),
                        pltpu.VMEM((CHUNK, GLA_DK), F32),
                        pltpu.VMEM((2, tq, D_MODEL), BF16)],
        compiler_params=pltpu.CompilerParams(
            dimension_semantics=("arbitrary",), vmem_limit_bytes=VMEM_LIMIT),
        name="layer_prompt",
    )(q, k3, k3, v3, v3, ga, sink_rows, ones_bd, qg, kg, vg, gg, la_hi, la_lo, tril2, x2d, w_out)


def _attn_sample_kernel(q_ref, kn_ref, vn_ref, ck_ref, cv_ref, ga_ref, sink_ref,
                        o_ref, wk_ref, wv_ref, *, t, bb):
    n_cache = ck_ref.shape[1]
    assert n_cache == LANES and t <= LANES
    lane = lax.broadcasted_iota(jnp.int32, (1, LANES), 1)
    lo = lane < HEAD_DIM
    is_new = lane < t
    col_group = lax.broadcasted_iota(jnp.int32, (1, KV_WIDTH), 1) // HEAD_DIM
    seg = 2 * LANES
    ones_bd = jnp.concatenate(
        [jnp.broadcast_to(jnp.where(col_group == g, 1.0, 0.0).astype(BF16), (seg, KV_WIDTH))
         for g in range(N_KV)], axis=0)
    pad = jnp.zeros((LANES - t, KV_WIDTH), BF16)
    neg_inf = jnp.float32(-jnp.inf)

    s, v_all = [], []
    for b in range(bb):
        rows = slice(b * t, (b + 1) * t)
        kn, vn, ck, cv = kn_ref[rows, :], vn_ref[rows, :], ck_ref[b], cv_ref[b]
        wk_ref[b] = jnp.concatenate([ck[t:, :], kn], axis=0)
        wv_ref[b] = jnp.concatenate([cv[t:, :], vn], axis=0)
        kcat = jnp.concatenate([ck.astype(BF16), kn.astype(BF16), pad], axis=0)
        vcat = jnp.concatenate([cv.astype(BF16), vn.astype(BF16), pad], axis=0)
        zero = jnp.zeros_like(kcat)
        k_bd = jnp.concatenate([jnp.where(col_group == g, kcat, zero) for g in range(N_KV)], axis=0)
        v_bd = jnp.concatenate([jnp.where(col_group == g, vcat, zero) for g in range(N_KV)], axis=0)
        v_all.append(jnp.concatenate([v_bd, ones_bd], axis=1))
        qs = jnp.concatenate(
            [q_ref[rows, r * KV_WIDTH:(r + 1) * KV_WIDTH] for r in range(GQA_REP)], axis=0)
        s.append(_dot_t(qs, k_bd))
    p, m = [], []
    for b in range(bb):
        c = [s[b][:, LANES * n:LANES * (n + 1)] for n in range(2 * N_KV)]
        c = [cn if n % 2 == 0 else jnp.where(is_new, cn, neg_inf) for n, cn in enumerate(c)]
        mb = [jnp.maximum(jnp.max(jnp.maximum(c[2 * g], c[2 * g + 1]), axis=-1, keepdims=True),
                          sink_ref[g]) for g in range(N_KV)]
        p.append(jnp.concatenate([jnp.exp2(c[n] - mb[n // 2]) for n in range(2 * N_KV)],
                                 axis=1).astype(BF16))
        m.append(mb)
    ov = [_dot(p[b], v_all[b]) for b in range(bb)]
    for b in range(bb):
        rows = slice(b * t, (b + 1) * t)
        es = [jnp.exp2(sink_ref[g] - m[b][g]) for g in range(N_KV)]
        e_sink = jnp.concatenate([jnp.where(lo, es[0], es[1]), jnp.where(lo, es[2], es[3])], axis=1)
        o = ov[b][:, :KV_WIDTH] / (ov[b][:, KV_WIDTH:] + e_sink)
        for r in range(GQA_REP):
            cols = slice(r * KV_WIDTH, (r + 1) * KV_WIDTH)
            gate = ga_ref[rows, cols].astype(F32)
            o_ref[rows, cols] = (o[r * t:(r + 1) * t, :] * gate).astype(BF16)


def _attn_sample(q, k, v, cache_k, cache_v, ga, sink_rows, batch, t, bb):
    n_cache = cache_k.shape[1]
    rows = lambda w: pl.BlockSpec((bb * t, w), lambda b: (b, 0))
    cache = pl.BlockSpec((bb, n_cache, KV_WIDTH), lambda b: (b, 0, 0))
    return pl.pallas_call(
        functools.partial(_attn_sample_kernel, t=t, bb=bb),
        grid=(batch // bb,),
        in_specs=[rows(ATT_WIDTH), rows(KV_WIDTH), rows(KV_WIDTH), cache, cache, rows(ATT_WIDTH),
                  _const_spec(sink_rows.shape)],
        out_specs=[rows(ATT_WIDTH), cache, cache],
        out_shape=[jax.ShapeDtypeStruct((batch * t, ATT_WIDTH), BF16),
                   jax.ShapeDtypeStruct((batch, n_cache, KV_WIDTH), F32),
                   jax.ShapeDtypeStruct((batch, n_cache, KV_WIDTH), F32)],
        compiler_params=pltpu.CompilerParams(
            dimension_semantics=("arbitrary",), vmem_limit_bytes=VMEM_LIMIT),
        name="attn_sample",
    )(q, k, v, cache_k, cache_v, ga, sink_rows)


def _gla_kernel(qg_ref, kg_ref, vg_ref, gg_ref, lah_ref, lal_ref, s0_ref, tril_ref,
                o_ref, sout_ref, st_ref, b_ref, q_ref, *, blk, tg, unroll):
    i = pl.program_id(1)
    n_groups = tg // (blk * unroll)

    @pl.when(i == 0)
    def _():
        st_ref[0] = s0_ref[0]

    rowi = lax.broadcasted_iota(jnp.int32, (blk, blk), 0)
    coli = lax.broadcasted_iota(jnp.int32, (blk, blk), 1)
    causal = rowi >= coli
    eye = (lax.broadcasted_iota(jnp.int32, (GLA_DK, GLA_DK), 0)
           == lax.broadcasted_iota(jnp.int32, (GLA_DK, GLA_DK), 1))

    heads = range(N_GLA)
    kcol = [slice(h * GLA_DK, (h + 1) * GLA_DK) for h in heads]
    vcol = [slice(h * GLA_DV, (h + 1) * GLA_DV) for h in heads]

    def cum_log2_decay(r):
        return _dot(tril_ref[...], jnp.concatenate([lah_ref[r, :], lal_ref[r, :]], axis=0))

    def decay_column(blast):
        dec_row = jnp.broadcast_to(jnp.exp2(blast), (GLA_DK, GLA_DK))
        return jnp.sum(jnp.where(eye, dec_row, 0.0), axis=-1, keepdims=True)

    def finish(ov, r, h):
        ms = jnp.mean(ov * ov, axis=-1, keepdims=True)
        gate = gg_ref[r, vcol[h]].astype(F32)
        o_ref[r, vcol[h]] = (ov * lax.rsqrt(ms + EPS) * gate).astype(BF16)

    def exact_group(rows, cur, nxt):
        st = [st_ref[cur, h] for h in heads]
        for r in rows:
            bcum = cum_log2_decay(r)
            for h in heads:
                b = bcum[:, kcol[h]]
                blast = b[blk - 1:blk, :]
                q = qg_ref[r, kcol[h]].astype(F32)
                k = kg_ref[r, kcol[h]].astype(F32)
                v = vg_ref[r, vcol[h]]
                b_ref[0:blk, :] = b
                q_ref[0:blk, :] = q

                def score_row(t, att_t):
                    w = jnp.where(rowi[:, 0:1] <= t, jnp.exp2(b_ref[pl.ds(t, 1), :] - b), 0.0)
                    col = jnp.sum(q_ref[pl.ds(t, 1), :] * k * w, axis=-1, keepdims=True)
                    return att_t + jnp.where(coli == t, col, 0.0)

                att_t = lax.fori_loop(0, blk, score_row, jnp.zeros((blk, blk), F32))
                q_dec = (q * jnp.exp2(b)).astype(BF16)
                k_end = (k * jnp.exp2(blast - b)).astype(BF16)
                ov = _dot_ta(att_t.astype(BF16), v) + _dot(q_dec, st[h].astype(BF16))
                st[h] = st[h] * decay_column(blast) + _dot_ta(k_end, v)
                finish(ov, r, h)
        for h in heads:
            st_ref[nxt, h] = st[h]

    def group(j, carry):
        rows = [pl.ds(pl.multiple_of((j * unroll + u) * blk, blk), blk) for u in range(unroll)]
        cur = (i * n_groups + j) % 2
        nxt = 1 - cur
        bcum = [cum_log2_decay(r) for r in rows]
        q_dec, k_inv, k_end, dec_col = {}, {}, {}, {}
        for u, r in enumerate(rows):
            for h in heads:
                b = bcum[u][:, kcol[h]]
                blast = b[blk - 1:blk, :]
                q = qg_ref[r, kcol[h]].astype(F32)
                k = kg_ref[r, kcol[h]].astype(F32)
                decay = jnp.exp2(b)
                q_dec[u, h] = (q * decay).astype(BF16)
                k_inv[u, h] = (k * (1.0 / decay)).astype(BF16)
                k_end[u, h] = (k * jnp.exp2(blast - b)).astype(BF16)
                dec_col[u, h] = decay_column(blast)
        att = {(u, h): _dot_t(q_dec[u, h], k_inv[u, h]) for u in range(unroll) for h in heads}
        d_st = {(u, h): _dot_ta(k_end[u, h], vg_ref[rows[u], vcol[h]])
                for u in range(unroll) for h in heads}
        att = {key: jnp.where(causal, a, 0.0).astype(BF16) for key, a in att.items()}
        st = [st_ref[cur, h] for h in heads]
        o = {}
        for u, r in enumerate(rows):
            for h in heads:
                o[u, h] = _dot(att[u, h], vg_ref[r, vcol[h]]) + _dot(q_dec[u, h], st[h].astype(BF16))
                st[h] = st[h] * dec_col[u, h] + d_st[u, h]
        for h in heads:
            st_ref[nxt, h] = st[h]
        for u, r in enumerate(rows):
            for h in heads:
                finish(o[u, h], r, h)

        total = functools.reduce(jnp.minimum, [bc[blk - 1:blk, :] for bc in bcum])

        @pl.when(jnp.min(total) < -LOG2_DECAY_LIMIT)
        def _():
            exact_group(rows, cur, nxt)
        return carry

    lax.fori_loop(0, n_groups, group, 0)

    @pl.when(i == pl.num_programs(1) - 1)
    def _():
        sout_ref[0] = st_ref[(pl.num_programs(1) * n_groups) % 2]


def _gla(qg, kg, vg, gg, la_hi, la_lo, s0, tril2, batch, seq, blk, tg, unroll=1):
    nblk = seq // tg
    rows = lambda w: pl.BlockSpec((tg, w), lambda b, i: (b * nblk + i, 0))
    state = pl.BlockSpec((1, N_GLA, GLA_DK, GLA_DV), lambda b, i: (b, 0, 0, 0))
    return pl.pallas_call(
        functools.partial(_gla_kernel, blk=blk, tg=tg, unroll=unroll),
        grid=(batch, nblk),
        in_specs=[rows(GLA_KW), rows(GLA_KW), rows(GLA_WIDTH), rows(GLA_WIDTH), rows(GLA_KW),
                  rows(GLA_KW), state, _const_spec(tril2.shape)],
        out_specs=[rows(GLA_WIDTH), state],
        out_shape=[jax.ShapeDtypeStruct((batch * seq, GLA_WIDTH), BF16),
                   jax.ShapeDtypeStruct((batch, N_GLA, GLA_DK, GLA_DV), F32)],
        scratch_shapes=[pltpu.VMEM((2, N_GLA, GLA_DK, GLA_DV), F32),
                        pltpu.VMEM((blk, GLA_DK), F32),
                        pltpu.VMEM((blk, GLA_DK), F32)],
        compiler_params=pltpu.CompilerParams(
            dimension_semantics=("arbitrary", "arbitrary"), vmem_limit_bytes=VMEM_LIMIT),
        name="gla",
    )(qg, kg, vg, gg, la_hi, la_lo, s0, tril2)


def _outproj_kernel(x_ref, ma_ref, mg_ref, w_ref, y_ref):
    y_ref[...] = (x_ref[...] + _dot(ma_ref[...], w_ref[:ATT_WIDTH, :])
                  + _dot(mg_ref[...], w_ref[ATT_WIDTH:, :]))


def _outproj(x2d, mix_a, mix_g, w, tm):
    m = x2d.shape[0]
    row = lambda w: pl.BlockSpec((tm, w), lambda i: (i, 0))
    return pl.pallas_call(
        _outproj_kernel,
        grid=(m // tm,),
        in_specs=[row(D_MODEL), row(ATT_WIDTH), row(GLA_WIDTH), _const_spec(w.shape)],
        out_specs=row(D_MODEL),
        out_shape=jax.ShapeDtypeStruct((m, D_MODEL), F32),
        compiler_params=pltpu.CompilerParams(
            dimension_semantics=("arbitrary",), vmem_limit_bytes=VMEM_LIMIT),
        name="outproj",
    )(x2d, mix_a, mix_g, w)


def _tril2(blk):
    t = np.tril(np.ones((blk, blk), np.float32))
    return jnp.asarray(np.concatenate([t, t], axis=1), BF16)


def _prep_w_in_kernel(u0_ref, u1_ref, u2_ref, u3_ref, lr_ref, main_ref, wlr_ref):
    units = jnp.concatenate([u0_ref[...], u1_ref[...], u2_ref[...], u3_ref[...]], axis=0)
    main_ref[...] = units.T.astype(BF16)

    @pl.when(pl.program_id(0) == 0)
    def _():
        lr = lr_ref[...]
        padded = jnp.concatenate([lr, jnp.zeros((LANES - lr.shape[0], lr.shape[1]), F32)], axis=0)
        wlr_ref[...] = padded.T.astype(BF16)


def _prep_w_in(w_in_t):
    n_cols, k = w_in_t.shape
    units_per_block = KV_WIDTH // HEAD_DIM

    def src_unit(j, p):
        u = j * units_per_block + p
        def regroup(v):
            return (v % N_KV) * GQA_REP + v // N_KV
        q0, ga0 = OFF_Q // HEAD_DIM, OFF_GA // HEAD_DIM
        n_att = ATT_WIDTH // HEAD_DIM
        u = jnp.where((u >= q0) & (u < q0 + n_att), q0 + regroup(u - q0), u)
        return jnp.where((u >= ga0) & (u < ga0 + n_att), ga0 + regroup(u - ga0), u)

    unit = lambda p: pl.BlockSpec((HEAD_DIM, k), lambda j: (src_unit(j, p), 0))
    return pl.pallas_call(
        _prep_w_in_kernel,
        grid=(MAIN_COLS // KV_WIDTH,),
        in_specs=[unit(0), unit(1), unit(2), unit(3),
                  pl.BlockSpec((GATE_RANK, k), lambda j: (MAIN_COLS // GATE_RANK, 0))],
        out_specs=[pl.BlockSpec((k, KV_WIDTH), lambda j: (0, j)),
                   pl.BlockSpec((k, LANES), lambda j: (0, 0))],
        out_shape=[jax.ShapeDtypeStruct((k, MAIN_COLS), BF16),
                   jax.ShapeDtypeStruct((k, LANES), BF16)],
        compiler_params=pltpu.CompilerParams(
            dimension_semantics=("arbitrary",), vmem_limit_bytes=VMEM_LIMIT),
        name="prep_w_in",
    )(w_in_t, w_in_t, w_in_t, w_in_t, w_in_t)


def _prep_w_out_kernel(w_ref, o_ref):
    o_ref[...] = w_ref[...].astype(BF16)


def _prep_w_out(w_out):
    n_att = ATT_WIDTH // HEAD_DIM

    def src_block(j):
        regrouped = (j % N_KV) * GQA_REP + j // N_KV
        return jnp.where(j < n_att, regrouped, j)

    rows, cols = w_out.shape
    return pl.pallas_call(
        _prep_w_out_kernel,
        grid=(rows // HEAD_DIM,),
        in_specs=[pl.BlockSpec((HEAD_DIM, cols), lambda j: (src_block(j), 0))],
        out_specs=pl.BlockSpec((HEAD_DIM, cols), lambda j: (j, 0)),
        out_shape=jax.ShapeDtypeStruct((rows, cols), BF16),
        compiler_params=pltpu.CompilerParams(dimension_semantics=("arbitrary",)),
        name="prep_w_out",
    )(w_out)


def _prep(norm_g, w_in, w_gate_up, b_gate, q_norm_g, k_norm_g, sinks, gla_norm_g, w_out):
    w_main, w_lr = _prep_w_in(w_in.T)
    w_gu = jnp.pad(w_gate_up, ((0, LANES - GATE_RANK), (0, 0))).astype(BF16)
    g_qk = jnp.concatenate([jnp.tile(q_norm_g, N_HEADS) * (ATT_SCALE * LOG2E),
                            jnp.tile(k_norm_g, N_KV)])[None, :]

    grp = np.arange(QK_WIDTH) // HEAD_DIM
    ind = (grp[:, None] == np.arange(LANES)[None, :]).astype(np.float32)
    g_sum = jnp.asarray(ind, BF16)
    g_exp = jnp.asarray(np.concatenate([ind.T, ind.T], axis=0), BF16)

    w_out_b = _prep_w_out(w_out)

    def sink_rows(t):
        sk = sinks.astype(F32).reshape(N_KV, GQA_REP) * LOG2E
        return jnp.broadcast_to(sk[:, :, None, None], (N_KV, GQA_REP, t, LANES)).reshape(
            N_KV, GQA_REP * t, LANES)

    return dict(norm_g=norm_g[None, :], w_main=w_main, w_lr=w_lr, w_gu=w_gu, b_gate=b_gate[None, :],
                g_qk=g_qk, g_gla=jnp.tile(gla_norm_g, N_GLA)[None, :], g_sum=g_sum, g_exp=g_exp,
                w_out=w_out_b, sink_rows=sink_rows)


def kernel(x_prompt, x_sample, cache_k, cache_v, state_gla, norm_g, w_in, w_gate_up, b_gate,
           q_norm_g, k_norm_g, sinks, gla_norm_g, w_out):
    depth = norm_g.shape[0]
    assert depth == 1
    B, S, _ = x_prompt.shape
    DB, T, _ = x_sample.shape
    n_cache = cache_k.shape[2]
    p = _prep(norm_g[0], w_in[0], w_gate_up[0], b_gate[0], q_norm_g[0], k_norm_g[0], sinks[0],
              gla_norm_g[0], w_out[0])

    xp = x_prompt.reshape(B * S, D_MODEL)
    q, k, v, ga, qg, kg, vg, gg, la_hi, la_lo = _inproj(xp, p, tm=512)
    assert n_cache == WINDOW
    y_p, win_k_p, win_v_p, gla_p = _layer_prompt(
        xp, q, k, v, ga, p["sink_rows"](CHUNK), qg, kg, vg, gg, la_hi, la_lo, _tril2(CHUNK), p["w_out"],
        B, S, tq=512, group=4)
    y_p = y_p.reshape(B, S, D_MODEL)

    xs = x_sample.reshape(DB * T, D_MODEL)
    q, k, v, ga, qg, kg, vg, gg, la_hi, la_lo = _inproj(xs, p, tm=DB * T)
    ck = cache_k[0].reshape(DB, n_cache, KV_WIDTH)
    cv = cache_v[0].reshape(DB, n_cache, KV_WIDTH)
    mix_a, win_k_s, win_v_s = _attn_sample(q, k, v, ck, cv, ga, p["sink_rows"](T), DB, T, bb=4)
    blk = min(CHUNK, T)
    mix_g, gla_s = _gla(qg, kg, vg, gg, la_hi, la_lo, state_gla[0], _tril2(blk), DB, T,
                        blk=blk, tg=T)
    y_s = _outproj(xs, mix_a, mix_g, p["w_out"], tm=DB * T).reshape(DB, T, D_MODEL)

    shape5 = lambda a: a.reshape(1, a.shape[0], n_cache, N_KV, HEAD_DIM)
    return (y_p, y_s, shape5(win_k_p), shape5(win_v_p), gla_p[None],
            shape5(win_k_s), shape5(win_v_s), gla_s[None])
```

```python
import functools

import jax
import jax.numpy as jnp
import numpy as np
from jax import lax
from jax.experimental import pallas as pl
from jax.experimental.pallas import tpu as pltpu

D_MODEL = 2048
CHUNK = 64
WINDOW = 128
HEAD_DIM = 64
N_HEADS = 16
N_KV = 4
GQA_REP = N_HEADS // N_KV
ATT_WIDTH = N_HEADS * HEAD_DIM
KV_WIDTH = N_KV * HEAD_DIM
N_GLA = 4
GLA_DK = 128
GLA_DV = 256
GLA_KW = N_GLA * GLA_DK
GLA_WIDTH = N_GLA * GLA_DV
GATE_RANK = 16
GATE_TAU = 16.0
EPS = 1e-6
ATT_SCALE = HEAD_DIM ** -0.5
LOG2E = float(np.log2(np.e))
LOG2_DECAY_LIMIT = 100.0
WIN_CHUNKS = WINDOW // CHUNK
WIN_KEYS = WINDOW + CHUNK

LANES = 128
QK_WIDTH = ATT_WIDTH + KV_WIDTH
OFF_Q, OFF_K, OFF_V = 0, ATT_WIDTH, ATT_WIDTH + KV_WIDTH
OFF_GA = OFF_V + KV_WIDTH
OFF_QG = OFF_GA + ATT_WIDTH
OFF_KG = OFF_QG + GLA_KW
OFF_VG = OFF_KG + GLA_KW
OFF_GG = OFF_VG + GLA_WIDTH
MAIN_COLS = OFF_GG + GLA_WIDTH

VMEM_LIMIT = 56 * 1024 * 1024
INPROJ_ROWS = 512
INPROJ_SUB = 256
LAYER_ROWS = 512
LAYER_GROUP = 4
SAMPLE_STREAMS = 4

BF16 = jnp.bfloat16
F32 = jnp.float32


def _dot(a, b):
    return jnp.dot(a, b, preferred_element_type=F32)


def _dot_t(a, b):
    return lax.dot_general(a, b, (((1,), (1,)), ((), ())), preferred_element_type=F32)


def _dot_ta(a, b):
    return lax.dot_general(a, b, (((0,), (0,)), ((), ())), preferred_element_type=F32)


def _silu(x):
    return x / (1.0 + jnp.exp(-x))


def _const_spec(shape):
    nd = len(shape)
    return pl.BlockSpec(shape, lambda *_: (0,) * nd, pipeline_mode=pl.Buffered(1))


def _inproj_kernel(x_ref, ng_ref, w_ref, wlr_ref, wgu_ref, bg_ref, gqk_ref, gng_ref, gsum_ref, gexp_ref,
                   q_ref, k_ref, v_ref, ga_ref, qg_ref, kg_ref, vg_ref, gg_ref, lah_ref, lal_ref,
                   *, sub):
    for s in range(x_ref.shape[0] // sub):
        rows = slice(s * sub, (s + 1) * sub)
        x = x_ref[rows, :]
        ms = jnp.mean(x * x, axis=-1, keepdims=True)
        xn = (x * lax.rsqrt(ms + EPS) * ng_ref[...]).astype(BF16)

        qk = _dot(xn, w_ref[:, OFF_Q:OFF_V])
        ga_ref[rows, :] = _silu(_dot(xn, w_ref[:, OFF_GA:OFF_QG])).astype(BF16)
        ss = _dot((qk * qk).astype(BF16), gsum_ref[...])
        lr = _dot(xn, wlr_ref[...]).astype(BF16)
        qg_ref[rows, :] = (_dot(xn, w_ref[:, OFF_QG:OFF_KG]) * (GLA_DK ** -0.5)).astype(BF16)
        kg_ref[rows, :] = _dot(xn, w_ref[:, OFF_KG:OFF_VG]).astype(BF16)
        r = lax.rsqrt(ss * (1.0 / HEAD_DIM) + EPS)
        r_hi = r.astype(BF16)
        r_lo = (r - r_hi.astype(F32)).astype(BF16)
        r_full = _dot(jnp.concatenate([r_hi, r_lo], axis=1), gexp_ref[...])
        z = _dot(lr, wgu_ref[...]) + bg_ref[...]
        gg_ref[rows, :] = (_silu(_dot(xn, w_ref[:, OFF_GG:MAIN_COLS])) * gng_ref[...]).astype(BF16)
        vg_ref[rows, :] = _dot(xn, w_ref[:, OFF_VG:OFF_GG]).astype(BF16)
        v_ref[rows, :] = _dot(xn, w_ref[:, OFF_V:OFF_GA])

        qkn = qk * r_full * gqk_ref[...]
        q_ref[rows, :] = qkn[:, :ATT_WIDTH].astype(BF16)
        k_ref[rows, :] = qkn[:, ATT_WIDTH:]
        log_sig = jnp.minimum(z, 0.0) - jnp.log(1.0 + jnp.exp(-jnp.abs(z)))
        la = log_sig * (LOG2E / GATE_TAU)
        la_hi = la.astype(BF16)
        lah_ref[rows, :] = la_hi
        lal_ref[rows, :] = (la - la_hi.astype(F32)).astype(BF16)


def _inproj(x2d, p, tm):
    m = x2d.shape[0]
    row = lambda w: pl.BlockSpec((tm, w), lambda i: (i, 0))
    outs = [(ATT_WIDTH, BF16), (KV_WIDTH, F32), (KV_WIDTH, F32), (ATT_WIDTH, BF16),
            (GLA_KW, BF16), (GLA_KW, BF16), (GLA_WIDTH, BF16), (GLA_WIDTH, BF16),
            (GLA_KW, BF16), (GLA_KW, BF16)]
    consts = [p["norm_g"], p["w_main"], p["w_lr"], p["w_gu"], p["b_gate"], p["g_qk"], p["g_gla"],
              p["g_sum"], p["g_exp"]]
    return pl.pallas_call(
        functools.partial(_inproj_kernel, sub=min(tm, INPROJ_SUB)),
        grid=(m // tm,),
        in_specs=[row(D_MODEL)] + [_const_spec(c.shape) for c in consts],
        out_specs=[row(w) for w, _ in outs],
        out_shape=[jax.ShapeDtypeStruct((m, w), dt) for w, dt in outs],
        compiler_params=pltpu.CompilerParams(
            dimension_semantics=("arbitrary",), vmem_limit_bytes=VMEM_LIMIT),
        name="inproj",
    )(x2d, *consts)


def _layer_prompt_kernel(q_ref, kp_ref, kc_ref, vp_ref, vc_ref, ga_ref, sink_ref,
                         qg_ref, kg_ref, vg_ref, gg_ref, lah_ref, lal_ref, tril_ref, x_ref, w_ref,
                         y_ref, wk_ref, wv_ref, sout_ref,
                         kw_ref, vw_ref, st_ref, b_ref, qx_ref, mix_ref, *, tq, group, nblk, n_steps):
    n_chunks = tq // CHUNK
    assert n_chunks % group == 0 and group >= WIN_CHUNKS and tq >= WINDOW
    t = pl.program_id(0)
    i = lax.rem(jnp.minimum(t, n_steps - 1), nblk)
    last = nblk - 1
    cur = i % 2
    nxt = 1 - cur
    mix_slot = t % 2
    prev_slot = 1 - mix_slot

    out_rows, out_cols = 256, 512
    out_pieces = [(slice(r, r + out_rows), slice(c, c + out_cols))
                  for r in range(0, tq, out_rows) for c in range(0, D_MODEL, out_cols)]

    def out_piece(n):
        r, c = out_pieces[n]
        y_ref[r, c] = x_ref[r, c] + _dot(mix_ref[prev_slot, r, :], w_ref[:, c])

    def mixers(do_out):
        kw_ref[0:WINDOW, :] = kp_ref[0].astype(BF16)
        kw_ref[WINDOW:, :] = kc_ref[0].astype(BF16)
        vw_ref[0:WINDOW, :] = vp_ref[0].astype(BF16)
        vw_ref[WINDOW:, :] = vc_ref[0].astype(BF16)

        @pl.when(i == 0)
        def _():
            st_ref[0] = jnp.zeros(st_ref.shape[1:], F32)

        @pl.when(i == last)
        def _():
            wk_ref[0] = kc_ref[0, tq - WINDOW:, :]
            wv_ref[0] = vc_ref[0, tq - WINDOW:, :]

        lane = lax.broadcasted_iota(jnp.int32, (1, LANES), 1)
        lo = lane < HEAD_DIM
        col_group = lax.broadcasted_iota(jnp.int32, (1, KV_WIDTH), 1) // HEAD_DIM
        col = lax.broadcasted_iota(jnp.int32, (1, N_KV * WIN_KEYS), 1)
        key_in_seg = col - WIN_KEYS * sum((col >= g * WIN_KEYS).astype(jnp.int32) for g in range(1, N_KV))
        neg_inf = jnp.float32(-jnp.inf)
        rows = [slice(c * CHUNK, (c + 1) * CHUNK) for c in range(n_chunks)]
        a_s, a_v, a_p, a_den, a_ov = {}, {}, {}, {}, {}

        def block_diag(win):
            zero = jnp.zeros_like(win)
            return jnp.concatenate([jnp.where(col_group == g, win, zero) for g in range(N_KV)], axis=0)

        def attn_scores(c):
            k_bd = block_diag(kw_ref[c * CHUNK:c * CHUNK + WIN_KEYS, :])
            a_v[c] = block_diag(vw_ref[c * CHUNK:c * CHUNK + WIN_KEYS, :])
            qs = jnp.concatenate(
                [q_ref[rows[c], r * KV_WIDTH:(r + 1) * KV_WIDTH] for r in range(GQA_REP)], axis=0)
            a_s[c] = _dot_t(qs, k_bd)

        def attn_softmax(c):
            su = a_s.pop(c)
            if c < WIN_CHUNKS:
                first_valid = jnp.where(i == 0, (WIN_CHUNKS - c) * CHUNK, 0)
                su = jnp.where(key_in_seg >= first_valid, su, neg_inf)
            cs = [su[:, LANES * n:LANES * (n + 1)] for n in range(6)]
            seg_max = [
                jnp.maximum(cs[0], jnp.where(lo, cs[1], neg_inf)),
                jnp.maximum(jnp.where(lo, neg_inf, cs[1]), cs[2]),
                jnp.maximum(cs[3], jnp.where(lo, cs[4], neg_inf)),
                jnp.maximum(jnp.where(lo, neg_inf, cs[4]), cs[5]),
            ]
            mu = [jnp.maximum(jnp.max(seg_max[g], axis=-1, keepdims=True), sink_ref[g])
                  for g in range(N_KV)]
            shift = [mu[0], jnp.where(lo, mu[0], mu[1]), mu[1], mu[2], jnp.where(lo, mu[2], mu[3]), mu[3]]
            e = [jnp.exp2(cs[n] - shift[n]) for n in range(6)]
            a_p[c] = jnp.concatenate(e, axis=1).astype(BF16)
            seg_sum = [e[0] + jnp.where(lo, e[1], 0.0), jnp.where(lo, 0.0, e[1]) + e[2],
                       e[3] + jnp.where(lo, e[4], 0.0), jnp.where(lo, 0.0, e[4]) + e[5]]
            a_den[c] = [jnp.sum(seg_sum[g], axis=-1, keepdims=True) + jnp.exp2(sink_ref[g] - mu[g])
                        for g in range(N_KV)]

        def attn_pv(c):
            a_ov[c] = _dot(a_p.pop(c), a_v.pop(c))

        def attn_finish(c):
            ov, den = a_ov.pop(c), a_den.pop(c)
            den = jnp.concatenate([jnp.where(lo, den[0], den[1]), jnp.where(lo, den[2], den[3])], axis=1)
            o = ov / den
            for r in range(GQA_REP):
                cols = slice(r * KV_WIDTH, (r + 1) * KV_WIDTH)
                gate = ga_ref[rows[c], cols].astype(F32)
                mix_ref[mix_slot, rows[c], cols] = (o[r * CHUNK:(r + 1) * CHUNK, :] * gate).astype(BF16)

        rowi = lax.broadcasted_iota(jnp.int32, (CHUNK, CHUNK), 0)
        coli = lax.broadcasted_iota(jnp.int32, (CHUNK, CHUNK), 1)
        causal = rowi >= coli
        eye = (lax.broadcasted_iota(jnp.int32, (GLA_DK, GLA_DK), 0)
               == lax.broadcasted_iota(jnp.int32, (GLA_DK, GLA_DK), 1))
        heads = range(N_GLA)
        kcol = [slice(h * GLA_DK, (h + 1) * GLA_DK) for h in heads]
        vcol = [slice(h * GLA_DV, (h + 1) * GLA_DV) for h in heads]
        mcol = [slice(ATT_WIDTH + h * GLA_DV, ATT_WIDTH + (h + 1) * GLA_DV) for h in heads]
        g_b, g_qd, g_ki, g_ke, g_dc, g_att, g_dst, g_o = {}, {}, {}, {}, {}, {}, {}, {}
        state = {}

        def cum_log2_decay(r):
            return _dot(tril_ref[...], jnp.concatenate([lah_ref[r, :], lal_ref[r, :]], axis=0))

        def decay_column(blast):
            dec_row = jnp.broadcast_to(jnp.exp2(blast), (GLA_DK, GLA_DK))
            return jnp.sum(jnp.where(eye, dec_row, 0.0), axis=-1, keepdims=True)

        def gla_finish_rows(ov, r, h):
            ms = jnp.mean(ov * ov, axis=-1, keepdims=True)
            gate = gg_ref[r, vcol[h]].astype(F32)
            mix_ref[mix_slot, r, mcol[h]] = (ov * lax.rsqrt(ms + EPS) * gate).astype(BF16)

        def gla_cumsum(c):
            g_b[c] = cum_log2_decay(rows[c])

        def gla_prepare(c):
            for h in heads:
                b = g_b[c][:, kcol[h]]
                blast = b[CHUNK - 1:CHUNK, :]
                q = qg_ref[rows[c], kcol[h]].astype(F32)
                k = kg_ref[rows[c], kcol[h]].astype(F32)
                decay = jnp.exp2(b)
                g_qd[c, h] = (q * decay).astype(BF16)
                g_ki[c, h] = (k * (1.0 / decay)).astype(BF16)
                g_ke[c, h] = (k * jnp.exp2(blast - b)).astype(BF16)
                g_dc[c, h] = decay_column(blast)

        def gla_scores(c):
            for h in heads:
                g_att[c, h] = _dot_t(g_qd[c, h], g_ki.pop((c, h)))
                g_dst[c, h] = _dot_ta(g_ke.pop((c, h)), vg_ref[rows[c], vcol[h]])

        def gla_outputs(c):
            for h in heads:
                att = jnp.where(causal, g_att.pop((c, h)), 0.0).astype(BF16)
                g_o[c, h] = (_dot(att, vg_ref[rows[c], vcol[h]])
                             + _dot(g_qd.pop((c, h)), state[h].astype(BF16)))
                state[h] = state[h] * g_dc.pop((c, h)) + g_dst.pop((c, h))

        def gla_finish(c):
            for h in heads:
                gla_finish_rows(g_o.pop((c, h)), rows[c], h)

        def gla_exact_step():
            def chunk(c, carry):
                r = pl.ds(pl.multiple_of(c * CHUNK, CHUNK), CHUNK)
                bcum = cum_log2_decay(r)
                for h in heads:
                    b = bcum[:, kcol[h]]
                    blast = b[CHUNK - 1:CHUNK, :]
                    q = qg_ref[r, kcol[h]].astype(F32)
                    k = kg_ref[r, kcol[h]].astype(F32)
                    v = vg_ref[r, vcol[h]]
                    b_ref[...] = b
                    qx_ref[...] = q

                    def score_row(tok, att_t):
                        w = jnp.where(rowi[:, 0:1] <= tok, jnp.exp2(b_ref[pl.ds(tok, 1), :] - b), 0.0)
                        colv = jnp.sum(qx_ref[pl.ds(tok, 1), :] * k * w, axis=-1, keepdims=True)
                        return att_t + jnp.where(coli == tok, colv, 0.0)

                    att_t = lax.fori_loop(0, CHUNK, score_row, jnp.zeros((CHUNK, CHUNK), F32))
                    q_dec = (q * jnp.exp2(b)).astype(BF16)
                    k_end = (k * jnp.exp2(blast - b)).astype(BF16)
                    st = st_ref[nxt, h]
                    ov = _dot_ta(att_t.astype(BF16), v) + _dot(q_dec, st.astype(BF16))
                    st_ref[nxt, h] = st * decay_column(blast) + _dot_ta(k_end, v)
                    gla_finish_rows(ov, r, h)
                return carry

            st_ref[nxt] = st_ref[cur]
            lax.fori_loop(0, n_chunks, chunk, 0)

        pending = list(range(len(out_pieces))) if do_out else []

        def out_some(n):
            for _ in range(min(n, len(pending))):
                out_piece(pending.pop(0))

        per_slot = -(-len(out_pieces) // (4 * (n_chunks // group)))
        for h in heads:
            state[h] = st_ref[cur, h]
        for g0 in range(0, n_chunks, group):
            cs = list(range(g0, g0 + group))
            half = group // 2
            for c in cs:
                gla_cumsum(c)
            out_some(per_slot)
            for c in cs:
                attn_scores(c)
            out_some(per_slot)
            for c in cs:
                gla_prepare(c)
            for c in cs:
                gla_scores(c)
            out_some(per_slot)
            for c in cs[:half]:
                attn_softmax(c)
            for c in cs[:half]:
                attn_pv(c)
            out_some(per_slot)
            for c in cs:
                gla_outputs(c)
            for c in cs[half:]:
                attn_softmax(c)
            for c in cs[half:]:
                attn_pv(c)
            for c in cs:
                gla_finish(c)
            for c in cs:
                attn_finish(c)
        out_some(len(pending))
        for h in heads:
            st_ref[nxt, h] = state[h]
        total = functools.reduce(jnp.minimum, [g_b[c][CHUNK - 1:CHUNK, :] for c in range(n_chunks)])

        @pl.when(jnp.min(total) < -LOG2_DECAY_LIMIT)
        def _():
            gla_exact_step()

        @pl.when(i == last)
        def _():
            sout_ref[0] = st_ref[nxt]

    @pl.when(t == 0)
    def _():
        mixers(do_out=False)

    @pl.when((t > 0) & (t < n_steps))
    def _():
        mixers(do_out=True)

    @pl.when(t == n_steps)
    def _():
        for n in range(len(out_pieces)):
            out_piece(n)


def _layer_prompt(x2d, q, k, v, ga, sink_rows, qg, kg, vg, gg, la_hi, la_lo, tril2, w_out, batch, seq,
                  tq, group):
    nblk = seq // tq
    n_steps = batch * nblk
    per_win = tq // WINDOW
    blk_of = lambda t: jnp.minimum(t, n_steps - 1)
    rows = lambda w: pl.BlockSpec((tq, w), lambda t: (blk_of(t), 0))
    late = lambda w: pl.BlockSpec((tq, w), lambda t: (jnp.maximum(t - 1, 0), 0))
    cur = pl.BlockSpec((1, tq, KV_WIDTH), lambda t: (blk_of(t) // nblk, blk_of(t) % nblk, 0))
    prev = pl.BlockSpec((1, WINDOW, KV_WIDTH),
                        lambda t: (blk_of(t) // nblk, jnp.maximum((blk_of(t) % nblk) * per_win - 1, 0), 0))
    win = pl.BlockSpec((1, WINDOW, KV_WIDTH), lambda t: (blk_of(t) // nblk, 0, 0))
    state = pl.BlockSpec((1, N_GLA, GLA_DK, GLA_DV), lambda t: (blk_of(t) // nblk, 0, 0, 0))
    k3 = k.reshape(batch, seq, KV_WIDTH)
    v3 = v.reshape(batch, seq, KV_WIDTH)
    return pl.pallas_call(
        functools.partial(_layer_prompt_kernel, tq=tq, group=group, nblk=nblk, n_steps=n_steps),
        grid=(n_steps + 1,),
        in_specs=[rows(ATT_WIDTH), prev, cur, prev, cur, rows(ATT_WIDTH), _const_spec(sink_rows.shape),
                  rows(GLA_KW), rows(GLA_KW), rows(GLA_WIDTH), rows(GLA_WIDTH), rows(GLA_KW),
                  rows(GLA_KW), _const_spec(tril2.shape), late(D_MODEL), _const_spec(w_out.shape)],
        out_specs=[late(D_MODEL), win, win, state],
        out_shape=[jax.ShapeDtypeStruct((batch * seq, D_MODEL), F32),
                   jax.ShapeDtypeStruct((batch, WINDOW, KV_WIDTH), F32),
                   jax.ShapeDtypeStruct((batch, WINDOW, KV_WIDTH), F32),
                   jax.ShapeDtypeStruct((batch, N_GLA, GLA_DK, GLA_DV), F32)],
        scratch_shapes=[pltpu.VMEM((WINDOW + tq, KV_WIDTH), BF16),
                        pltpu.VMEM((WINDOW + tq, KV_WIDTH), BF16),
                        pltpu.VMEM((2, N_GLA, GLA_DK, GLA_DV), F32),
                        pltpu.VMEM((CHUNK, GLA_DK), F32),
                        pltpu.VMEM((CHUNK, GLA_DK), F32),
                        pltpu.VMEM((2, tq, D_MODEL), BF16)],
        compiler_params=pltpu.CompilerParams(
            dimension_semantics=("arbitrary",), vmem_limit_bytes=VMEM_LIMIT),
        name="layer_prompt",
    )(q, k3, k3, v3, v3, ga, sink_rows, qg, kg, vg, gg, la_hi, la_lo, tril2, x2d, w_out)


def _attn_sample_kernel(q_ref, kn_ref, vn_ref, ck_ref, cv_ref, ga_ref, sink_ref,
                        o_ref, wk_ref, wv_ref, *, t, bb):
    n_cache = ck_ref.shape[1]
    assert n_cache == LANES and t <= LANES
    lane = lax.broadcasted_iota(jnp.int32, (1, LANES), 1)
    lo = lane < HEAD_DIM
    is_new = lane < t
    col_group = lax.broadcasted_iota(jnp.int32, (1, KV_WIDTH), 1) // HEAD_DIM
    seg = 2 * LANES
    ones_bd = jnp.concatenate(
        [jnp.broadcast_to(jnp.where(col_group == g, 1.0, 0.0).astype(BF16), (seg, KV_WIDTH))
         for g in range(N_KV)], axis=0)
    pad = jnp.zeros((LANES - t, KV_WIDTH), BF16)
    neg_inf = jnp.float32(-jnp.inf)

    s, v_all = [], []
    for b in range(bb):
        rows = slice(b * t, (b + 1) * t)
        kn, vn, ck, cv = kn_ref[rows, :], vn_ref[rows, :], ck_ref[b], cv_ref[b]
        wk_ref[b] = jnp.concatenate([ck[t:, :], kn], axis=0)
        wv_ref[b] = jnp.concatenate([cv[t:, :], vn], axis=0)
        kcat = jnp.concatenate([ck.astype(BF16), kn.astype(BF16), pad], axis=0)
        vcat = jnp.concatenate([cv.astype(BF16), vn.astype(BF16), pad], axis=0)
        zero = jnp.zeros_like(kcat)
        k_bd = jnp.concatenate([jnp.where(col_group == g, kcat, zero) for g in range(N_KV)], axis=0)
        v_bd = jnp.concatenate([jnp.where(col_group == g, vcat, zero) for g in range(N_KV)], axis=0)
        v_all.append(jnp.concatenate([v_bd, ones_bd], axis=1))
        qs = jnp.concatenate(
            [q_ref[rows, r * KV_WIDTH:(r + 1) * KV_WIDTH] for r in range(GQA_REP)], axis=0)
        s.append(_dot_t(qs, k_bd))
    p, m = [], []
    for b in range(bb):
        c = [s[b][:, LANES * n:LANES * (n + 1)] for n in range(2 * N_KV)]
        c = [cn if n % 2 == 0 else jnp.where(is_new, cn, neg_inf) for n, cn in enumerate(c)]
        mb = [jnp.maximum(jnp.max(jnp.maximum(c[2 * g], c[2 * g + 1]), axis=-1, keepdims=True),
                          sink_ref[g]) for g in range(N_KV)]
        p.append(jnp.concatenate([jnp.exp2(c[n] - mb[n // 2]) for n in range(2 * N_KV)],
                                 axis=1).astype(BF16))
        m.append(mb)
    ov = [_dot(p[b], v_all[b]) for b in range(bb)]
    for b in range(bb):
        rows = slice(b * t, (b + 1) * t)
        es = [jnp.exp2(sink_ref[g] - m[b][g]) for g in range(N_KV)]
        e_sink = jnp.concatenate([jnp.where(lo, es[0], es[1]), jnp.where(lo, es[2], es[3])], axis=1)
        o = ov[b][:, :KV_WIDTH] / (ov[b][:, KV_WIDTH:] + e_sink)
        for r in range(GQA_REP):
            cols = slice(r * KV_WIDTH, (r + 1) * KV_WIDTH)
            gate = ga_ref[rows, cols].astype(F32)
            o_ref[rows, cols] = (o[r * t:(r + 1) * t, :] * gate).astype(BF16)


def _attn_sample(q, k, v, cache_k, cache_v, ga, sink_rows, batch, t, bb):
    n_cache = cache_k.shape[1]
    rows = lambda w: pl.BlockSpec((bb * t, w), lambda b: (b, 0))
    cache = pl.BlockSpec((bb, n_cache, KV_WIDTH), lambda b: (b, 0, 0))
    return pl.pallas_call(
        functools.partial(_attn_sample_kernel, t=t, bb=bb),
        grid=(batch // bb,),
        in_specs=[rows(ATT_WIDTH), rows(KV_WIDTH), rows(KV_WIDTH), cache, cache, rows(ATT_WIDTH),
                  _const_spec(sink_rows.shape)],
        out_specs=[rows(ATT_WIDTH), cache, cache],
        out_shape=[jax.ShapeDtypeStruct((batch * t, ATT_WIDTH), BF16),
                   jax.ShapeDtypeStruct((batch, n_cache, KV_WIDTH), F32),
                   jax.ShapeDtypeStruct((batch, n_cache, KV_WIDTH), F32)],
        compiler_params=pltpu.CompilerParams(
            dimension_semantics=("arbitrary",), vmem_limit_bytes=VMEM_LIMIT),
        name="attn_sample",
    )(q, k, v, cache_k, cache_v, ga, sink_rows)


def _gla_kernel(qg_ref, kg_ref, vg_ref, gg_ref, lah_ref, lal_ref, s0_ref, tril_ref,
                o_ref, sout_ref, st_ref, b_ref, q_ref, *, blk, tg, unroll):
    i = pl.program_id(1)
    n_groups = tg // (blk * unroll)

    @pl.when(i == 0)
    def _():
        st_ref[0] = s0_ref[0]

    rowi = lax.broadcasted_iota(jnp.int32, (blk, blk), 0)
    coli = lax.broadcasted_iota(jnp.int32, (blk, blk), 1)
    causal = rowi >= coli
    eye = (lax.broadcasted_iota(jnp.int32, (GLA_DK, GLA_DK), 0)
           == lax.broadcasted_iota(jnp.int32, (GLA_DK, GLA_DK), 1))

    heads = range(N_GLA)
    kcol = [slice(h * GLA_DK, (h + 1) * GLA_DK) for h in heads]
    vcol = [slice(h * GLA_DV, (h + 1) * GLA_DV) for h in heads]

    def cum_log2_decay(r):
        return _dot(tril_ref[...], jnp.concatenate([lah_ref[r, :], lal_ref[r, :]], axis=0))

    def decay_column(blast):
        dec_row = jnp.broadcast_to(jnp.exp2(blast), (GLA_DK, GLA_DK))
        return jnp.sum(jnp.where(eye, dec_row, 0.0), axis=-1, keepdims=True)

    def finish(ov, r, h):
        ms = jnp.mean(ov * ov, axis=-1, keepdims=True)
        gate = gg_ref[r, vcol[h]].astype(F32)
        o_ref[r, vcol[h]] = (ov * lax.rsqrt(ms + EPS) * gate).astype(BF16)

    def exact_group(rows, cur, nxt):
        st = [st_ref[cur, h] for h in heads]
        for r in rows:
            bcum = cum_log2_decay(r)
            for h in heads:
                b = bcum[:, kcol[h]]
                blast = b[blk - 1:blk, :]
                q = qg_ref[r, kcol[h]].astype(F32)
                k = kg_ref[r, kcol[h]].astype(F32)
                v = vg_ref[r, vcol[h]]
                b_ref[...] = b
                q_ref[...] = q

                def score_row(t, att_t):
                    w = jnp.where(rowi[:, 0:1] <= t, jnp.exp2(b_ref[pl.ds(t, 1), :] - b), 0.0)
                    col = jnp.sum(q_ref[pl.ds(t, 1), :] * k * w, axis=-1, keepdims=True)
                    return att_t + jnp.where(coli == t, col, 0.0)

                att_t = lax.fori_loop(0, blk, score_row, jnp.zeros((blk, blk), F32))
                q_dec = (q * jnp.exp2(b)).astype(BF16)
                k_end = (k * jnp.exp2(blast - b)).astype(BF16)
                ov = _dot_ta(att_t.astype(BF16), v) + _dot(q_dec, st[h].astype(BF16))
                st[h] = st[h] * decay_column(blast) + _dot_ta(k_end, v)
                finish(ov, r, h)
        for h in heads:
            st_ref[nxt, h] = st[h]

    def group(j, carry):
        rows = [pl.ds(pl.multiple_of((j * unroll + u) * blk, blk), blk) for u in range(unroll)]
        cur = (i * n_groups + j) % 2
        nxt = 1 - cur
        bcum = [cum_log2_decay(r) for r in rows]
        q_dec, k_inv, k_end, dec_col = {}, {}, {}, {}
        for u, r in enumerate(rows):
            for h in heads:
                b = bcum[u][:, kcol[h]]
                blast = b[blk - 1:blk, :]
                q = qg_ref[r, kcol[h]].astype(F32)
                k = kg_ref[r, kcol[h]].astype(F32)
                decay = jnp.exp2(b)
                q_dec[u, h] = (q * decay).astype(BF16)
                k_inv[u, h] = (k * (1.0 / decay)).astype(BF16)
                k_end[u, h] = (k * jnp.exp2(blast - b)).astype(BF16)
                dec_col[u, h] = decay_column(blast)
        att = {(u, h): _dot_t(q_dec[u, h], k_inv[u, h]) for u in range(unroll) for h in heads}
        d_st = {(u, h): _dot_ta(k_end[u, h], vg_ref[rows[u], vcol[h]])
                for u in range(unroll) for h in heads}
        att = {key: jnp.where(causal, a, 0.0).astype(BF16) for key, a in att.items()}
        st = [st_ref[cur, h] for h in heads]
        o = {}
        for u, r in enumerate(rows):
            for h in heads:
                o[u, h] = _dot(att[u, h], vg_ref[r, vcol[h]]) + _dot(q_dec[u, h], st[h].astype(BF16))
                st[h] = st[h] * dec_col[u, h] + d_st[u, h]
        for h in heads:
            st_ref[nxt, h] = st[h]
        for u, r in enumerate(rows):
            for h in heads:
                finish(o[u, h], r, h)

        total = functools.reduce(jnp.minimum, [bc[blk - 1:blk, :] for bc in bcum])

        @pl.when(jnp.min(total) < -LOG2_DECAY_LIMIT)
        def _():
            exact_group(rows, cur, nxt)
        return carry

    lax.fori_loop(0, n_groups, group, 0)

    @pl.when(i == pl.num_programs(1) - 1)
    def _():
        sout_ref[0] = st_ref[(pl.num_programs(1) * n_groups) % 2]


def _gla(qg, kg, vg, gg, la_hi, la_lo, s0, tril2, batch, seq, blk, tg, unroll=1):
    nblk = seq // tg
    rows = lambda w: pl.BlockSpec((tg, w), lambda b, i: (b * nblk + i, 0))
    state = pl.BlockSpec((1, N_GLA, GLA_DK, GLA_DV), lambda b, i: (b, 0, 0, 0))
    return pl.pallas_call(
        functools.partial(_gla_kernel, blk=blk, tg=tg, unroll=unroll),
        grid=(batch, nblk),
        in_specs=[rows(GLA_KW), rows(GLA_KW), rows(GLA_WIDTH), rows(GLA_WIDTH), rows(GLA_KW),
                  rows(GLA_KW), state, _const_spec(tril2.shape)],
        out_specs=[rows(GLA_WIDTH), state],
        out_shape=[jax.ShapeDtypeStruct((batch * seq, GLA_WIDTH), BF16),
                   jax.ShapeDtypeStruct((batch, N_GLA, GLA_DK, GLA_DV), F32)],
        scratch_shapes=[pltpu.VMEM((2, N_GLA, GLA_DK, GLA_DV), F32),
                        pltpu.VMEM((blk, GLA_DK), F32),
                        pltpu.VMEM((blk, GLA_DK), F32)],
        compiler_params=pltpu.CompilerParams(
            dimension_semantics=("arbitrary", "arbitrary"), vmem_limit_bytes=VMEM_LIMIT),
        name="gla",
    )(qg, kg, vg, gg, la_hi, la_lo, s0, tril2)


def _outproj_kernel(x_ref, ma_ref, mg_ref, w_ref, y_ref):
    y_ref[...] = (x_ref[...] + _dot(ma_ref[...], w_ref[:ATT_WIDTH, :])
                  + _dot(mg_ref[...], w_ref[ATT_WIDTH:, :]))


def _outproj(x2d, mix_a, mix_g, w, tm):
    m = x2d.shape[0]
    row = lambda w: pl.BlockSpec((tm, w), lambda i: (i, 0))
    return pl.pallas_call(
        _outproj_kernel,
        grid=(m // tm,),
        in_specs=[row(D_MODEL), row(ATT_WIDTH), row(GLA_WIDTH), _const_spec(w.shape)],
        out_specs=row(D_MODEL),
        out_shape=jax.ShapeDtypeStruct((m, D_MODEL), F32),
        compiler_params=pltpu.CompilerParams(
            dimension_semantics=("arbitrary",), vmem_limit_bytes=VMEM_LIMIT),
        name="outproj",
    )(x2d, mix_a, mix_g, w)


def _regroup_unit(u):
    return (u % N_KV) * GQA_REP + u // N_KV


def _prep_w_in_kernel(u0_ref, u1_ref, u2_ref, u3_ref, lr_ref, main_ref, wlr_ref):
    units = jnp.concatenate([u0_ref[...], u1_ref[...], u2_ref[...], u3_ref[...]], axis=0)
    main_ref[...] = units.T.astype(BF16)

    @pl.when(pl.program_id(0) == 0)
    def _():
        lr = lr_ref[...]
        padded = jnp.concatenate([lr, jnp.zeros((LANES - lr.shape[0], lr.shape[1]), F32)], axis=0)
        wlr_ref[...] = padded.T.astype(BF16)


def _prep_w_in(w_in_t):
    _, k = w_in_t.shape
    units_per_block = KV_WIDTH // HEAD_DIM
    n_att = ATT_WIDTH // HEAD_DIM

    def src_unit(j, p):
        u = j * units_per_block + p
        q0, ga0 = OFF_Q // HEAD_DIM, OFF_GA // HEAD_DIM
        u = jnp.where((u >= q0) & (u < q0 + n_att), q0 + _regroup_unit(u - q0), u)
        return jnp.where((u >= ga0) & (u < ga0 + n_att), ga0 + _regroup_unit(u - ga0), u)

    unit = lambda p: pl.BlockSpec((HEAD_DIM, k), lambda j: (src_unit(j, p), 0))
    return pl.pallas_call(
        _prep_w_in_kernel,
        grid=(MAIN_COLS // KV_WIDTH,),
        in_specs=[unit(0), unit(1), unit(2), unit(3),
                  pl.BlockSpec((GATE_RANK, k), lambda j: (MAIN_COLS // GATE_RANK, 0))],
        out_specs=[pl.BlockSpec((k, KV_WIDTH), lambda j: (0, j)),
                   pl.BlockSpec((k, LANES), lambda j: (0, 0))],
        out_shape=[jax.ShapeDtypeStruct((k, MAIN_COLS), BF16),
                   jax.ShapeDtypeStruct((k, LANES), BF16)],
        compiler_params=pltpu.CompilerParams(
            dimension_semantics=("arbitrary",), vmem_limit_bytes=VMEM_LIMIT),
        name="prep_w_in",
    )(w_in_t, w_in_t, w_in_t, w_in_t, w_in_t)


def _prep_w_out_kernel(u0_ref, u1_ref, u2_ref, u3_ref, o_ref):
    o_ref[...] = jnp.concatenate(
        [u0_ref[...], u1_ref[...], u2_ref[...], u3_ref[...]], axis=0).astype(BF16)


def _prep_w_out(w_out):
    rows, cols = w_out.shape
    units_per_block = KV_WIDTH // HEAD_DIM
    n_att = ATT_WIDTH // HEAD_DIM

    def src_unit(j, p):
        u = j * units_per_block + p
        return jnp.where(u < n_att, _regroup_unit(u), u)

    unit = lambda p: pl.BlockSpec((HEAD_DIM, cols), lambda j: (src_unit(j, p), 0))
    return pl.pallas_call(
        _prep_w_out_kernel,
        grid=(rows // KV_WIDTH,),
        in_specs=[unit(0), unit(1), unit(2), unit(3)],
        out_specs=pl.BlockSpec((KV_WIDTH, cols), lambda j: (j, 0)),
        out_shape=jax.ShapeDtypeStruct((rows, cols), BF16),
        compiler_params=pltpu.CompilerParams(dimension_semantics=("arbitrary",)),
        name="prep_w_out",
    )(w_out, w_out, w_out, w_out)


def _tril2(blk):
    t = np.tril(np.ones((blk, blk), np.float32))
    return jnp.asarray(np.concatenate([t, t], axis=1), BF16)


def _prep(norm_g, w_in, w_gate_up, b_gate, q_norm_g, k_norm_g, sinks, gla_norm_g, w_out):
    w_main, w_lr = _prep_w_in(w_in.T)
    w_gu = jnp.pad(w_gate_up, ((0, LANES - GATE_RANK), (0, 0))).astype(BF16)
    g_qk = jnp.concatenate([jnp.tile(q_norm_g, N_HEADS) * (ATT_SCALE * LOG2E),
                            jnp.tile(k_norm_g, N_KV)])[None, :]

    grp = np.arange(QK_WIDTH) // HEAD_DIM
    ind = (grp[:, None] == np.arange(LANES)[None, :]).astype(np.float32)
    g_sum = jnp.asarray(ind, BF16)
    g_exp = jnp.asarray(np.concatenate([ind.T, ind.T], axis=0), BF16)

    def sink_rows(t):
        sk = sinks.astype(F32).reshape(N_KV, GQA_REP) * LOG2E
        return jnp.broadcast_to(sk[:, :, None, None], (N_KV, GQA_REP, t, LANES)).reshape(
            N_KV, GQA_REP * t, LANES)

    return dict(norm_g=norm_g[None, :], w_main=w_main, w_lr=w_lr, w_gu=w_gu, b_gate=b_gate[None, :],
                g_qk=g_qk, g_gla=jnp.tile(gla_norm_g, N_GLA)[None, :], g_sum=g_sum, g_exp=g_exp,
                w_out=_prep_w_out(w_out), sink_rows=sink_rows)


def kernel(x_prompt, x_sample, cache_k, cache_v, state_gla, norm_g, w_in, w_gate_up, b_gate,
           q_norm_g, k_norm_g, sinks, gla_norm_g, w_out):
    depth = norm_g.shape[0]
    assert depth == 1
    B, S, _ = x_prompt.shape
    DB, T, _ = x_sample.shape
    n_cache = cache_k.shape[2]
    assert n_cache == WINDOW
    p = _prep(norm_g[0], w_in[0], w_gate_up[0], b_gate[0], q_norm_g[0], k_norm_g[0], sinks[0],
              gla_norm_g[0], w_out[0])

    xp = x_prompt.reshape(B * S, D_MODEL)
    q, k, v, ga, qg, kg, vg, gg, la_hi, la_lo = _inproj(xp, p, tm=INPROJ_ROWS)
    y_p, win_k_p, win_v_p, gla_p = _layer_prompt(
        xp, q, k, v, ga, p["sink_rows"](CHUNK), qg, kg, vg, gg, la_hi, la_lo, _tril2(CHUNK), p["w_out"],
        B, S, tq=LAYER_ROWS, group=LAYER_GROUP)
    y_p = y_p.reshape(B, S, D_MODEL)

    xs = x_sample.reshape(DB * T, D_MODEL)
    q, k, v, ga, qg, kg, vg, gg, la_hi, la_lo = _inproj(xs, p, tm=DB * T)
    ck = cache_k[0].reshape(DB, n_cache, KV_WIDTH)
    cv = cache_v[0].reshape(DB, n_cache, KV_WIDTH)
    mix_a, win_k_s, win_v_s = _attn_sample(q, k, v, ck, cv, ga, p["sink_rows"](T), DB, T,
                                           bb=SAMPLE_STREAMS)
    blk = min(CHUNK, T)
    mix_g, gla_s = _gla(qg, kg, vg, gg, la_hi, la_lo, state_gla[0], _tril2(blk), DB, T,
                        blk=blk, tg=T)
    y_s = _outproj(xs, mix_a, mix_g, p["w_out"], tm=DB * T).reshape(DB, T, D_MODEL)

    shape5 = lambda a: a.reshape(1, a.shape[0], n_cache, N_KV, HEAD_DIM)
    return (y_p, y_s, shape5(win_k_p), shape5(win_v_p), gla_p[None],
            shape5(win_k_s), shape5(win_v_s), gla_s[None])
```

```python
import functools

import jax
import jax.numpy as jnp
import numpy as np
from jax import lax
from jax.experimental import pallas as pl
from jax.experimental.pallas import tpu as pltpu

D_MODEL = 2048
CHUNK = 64
WINDOW = 128
HEAD_DIM = 64
N_HEADS = 16
N_KV = 4
GQA_REP = N_HEADS // N_KV
ATT_WIDTH = N_HEADS * HEAD_DIM
KV_WIDTH = N_KV * HEAD_DIM
N_GLA = 4
GLA_DK = 128
GLA_DV = 256
GLA_KW = N_GLA * GLA_DK
GLA_WIDTH = N_GLA * GLA_DV
GATE_RANK = 16
GATE_TAU = 16.0
EPS = 1e-6
ATT_SCALE = HEAD_DIM ** -0.5
LOG2E = float(np.log2(np.e))
LOG2_DECAY_LIMIT = 100.0
WIN_CHUNKS = WINDOW // CHUNK
WIN_KEYS = WINDOW + CHUNK

LANES = 128
QK_WIDTH = ATT_WIDTH + KV_WIDTH
OFF_Q, OFF_K, OFF_V = 0, ATT_WIDTH, ATT_WIDTH + KV_WIDTH
OFF_GA = OFF_V + KV_WIDTH
OFF_QG = OFF_GA + ATT_WIDTH
OFF_KG = OFF_QG + GLA_KW
OFF_VG = OFF_KG + GLA_KW
OFF_GG = OFF_VG + GLA_WIDTH
MAIN_COLS = OFF_GG + GLA_WIDTH

VMEM_LIMIT = 56 * 1024 * 1024
INPROJ_ROWS = 512
INPROJ_SUB = 256
LAYER_ROWS = 512
LAYER_GROUP = 4
SAMPLE_STREAMS = 4

BF16 = jnp.bfloat16
F32 = jnp.float32


def _dot(a, b):
    return jnp.dot(a, b, preferred_element_type=F32)


def _dot_t(a, b):
    return lax.dot_general(a, b, (((1,), (1,)), ((), ())), preferred_element_type=F32)


def _dot_ta(a, b):
    return lax.dot_general(a, b, (((0,), (0,)), ((), ())), preferred_element_type=F32)


def _silu(x):
    return x / (1.0 + jnp.exp(-x))


def _const_spec(shape):
    nd = len(shape)
    return pl.BlockSpec(shape, lambda *_: (0,) * nd, pipeline_mode=pl.Buffered(1))


def _inproj_kernel(x_ref, ng_ref, w_ref, wlr_ref, wgu_ref, bg_ref, gqk_ref, gng_ref,
                   q_ref, k_ref, v_ref, ga_ref, qg_ref, kg_ref, vg_ref, gg_ref, lah_ref, lal_ref,
                   *, sub):
    lo = lax.broadcasted_iota(jnp.int32, (1, LANES), 1) < HEAD_DIM
    for s in range(x_ref.shape[0] // sub):
        rows = slice(s * sub, (s + 1) * sub)
        x = x_ref[rows, :]
        ms = jnp.mean(x * x, axis=-1, keepdims=True)
        xn = (x * lax.rsqrt(ms + EPS) * ng_ref[...]).astype(BF16)

        qk = _dot(xn, w_ref[:, OFF_Q:OFF_V])
        ga_ref[rows, :] = _silu(_dot(xn, w_ref[:, OFF_GA:OFF_QG])).astype(BF16)
        lr = _dot(xn, wlr_ref[...]).astype(BF16)
        qg_ref[rows, :] = (_dot(xn, w_ref[:, OFF_QG:OFF_KG]) * (GLA_DK ** -0.5)).astype(BF16)
        kg_ref[rows, :] = _dot(xn, w_ref[:, OFF_KG:OFF_VG]).astype(BF16)
        z = _dot(lr, wgu_ref[...]) + bg_ref[...]
        gg_ref[rows, :] = (_silu(_dot(xn, w_ref[:, OFF_GG:MAIN_COLS])) * gng_ref[...]).astype(BF16)
        vg_ref[rows, :] = _dot(xn, w_ref[:, OFF_VG:OFF_GG]).astype(BF16)
        v_ref[rows, :] = _dot(xn, w_ref[:, OFF_V:OFF_GA])

        for n in range(QK_WIDTH // LANES):
            cols = slice(n * LANES, (n + 1) * LANES)
            blk = qk[:, cols]
            sq = blk * blk
            ms_lo = jnp.sum(jnp.where(lo, sq, 0.0), axis=-1, keepdims=True) * (1.0 / HEAD_DIM)
            ms_hi = jnp.sum(jnp.where(lo, 0.0, sq), axis=-1, keepdims=True) * (1.0 / HEAD_DIM)
            scale = jnp.where(lo, lax.rsqrt(ms_lo + EPS), lax.rsqrt(ms_hi + EPS))
            normed = blk * scale * gqk_ref[:, cols]
            if n < ATT_WIDTH // LANES:
                q_ref[rows, cols] = normed.astype(BF16)
            else:
                k_ref[rows, n * LANES - ATT_WIDTH:(n + 1) * LANES - ATT_WIDTH] = normed
        log_sig = jnp.minimum(z, 0.0) - jnp.log(1.0 + jnp.exp(-jnp.abs(z)))
        la = log_sig * (LOG2E / GATE_TAU)
        la_hi = la.astype(BF16)
        lah_ref[rows, :] = la_hi
        lal_ref[rows, :] = (la - la_hi.astype(F32)).astype(BF16)


def _inproj(x2d, p, tm):
    m = x2d.shape[0]
    row = lambda w: pl.BlockSpec((tm, w), lambda i: (i, 0))
    outs = [(ATT_WIDTH, BF16), (KV_WIDTH, F32), (KV_WIDTH, F32), (ATT_WIDTH, BF16),
            (GLA_KW, BF16), (GLA_KW, BF16), (GLA_WIDTH, BF16), (GLA_WIDTH, BF16),
            (GLA_KW, BF16), (GLA_KW, BF16)]
    consts = [p["norm_g"], p["w_main"], p["w_lr"], p["w_gu"], p["b_gate"], p["g_qk"], p["g_gla"]]
    return pl.pallas_call(
        functools.partial(_inproj_kernel, sub=min(tm, INPROJ_SUB)),
        grid=(m // tm,),
        in_specs=[row(D_MODEL)] + [_const_spec(c.shape) for c in consts],
        out_specs=[row(w) for w, _ in outs],
        out_shape=[jax.ShapeDtypeStruct((m, w), dt) for w, dt in outs],
        compiler_params=pltpu.CompilerParams(
            dimension_semantics=("arbitrary",), vmem_limit_bytes=VMEM_LIMIT),
        name="inproj",
    )(x2d, *consts)


def _layer_prompt_kernel(q_ref, kp_ref, kc_ref, vp_ref, vc_ref, ga_ref, sink_ref,
                         qg_ref, kg_ref, vg_ref, gg_ref, lah_ref, lal_ref, tril_ref, x_ref, w_ref,
                         y_ref, wk_ref, wv_ref, sout_ref,
                         kw_ref, vw_ref, st_ref, b_ref, qx_ref, mix_ref, *, tq, group, nblk, n_steps):
    n_chunks = tq // CHUNK
    assert n_chunks % group == 0 and group >= WIN_CHUNKS and tq >= WINDOW
    t = pl.program_id(0)
    i = lax.rem(jnp.minimum(t, n_steps - 1), nblk)
    last = nblk - 1
    cur = i % 2
    nxt = 1 - cur
    mix_slot = t % 2
    prev_slot = 1 - mix_slot

    out_rows, out_cols = 256, 512
    out_pieces = [(slice(r, r + out_rows), slice(c, c + out_cols))
                  for r in range(0, tq, out_rows) for c in range(0, D_MODEL, out_cols)]

    def out_piece(n):
        r, c = out_pieces[n]
        y_ref[r, c] = x_ref[r, c] + _dot(mix_ref[prev_slot, r, :], w_ref[:, c])

    def mixers(do_out):
        kw_ref[0:WINDOW, :] = kp_ref[0].astype(BF16)
        kw_ref[WINDOW:, :] = kc_ref[0].astype(BF16)
        vw_ref[0:WINDOW, :] = vp_ref[0].astype(BF16)
        vw_ref[WINDOW:, :] = vc_ref[0].astype(BF16)

        @pl.when(i == 0)
        def _():
            st_ref[0] = jnp.zeros(st_ref.shape[1:], F32)

        @pl.when(i == last)
        def _():
            wk_ref[0] = kc_ref[0, tq - WINDOW:, :]
            wv_ref[0] = vc_ref[0, tq - WINDOW:, :]

        lane = lax.broadcasted_iota(jnp.int32, (1, LANES), 1)
        lo = lane < HEAD_DIM
        col_group = lax.broadcasted_iota(jnp.int32, (1, KV_WIDTH), 1) // HEAD_DIM
        col = lax.broadcasted_iota(jnp.int32, (1, N_KV * WIN_KEYS), 1)
        key_in_seg = col - WIN_KEYS * sum((col >= g * WIN_KEYS).astype(jnp.int32) for g in range(1, N_KV))
        neg_inf = jnp.float32(-jnp.inf)
        rows = [slice(c * CHUNK, (c + 1) * CHUNK) for c in range(n_chunks)]
        a_s, a_v, a_p, a_den, a_ov = {}, {}, {}, {}, {}

        def block_diag(win):
            zero = jnp.zeros_like(win)
            return jnp.concatenate([jnp.where(col_group == g, win, zero) for g in range(N_KV)], axis=0)

        def attn_scores(c):
            k_bd = block_diag(kw_ref[c * CHUNK:c * CHUNK + WIN_KEYS, :])
            a_v[c] = block_diag(vw_ref[c * CHUNK:c * CHUNK + WIN_KEYS, :])
            qs = jnp.concatenate(
                [q_ref[rows[c], r * KV_WIDTH:(r + 1) * KV_WIDTH] for r in range(GQA_REP)], axis=0)
            a_s[c] = _dot_t(qs, k_bd)

        def attn_softmax(c):
            su = a_s.pop(c)
            if c < WIN_CHUNKS:
                first_valid = jnp.where(i == 0, (WIN_CHUNKS - c) * CHUNK, 0)
                su = jnp.where(key_in_seg >= first_valid, su, neg_inf)
            cs = [su[:, LANES * n:LANES * (n + 1)] for n in range(6)]
            seg_max = [
                jnp.maximum(cs[0], jnp.where(lo, cs[1], neg_inf)),
                jnp.maximum(jnp.where(lo, neg_inf, cs[1]), cs[2]),
                jnp.maximum(cs[3], jnp.where(lo, cs[4], neg_inf)),
                jnp.maximum(jnp.where(lo, neg_inf, cs[4]), cs[5]),
            ]
            mu = [jnp.maximum(jnp.max(seg_max[g], axis=-1, keepdims=True), sink_ref[g])
                  for g in range(N_KV)]
            shift = [mu[0], jnp.where(lo, mu[0], mu[1]), mu[1], mu[2], jnp.where(lo, mu[2], mu[3]), mu[3]]
            e = [jnp.exp2(cs[n] - shift[n]) for n in range(6)]
            a_p[c] = jnp.concatenate(e, axis=1).astype(BF16)
            seg_sum = [e[0] + jnp.where(lo, e[1], 0.0), jnp.where(lo, 0.0, e[1]) + e[2],
                       e[3] + jnp.where(lo, e[4], 0.0), jnp.where(lo, 0.0, e[4]) + e[5]]
            a_den[c] = [jnp.sum(seg_sum[g], axis=-1, keepdims=True) + jnp.exp2(sink_ref[g] - mu[g])
                        for g in range(N_KV)]

        def attn_pv(c):
            a_ov[c] = _dot(a_p.pop(c), a_v.pop(c))

        def attn_finish(c):
            ov, den = a_ov.pop(c), a_den.pop(c)
            den = jnp.concatenate([jnp.where(lo, den[0], den[1]), jnp.where(lo, den[2], den[3])], axis=1)
            o = ov / den
            for r in range(GQA_REP):
                cols = slice(r * KV_WIDTH, (r + 1) * KV_WIDTH)
                gate = ga_ref[rows[c], cols].astype(F32)
                mix_ref[mix_slot, rows[c], cols] = (o[r * CHUNK:(r + 1) * CHUNK, :] * gate).astype(BF16)

        rowi = lax.broadcasted_iota(jnp.int32, (CHUNK, CHUNK), 0)
        coli = lax.broadcasted_iota(jnp.int32, (CHUNK, CHUNK), 1)
        causal = rowi >= coli
        eye = (lax.broadcasted_iota(jnp.int32, (GLA_DK, GLA_DK), 0)
               == lax.broadcasted_iota(jnp.int32, (GLA_DK, GLA_DK), 1))
        heads = range(N_GLA)
        kcol = [slice(h * GLA_DK, (h + 1) * GLA_DK) for h in heads]
        vcol = [slice(h * GLA_DV, (h + 1) * GLA_DV) for h in heads]
        mcol = [slice(ATT_WIDTH + h * GLA_DV, ATT_WIDTH + (h + 1) * GLA_DV) for h in heads]
        g_b, g_qd, g_ki, g_ke, g_dc, g_att, g_dst, g_o = {}, {}, {}, {}, {}, {}, {}, {}
        state = {}

        def cum_log2_decay(r):
            return _dot(tril_ref[...], jnp.concatenate([lah_ref[r, :], lal_ref[r, :]], axis=0))

        def decay_column(blast):
            dec_row = jnp.broadcast_to(jnp.exp2(blast), (GLA_DK, GLA_DK))
            return jnp.sum(jnp.where(eye, dec_row, 0.0), axis=-1, keepdims=True)

        def gla_finish_rows(ov, r, h):
            ms = jnp.mean(ov * ov, axis=-1, keepdims=True)
            gate = gg_ref[r, vcol[h]].astype(F32)
            mix_ref[mix_slot, r, mcol[h]] = (ov * lax.rsqrt(ms + EPS) * gate).astype(BF16)

        def gla_cumsum(c):
            g_b[c] = cum_log2_decay(rows[c])

        def gla_prepare(c):
            for h in heads:
                b = g_b[c][:, kcol[h]]
                blast = b[CHUNK - 1:CHUNK, :]
                q = qg_ref[rows[c], kcol[h]].astype(F32)
                k = kg_ref[rows[c], kcol[h]].astype(F32)
                decay = jnp.exp2(b)
                g_qd[c, h] = (q * decay).astype(BF16)
                g_ki[c, h] = (k * (1.0 / decay)).astype(BF16)
                g_ke[c, h] = (k * jnp.exp2(blast - b)).astype(BF16)
                g_dc[c, h] = decay_column(blast)

        def gla_scores(c):
            for h in heads:
                g_att[c, h] = _dot_t(g_qd[c, h], g_ki.pop((c, h)))
                g_dst[c, h] = _dot_ta(g_ke.pop((c, h)), vg_ref[rows[c], vcol[h]])

        def gla_outputs(c):
            for h in heads:
                att = jnp.where(causal, g_att.pop((c, h)), 0.0).astype(BF16)
                g_o[c, h] = (_dot(att, vg_ref[rows[c], vcol[h]])
                             + _dot(g_qd.pop((c, h)), state[h].astype(BF16)))
                state[h] = state[h] * g_dc.pop((c, h)) + g_dst.pop((c, h))

        def gla_finish(c):
            for h in heads:
                gla_finish_rows(g_o.pop((c, h)), rows[c], h)

        def gla_exact_step():
            def chunk(c, carry):
                r = pl.ds(pl.multiple_of(c * CHUNK, CHUNK), CHUNK)
                bcum = cum_log2_decay(r)
                for h in heads:
                    b = bcum[:, kcol[h]]
                    blast = b[CHUNK - 1:CHUNK, :]
                    q = qg_ref[r, kcol[h]].astype(F32)
                    k = kg_ref[r, kcol[h]].astype(F32)
                    v = vg_ref[r, vcol[h]]
                    b_ref[...] = b
                    qx_ref[...] = q

                    def score_row(tok, att_t):
                        w = jnp.where(rowi[:, 0:1] <= tok, jnp.exp2(b_ref[pl.ds(tok, 1), :] - b), 0.0)
                        colv = jnp.sum(qx_ref[pl.ds(tok, 1), :] * k * w, axis=-1, keepdims=True)
                        return att_t + jnp.where(coli == tok, colv, 0.0)

                    att_t = lax.fori_loop(0, CHUNK, score_row, jnp.zeros((CHUNK, CHUNK), F32))
                    q_dec = (q * jnp.exp2(b)).astype(BF16)
                    k_end = (k * jnp.exp2(blast - b)).astype(BF16)
                    st = st_ref[nxt, h]
                    ov = _dot_ta(att_t.astype(BF16), v) + _dot(q_dec, st.astype(BF16))
                    st_ref[nxt, h] = st * decay_column(blast) + _dot_ta(k_end, v)
                    gla_finish_rows(ov, r, h)
                return carry

            st_ref[nxt] = st_ref[cur]
            lax.fori_loop(0, n_chunks, chunk, 0)

        pending = list(range(len(out_pieces))) if do_out else []

        def out_some(n):
            for _ in range(min(n, len(pending))):
                out_piece(pending.pop(0))

        per_slot = -(-len(out_pieces) // (4 * (n_chunks // group)))
        for h in heads:
            state[h] = st_ref[cur, h]
        for g0 in range(0, n_chunks, group):
            cs = list(range(g0, g0 + group))
            half = group // 2
            for c in cs:
                gla_cumsum(c)
            out_some(per_slot)
            for c in cs:
                attn_scores(c)
            out_some(per_slot)
            for c in cs:
                gla_prepare(c)
            for c in cs:
                gla_scores(c)
            out_some(per_slot)
            for c in cs[:half]:
                attn_softmax(c)
            for c in cs[:half]:
                attn_pv(c)
            out_some(per_slot)
            for c in cs:
                gla_outputs(c)
            for c in cs[half:]:
                attn_softmax(c)
            for c in cs[half:]:
                attn_pv(c)
            for c in cs:
                gla_finish(c)
            for c in cs:
                attn_finish(c)
        out_some(len(pending))
        for h in heads:
            st_ref[nxt, h] = state[h]
        total = functools.reduce(jnp.minimum, [g_b[c][CHUNK - 1:CHUNK, :] for c in range(n_chunks)])

        @pl.when(jnp.min(total) < -LOG2_DECAY_LIMIT)
        def _():
            gla_exact_step()

        @pl.when(i == last)
        def _():
            sout_ref[0] = st_ref[nxt]

    @pl.when(t == 0)
    def _():
        mixers(do_out=False)

    @pl.when((t > 0) & (t < n_steps))
    def _():
        mixers(do_out=True)

    @pl.when(t == n_steps)
    def _():
        for n in range(len(out_pieces)):
            out_piece(n)


def _layer_prompt(x2d, q, k, v, ga, sink_rows, qg, kg, vg, gg, la_hi, la_lo, tril2, w_out, batch, seq,
                  tq, group):
    nblk = seq // tq
    n_steps = batch * nblk
    per_win = tq // WINDOW
    blk_of = lambda t: jnp.minimum(t, n_steps - 1)
    rows = lambda w: pl.BlockSpec((tq, w), lambda t: (blk_of(t), 0))
    late = lambda w: pl.BlockSpec((tq, w), lambda t: (jnp.maximum(t - 1, 0), 0))
    cur = pl.BlockSpec((1, tq, KV_WIDTH), lambda t: (blk_of(t) // nblk, blk_of(t) % nblk, 0))
    prev = pl.BlockSpec((1, WINDOW, KV_WIDTH),
                        lambda t: (blk_of(t) // nblk, jnp.maximum((blk_of(t) % nblk) * per_win - 1, 0), 0))
    win = pl.BlockSpec((1, WINDOW, KV_WIDTH), lambda t: (blk_of(t) // nblk, 0, 0))
    state = pl.BlockSpec((1, N_GLA, GLA_DK, GLA_DV), lambda t: (blk_of(t) // nblk, 0, 0, 0))
    k3 = k.reshape(batch, seq, KV_WIDTH)
    v3 = v.reshape(batch, seq, KV_WIDTH)
    return pl.pallas_call(
        functools.partial(_layer_prompt_kernel, tq=tq, group=group, nblk=nblk, n_steps=n_steps),
        grid=(n_steps + 1,),
        in_specs=[rows(ATT_WIDTH), prev, cur, prev, cur, rows(ATT_WIDTH), _const_spec(sink_rows.shape),
                  rows(GLA_KW), rows(GLA_KW), rows(GLA_WIDTH), rows(GLA_WIDTH), rows(GLA_KW),
                  rows(GLA_KW), _const_spec(tril2.shape), late(D_MODEL), _const_spec(w_out.shape)],
        out_specs=[late(D_MODEL), win, win, state],
        out_shape=[jax.ShapeDtypeStruct((batch * seq, D_MODEL), F32),
                   jax.ShapeDtypeStruct((batch, WINDOW, KV_WIDTH), F32),
                   jax.ShapeDtypeStruct((batch, WINDOW, KV_WIDTH), F32),
                   jax.ShapeDtypeStruct((batch, N_GLA, GLA_DK, GLA_DV), F32)],
        scratch_shapes=[pltpu.VMEM((WINDOW + tq, KV_WIDTH), BF16),
                        pltpu.VMEM((WINDOW + tq, KV_WIDTH), BF16),
                        pltpu.VMEM((2, N_GLA, GLA_DK, GLA_DV), F32),
                        pltpu.VMEM((CHUNK, GLA_DK), F32),
                        pltpu.VMEM((CHUNK, GLA_DK), F32),
                        pltpu.VMEM((2, tq, D_MODEL), BF16)],
        compiler_params=pltpu.CompilerParams(
            dimension_semantics=("arbitrary",), vmem_limit_bytes=VMEM_LIMIT),
        name="layer_prompt",
    )(q, k3, k3, v3, v3, ga, sink_rows, qg, kg, vg, gg, la_hi, la_lo, tril2, x2d, w_out)


def _attn_sample_kernel(q_ref, kn_ref, vn_ref, ck_ref, cv_ref, ga_ref, sink_ref,
                        o_ref, wk_ref, wv_ref, *, t, bb):
    n_cache = ck_ref.shape[1]
    assert n_cache == LANES and t <= LANES
    lane = lax.broadcasted_iota(jnp.int32, (1, LANES), 1)
    lo = lane < HEAD_DIM
    is_new = lane < t
    col_group = lax.broadcasted_iota(jnp.int32, (1, KV_WIDTH), 1) // HEAD_DIM
    seg = 2 * LANES
    ones_bd = jnp.concatenate(
        [jnp.broadcast_to(jnp.where(col_group == g, 1.0, 0.0).astype(BF16), (seg, KV_WIDTH))
         for g in range(N_KV)], axis=0)
    pad = jnp.zeros((LANES - t, KV_WIDTH), BF16)
    neg_inf = jnp.float32(-jnp.inf)

    s, v_all = [], []
    for b in range(bb):
        rows = slice(b * t, (b + 1) * t)
        kn, vn, ck, cv = kn_ref[rows, :], vn_ref[rows, :], ck_ref[b], cv_ref[b]
        wk_ref[b] = jnp.concatenate([ck[t:, :], kn], axis=0)
        wv_ref[b] = jnp.concatenate([cv[t:, :], vn], axis=0)
        kcat = jnp.concatenate([ck.astype(BF16), kn.astype(BF16), pad], axis=0)
        vcat = jnp.concatenate([cv.astype(BF16), vn.astype(BF16), pad], axis=0)
        zero = jnp.zeros_like(kcat)
        k_bd = jnp.concatenate([jnp.where(col_group == g, kcat, zero) for g in range(N_KV)], axis=0)
        v_bd = jnp.concatenate([jnp.where(col_group == g, vcat, zero) for g in range(N_KV)], axis=0)
        v_all.append(jnp.concatenate([v_bd, ones_bd], axis=1))
        qs = jnp.concatenate(
            [q_ref[rows, r * KV_WIDTH:(r + 1) * KV_WIDTH] for r in range(GQA_REP)], axis=0)
        s.append(_dot_t(qs, k_bd))
    p, m = [], []
    for b in range(bb):
        c = [s[b][:, LANES * n:LANES * (n + 1)] for n in range(2 * N_KV)]
        c = [cn if n % 2 == 0 else jnp.where(is_new, cn, neg_inf) for n, cn in enumerate(c)]
        mb = [jnp.maximum(jnp.max(jnp.maximum(c[2 * g], c[2 * g + 1]), axis=-1, keepdims=True),
                          sink_ref[g]) for g in range(N_KV)]
        p.append(jnp.concatenate([jnp.exp2(c[n] - mb[n // 2]) for n in range(2 * N_KV)],
                                 axis=1).astype(BF16))
        m.append(mb)
    ov = [_dot(p[b], v_all[b]) for b in range(bb)]
    for b in range(bb):
        rows = slice(b * t, (b + 1) * t)
        es = [jnp.exp2(sink_ref[g] - m[b][g]) for g in range(N_KV)]
        e_sink = jnp.concatenate([jnp.where(lo, es[0], es[1]), jnp.where(lo, es[2], es[3])], axis=1)
        o = ov[b][:, :KV_WIDTH] / (ov[b][:, KV_WIDTH:] + e_sink)
        for r in range(GQA_REP):
            cols = slice(r * KV_WIDTH, (r + 1) * KV_WIDTH)
            gate = ga_ref[rows, cols].astype(F32)
            o_ref[rows, cols] = (o[r * t:(r + 1) * t, :] * gate).astype(BF16)


def _attn_sample(q, k, v, cache_k, cache_v, ga, sink_rows, batch, t, bb):
    n_cache = cache_k.shape[1]
    rows = lambda w: pl.BlockSpec((bb * t, w), lambda b: (b, 0))
    cache = pl.BlockSpec((bb, n_cache, KV_WIDTH), lambda b: (b, 0, 0))
    return pl.pallas_call(
        functools.partial(_attn_sample_kernel, t=t, bb=bb),
        grid=(batch // bb,),
        in_specs=[rows(ATT_WIDTH), rows(KV_WIDTH), rows(KV_WIDTH), cache, cache, rows(ATT_WIDTH),
                  _const_spec(sink_rows.shape)],
        out_specs=[rows(ATT_WIDTH), cache, cache],
        out_shape=[jax.ShapeDtypeStruct((batch * t, ATT_WIDTH), BF16),
                   jax.ShapeDtypeStruct((batch, n_cache, KV_WIDTH), F32),
                   jax.ShapeDtypeStruct((batch, n_cache, KV_WIDTH), F32)],
        compiler_params=pltpu.CompilerParams(
            dimension_semantics=("arbitrary",), vmem_limit_bytes=VMEM_LIMIT),
        name="attn_sample",
    )(q, k, v, cache_k, cache_v, ga, sink_rows)


def _gla_kernel(qg_ref, kg_ref, vg_ref, gg_ref, lah_ref, lal_ref, s0_ref, tril_ref,
                o_ref, sout_ref, st_ref, b_ref, q_ref, *, blk, tg, unroll):
    i = pl.program_id(1)
    n_groups = tg // (blk * unroll)

    @pl.when(i == 0)
    def _():
        st_ref[0] = s0_ref[0]

    rowi = lax.broadcasted_iota(jnp.int32, (blk, blk), 0)
    coli = lax.broadcasted_iota(jnp.int32, (blk, blk), 1)
    causal = rowi >= coli
    eye = (lax.broadcasted_iota(jnp.int32, (GLA_DK, GLA_DK), 0)
           == lax.broadcasted_iota(jnp.int32, (GLA_DK, GLA_DK), 1))

    heads = range(N_GLA)
    kcol = [slice(h * GLA_DK, (h + 1) * GLA_DK) for h in heads]
    vcol = [slice(h * GLA_DV, (h + 1) * GLA_DV) for h in heads]

    def cum_log2_decay(r):
        return _dot(tril_ref[...], jnp.concatenate([lah_ref[r, :], lal_ref[r, :]], axis=0))

    def decay_column(blast):
        dec_row = jnp.broadcast_to(jnp.exp2(blast), (GLA_DK, GLA_DK))
        return jnp.sum(jnp.where(eye, dec_row, 0.0), axis=-1, keepdims=True)

    def finish(ov, r, h):
        ms = jnp.mean(ov * ov, axis=-1, keepdims=True)
        gate = gg_ref[r, vcol[h]].astype(F32)
        o_ref[r, vcol[h]] = (ov * lax.rsqrt(ms + EPS) * gate).astype(BF16)

    def exact_group(rows, cur, nxt):
        st = [st_ref[cur, h] for h in heads]
        for r in rows:
            bcum = cum_log2_decay(r)
            for h in heads:
                b = bcum[:, kcol[h]]
                blast = b[blk - 1:blk, :]
                q = qg_ref[r, kcol[h]].astype(F32)
                k = kg_ref[r, kcol[h]].astype(F32)
                v = vg_ref[r, vcol[h]]
                b_ref[...] = b
                q_ref[...] = q

                def score_row(t, att_t):
                    w = jnp.where(rowi[:, 0:1] <= t, jnp.exp2(b_ref[pl.ds(t, 1), :] - b), 0.0)
                    col = jnp.sum(q_ref[pl.ds(t, 1), :] * k * w, axis=-1, keepdims=True)
                    return att_t + jnp.where(coli == t, col, 0.0)

                att_t = lax.fori_loop(0, blk, score_row, jnp.zeros((blk, blk), F32))
                q_dec = (q * jnp.exp2(b)).astype(BF16)
                k_end = (k * jnp.exp2(blast - b)).astype(BF16)
                ov = _dot_ta(att_t.astype(BF16), v) + _dot(q_dec, st[h].astype(BF16))
                st[h] = st[h] * decay_column(blast) + _dot_ta(k_end, v)
                finish(ov, r, h)
        for h in heads:
            st_ref[nxt, h] = st[h]

    def group(j, carry):
        rows = [pl.ds(pl.multiple_of((j * unroll + u) * blk, blk), blk) for u in range(unroll)]
        cur = (i * n_groups + j) % 2
        nxt = 1 - cur
        bcum = [cum_log2_decay(r) for r in rows]
        q_dec, k_inv, k_end, dec_col = {}, {}, {}, {}
        for u, r in enumerate(rows):
            for h in heads:
                b = bcum[u][:, kcol[h]]
                blast = b[blk - 1:blk, :]
                q = qg_ref[r, kcol[h]].astype(F32)
                k = kg_ref[r, kcol[h]].astype(F32)
                decay = jnp.exp2(b)
                q_dec[u, h] = (q * decay).astype(BF16)
                k_inv[u, h] = (k * (1.0 / decay)).astype(BF16)
                k_end[u, h] = (k * jnp.exp2(blast - b)).astype(BF16)
                dec_col[u, h] = decay_column(blast)
        att = {(u, h): _dot_t(q_dec[u, h], k_inv[u, h]) for u in range(unroll) for h in heads}
        d_st = {(u, h): _dot_ta(k_end[u, h], vg_ref[rows[u], vcol[h]])
                for u in range(unroll) for h in heads}
        att = {key: jnp.where(causal, a, 0.0).astype(BF16) for key, a in att.items()}
        st = [st_ref[cur, h] for h in heads]
        o = {}
        for u, r in enumerate(rows):
            for h in heads:
                o[u, h] = _dot(att[u, h], vg_ref[r, vcol[h]]) + _dot(q_dec[u, h], st[h].astype(BF16))
                st[h] = st[h] * dec_col[u, h] + d_st[u, h]
        for h in heads:
            st_ref[nxt, h] = st[h]
        for u, r in enumerate(rows):
            for h in heads:
                finish(o[u, h], r, h)

        total = functools.reduce(jnp.minimum, [bc[blk - 1:blk, :] for bc in bcum])

        @pl.when(jnp.min(total) < -LOG2_DECAY_LIMIT)
        def _():
            exact_group(rows, cur, nxt)
        return carry

    lax.fori_loop(0, n_groups, group, 0)

    @pl.when(i == pl.num_programs(1) - 1)
    def _():
        sout_ref[0] = st_ref[(pl.num_programs(1) * n_groups) % 2]


def _gla(qg, kg, vg, gg, la_hi, la_lo, s0, tril2, batch, seq, blk, tg, unroll=1):
    nblk = seq // tg
    rows = lambda w: pl.BlockSpec((tg, w), lambda b, i: (b * nblk + i, 0))
    state = pl.BlockSpec((1, N_GLA, GLA_DK, GLA_DV), lambda b, i: (b, 0, 0, 0))
    return pl.pallas_call(
        functools.partial(_gla_kernel, blk=blk, tg=tg, unroll=unroll),
        grid=(batch, nblk),
        in_specs=[rows(GLA_KW), rows(GLA_KW), rows(GLA_WIDTH), rows(GLA_WIDTH), rows(GLA_KW),
                  rows(GLA_KW), state, _const_spec(tril2.shape)],
        out_specs=[rows(GLA_WIDTH), state],
        out_shape=[jax.ShapeDtypeStruct((batch * seq, GLA_WIDTH), BF16),
                   jax.ShapeDtypeStruct((batch, N_GLA, GLA_DK, GLA_DV), F32)],
        scratch_shapes=[pltpu.VMEM((2, N_GLA, GLA_DK, GLA_DV), F32),
                        pltpu.VMEM((blk, GLA_DK), F32),
                        pltpu.VMEM((blk, GLA_DK), F32)],
        compiler_params=pltpu.CompilerParams(
            dimension_semantics=("arbitrary", "arbitrary"), vmem_limit_bytes=VMEM_LIMIT),
        name="gla",
    )(qg, kg, vg, gg, la_hi, la_lo, s0, tril2)


def _outproj_kernel(x_ref, ma_ref, mg_ref, w_ref, y_ref):
    y_ref[...] = (x_ref[...] + _dot(ma_ref[...], w_ref[:ATT_WIDTH, :])
                  + _dot(mg_ref[...], w_ref[ATT_WIDTH:, :]))


def _outproj(x2d, mix_a, mix_g, w, tm):
    m = x2d.shape[0]
    row = lambda w: pl.BlockSpec((tm, w), lambda i: (i, 0))
    return pl.pallas_call(
        _outproj_kernel,
        grid=(m // tm,),
        in_specs=[row(D_MODEL), row(ATT_WIDTH), row(GLA_WIDTH), _const_spec(w.shape)],
        out_specs=row(D_MODEL),
        out_shape=jax.ShapeDtypeStruct((m, D_MODEL), F32),
        compiler_params=pltpu.CompilerParams(
            dimension_semantics=("arbitrary",), vmem_limit_bytes=VMEM_LIMIT),
        name="outproj",
    )(x2d, mix_a, mix_g, w)


def _regroup_unit(u):
    return (u % N_KV) * GQA_REP + u // N_KV


def _prep_w_in_kernel(u0_ref, u1_ref, u2_ref, u3_ref, lr_ref, main_ref, wlr_ref):
    units = jnp.concatenate([u0_ref[...], u1_ref[...], u2_ref[...], u3_ref[...]], axis=0)
    main_ref[...] = units.T.astype(BF16)

    @pl.when(pl.program_id(0) == 0)
    def _():
        lr = lr_ref[...]
        padded = jnp.concatenate([lr, jnp.zeros((LANES - lr.shape[0], lr.shape[1]), F32)], axis=0)
        wlr_ref[...] = padded.T.astype(BF16)


def _prep_w_in(w_in_t):
    _, k = w_in_t.shape
    units_per_block = KV_WIDTH // HEAD_DIM
    n_att = ATT_WIDTH // HEAD_DIM

    def src_unit(j, p):
        u = j * units_per_block + p
        q0, ga0 = OFF_Q // HEAD_DIM, OFF_GA // HEAD_DIM
        u = jnp.where((u >= q0) & (u < q0 + n_att), q0 + _regroup_unit(u - q0), u)
        return jnp.where((u >= ga0) & (u < ga0 + n_att), ga0 + _regroup_unit(u - ga0), u)

    unit = lambda p: pl.BlockSpec((HEAD_DIM, k), lambda j: (src_unit(j, p), 0))
    return pl.pallas_call(
        _prep_w_in_kernel,
        grid=(MAIN_COLS // KV_WIDTH,),
        in_specs=[unit(0), unit(1), unit(2), unit(3),
                  pl.BlockSpec((GATE_RANK, k), lambda j: (MAIN_COLS // GATE_RANK, 0))],
        out_specs=[pl.BlockSpec((k, KV_WIDTH), lambda j: (0, j)),
                   pl.BlockSpec((k, LANES), lambda j: (0, 0))],
        out_shape=[jax.ShapeDtypeStruct((k, MAIN_COLS), BF16),
                   jax.ShapeDtypeStruct((k, LANES), BF16)],
        compiler_params=pltpu.CompilerParams(
            dimension_semantics=("arbitrary",), vmem_limit_bytes=VMEM_LIMIT),
        name="prep_w_in",
    )(w_in_t, w_in_t, w_in_t, w_in_t, w_in_t)


def _prep_w_out_kernel(u0_ref, u1_ref, u2_ref, u3_ref, o_ref):
    o_ref[...] = jnp.concatenate(
        [u0_ref[...], u1_ref[...], u2_ref[...], u3_ref[...]], axis=0).astype(BF16)


def _prep_w_out(w_out):
    rows, cols = w_out.shape
    units_per_block = KV_WIDTH // HEAD_DIM
    n_att = ATT_WIDTH // HEAD_DIM

    def src_unit(j, p):
        u = j * units_per_block + p
        return jnp.where(u < n_att, _regroup_unit(u), u)

    unit = lambda p: pl.BlockSpec((HEAD_DIM, cols), lambda j: (src_unit(j, p), 0))
    return pl.pallas_call(
        _prep_w_out_kernel,
        grid=(rows // KV_WIDTH,),
        in_specs=[unit(0), unit(1), unit(2), unit(3)],
        out_specs=pl.BlockSpec((KV_WIDTH, cols), lambda j: (j, 0)),
        out_shape=jax.ShapeDtypeStruct((rows, cols), BF16),
        compiler_params=pltpu.CompilerParams(dimension_semantics=("arbitrary",)),
        name="prep_w_out",
    )(w_out, w_out, w_out, w_out)


def _tril2(blk):
    t = np.tril(np.ones((blk, blk), np.float32))
    return jnp.asarray(np.concatenate([t, t], axis=1), BF16)


def _prep(norm_g, w_in, w_gate_up, b_gate, q_norm_g, k_norm_g, sinks, gla_norm_g, w_out):
    w_main, w_lr = _prep_w_in(w_in.T)
    w_gu = jnp.pad(w_gate_up, ((0, LANES - GATE_RANK), (0, 0))).astype(BF16)
    g_qk = jnp.concatenate([jnp.tile(q_norm_g, N_HEADS) * (ATT_SCALE * LOG2E),
                            jnp.tile(k_norm_g, N_KV)])[None, :]

    def sink_rows(t):
        sk = sinks.astype(F32).reshape(N_KV, GQA_REP) * LOG2E
        return jnp.broadcast_to(sk[:, :, None, None], (N_KV, GQA_REP, t, LANES)).reshape(
            N_KV, GQA_REP * t, LANES)

    return dict(norm_g=norm_g[None, :], w_main=w_main, w_lr=w_lr, w_gu=w_gu, b_gate=b_gate[None, :],
                g_qk=g_qk, g_gla=jnp.tile(gla_norm_g, N_GLA)[None, :],
                w_out=_prep_w_out(w_out), sink_rows=sink_rows)


def kernel(x_prompt, x_sample, cache_k, cache_v, state_gla, norm_g, w_in, w_gate_up, b_gate,
           q_norm_g, k_norm_g, sinks, gla_norm_g, w_out):
    depth = norm_g.shape[0]
    assert depth == 1
    B, S, _ = x_prompt.shape
    DB, T, _ = x_sample.shape
    n_cache = cache_k.shape[2]
    assert n_cache == WINDOW
    p = _prep(norm_g[0], w_in[0], w_gate_up[0], b_gate[0], q_norm_g[0], k_norm_g[0], sinks[0],
              gla_norm_g[0], w_out[0])

    xp = x_prompt.reshape(B * S, D_MODEL)
    q, k, v, ga, qg, kg, vg, gg, la_hi, la_lo = _inproj(xp, p, tm=INPROJ_ROWS)
    y_p, win_k_p, win_v_p, gla_p = _layer_prompt(
        xp, q, k, v, ga, p["sink_rows"](CHUNK), qg, kg, vg, gg, la_hi, la_lo, _tril2(CHUNK), p["w_out"],
        B, S, tq=LAYER_ROWS, group=LAYER_GROUP)
    y_p = y_p.reshape(B, S, D_MODEL)

    xs = x_sample.reshape(DB * T, D_MODEL)
    q, k, v, ga, qg, kg, vg, gg, la_hi, la_lo = _inproj(xs, p, tm=DB * T)
    ck = cache_k[0].reshape(DB, n_cache, KV_WIDTH)
    cv = cache_v[0].reshape(DB, n_cache, KV_WIDTH)
    mix_a, win_k_s, win_v_s = _attn_sample(q, k, v, ck, cv, ga, p["sink_rows"](T), DB, T,
                                           bb=SAMPLE_STREAMS)
    blk = min(CHUNK, T)
    mix_g, gla_s = _gla(qg, kg, vg, gg, la_hi, la_lo, state_gla[0], _tril2(blk), DB, T,
                        blk=blk, tg=T)
    y_s = _outproj(xs, mix_a, mix_g, p["w_out"], tm=DB * T).reshape(DB, T, D_MODEL)

    shape5 = lambda a: a.reshape(1, a.shape[0], n_cache, N_KV, HEAD_DIM)
    return (y_p, y_s, shape5(win_k_p), shape5(win_v_p), gla_p[None],
            shape5(win_k_s), shape5(win_v_s), gla_s[None])
```

```python
import functools

import jax
import jax.numpy as jnp
import numpy as np
from jax import lax
from jax.experimental import pallas as pl
from jax.experimental.pallas import tpu as pltpu

D_MODEL = 2048
CHUNK = 64
WINDOW = 128
HEAD_DIM = 64
N_HEADS = 16
N_KV = 4
GQA_REP = N_HEADS // N_KV
ATT_WIDTH = N_HEADS * HEAD_DIM
KV_WIDTH = N_KV * HEAD_DIM
N_GLA = 4
GLA_DK = 128
GLA_DV = 256
GLA_KW = N_GLA * GLA_DK
GLA_WIDTH = N_GLA * GLA_DV
GATE_RANK = 16
GATE_TAU = 16.0
EPS = 1e-6
ATT_SCALE = HEAD_DIM ** -0.5
LOG2E = float(np.log2(np.e))
LOG2_DECAY_LIMIT = 100.0
WIN_CHUNKS = WINDOW // CHUNK
WIN_KEYS = WINDOW + CHUNK

LANES = 128
QK_WIDTH = ATT_WIDTH + KV_WIDTH
OFF_Q, OFF_K, OFF_V = 0, ATT_WIDTH, ATT_WIDTH + KV_WIDTH
OFF_GA = OFF_V + KV_WIDTH
OFF_QG = OFF_GA + ATT_WIDTH
OFF_KG = OFF_QG + GLA_KW
OFF_VG = OFF_KG + GLA_KW
OFF_GG = OFF_VG + GLA_WIDTH
MAIN_COLS = OFF_GG + GLA_WIDTH

VMEM_LIMIT = 56 * 1024 * 1024
INPROJ_ROWS = 512
INPROJ_SUB = 256
LAYER_ROWS = 512
LAYER_GROUP = 4
SAMPLE_STREAMS = 4

BF16 = jnp.bfloat16
F32 = jnp.float32


def _dot(a, b):
    return jnp.dot(a, b, preferred_element_type=F32)


def _dot_t(a, b):
    return lax.dot_general(a, b, (((1,), (1,)), ((), ())), preferred_element_type=F32)


def _dot_ta(a, b):
    return lax.dot_general(a, b, (((0,), (0,)), ((), ())), preferred_element_type=F32)


def _silu(x):
    return x / (1.0 + jnp.exp(-x))


def _const_spec(shape):
    nd = len(shape)
    return pl.BlockSpec(shape, lambda *_: (0,) * nd, pipeline_mode=pl.Buffered(1))


def _inproj_kernel(x_ref, ng_ref, w_ref, wlr_ref, wgu_ref, bg_ref, gqk_ref, gng_ref,
                   q_ref, k_ref, v_ref, ga_ref, qg_ref, kg_ref, vg_ref, gg_ref, lah_ref, lal_ref,
                   *, sub):
    lo = lax.broadcasted_iota(jnp.int32, (1, LANES), 1) < HEAD_DIM
    for s in range(x_ref.shape[0] // sub):
        rows = slice(s * sub, (s + 1) * sub)
        x = x_ref[rows, :]
        ms = jnp.mean(x * x, axis=-1, keepdims=True)
        xn = (x * lax.rsqrt(ms + EPS) * ng_ref[...]).astype(BF16)

        qk = _dot(xn, w_ref[:, OFF_Q:OFF_V])
        ga_ref[rows, :] = _silu(_dot(xn, w_ref[:, OFF_GA:OFF_QG])).astype(BF16)
        lr = _dot(xn, wlr_ref[...]).astype(BF16)
        qg_ref[rows, :] = (_dot(xn, w_ref[:, OFF_QG:OFF_KG]) * (GLA_DK ** -0.5)).astype(BF16)
        kg_ref[rows, :] = _dot(xn, w_ref[:, OFF_KG:OFF_VG]).astype(BF16)
        z = _dot(lr, wgu_ref[...]) + bg_ref[...]
        gg_ref[rows, :] = (_silu(_dot(xn, w_ref[:, OFF_GG:MAIN_COLS])) * gng_ref[...]).astype(BF16)
        vg_ref[rows, :] = _dot(xn, w_ref[:, OFF_VG:OFF_GG]).astype(BF16)
        v_ref[rows, :] = _dot(xn, w_ref[:, OFF_V:OFF_GA])

        for n in range(QK_WIDTH // LANES):
            cols = slice(n * LANES, (n + 1) * LANES)
            blk = qk[:, cols]
            sq = blk * blk
            ms_lo = jnp.sum(jnp.where(lo, sq, 0.0), axis=-1, keepdims=True) * (1.0 / HEAD_DIM)
            ms_hi = jnp.sum(jnp.where(lo, 0.0, sq), axis=-1, keepdims=True) * (1.0 / HEAD_DIM)
            scale = jnp.where(lo, lax.rsqrt(ms_lo + EPS), lax.rsqrt(ms_hi + EPS))
            normed = blk * scale * gqk_ref[:, cols]
            if n < ATT_WIDTH // LANES:
                q_ref[rows, cols] = normed.astype(BF16)
            else:
                k_ref[rows, n * LANES - ATT_WIDTH:(n + 1) * LANES - ATT_WIDTH] = normed
        log_sig = jnp.minimum(z, 0.0) - jnp.log(1.0 + jnp.exp(-jnp.abs(z)))
        la = log_sig * (LOG2E / GATE_TAU)
        la_hi = la.astype(BF16)
        lah_ref[rows, :] = la_hi
        lal_ref[rows, :] = (la - la_hi.astype(F32)).astype(BF16)


def _inproj(x2d, p, tm):
    m = x2d.shape[0]
    row = lambda w: pl.BlockSpec((tm, w), lambda i: (i, 0))
    outs = [(ATT_WIDTH, BF16), (KV_WIDTH, F32), (KV_WIDTH, F32), (ATT_WIDTH, BF16),
            (GLA_KW, BF16), (GLA_KW, BF16), (GLA_WIDTH, BF16), (GLA_WIDTH, BF16),
            (GLA_KW, BF16), (GLA_KW, BF16)]
    consts = [p["norm_g"], p["w_main"], p["w_lr"], p["w_gu"], p["b_gate"], p["g_qk"], p["g_gla"]]
    return pl.pallas_call(
        functools.partial(_inproj_kernel, sub=min(tm, INPROJ_SUB)),
        grid=(m // tm,),
        in_specs=[row(D_MODEL)] + [_const_spec(c.shape) for c in consts],
        out_specs=[row(w) for w, _ in outs],
        out_shape=[jax.ShapeDtypeStruct((m, w), dt) for w, dt in outs],
        compiler_params=pltpu.CompilerParams(
            dimension_semantics=("arbitrary",), vmem_limit_bytes=VMEM_LIMIT),
        name="inproj",
    )(x2d, *consts)


def _layer_prompt_kernel(q_ref, kp_ref, kc_ref, vp_ref, vc_ref, ga_ref, sink_ref,
                         qg_ref, kg_ref, vg_ref, gg_ref, lah_ref, lal_ref, tril_ref, x_ref, w_ref,
                         y_ref, wk_ref, wv_ref, sout_ref,
                         kw_ref, vw_ref, st_ref, b_ref, qx_ref, mix_ref, *, tq, group, nblk, n_steps):
    n_chunks = tq // CHUNK
    assert n_chunks % group == 0 and group >= WIN_CHUNKS and tq >= WINDOW
    t = pl.program_id(0)
    i = lax.rem(jnp.minimum(t, n_steps - 1), nblk)
    last = nblk - 1
    cur = i % 2
    nxt = 1 - cur
    mix_slot = t % 2
    prev_slot = 1 - mix_slot

    out_rows, out_cols = 512, 512
    out_pieces = [(slice(r, r + out_rows), slice(c, c + out_cols))
                  for r in range(0, tq, out_rows) for c in range(0, D_MODEL, out_cols)]

    def out_piece(n):
        r, c = out_pieces[n]
        y_ref[r, c] = x_ref[r, c] + _dot(mix_ref[prev_slot, r, :], w_ref[:, c])

    def mixers(do_out):
        kw_ref[0:WINDOW, :] = kp_ref[0].astype(BF16)
        kw_ref[WINDOW:, :] = kc_ref[0].astype(BF16)
        vw_ref[0:WINDOW, :] = vp_ref[0].astype(BF16)
        vw_ref[WINDOW:, :] = vc_ref[0].astype(BF16)

        @pl.when(i == 0)
        def _():
            st_ref[0] = jnp.zeros(st_ref.shape[1:], F32)

        @pl.when(i == last)
        def _():
            wk_ref[0] = kc_ref[0, tq - WINDOW:, :]
            wv_ref[0] = vc_ref[0, tq - WINDOW:, :]

        lane = lax.broadcasted_iota(jnp.int32, (1, LANES), 1)
        lo = lane < HEAD_DIM
        col_group = lax.broadcasted_iota(jnp.int32, (1, KV_WIDTH), 1) // HEAD_DIM
        col = lax.broadcasted_iota(jnp.int32, (1, N_KV * WIN_KEYS), 1)
        key_in_seg = col - WIN_KEYS * sum((col >= g * WIN_KEYS).astype(jnp.int32) for g in range(1, N_KV))
        neg_inf = jnp.float32(-jnp.inf)
        rows = [slice(c * CHUNK, (c + 1) * CHUNK) for c in range(n_chunks)]
        a_s, a_v, a_p, a_den, a_ov = {}, {}, {}, {}, {}

        def block_diag(win):
            zero = jnp.zeros_like(win)
            return jnp.concatenate([jnp.where(col_group == g, win, zero) for g in range(N_KV)], axis=0)

        def attn_scores(c):
            k_bd = block_diag(kw_ref[c * CHUNK:c * CHUNK + WIN_KEYS, :])
            a_v[c] = block_diag(vw_ref[c * CHUNK:c * CHUNK + WIN_KEYS, :])
            qs = jnp.concatenate(
                [q_ref[rows[c], r * KV_WIDTH:(r + 1) * KV_WIDTH] for r in range(GQA_REP)], axis=0)
            a_s[c] = _dot_t(qs, k_bd)

        def attn_softmax(c):
            su = a_s.pop(c)
            if c < WIN_CHUNKS:
                first_valid = jnp.where(i == 0, (WIN_CHUNKS - c) * CHUNK, 0)
                su = jnp.where(key_in_seg >= first_valid, su, neg_inf)
            cs = [su[:, LANES * n:LANES * (n + 1)] for n in range(6)]
            seg_max = [
                jnp.maximum(cs[0], jnp.where(lo, cs[1], neg_inf)),
                jnp.maximum(jnp.where(lo, neg_inf, cs[1]), cs[2]),
                jnp.maximum(cs[3], jnp.where(lo, cs[4], neg_inf)),
                jnp.maximum(jnp.where(lo, neg_inf, cs[4]), cs[5]),
            ]
            mu = [jnp.maximum(jnp.max(seg_max[g], axis=-1, keepdims=True), sink_ref[g])
                  for g in range(N_KV)]
            shift = [mu[0], jnp.where(lo, mu[0], mu[1]), mu[1], mu[2], jnp.where(lo, mu[2], mu[3]), mu[3]]
            e = [jnp.exp2(cs[n] - shift[n]) for n in range(6)]
            a_p[c] = jnp.concatenate(e, axis=1).astype(BF16)
            seg_sum = [e[0] + jnp.where(lo, e[1], 0.0), jnp.where(lo, 0.0, e[1]) + e[2],
                       e[3] + jnp.where(lo, e[4], 0.0), jnp.where(lo, 0.0, e[4]) + e[5]]
            a_den[c] = [jnp.sum(seg_sum[g], axis=-1, keepdims=True) + jnp.exp2(sink_ref[g] - mu[g])
                        for g in range(N_KV)]

        def attn_pv(c):
            a_ov[c] = _dot(a_p.pop(c), a_v.pop(c))

        def attn_finish(c):
            ov, den = a_ov.pop(c), a_den.pop(c)
            den = jnp.concatenate([jnp.where(lo, den[0], den[1]), jnp.where(lo, den[2], den[3])], axis=1)
            o = ov / den
            for r in range(GQA_REP):
                cols = slice(r * KV_WIDTH, (r + 1) * KV_WIDTH)
                gate = ga_ref[rows[c], cols].astype(F32)
                mix_ref[mix_slot, rows[c], cols] = (o[r * CHUNK:(r + 1) * CHUNK, :] * gate).astype(BF16)

        rowi = lax.broadcasted_iota(jnp.int32, (CHUNK, CHUNK), 0)
        coli = lax.broadcasted_iota(jnp.int32, (CHUNK, CHUNK), 1)
        causal = rowi >= coli
        eye = (lax.broadcasted_iota(jnp.int32, (GLA_DK, GLA_DK), 0)
               == lax.broadcasted_iota(jnp.int32, (GLA_DK, GLA_DK), 1))
        heads = range(N_GLA)
        kcol = [slice(h * GLA_DK, (h + 1) * GLA_DK) for h in heads]
        vcol = [slice(h * GLA_DV, (h + 1) * GLA_DV) for h in heads]
        mcol = [slice(ATT_WIDTH + h * GLA_DV, ATT_WIDTH + (h + 1) * GLA_DV) for h in heads]
        g_b, g_qd, g_ki, g_ke, g_dc, g_att, g_dst, g_o = {}, {}, {}, {}, {}, {}, {}, {}
        state = {}

        def cum_log2_decay(r):
            return _dot(tril_ref[...], jnp.concatenate([lah_ref[r, :], lal_ref[r, :]], axis=0))

        def decay_column(blast):
            dec_row = jnp.broadcast_to(jnp.exp2(blast), (GLA_DK, GLA_DK))
            return jnp.sum(jnp.where(eye, dec_row, 0.0), axis=-1, keepdims=True)

        def gla_finish_rows(ov, r, h):
            ms = jnp.mean(ov * ov, axis=-1, keepdims=True)
            gate = gg_ref[r, vcol[h]].astype(F32)
            mix_ref[mix_slot, r, mcol[h]] = (ov * lax.rsqrt(ms + EPS) * gate).astype(BF16)

        def gla_cumsum(c):
            g_b[c] = cum_log2_decay(rows[c])

        def gla_prepare(c):
            for h in heads:
                b = g_b[c][:, kcol[h]]
                blast = b[CHUNK - 1:CHUNK, :]
                q = qg_ref[rows[c], kcol[h]].astype(F32)
                k = kg_ref[rows[c], kcol[h]].astype(F32)
                decay = jnp.exp2(b)
                g_qd[c, h] = (q * decay).astype(BF16)
                g_ki[c, h] = (k * (1.0 / decay)).astype(BF16)
                g_ke[c, h] = (k * jnp.exp2(blast - b)).astype(BF16)
                g_dc[c, h] = decay_column(blast)

        def gla_scores(c):
            for h in heads:
                g_att[c, h] = _dot_t(g_qd[c, h], g_ki.pop((c, h)))
                g_dst[c, h] = _dot_ta(g_ke.pop((c, h)), vg_ref[rows[c], vcol[h]])

        def gla_outputs(c):
            for h in heads:
                att = jnp.where(causal, g_att.pop((c, h)), 0.0).astype(BF16)
                lhs = jnp.concatenate([g_qd.pop((c, h)), att], axis=1)
                rhs = jnp.concatenate([state[h].astype(BF16), vg_ref[rows[c], vcol[h]]], axis=0)
                g_o[c, h] = _dot(lhs, rhs)
                state[h] = state[h] * g_dc.pop((c, h)) + g_dst.pop((c, h))

        def gla_finish(c):
            for h in heads:
                gla_finish_rows(g_o.pop((c, h)), rows[c], h)

        def gla_exact_step():
            def chunk(c, carry):
                r = pl.ds(pl.multiple_of(c * CHUNK, CHUNK), CHUNK)
                bcum = cum_log2_decay(r)
                for h in heads:
                    b = bcum[:, kcol[h]]
                    blast = b[CHUNK - 1:CHUNK, :]
                    q = qg_ref[r, kcol[h]].astype(F32)
                    k = kg_ref[r, kcol[h]].astype(F32)
                    v = vg_ref[r, vcol[h]]
                    b_ref[...] = b
                    qx_ref[...] = q

                    def score_row(tok, att_t):
                        w = jnp.where(rowi[:, 0:1] <= tok, jnp.exp2(b_ref[pl.ds(tok, 1), :] - b), 0.0)
                        colv = jnp.sum(qx_ref[pl.ds(tok, 1), :] * k * w, axis=-1, keepdims=True)
                        return att_t + jnp.where(coli == tok, colv, 0.0)

                    att_t = lax.fori_loop(0, CHUNK, score_row, jnp.zeros((CHUNK, CHUNK), F32))
                    q_dec = (q * jnp.exp2(b)).astype(BF16)
                    k_end = (k * jnp.exp2(blast - b)).astype(BF16)
                    st = st_ref[nxt, h]
                    ov = _dot_ta(att_t.astype(BF16), v) + _dot(q_dec, st.astype(BF16))
                    st_ref[nxt, h] = st * decay_column(blast) + _dot_ta(k_end, v)
                    gla_finish_rows(ov, r, h)
                return carry

            st_ref[nxt] = st_ref[cur]
            lax.fori_loop(0, n_chunks, chunk, 0)

        pending = list(range(len(out_pieces))) if do_out else []
        n_slots = 4 * (n_chunks // group)
        slots_seen = [0]

        def out_slot():
            slots_seen[0] += 1
            due = len(out_pieces) * slots_seen[0] // n_slots - (len(out_pieces) - len(pending))
            for _ in range(min(due, len(pending))):
                out_piece(pending.pop(0))

        for h in heads:
            state[h] = st_ref[cur, h]
        for g0 in range(0, n_chunks, group):
            cs = list(range(g0, g0 + group))
            half = group // 2
            for c in cs:
                gla_cumsum(c)
            out_slot()
            for c in cs:
                attn_scores(c)
            out_slot()
            for c in cs:
                gla_prepare(c)
            for c in cs:
                gla_scores(c)
            out_slot()
            for c in cs[:half]:
                attn_softmax(c)
            for c in cs[:half]:
                attn_pv(c)
            out_slot()
            for c in cs:
                gla_outputs(c)
            for c in cs[half:]:
                attn_softmax(c)
            for c in cs[half:]:
                attn_pv(c)
            for c in cs:
                gla_finish(c)
            for c in cs:
                attn_finish(c)
        assert not pending
        for h in heads:
            st_ref[nxt, h] = state[h]
        total = functools.reduce(jnp.minimum, [g_b[c][CHUNK - 1:CHUNK, :] for c in range(n_chunks)])

        @pl.when(jnp.min(total) < -LOG2_DECAY_LIMIT)
        def _():
            gla_exact_step()

        @pl.when(i == last)
        def _():
            sout_ref[0] = st_ref[nxt]

    @pl.when(t == 0)
    def _():
        mixers(do_out=False)

    @pl.when((t > 0) & (t < n_steps))
    def _():
        mixers(do_out=True)

    @pl.when(t == n_steps)
    def _():
        for n in range(len(out_pieces)):
            out_piece(n)


def _layer_prompt(x2d, q, k, v, ga, sink_rows, qg, kg, vg, gg, la_hi, la_lo, tril2, w_out, batch, seq,
                  tq, group):
    nblk = seq // tq
    n_steps = batch * nblk
    per_win = tq // WINDOW
    blk_of = lambda t: jnp.minimum(t, n_steps - 1)
    rows = lambda w: pl.BlockSpec((tq, w), lambda t: (blk_of(t), 0))
    late = lambda w: pl.BlockSpec((tq, w), lambda t: (jnp.maximum(t - 1, 0), 0))
    cur = pl.BlockSpec((1, tq, KV_WIDTH), lambda t: (blk_of(t) // nblk, blk_of(t) % nblk, 0))
    prev = pl.BlockSpec((1, WINDOW, KV_WIDTH),
                        lambda t: (blk_of(t) // nblk, jnp.maximum((blk_of(t) % nblk) * per_win - 1, 0), 0))
    win = pl.BlockSpec((1, WINDOW, KV_WIDTH), lambda t: (blk_of(t) // nblk, 0, 0))
    state = pl.BlockSpec((1, N_GLA, GLA_DK, GLA_DV), lambda t: (blk_of(t) // nblk, 0, 0, 0))
    k3 = k.reshape(batch, seq, KV_WIDTH)
    v3 = v.reshape(batch, seq, KV_WIDTH)
    return pl.pallas_call(
        functools.partial(_layer_prompt_kernel, tq=tq, group=group, nblk=nblk, n_steps=n_steps),
        grid=(n_steps + 1,),
        in_specs=[rows(ATT_WIDTH), prev, cur, prev, cur, rows(ATT_WIDTH), _const_spec(sink_rows.shape),
                  rows(GLA_KW), rows(GLA_KW), rows(GLA_WIDTH), rows(GLA_WIDTH), rows(GLA_KW),
                  rows(GLA_KW), _const_spec(tril2.shape), late(D_MODEL), _const_spec(w_out.shape)],
        out_specs=[late(D_MODEL), win, win, state],
        out_shape=[jax.ShapeDtypeStruct((batch * seq, D_MODEL), F32),
                   jax.ShapeDtypeStruct((batch, WINDOW, KV_WIDTH), F32),
                   jax.ShapeDtypeStruct((batch, WINDOW, KV_WIDTH), F32),
                   jax.ShapeDtypeStruct((batch, N_GLA, GLA_DK, GLA_DV), F32)],
        scratch_shapes=[pltpu.VMEM((WINDOW + tq, KV_WIDTH), BF16),
                        pltpu.VMEM((WINDOW + tq, KV_WIDTH), BF16),
                        pltpu.VMEM((2, N_GLA, GLA_DK, GLA_DV), F32),
                        pltpu.VMEM((CHUNK, GLA_DK), F32),
                        pltpu.VMEM((CHUNK, GLA_DK), F32),
                        pltpu.VMEM((2, tq, D_MODEL), BF16)],
        compiler_params=pltpu.CompilerParams(
            dimension_semantics=("arbitrary",), vmem_limit_bytes=VMEM_LIMIT),
        name="layer_prompt",
    )(q, k3, k3, v3, v3, ga, sink_rows, qg, kg, vg, gg, la_hi, la_lo, tril2, x2d, w_out)


def _attn_sample_kernel(q_ref, kn_ref, vn_ref, ck_ref, cv_ref, ga_ref, sink_ref,
                        o_ref, wk_ref, wv_ref, *, t, bb):
    n_cache = ck_ref.shape[1]
    assert n_cache == LANES and t <= LANES
    lane = lax.broadcasted_iota(jnp.int32, (1, LANES), 1)
    lo = lane < HEAD_DIM
    is_new = lane < t
    col_group = lax.broadcasted_iota(jnp.int32, (1, KV_WIDTH), 1) // HEAD_DIM
    seg = 2 * LANES
    ones_bd = jnp.concatenate(
        [jnp.broadcast_to(jnp.where(col_group == g, 1.0, 0.0).astype(BF16), (seg, KV_WIDTH))
         for g in range(N_KV)], axis=0)
    pad = jnp.zeros((LANES - t, KV_WIDTH), BF16)
    neg_inf = jnp.float32(-jnp.inf)

    s, v_all = [], []
    for b in range(bb):
        rows = slice(b * t, (b + 1) * t)
        kn, vn, ck, cv = kn_ref[rows, :], vn_ref[rows, :], ck_ref[b], cv_ref[b]
        wk_ref[b] = jnp.concatenate([ck[t:, :], kn], axis=0)
        wv_ref[b] = jnp.concatenate([cv[t:, :], vn], axis=0)
        kcat = jnp.concatenate([ck.astype(BF16), kn.astype(BF16), pad], axis=0)
        vcat = jnp.concatenate([cv.astype(BF16), vn.astype(BF16), pad], axis=0)
        zero = jnp.zeros_like(kcat)
        k_bd = jnp.concatenate([jnp.where(col_group == g, kcat, zero) for g in range(N_KV)], axis=0)
        v_bd = jnp.concatenate([jnp.where(col_group == g, vcat, zero) for g in range(N_KV)], axis=0)
        v_all.append(jnp.concatenate([v_bd, ones_bd], axis=1))
        qs = jnp.concatenate(
            [q_ref[rows, r * KV_WIDTH:(r + 1) * KV_WIDTH] for r in range(GQA_REP)], axis=0)
        s.append(_dot_t(qs, k_bd))
    p, m = [], []
    for b in range(bb):
        c = [s[b][:, LANES * n:LANES * (n + 1)] for n in range(2 * N_KV)]
        c = [cn if n % 2 == 0 else jnp.where(is_new, cn, neg_inf) for n, cn in enumerate(c)]
        mb = [jnp.maximum(jnp.max(jnp.maximum(c[2 * g], c[2 * g + 1]), axis=-1, keepdims=True),
                          sink_ref[g]) for g in range(N_KV)]
        p.append(jnp.concatenate([jnp.exp2(c[n] - mb[n // 2]) for n in range(2 * N_KV)],
                                 axis=1).astype(BF16))
        m.append(mb)
    ov = [_dot(p[b], v_all[b]) for b in range(bb)]
    for b in range(bb):
        rows = slice(b * t, (b + 1) * t)
        es = [jnp.exp2(sink_ref[g] - m[b][g]) for g in range(N_KV)]
        e_sink = jnp.concatenate([jnp.where(lo, es[0], es[1]), jnp.where(lo, es[2], es[3])], axis=1)
        o = ov[b][:, :KV_WIDTH] / (ov[b][:, KV_WIDTH:] + e_sink)
        for r in range(GQA_REP):
            cols = slice(r * KV_WIDTH, (r + 1) * KV_WIDTH)
            gate = ga_ref[rows, cols].astype(F32)
            o_ref[rows, cols] = (o[r * t:(r + 1) * t, :] * gate).astype(BF16)


def _attn_sample(q, k, v, cache_k, cache_v, ga, sink_rows, batch, t, bb):
    n_cache = cache_k.shape[1]
    rows = lambda w: pl.BlockSpec((bb * t, w), lambda b: (b, 0))
    cache = pl.BlockSpec((bb, n_cache, KV_WIDTH), lambda b: (b, 0, 0))
    return pl.pallas_call(
        functools.partial(_attn_sample_kernel, t=t, bb=bb),
        grid=(batch // bb,),
        in_specs=[rows(ATT_WIDTH), rows(KV_WIDTH), rows(KV_WIDTH), cache, cache, rows(ATT_WIDTH),
                  _const_spec(sink_rows.shape)],
        out_specs=[rows(ATT_WIDTH), cache, cache],
        out_shape=[jax.ShapeDtypeStruct((batch * t, ATT_WIDTH), BF16),
                   jax.ShapeDtypeStruct((batch, n_cache, KV_WIDTH), F32),
                   jax.ShapeDtypeStruct((batch, n_cache, KV_WIDTH), F32)],
        compiler_params=pltpu.CompilerParams(
            dimension_semantics=("arbitrary",), vmem_limit_bytes=VMEM_LIMIT),
        name="attn_sample",
    )(q, k, v, cache_k, cache_v, ga, sink_rows)


def _gla_kernel(qg_ref, kg_ref, vg_ref, gg_ref, lah_ref, lal_ref, s0_ref, tril_ref,
                o_ref, sout_ref, st_ref, b_ref, q_ref, *, blk, tg, unroll):
    i = pl.program_id(1)
    n_groups = tg // (blk * unroll)

    @pl.when(i == 0)
    def _():
        st_ref[0] = s0_ref[0]

    rowi = lax.broadcasted_iota(jnp.int32, (blk, blk), 0)
    coli = lax.broadcasted_iota(jnp.int32, (blk, blk), 1)
    causal = rowi >= coli
    eye = (lax.broadcasted_iota(jnp.int32, (GLA_DK, GLA_DK), 0)
           == lax.broadcasted_iota(jnp.int32, (GLA_DK, GLA_DK), 1))

    heads = range(N_GLA)
    kcol = [slice(h * GLA_DK, (h + 1) * GLA_DK) for h in heads]
    vcol = [slice(h * GLA_DV, (h + 1) * GLA_DV) for h in heads]

    def cum_log2_decay(r):
        return _dot(tril_ref[...], jnp.concatenate([lah_ref[r, :], lal_ref[r, :]], axis=0))

    def decay_column(blast):
        dec_row = jnp.broadcast_to(jnp.exp2(blast), (GLA_DK, GLA_DK))
        return jnp.sum(jnp.where(eye, dec_row, 0.0), axis=-1, keepdims=True)

    def finish(ov, r, h):
        ms = jnp.mean(ov * ov, axis=-1, keepdims=True)
        gate = gg_ref[r, vcol[h]].astype(F32)
        o_ref[r, vcol[h]] = (ov * lax.rsqrt(ms + EPS) * gate).astype(BF16)

    def exact_group(rows, cur, nxt):
        st = [st_ref[cur, h] for h in heads]
        for r in rows:
            bcum = cum_log2_decay(r)
            for h in heads:
                b = bcum[:, kcol[h]]
                blast = b[blk - 1:blk, :]
                q = qg_ref[r, kcol[h]].astype(F32)
                k = kg_ref[r, kcol[h]].astype(F32)
                v = vg_ref[r, vcol[h]]
                b_ref[...] = b
                q_ref[...] = q

                def score_row(t, att_t):
                    w = jnp.where(rowi[:, 0:1] <= t, jnp.exp2(b_ref[pl.ds(t, 1), :] - b), 0.0)
                    col = jnp.sum(q_ref[pl.ds(t, 1), :] * k * w, axis=-1, keepdims=True)
                    return att_t + jnp.where(coli == t, col, 0.0)

                att_t = lax.fori_loop(0, blk, score_row, jnp.zeros((blk, blk), F32))
                q_dec = (q * jnp.exp2(b)).astype(BF16)
                k_end = (k * jnp.exp2(blast - b)).astype(BF16)
                ov = _dot_ta(att_t.astype(BF16), v) + _dot(q_dec, st[h].astype(BF16))
                st[h] = st[h] * decay_column(blast) + _dot_ta(k_end, v)
                finish(ov, r, h)
        for h in heads:
            st_ref[nxt, h] = st[h]

    def group(j, carry):
        rows = [pl.ds(pl.multiple_of((j * unroll + u) * blk, blk), blk) for u in range(unroll)]
        cur = (i * n_groups + j) % 2
        nxt = 1 - cur
        bcum = [cum_log2_decay(r) for r in rows]
        q_dec, k_inv, k_end, dec_col = {}, {}, {}, {}
        for u, r in enumerate(rows):
            for h in heads:
                b = bcum[u][:, kcol[h]]
                blast = b[blk - 1:blk, :]
                q = qg_ref[r, kcol[h]].astype(F32)
                k = kg_ref[r, kcol[h]].astype(F32)
                decay = jnp.exp2(b)
                q_dec[u, h] = (q * decay).astype(BF16)
                k_inv[u, h] = (k * (1.0 / decay)).astype(BF16)
                k_end[u, h] = (k * jnp.exp2(blast - b)).astype(BF16)
                dec_col[u, h] = decay_column(blast)
        att = {(u, h): _dot_t(q_dec[u, h], k_inv[u, h]) for u in range(unroll) for h in heads}
        d_st = {(u, h): _dot_ta(k_end[u, h], vg_ref[rows[u], vcol[h]])
                for u in range(unroll) for h in heads}
        att = {key: jnp.where(causal, a, 0.0).astype(BF16) for key, a in att.items()}
        st = [st_ref[cur, h] for h in heads]
        o = {}
        for u, r in enumerate(rows):
            for h in heads:
                o[u, h] = _dot(att[u, h], vg_ref[r, vcol[h]]) + _dot(q_dec[u, h], st[h].astype(BF16))
                st[h] = st[h] * dec_col[u, h] + d_st[u, h]
        for h in heads:
            st_ref[nxt, h] = st[h]
        for u, r in enumerate(rows):
            for h in heads:
                finish(o[u, h], r, h)

        total = functools.reduce(jnp.minimum, [bc[blk - 1:blk, :] for bc in bcum])

        @pl.when(jnp.min(total) < -LOG2_DECAY_LIMIT)
        def _():
            exact_group(rows, cur, nxt)
        return carry

    lax.fori_loop(0, n_groups, group, 0)

    @pl.when(i == pl.num_programs(1) - 1)
    def _():
        sout_ref[0] = st_ref[(pl.num_programs(1) * n_groups) % 2]


def _gla(qg, kg, vg, gg, la_hi, la_lo, s0, tril2, batch, seq, blk, tg, unroll=1):
    nblk = seq // tg
    rows = lambda w: pl.BlockSpec((tg, w), lambda b, i: (b * nblk + i, 0))
    state = pl.BlockSpec((1, N_GLA, GLA_DK, GLA_DV), lambda b, i: (b, 0, 0, 0))
    return pl.pallas_call(
        functools.partial(_gla_kernel, blk=blk, tg=tg, unroll=unroll),
        grid=(batch, nblk),
        in_specs=[rows(GLA_KW), rows(GLA_KW), rows(GLA_WIDTH), rows(GLA_WIDTH), rows(GLA_KW),
                  rows(GLA_KW), state, _const_spec(tril2.shape)],
        out_specs=[rows(GLA_WIDTH), state],
        out_shape=[jax.ShapeDtypeStruct((batch * seq, GLA_WIDTH), BF16),
                   jax.ShapeDtypeStruct((batch, N_GLA, GLA_DK, GLA_DV), F32)],
        scratch_shapes=[pltpu.VMEM((2, N_GLA, GLA_DK, GLA_DV), F32),
                        pltpu.VMEM((blk, GLA_DK), F32),
                        pltpu.VMEM((blk, GLA_DK), F32)],
        compiler_params=pltpu.CompilerParams(
            dimension_semantics=("arbitrary", "arbitrary"), vmem_limit_bytes=VMEM_LIMIT),
        name="gla",
    )(qg, kg, vg, gg, la_hi, la_lo, s0, tril2)


def _outproj_kernel(x_ref, ma_ref, mg_ref, w_ref, y_ref):
    y_ref[...] = (x_ref[...] + _dot(ma_ref[...], w_ref[:ATT_WIDTH, :])
                  + _dot(mg_ref[...], w_ref[ATT_WIDTH:, :]))


def _outproj(x2d, mix_a, mix_g, w, tm):
    m = x2d.shape[0]
    row = lambda w: pl.BlockSpec((tm, w), lambda i: (i, 0))
    return pl.pallas_call(
        _outproj_kernel,
        grid=(m // tm,),
        in_specs=[row(D_MODEL), row(ATT_WIDTH), row(GLA_WIDTH), _const_spec(w.shape)],
        out_specs=row(D_MODEL),
        out_shape=jax.ShapeDtypeStruct((m, D_MODEL), F32),
        compiler_params=pltpu.CompilerParams(
            dimension_semantics=("arbitrary",), vmem_limit_bytes=VMEM_LIMIT),
        name="outproj",
    )(x2d, mix_a, mix_g, w)


def _regroup_unit(u):
    return (u % N_KV) * GQA_REP + u // N_KV


def _prep_w_in_kernel(u0_ref, u1_ref, u2_ref, u3_ref, lr_ref, main_ref, wlr_ref):
    units = jnp.concatenate([u0_ref[...], u1_ref[...], u2_ref[...], u3_ref[...]], axis=0)
    main_ref[...] = units.T.astype(BF16)

    @pl.when(pl.program_id(0) == 0)
    def _():
        lr = lr_ref[...]
        padded = jnp.concatenate([lr, jnp.zeros((LANES - lr.shape[0], lr.shape[1]), F32)], axis=0)
        wlr_ref[...] = padded.T.astype(BF16)


def _prep_w_in(w_in_t):
    _, k = w_in_t.shape
    units_per_block = KV_WIDTH // HEAD_DIM
    n_att = ATT_WIDTH // HEAD_DIM

    def src_unit(j, p):
        u = j * units_per_block + p
        q0, ga0 = OFF_Q // HEAD_DIM, OFF_GA // HEAD_DIM
        u = jnp.where((u >= q0) & (u < q0 + n_att), q0 + _regroup_unit(u - q0), u)
        return jnp.where((u >= ga0) & (u < ga0 + n_att), ga0 + _regroup_unit(u - ga0), u)

    unit = lambda p: pl.BlockSpec((HEAD_DIM, k), lambda j: (src_unit(j, p), 0))
    return pl.pallas_call(
        _prep_w_in_kernel,
        grid=(MAIN_COLS // KV_WIDTH,),
        in_specs=[unit(0), unit(1), unit(2), unit(3),
                  pl.BlockSpec((GATE_RANK, k), lambda j: (MAIN_COLS // GATE_RANK, 0))],
        out_specs=[pl.BlockSpec((k, KV_WIDTH), lambda j: (0, j)),
                   pl.BlockSpec((k, LANES), lambda j: (0, 0))],
        out_shape=[jax.ShapeDtypeStruct((k, MAIN_COLS), BF16),
                   jax.ShapeDtypeStruct((k, LANES), BF16)],
        compiler_params=pltpu.CompilerParams(
            dimension_semantics=("arbitrary",), vmem_limit_bytes=VMEM_LIMIT),
        name="prep_w_in",
    )(w_in_t, w_in_t, w_in_t, w_in_t, w_in_t)


def _prep_w_out_kernel(u0_ref, u1_ref, u2_ref, u3_ref, o_ref):
    o_ref[...] = jnp.concatenate(
        [u0_ref[...], u1_ref[...], u2_ref[...], u3_ref[...]], axis=0).astype(BF16)


def _prep_w_out(w_out):
    rows, cols = w_out.shape
    units_per_block = KV_WIDTH // HEAD_DIM
    n_att = ATT_WIDTH // HEAD_DIM

    def src_unit(j, p):
        u = j * units_per_block + p
        return jnp.where(u < n_att, _regroup_unit(u), u)

    unit = lambda p: pl.BlockSpec((HEAD_DIM, cols), lambda j: (src_unit(j, p), 0))
    return pl.pallas_call(
        _prep_w_out_kernel,
        grid=(rows // KV_WIDTH,),
        in_specs=[unit(0), unit(1), unit(2), unit(3)],
        out_specs=pl.BlockSpec((KV_WIDTH, cols), lambda j: (j, 0)),
        out_shape=jax.ShapeDtypeStruct((rows, cols), BF16),
        compiler_params=pltpu.CompilerParams(dimension_semantics=("arbitrary",)),
        name="prep_w_out",
    )(w_out, w_out, w_out, w_out)


def _tril2(blk):
    t = np.tril(np.ones((blk, blk), np.float32))
    return jnp.asarray(np.concatenate([t, t], axis=1), BF16)


def _prep(norm_g, w_in, w_gate_up, b_gate, q_norm_g, k_norm_g, sinks, gla_norm_g, w_out):
    w_main, w_lr = _prep_w_in(w_in.T)
    w_gu = jnp.pad(w_gate_up, ((0, LANES - GATE_RANK), (0, 0))).astype(BF16)
    g_qk = jnp.concatenate([jnp.tile(q_norm_g, N_HEADS) * (ATT_SCALE * LOG2E),
                            jnp.tile(k_norm_g, N_KV)])[None, :]

    def sink_rows(t):
        sk = sinks.astype(F32).reshape(N_KV, GQA_REP) * LOG2E
        return jnp.broadcast_to(sk[:, :, None, None], (N_KV, GQA_REP, t, LANES)).reshape(
            N_KV, GQA_REP * t, LANES)

    return dict(norm_g=norm_g[None, :], w_main=w_main, w_lr=w_lr, w_gu=w_gu, b_gate=b_gate[None, :],
                g_qk=g_qk, g_gla=jnp.tile(gla_norm_g, N_GLA)[None, :],
                w_out=_prep_w_out(w_out), sink_rows=sink_rows)


def kernel(x_prompt, x_sample, cache_k, cache_v, state_gla, norm_g, w_in, w_gate_up, b_gate,
           q_norm_g, k_norm_g, sinks, gla_norm_g, w_out):
    depth = norm_g.shape[0]
    assert depth == 1
    B, S, _ = x_prompt.shape
    DB, T, _ = x_sample.shape
    n_cache = cache_k.shape[2]
    assert n_cache == WINDOW
    p = _prep(norm_g[0], w_in[0], w_gate_up[0], b_gate[0], q_norm_g[0], k_norm_g[0], sinks[0],
              gla_norm_g[0], w_out[0])

    xp = x_prompt.reshape(B * S, D_MODEL)
    q, k, v, ga, qg, kg, vg, gg, la_hi, la_lo = _inproj(xp, p, tm=INPROJ_ROWS)
    y_p, win_k_p, win_v_p, gla_p = _layer_prompt(
        xp, q, k, v, ga, p["sink_rows"](CHUNK), qg, kg, vg, gg, la_hi, la_lo, _tril2(CHUNK), p["w_out"],
        B, S, tq=LAYER_ROWS, group=LAYER_GROUP)
    y_p = y_p.reshape(B, S, D_MODEL)

    xs = x_sample.reshape(DB * T, D_MODEL)
    q, k, v, ga, qg, kg, vg, gg, la_hi, la_lo = _inproj(xs, p, tm=DB * T)
    ck = cache_k[0].reshape(DB, n_cache, KV_WIDTH)
    cv = cache_v[0].reshape(DB, n_cache, KV_WIDTH)
    mix_a, win_k_s, win_v_s = _attn_sample(q, k, v, ck, cv, ga, p["sink_rows"](T), DB, T,
                                           bb=SAMPLE_STREAMS)
    blk = min(CHUNK, T)
    mix_g, gla_s = _gla(qg, kg, vg, gg, la_hi, la_lo, state_gla[0], _tril2(blk), DB, T,
                        blk=blk, tg=T)
    y_s = _outproj(xs, mix_a, mix_g, p["w_out"], tm=DB * T).reshape(DB, T, D_MODEL)

    shape5 = lambda a: a.reshape(1, a.shape[0], n_cache, N_KV, HEAD_DIM)
    return (y_p, y_s, shape5(win_k_p), shape5(win_v_p), gla_p[None],
            shape5(win_k_s), shape5(win_v_s), gla_s[None])
```

```python
import functools

import jax
import jax.numpy as jnp
import numpy as np
from jax import lax
from jax.experimental import pallas as pl
from jax.experimental.pallas import tpu as pltpu

D_MODEL = 2048
CHUNK = 64
WINDOW = 128
HEAD_DIM = 64
N_HEADS = 16
N_KV = 4
GQA_REP = N_HEADS // N_KV
ATT_WIDTH = N_HEADS * HEAD_DIM
KV_WIDTH = N_KV * HEAD_DIM
N_GLA = 4
GLA_DK = 128
GLA_DV = 256
GLA_KW = N_GLA * GLA_DK
GLA_WIDTH = N_GLA * GLA_DV
GATE_RANK = 16
GATE_TAU = 16.0
EPS = 1e-6
ATT_SCALE = HEAD_DIM ** -0.5
LOG2E = float(np.log2(np.e))
LOG2_DECAY_LIMIT = 100.0
WIN_CHUNKS = WINDOW // CHUNK
WIN_KEYS = WINDOW + CHUNK

LANES = 128
QK_WIDTH = ATT_WIDTH + KV_WIDTH
OFF_Q, OFF_K, OFF_V = 0, ATT_WIDTH, ATT_WIDTH + KV_WIDTH
OFF_GA = OFF_V + KV_WIDTH
OFF_QG = OFF_GA + ATT_WIDTH
OFF_KG = OFF_QG + GLA_KW
OFF_VG = OFF_KG + GLA_KW
OFF_GG = OFF_VG + GLA_WIDTH
MAIN_COLS = OFF_GG + GLA_WIDTH

VMEM_LIMIT = 56 * 1024 * 1024
INPROJ_ROWS = 512
INPROJ_SUB = 256
LAYER_ROWS = 512
LAYER_GROUP = 4
SAMPLE_STREAMS = 4

BF16 = jnp.bfloat16
F32 = jnp.float32


def _dot(a, b):
    return jnp.dot(a, b, preferred_element_type=F32)


def _dot_t(a, b):
    return lax.dot_general(a, b, (((1,), (1,)), ((), ())), preferred_element_type=F32)


def _dot_ta(a, b):
    return lax.dot_general(a, b, (((0,), (0,)), ((), ())), preferred_element_type=F32)


def _silu(x):
    return x / (1.0 + jnp.exp(-x))


def _const_spec(shape):
    nd = len(shape)
    return pl.BlockSpec(shape, lambda *_: (0,) * nd, pipeline_mode=pl.Buffered(1))


def _inproj_kernel(x_ref, ng_ref, w_ref, wlr_ref, wgu_ref, bg_ref, gqk_ref, gng_ref,
                   q_ref, k_ref, v_ref, ga_ref, qg_ref, kg_ref, vg_ref, gg_ref, lah_ref, lal_ref,
                   *, sub):
    lo = lax.broadcasted_iota(jnp.int32, (1, LANES), 1) < HEAD_DIM
    for s in range(x_ref.shape[0] // sub):
        rows = slice(s * sub, (s + 1) * sub)
        x = x_ref[rows, :]
        ms = jnp.mean(x * x, axis=-1, keepdims=True)
        xn = (x * lax.rsqrt(ms + EPS) * ng_ref[...]).astype(BF16)

        qk = _dot(xn, w_ref[:, OFF_Q:OFF_V])
        ga_ref[rows, :] = _silu(_dot(xn, w_ref[:, OFF_GA:OFF_QG])).astype(BF16)
        lr = _dot(xn, wlr_ref[...]).astype(BF16)
        qg_ref[rows, :] = (_dot(xn, w_ref[:, OFF_QG:OFF_KG]) * (GLA_DK ** -0.5)).astype(BF16)
        kg_ref[rows, :] = _dot(xn, w_ref[:, OFF_KG:OFF_VG]).astype(BF16)
        z = _dot(lr, wgu_ref[...]) + bg_ref[...]
        gg_ref[rows, :] = (_silu(_dot(xn, w_ref[:, OFF_GG:MAIN_COLS])) * gng_ref[...]).astype(BF16)
        vg_ref[rows, :] = _dot(xn, w_ref[:, OFF_VG:OFF_GG]).astype(BF16)
        v_ref[rows, :] = _dot(xn, w_ref[:, OFF_V:OFF_GA])

        for n in range(QK_WIDTH // LANES):
            cols = slice(n * LANES, (n + 1) * LANES)
            blk = qk[:, cols]
            sq = blk * blk
            ms_lo = jnp.sum(jnp.where(lo, sq, 0.0), axis=-1, keepdims=True) * (1.0 / HEAD_DIM)
            ms_hi = jnp.sum(jnp.where(lo, 0.0, sq), axis=-1, keepdims=True) * (1.0 / HEAD_DIM)
            scale = jnp.where(lo, lax.rsqrt(ms_lo + EPS), lax.rsqrt(ms_hi + EPS))
            normed = blk * scale * gqk_ref[:, cols]
            if n < ATT_WIDTH // LANES:
                q_ref[rows, cols] = normed.astype(BF16)
            else:
                k_ref[rows, n * LANES - ATT_WIDTH:(n + 1) * LANES - ATT_WIDTH] = normed
        log_sig = jnp.minimum(z, 0.0) - jnp.log(1.0 + jnp.exp(-jnp.abs(z)))
        la = log_sig * (LOG2E / GATE_TAU)
        la_hi = la.astype(BF16)
        lah_ref[rows, :] = la_hi
        lal_ref[rows, :] = (la - la_hi.astype(F32)).astype(BF16)


def _inproj(x2d, p, tm):
    m = x2d.shape[0]
    row = lambda w: pl.BlockSpec((tm, w), lambda i: (i, 0))
    outs = [(ATT_WIDTH, BF16), (KV_WIDTH, F32), (KV_WIDTH, F32), (ATT_WIDTH, BF16),
            (GLA_KW, BF16), (GLA_KW, BF16), (GLA_WIDTH, BF16), (GLA_WIDTH, BF16),
            (GLA_KW, BF16), (GLA_KW, BF16)]
    consts = [p["norm_g"], p["w_main"], p["w_lr"], p["w_gu"], p["b_gate"], p["g_qk"], p["g_gla"]]
    return pl.pallas_call(
        functools.partial(_inproj_kernel, sub=min(tm, INPROJ_SUB)),
        grid=(m // tm,),
        in_specs=[row(D_MODEL)] + [_const_spec(c.shape) for c in consts],
        out_specs=[row(w) for w, _ in outs],
        out_shape=[jax.ShapeDtypeStruct((m, w), dt) for w, dt in outs],
        compiler_params=pltpu.CompilerParams(
            dimension_semantics=("arbitrary",), vmem_limit_bytes=VMEM_LIMIT),
        name="inproj",
    )(x2d, *consts)


def _layer_prompt_kernel(q_ref, kp_ref, kc_ref, vp_ref, vc_ref, ga_ref, sink_ref,
                         qg_ref, kg_ref, vg_ref, gg_ref, lah_ref, lal_ref, tril_ref, x_ref, w_ref,
                         y_ref, wk_ref, wv_ref, sout_ref,
                         kw_ref, vw_ref, st_ref, b_ref, qx_ref, mix_ref, *, tq, group, nblk, n_steps):
    n_chunks = tq // CHUNK
    assert n_chunks % group == 0 and group >= WIN_CHUNKS and tq >= WINDOW
    t = pl.program_id(0)
    i = lax.rem(jnp.minimum(t, n_steps - 1), nblk)
    last = nblk - 1
    cur = i % 2
    nxt = 1 - cur
    mix_slot = t % 2
    prev_slot = 1 - mix_slot

    out_rows, out_cols = 512, 512
    out_pieces = [(slice(r, r + out_rows), slice(c, c + out_cols))
                  for r in range(0, tq, out_rows) for c in range(0, D_MODEL, out_cols)]

    def out_piece(n):
        r, c = out_pieces[n]
        y_ref[r, c] = x_ref[r, c] + _dot(mix_ref[prev_slot, r, :], w_ref[:, c])

    def mixers(do_out):
        kw_ref[0:WINDOW, :] = kp_ref[0].astype(BF16)
        kw_ref[WINDOW:, :] = kc_ref[0].astype(BF16)
        vw_ref[0:WINDOW, :] = vp_ref[0].astype(BF16)
        vw_ref[WINDOW:, :] = vc_ref[0].astype(BF16)

        @pl.when(i == 0)
        def _():
            st_ref[0] = jnp.zeros(st_ref.shape[1:], F32)

        @pl.when(i == last)
        def _():
            wk_ref[0] = kc_ref[0, tq - WINDOW:, :]
            wv_ref[0] = vc_ref[0, tq - WINDOW:, :]

        lane = lax.broadcasted_iota(jnp.int32, (1, LANES), 1)
        lo = lane < HEAD_DIM
        col_group = lax.broadcasted_iota(jnp.int32, (1, KV_WIDTH), 1) // HEAD_DIM
        col = lax.broadcasted_iota(jnp.int32, (1, N_KV * WIN_KEYS), 1)
        key_in_seg = col - WIN_KEYS * sum((col >= g * WIN_KEYS).astype(jnp.int32) for g in range(1, N_KV))
        neg_inf = jnp.float32(-jnp.inf)
        rows = [slice(c * CHUNK, (c + 1) * CHUNK) for c in range(n_chunks)]
        a_s, a_v, a_p, a_den, a_ov = {}, {}, {}, {}, {}

        def block_diag(win):
            zero = jnp.zeros_like(win)
            return jnp.concatenate([jnp.where(col_group == g, win, zero) for g in range(N_KV)], axis=0)

        def attn_scores(c):
            k_bd = block_diag(kw_ref[c * CHUNK:c * CHUNK + WIN_KEYS, :])
            a_v[c] = block_diag(vw_ref[c * CHUNK:c * CHUNK + WIN_KEYS, :])
            qs = jnp.concatenate(
                [q_ref[rows[c], r * KV_WIDTH:(r + 1) * KV_WIDTH] for r in range(GQA_REP)], axis=0)
            a_s[c] = _dot_t(qs, k_bd)

        def attn_softmax(c):
            su = a_s.pop(c)
            if c < WIN_CHUNKS:
                first_valid = jnp.where(i == 0, (WIN_CHUNKS - c) * CHUNK, 0)
                su = jnp.where(key_in_seg >= first_valid, su, neg_inf)
            cs = [su[:, LANES * n:LANES * (n + 1)] for n in range(6)]
            seg_max = [
                jnp.maximum(cs[0], jnp.where(lo, cs[1], neg_inf)),
                jnp.maximum(jnp.where(lo, neg_inf, cs[1]), cs[2]),
                jnp.maximum(cs[3], jnp.where(lo, cs[4], neg_inf)),
                jnp.maximum(jnp.where(lo, neg_inf, cs[4]), cs[5]),
            ]
            mu = [jnp.maximum(jnp.max(seg_max[g], axis=-1, keepdims=True), sink_ref[g])
                  for g in range(N_KV)]
            shift = [mu[0], jnp.where(lo, mu[0], mu[1]), mu[1], mu[2], jnp.where(lo, mu[2], mu[3]), mu[3]]
            e = [jnp.exp2(cs[n] - shift[n]) for n in range(6)]
            a_p[c] = jnp.concatenate(e, axis=1).astype(BF16)
            seg_sum = [e[0] + jnp.where(lo, e[1], 0.0), jnp.where(lo, 0.0, e[1]) + e[2],
                       e[3] + jnp.where(lo, e[4], 0.0), jnp.where(lo, 0.0, e[4]) + e[5]]
            a_den[c] = [jnp.sum(seg_sum[g], axis=-1, keepdims=True) + jnp.exp2(sink_ref[g] - mu[g])
                        for g in range(N_KV)]

        def attn_pv(c):
            a_ov[c] = _dot(a_p.pop(c), a_v.pop(c))

        def attn_finish(c):
            ov, den = a_ov.pop(c), a_den.pop(c)
            den = jnp.concatenate([jnp.where(lo, den[0], den[1]), jnp.where(lo, den[2], den[3])], axis=1)
            o = ov / den
            for r in range(GQA_REP):
                cols = slice(r * KV_WIDTH, (r + 1) * KV_WIDTH)
                gate = ga_ref[rows[c], cols].astype(F32)
                mix_ref[mix_slot, rows[c], cols] = (o[r * CHUNK:(r + 1) * CHUNK, :] * gate).astype(BF16)

        rowi = lax.broadcasted_iota(jnp.int32, (CHUNK, CHUNK), 0)
        coli = lax.broadcasted_iota(jnp.int32, (CHUNK, CHUNK), 1)
        causal = rowi >= coli
        eye = (lax.broadcasted_iota(jnp.int32, (GLA_DK, GLA_DK), 0)
               == lax.broadcasted_iota(jnp.int32, (GLA_DK, GLA_DK), 1))
        heads = range(N_GLA)
        kcol = [slice(h * GLA_DK, (h + 1) * GLA_DK) for h in heads]
        vcol = [slice(h * GLA_DV, (h + 1) * GLA_DV) for h in heads]
        mcol = [slice(ATT_WIDTH + h * GLA_DV, ATT_WIDTH + (h + 1) * GLA_DV) for h in heads]
        g_b, g_qd, g_ki, g_ke, g_dc, g_att, g_dst, g_o = {}, {}, {}, {}, {}, {}, {}, {}
        state = {}

        def cum_log2_decay(r):
            return _dot(tril_ref[...], jnp.concatenate([lah_ref[r, :], lal_ref[r, :]], axis=0))

        def decay_column(blast):
            dec_row = jnp.broadcast_to(jnp.exp2(blast), (GLA_DK, GLA_DK))
            return jnp.sum(jnp.where(eye, dec_row, 0.0), axis=-1, keepdims=True)

        def gla_finish_rows(ov, r, h):
            ms = jnp.mean(ov * ov, axis=-1, keepdims=True)
            gate = gg_ref[r, vcol[h]].astype(F32)
            mix_ref[mix_slot, r, mcol[h]] = (ov * lax.rsqrt(ms + EPS) * gate).astype(BF16)

        def gla_cumsum(c):
            g_b[c] = cum_log2_decay(rows[c])

        def gla_prepare(c):
            for h in heads:
                b = g_b[c][:, kcol[h]]
                blast = b[CHUNK - 1:CHUNK, :]
                q = qg_ref[rows[c], kcol[h]].astype(F32)
                k = kg_ref[rows[c], kcol[h]].astype(F32)
                decay = jnp.exp2(b)
                g_qd[c, h] = (q * decay).astype(BF16)
                g_ki[c, h] = (k * (1.0 / decay)).astype(BF16)
                g_ke[c, h] = (k * jnp.exp2(blast - b)).astype(BF16)
                g_dc[c, h] = decay_column(blast)

        def gla_scores(c):
            for h in heads:
                g_att[c, h] = _dot_t(g_qd[c, h], g_ki.pop((c, h)))
                g_dst[c, h] = _dot_ta(g_ke.pop((c, h)), vg_ref[rows[c], vcol[h]])

        def gla_outputs(c):
            for h in heads:
                att = jnp.where(causal, g_att.pop((c, h)), 0.0).astype(BF16)
                lhs = jnp.concatenate([g_qd.pop((c, h)), att], axis=1)
                rhs = jnp.concatenate([state[h].astype(BF16), vg_ref[rows[c], vcol[h]]], axis=0)
                g_o[c, h] = _dot(lhs, rhs)
                state[h] = state[h] * g_dc.pop((c, h)) + g_dst.pop((c, h))

        def gla_finish(c):
            for h in heads:
                gla_finish_rows(g_o.pop((c, h)), rows[c], h)

        def gla_exact_step():
            def chunk(c, carry):
                r = pl.ds(pl.multiple_of(c * CHUNK, CHUNK), CHUNK)
                bcum = cum_log2_decay(r)
                for h in heads:
                    b = bcum[:, kcol[h]]
                    blast = b[CHUNK - 1:CHUNK, :]
                    q = qg_ref[r, kcol[h]].astype(F32)
                    k = kg_ref[r, kcol[h]].astype(F32)
                    v = vg_ref[r, vcol[h]]
                    b_ref[...] = b
                    qx_ref[...] = q

                    def score_row(tok, att_t):
                        w = jnp.where(rowi[:, 0:1] <= tok, jnp.exp2(b_ref[pl.ds(tok, 1), :] - b), 0.0)
                        colv = jnp.sum(qx_ref[pl.ds(tok, 1), :] * k * w, axis=-1, keepdims=True)
                        return att_t + jnp.where(coli == tok, colv, 0.0)

                    att_t = lax.fori_loop(0, CHUNK, score_row, jnp.zeros((CHUNK, CHUNK), F32))
                    q_dec = (q * jnp.exp2(b)).astype(BF16)
                    k_end = (k * jnp.exp2(blast - b)).astype(BF16)
                    st = st_ref[nxt, h]
                    ov = _dot_ta(att_t.astype(BF16), v) + _dot(q_dec, st.astype(BF16))
                    st_ref[nxt, h] = st * decay_column(blast) + _dot_ta(k_end, v)
                    gla_finish_rows(ov, r, h)
                return carry

            st_ref[nxt] = st_ref[cur]
            lax.fori_loop(0, n_chunks, chunk, 0)

        pending = list(range(len(out_pieces))) if do_out else []
        n_slots = 4 * (n_chunks // group)
        slots_seen = [0]

        def out_slot():
            slots_seen[0] += 1
            due = len(out_pieces) * slots_seen[0] // n_slots - (len(out_pieces) - len(pending))
            for _ in range(min(due, len(pending))):
                out_piece(pending.pop(0))

        for h in heads:
            state[h] = st_ref[cur, h]
        for g0 in range(0, n_chunks, group):
            cs = list(range(g0, g0 + group))
            half = group // 2
            for c in cs:
                gla_cumsum(c)
            out_slot()
            for c in cs[:half]:
                attn_scores(c)
            out_slot()
            for c in cs:
                gla_prepare(c)
            for c in cs:
                gla_scores(c)
            out_slot()
            for c in cs[:half]:
                attn_softmax(c)
            for c in cs[:half]:
                attn_pv(c)
            for c in cs[half:]:
                attn_scores(c)
            out_slot()
            for c in cs:
                gla_outputs(c)
            for c in cs[half:]:
                attn_softmax(c)
            for c in cs[half:]:
                attn_pv(c)
            for c in cs:
                gla_finish(c)
            for c in cs:
                attn_finish(c)
        assert not pending
        for h in heads:
            st_ref[nxt, h] = state[h]
        total = functools.reduce(jnp.minimum, [g_b[c][CHUNK - 1:CHUNK, :] for c in range(n_chunks)])

        @pl.when(jnp.min(total) < -LOG2_DECAY_LIMIT)
        def _():
            gla_exact_step()

        @pl.when(i == last)
        def _():
            sout_ref[0] = st_ref[nxt]

    @pl.when(t == 0)
    def _():
        mixers(do_out=False)

    @pl.when((t > 0) & (t < n_steps))
    def _():
        mixers(do_out=True)

    @pl.when(t == n_steps)
    def _():
        for n in range(len(out_pieces)):
            out_piece(n)


def _layer_prompt(x2d, q, k, v, ga, sink_rows, qg, kg, vg, gg, la_hi, la_lo, tril2, w_out, batch, seq,
                  tq, group):
    nblk = seq // tq
    n_steps = batch * nblk
    per_win = tq // WINDOW
    blk_of = lambda t: jnp.minimum(t, n_steps - 1)
    rows = lambda w: pl.BlockSpec((tq, w), lambda t: (blk_of(t), 0))
    late = lambda w: pl.BlockSpec((tq, w), lambda t: (jnp.maximum(t - 1, 0), 0))
    cur = pl.BlockSpec((1, tq, KV_WIDTH), lambda t: (blk_of(t) // nblk, blk_of(t) % nblk, 0))
    prev = pl.BlockSpec((1, WINDOW, KV_WIDTH),
                        lambda t: (blk_of(t) // nblk, jnp.maximum((blk_of(t) % nblk) * per_win - 1, 0), 0))
    win = pl.BlockSpec((1, WINDOW, KV_WIDTH), lambda t: (blk_of(t) // nblk, 0, 0))
    state = pl.BlockSpec((1, N_GLA, GLA_DK, GLA_DV), lambda t: (blk_of(t) // nblk, 0, 0, 0))
    k3 = k.reshape(batch, seq, KV_WIDTH)
    v3 = v.reshape(batch, seq, KV_WIDTH)
    return pl.pallas_call(
        functools.partial(_layer_prompt_kernel, tq=tq, group=group, nblk=nblk, n_steps=n_steps),
        grid=(n_steps + 1,),
        in_specs=[rows(ATT_WIDTH), prev, cur, prev, cur, rows(ATT_WIDTH), _const_spec(sink_rows.shape),
                  rows(GLA_KW), rows(GLA_KW), rows(GLA_WIDTH), rows(GLA_WIDTH), rows(GLA_KW),
                  rows(GLA_KW), _const_spec(tril2.shape), late(D_MODEL), _const_spec(w_out.shape)],
        out_specs=[late(D_MODEL), win, win, state],
        out_shape=[jax.ShapeDtypeStruct((batch * seq, D_MODEL), F32),
                   jax.ShapeDtypeStruct((batch, WINDOW, KV_WIDTH), F32),
                   jax.ShapeDtypeStruct((batch, WINDOW, KV_WIDTH), F32),
                   jax.ShapeDtypeStruct((batch, N_GLA, GLA_DK, GLA_DV), F32)],
        scratch_shapes=[pltpu.VMEM((WINDOW + tq, KV_WIDTH), BF16),
                        pltpu.VMEM((WINDOW + tq, KV_WIDTH), BF16),
                        pltpu.VMEM((2, N_GLA, GLA_DK, GLA_DV), F32),
                        pltpu.VMEM((CHUNK, GLA_DK), F32),
                        pltpu.VMEM((CHUNK, GLA_DK), F32),
                        pltpu.VMEM((2, tq, D_MODEL), BF16)],
        compiler_params=pltpu.CompilerParams(
            dimension_semantics=("arbitrary",), vmem_limit_bytes=VMEM_LIMIT),
        name="layer_prompt",
    )(q, k3, k3, v3, v3, ga, sink_rows, qg, kg, vg, gg, la_hi, la_lo, tril2, x2d, w_out)


def _attn_sample_kernel(q_ref, kn_ref, vn_ref, ck_ref, cv_ref, ga_ref, sink_ref,
                        o_ref, wk_ref, wv_ref, *, t, bb):
    n_cache = ck_ref.shape[1]
    assert n_cache == LANES and t <= LANES
    lane = lax.broadcasted_iota(jnp.int32, (1, LANES), 1)
    lo = lane < HEAD_DIM
    is_new = lane < t
    col_group = lax.broadcasted_iota(jnp.int32, (1, KV_WIDTH), 1) // HEAD_DIM
    seg = 2 * LANES
    ones_bd = jnp.concatenate(
        [jnp.broadcast_to(jnp.where(col_group == g, 1.0, 0.0).astype(BF16), (seg, KV_WIDTH))
         for g in range(N_KV)], axis=0)
    pad = jnp.zeros((LANES - t, KV_WIDTH), BF16)
    neg_inf = jnp.float32(-jnp.inf)

    s, v_all = [], []
    for b in range(bb):
        rows = slice(b * t, (b + 1) * t)
        kn, vn, ck, cv = kn_ref[rows, :], vn_ref[rows, :], ck_ref[b], cv_ref[b]
        wk_ref[b] = jnp.concatenate([ck[t:, :], kn], axis=0)
        wv_ref[b] = jnp.concatenate([cv[t:, :], vn], axis=0)
        kcat = jnp.concatenate([ck.astype(BF16), kn.astype(BF16), pad], axis=0)
        vcat = jnp.concatenate([cv.astype(BF16), vn.astype(BF16), pad], axis=0)
        zero = jnp.zeros_like(kcat)
        k_bd = jnp.concatenate([jnp.where(col_group == g, kcat, zero) for g in range(N_KV)], axis=0)
        v_bd = jnp.concatenate([jnp.where(col_group == g, vcat, zero) for g in range(N_KV)], axis=0)
        v_all.append(jnp.concatenate([v_bd, ones_bd], axis=1))
        qs = jnp.concatenate(
            [q_ref[rows, r * KV_WIDTH:(r + 1) * KV_WIDTH] for r in range(GQA_REP)], axis=0)
        s.append(_dot_t(qs, k_bd))
    p, m = [], []
    for b in range(bb):
        c = [s[b][:, LANES * n:LANES * (n + 1)] for n in range(2 * N_KV)]
        c = [cn if n % 2 == 0 else jnp.where(is_new, cn, neg_inf) for n, cn in enumerate(c)]
        mb = [jnp.maximum(jnp.max(jnp.maximum(c[2 * g], c[2 * g + 1]), axis=-1, keepdims=True),
                          sink_ref[g]) for g in range(N_KV)]
        p.append(jnp.concatenate([jnp.exp2(c[n] - mb[n // 2]) for n in range(2 * N_KV)],
                                 axis=1).astype(BF16))
        m.append(mb)
    ov = [_dot(p[b], v_all[b]) for b in range(bb)]
    for b in range(bb):
        rows = slice(b * t, (b + 1) * t)
        es = [jnp.exp2(sink_ref[g] - m[b][g]) for g in range(N_KV)]
        e_sink = jnp.concatenate([jnp.where(lo, es[0], es[1]), jnp.where(lo, es[2], es[3])], axis=1)
        o = ov[b][:, :KV_WIDTH] / (ov[b][:, KV_WIDTH:] + e_sink)
        for r in range(GQA_REP):
            cols = slice(r * KV_WIDTH, (r + 1) * KV_WIDTH)
            gate = ga_ref[rows, cols].astype(F32)
            o_ref[rows, cols] = (o[r * t:(r + 1) * t, :] * gate).astype(BF16)


def _attn_sample(q, k, v, cache_k, cache_v, ga, sink_rows, batch, t, bb):
    n_cache = cache_k.shape[1]
    rows = lambda w: pl.BlockSpec((bb * t, w), lambda b: (b, 0))
    cache = pl.BlockSpec((bb, n_cache, KV_WIDTH), lambda b: (b, 0, 0))
    return pl.pallas_call(
        functools.partial(_attn_sample_kernel, t=t, bb=bb),
        grid=(batch // bb,),
        in_specs=[rows(ATT_WIDTH), rows(KV_WIDTH), rows(KV_WIDTH), cache, cache, rows(ATT_WIDTH),
                  _const_spec(sink_rows.shape)],
        out_specs=[rows(ATT_WIDTH), cache, cache],
        out_shape=[jax.ShapeDtypeStruct((batch * t, ATT_WIDTH), BF16),
                   jax.ShapeDtypeStruct((batch, n_cache, KV_WIDTH), F32),
                   jax.ShapeDtypeStruct((batch, n_cache, KV_WIDTH), F32)],
        compiler_params=pltpu.CompilerParams(
            dimension_semantics=("arbitrary",), vmem_limit_bytes=VMEM_LIMIT),
        name="attn_sample",
    )(q, k, v, cache_k, cache_v, ga, sink_rows)


def _gla_kernel(qg_ref, kg_ref, vg_ref, gg_ref, lah_ref, lal_ref, s0_ref, tril_ref,
                o_ref, sout_ref, st_ref, b_ref, q_ref, *, blk, tg, unroll):
    i = pl.program_id(1)
    n_groups = tg // (blk * unroll)

    @pl.when(i == 0)
    def _():
        st_ref[0] = s0_ref[0]

    rowi = lax.broadcasted_iota(jnp.int32, (blk, blk), 0)
    coli = lax.broadcasted_iota(jnp.int32, (blk, blk), 1)
    causal = rowi >= coli
    eye = (lax.broadcasted_iota(jnp.int32, (GLA_DK, GLA_DK), 0)
           == lax.broadcasted_iota(jnp.int32, (GLA_DK, GLA_DK), 1))

    heads = range(N_GLA)
    kcol = [slice(h * GLA_DK, (h + 1) * GLA_DK) for h in heads]
    vcol = [slice(h * GLA_DV, (h + 1) * GLA_DV) for h in heads]

    def cum_log2_decay(r):
        return _dot(tril_ref[...], jnp.concatenate([lah_ref[r, :], lal_ref[r, :]], axis=0))

    def decay_column(blast):
        dec_row = jnp.broadcast_to(jnp.exp2(blast), (GLA_DK, GLA_DK))
        return jnp.sum(jnp.where(eye, dec_row, 0.0), axis=-1, keepdims=True)

    def finish(ov, r, h):
        ms = jnp.mean(ov * ov, axis=-1, keepdims=True)
        gate = gg_ref[r, vcol[h]].astype(F32)
        o_ref[r, vcol[h]] = (ov * lax.rsqrt(ms + EPS) * gate).astype(BF16)

    def exact_group(rows, cur, nxt):
        st = [st_ref[cur, h] for h in heads]
        for r in rows:
            bcum = cum_log2_decay(r)
            for h in heads:
                b = bcum[:, kcol[h]]
                blast = b[blk - 1:blk, :]
                q = qg_ref[r, kcol[h]].astype(F32)
                k = kg_ref[r, kcol[h]].astype(F32)
                v = vg_ref[r, vcol[h]]
                b_ref[...] = b
                q_ref[...] = q

                def score_row(t, att_t):
                    w = jnp.where(rowi[:, 0:1] <= t, jnp.exp2(b_ref[pl.ds(t, 1), :] - b), 0.0)
                    col = jnp.sum(q_ref[pl.ds(t, 1), :] * k * w, axis=-1, keepdims=True)
                    return att_t + jnp.where(coli == t, col, 0.0)

                att_t = lax.fori_loop(0, blk, score_row, jnp.zeros((blk, blk), F32))
                q_dec = (q * jnp.exp2(b)).astype(BF16)
                k_end = (k * jnp.exp2(blast - b)).astype(BF16)
                ov = _dot_ta(att_t.astype(BF16), v) + _dot(q_dec, st[h].astype(BF16))
                st[h] = st[h] * decay_column(blast) + _dot_ta(k_end, v)
                finish(ov, r, h)
        for h in heads:
            st_ref[nxt, h] = st[h]

    def group(j, carry):
        rows = [pl.ds(pl.multiple_of((j * unroll + u) * blk, blk), blk) for u in range(unroll)]
        cur = (i * n_groups + j) % 2
        nxt = 1 - cur
        bcum = [cum_log2_decay(r) for r in rows]
        q_dec, k_inv, k_end, dec_col = {}, {}, {}, {}
        for u, r in enumerate(rows):
            for h in heads:
                b = bcum[u][:, kcol[h]]
                blast = b[blk - 1:blk, :]
                q = qg_ref[r, kcol[h]].astype(F32)
                k = kg_ref[r, kcol[h]].astype(F32)
                decay = jnp.exp2(b)
                q_dec[u, h] = (q * decay).astype(BF16)
                k_inv[u, h] = (k * (1.0 / decay)).astype(BF16)
                k_end[u, h] = (k * jnp.exp2(blast - b)).astype(BF16)
                dec_col[u, h] = decay_column(blast)
        att = {(u, h): _dot_t(q_dec[u, h], k_inv[u, h]) for u in range(unroll) for h in heads}
        d_st = {(u, h): _dot_ta(k_end[u, h], vg_ref[rows[u], vcol[h]])
                for u in range(unroll) for h in heads}
        att = {key: jnp.where(causal, a, 0.0).astype(BF16) for key, a in att.items()}
        st = [st_ref[cur, h] for h in heads]
        o = {}
        for u, r in enumerate(rows):
            for h in heads:
                o[u, h] = _dot(att[u, h], vg_ref[r, vcol[h]]) + _dot(q_dec[u, h], st[h].astype(BF16))
                st[h] = st[h] * dec_col[u, h] + d_st[u, h]
        for h in heads:
            st_ref[nxt, h] = st[h]
        for u, r in enumerate(rows):
            for h in heads:
                finish(o[u, h], r, h)

        total = functools.reduce(jnp.minimum, [bc[blk - 1:blk, :] for bc in bcum])

        @pl.when(jnp.min(total) < -LOG2_DECAY_LIMIT)
        def _():
            exact_group(rows, cur, nxt)
        return carry

    lax.fori_loop(0, n_groups, group, 0)

    @pl.when(i == pl.num_programs(1) - 1)
    def _():
        sout_ref[0] = st_ref[(pl.num_programs(1) * n_groups) % 2]


def _gla(qg, kg, vg, gg, la_hi, la_lo, s0, tril2, batch, seq, blk, tg, unroll=1):
    nblk = seq // tg
    rows = lambda w: pl.BlockSpec((tg, w), lambda b, i: (b * nblk + i, 0))
    state = pl.BlockSpec((1, N_GLA, GLA_DK, GLA_DV), lambda b, i: (b, 0, 0, 0))
    return pl.pallas_call(
        functools.partial(_gla_kernel, blk=blk, tg=tg, unroll=unroll),
        grid=(batch, nblk),
        in_specs=[rows(GLA_KW), rows(GLA_KW), rows(GLA_WIDTH), rows(GLA_WIDTH), rows(GLA_KW),
                  rows(GLA_KW), state, _const_spec(tril2.shape)],
        out_specs=[rows(GLA_WIDTH), state],
        out_shape=[jax.ShapeDtypeStruct((batch * seq, GLA_WIDTH), BF16),
                   jax.ShapeDtypeStruct((batch, N_GLA, GLA_DK, GLA_DV), F32)],
        scratch_shapes=[pltpu.VMEM((2, N_GLA, GLA_DK, GLA_DV), F32),
                        pltpu.VMEM((blk, GLA_DK), F32),
                        pltpu.VMEM((blk, GLA_DK), F32)],
        compiler_params=pltpu.CompilerParams(
            dimension_semantics=("arbitrary", "arbitrary"), vmem_limit_bytes=VMEM_LIMIT),
        name="gla",
    )(qg, kg, vg, gg, la_hi, la_lo, s0, tril2)


def _outproj_kernel(x_ref, ma_ref, mg_ref, w_ref, y_ref):
    y_ref[...] = (x_ref[...] + _dot(ma_ref[...], w_ref[:ATT_WIDTH, :])
                  + _dot(mg_ref[...], w_ref[ATT_WIDTH:, :]))


def _outproj(x2d, mix_a, mix_g, w, tm):
    m = x2d.shape[0]
    row = lambda w: pl.BlockSpec((tm, w), lambda i: (i, 0))
    return pl.pallas_call(
        _outproj_kernel,
        grid=(m // tm,),
        in_specs=[row(D_MODEL), row(ATT_WIDTH), row(GLA_WIDTH), _const_spec(w.shape)],
        out_specs=row(D_MODEL),
        out_shape=jax.ShapeDtypeStruct((m, D_MODEL), F32),
        compiler_params=pltpu.CompilerParams(
            dimension_semantics=("arbitrary",), vmem_limit_bytes=VMEM_LIMIT),
        name="outproj",
    )(x2d, mix_a, mix_g, w)


def _regroup_unit(u):
    return (u % N_KV) * GQA_REP + u // N_KV


def _prep_w_in_kernel(u0_ref, u1_ref, u2_ref, u3_ref, lr_ref, main_ref, wlr_ref):
    units = jnp.concatenate([u0_ref[...], u1_ref[...], u2_ref[...], u3_ref[...]], axis=0)
    main_ref[...] = units.T.astype(BF16)

    @pl.when(pl.program_id(0) == 0)
    def _():
        lr = lr_ref[...]
        padded = jnp.concatenate([lr, jnp.zeros((LANES - lr.shape[0], lr.shape[1]), F32)], axis=0)
        wlr_ref[...] = padded.T.astype(BF16)


def _prep_w_in(w_in_t):
    _, k = w_in_t.shape
    units_per_block = KV_WIDTH // HEAD_DIM
    n_att = ATT_WIDTH // HEAD_DIM

    def src_unit(j, p):
        u = j * units_per_block + p
        q0, ga0 = OFF_Q // HEAD_DIM, OFF_GA // HEAD_DIM
        u = jnp.where((u >= q0) & (u < q0 + n_att), q0 + _regroup_unit(u - q0), u)
        return jnp.where((u >= ga0) & (u < ga0 + n_att), ga0 + _regroup_unit(u - ga0), u)

    unit = lambda p: pl.BlockSpec((HEAD_DIM, k), lambda j: (src_unit(j, p), 0))
    return pl.pallas_call(
        _prep_w_in_kernel,
        grid=(MAIN_COLS // KV_WIDTH,),
        in_specs=[unit(0), unit(1), unit(2), unit(3),
                  pl.BlockSpec((GATE_RANK, k), lambda j: (MAIN_COLS // GATE_RANK, 0))],
        out_specs=[pl.BlockSpec((k, KV_WIDTH), lambda j: (0, j)),
                   pl.BlockSpec((k, LANES), lambda j: (0, 0))],
        out_shape=[jax.ShapeDtypeStruct((k, MAIN_COLS), BF16),
                   jax.ShapeDtypeStruct((k, LANES), BF16)],
        compiler_params=pltpu.CompilerParams(
            dimension_semantics=("arbitrary",), vmem_limit_bytes=VMEM_LIMIT),
        name="prep_w_in",
    )(w_in_t, w_in_t, w_in_t, w_in_t, w_in_t)


def _prep_w_out_kernel(u0_ref, u1_ref, u2_ref, u3_ref, o_ref):
    o_ref[...] = jnp.concatenate(
        [u0_ref[...], u1_ref[...], u2_ref[...], u3_ref[...]], axis=0).astype(BF16)


def _prep_w_out(w_out):
    rows, cols = w_out.shape
    units_per_block = KV_WIDTH // HEAD_DIM
    n_att = ATT_WIDTH // HEAD_DIM

    def src_unit(j, p):
        u = j * units_per_block + p
        return jnp.where(u < n_att, _regroup_unit(u), u)

    unit = lambda p: pl.BlockSpec((HEAD_DIM, cols), lambda j: (src_unit(j, p), 0))
    return pl.pallas_call(
        _prep_w_out_kernel,
        grid=(rows // KV_WIDTH,),
        in_specs=[unit(0), unit(1), unit(2), unit(3)],
        out_specs=pl.BlockSpec((KV_WIDTH, cols), lambda j: (j, 0)),
        out_shape=jax.ShapeDtypeStruct((rows, cols), BF16),
        compiler_params=pltpu.CompilerParams(dimension_semantics=("arbitrary",)),
        name="prep_w_out",
    )(w_out, w_out, w_out, w_out)


def _tril2(blk):
    t = np.tril(np.ones((blk, blk), np.float32))
    return jnp.asarray(np.concatenate([t, t], axis=1), BF16)


def _prep(norm_g, w_in, w_gate_up, b_gate, q_norm_g, k_norm_g, sinks, gla_norm_g, w_out):
    w_main, w_lr = _prep_w_in(w_in.T)
    w_gu = jnp.pad(w_gate_up, ((0, LANES - GATE_RANK), (0, 0))).astype(BF16)
    g_qk = jnp.concatenate([jnp.tile(q_norm_g, N_HEADS) * (ATT_SCALE * LOG2E),
                            jnp.tile(k_norm_g, N_KV)])[None, :]

    def sink_rows(t):
        sk = sinks.astype(F32).reshape(N_KV, GQA_REP) * LOG2E
        return jnp.broadcast_to(sk[:, :, None, None], (N_KV, GQA_REP, t, LANES)).reshape(
            N_KV, GQA_REP * t, LANES)

    return dict(norm_g=norm_g[None, :], w_main=w_main, w_lr=w_lr, w_gu=w_gu, b_gate=b_gate[None, :],
                g_qk=g_qk, g_gla=jnp.tile(gla_norm_g, N_GLA)[None, :],
                w_out=_prep_w_out(w_out), sink_rows=sink_rows)


def kernel(x_prompt, x_sample, cache_k, cache_v, state_gla, norm_g, w_in, w_gate_up, b_gate,
           q_norm_g, k_norm_g, sinks, gla_norm_g, w_out):
    depth = norm_g.shape[0]
    assert depth == 1
    B, S, _ = x_prompt.shape
    DB, T, _ = x_sample.shape
    n_cache = cache_k.shape[2]
    assert n_cache == WINDOW
    p = _prep(norm_g[0], w_in[0], w_gate_up[0], b_gate[0], q_norm_g[0], k_norm_g[0], sinks[0],
              gla_norm_g[0], w_out[0])

    xp = x_prompt.reshape(B * S, D_MODEL)
    q, k, v, ga, qg, kg, vg, gg, la_hi, la_lo = _inproj(xp, p, tm=INPROJ_ROWS)
    y_p, win_k_p, win_v_p, gla_p = _layer_prompt(
        xp, q, k, v, ga, p["sink_rows"](CHUNK), qg, kg, vg, gg, la_hi, la_lo, _tril2(CHUNK), p["w_out"],
        B, S, tq=LAYER_ROWS, group=LAYER_GROUP)
    y_p = y_p.reshape(B, S, D_MODEL)

    xs = x_sample.reshape(DB * T, D_MODEL)
    q, k, v, ga, qg, kg, vg, gg, la_hi, la_lo = _inproj(xs, p, tm=DB * T)
    ck = cache_k[0].reshape(DB, n_cache, KV_WIDTH)
    cv = cache_v[0].reshape(DB, n_cache, KV_WIDTH)
    mix_a, win_k_s, win_v_s = _attn_sample(q, k, v, ck, cv, ga, p["sink_rows"](T), DB, T,
                                           bb=SAMPLE_STREAMS)
    blk = min(CHUNK, T)
    mix_g, gla_s = _gla(qg, kg, vg, gg, la_hi, la_lo, state_gla[0], _tril2(blk), DB, T,
                        blk=blk, tg=T)
    y_s = _outproj(xs, mix_a, mix_g, p["w_out"], tm=DB * T).reshape(DB, T, D_MODEL)

    shape5 = lambda a: a.reshape(1, a.shape[0], n_cache, N_KV, HEAD_DIM)
    return (y_p, y_s, shape5(win_k_p), shape5(win_v_p), gla_p[None],
            shape5(win_k_s), shape5(win_v_s), gla_s[None])
```

```python
import functools

import jax
import jax.numpy as jnp
import numpy as np
from jax import lax
from jax.experimental import pallas as pl
from jax.experimental.pallas import tpu as pltpu

D_MODEL = 2048
CHUNK = 64
WINDOW = 128
HEAD_DIM = 64
N_HEADS = 16
N_KV = 4
GQA_REP = N_HEADS // N_KV
ATT_WIDTH = N_HEADS * HEAD_DIM
KV_WIDTH = N_KV * HEAD_DIM
N_GLA = 4
GLA_DK = 128
GLA_DV = 256
GLA_KW = N_GLA * GLA_DK
GLA_WIDTH = N_GLA * GLA_DV
GATE_RANK = 16
GATE_TAU = 16.0
EPS = 1e-6
ATT_SCALE = HEAD_DIM ** -0.5
LOG2E = float(np.log2(np.e))
LOG2_DECAY_LIMIT = 100.0
WIN_CHUNKS = WINDOW // CHUNK
WIN_KEYS = WINDOW + CHUNK

LANES = 128
QK_WIDTH = ATT_WIDTH + KV_WIDTH
OFF_Q, OFF_K, OFF_V = 0, ATT_WIDTH, ATT_WIDTH + KV_WIDTH
OFF_GA = OFF_V + KV_WIDTH
OFF_QG = OFF_GA + ATT_WIDTH
OFF_KG = OFF_QG + GLA_KW
OFF_VG = OFF_KG + GLA_KW
OFF_GG = OFF_VG + GLA_WIDTH
MAIN_COLS = OFF_GG + GLA_WIDTH

VMEM_LIMIT = 56 * 1024 * 1024
INPROJ_ROWS = 512
INPROJ_SUB = 256
LAYER_ROWS = 512
LAYER_GROUP = 4
SAMPLE_STREAMS = 4

BF16 = jnp.bfloat16
F32 = jnp.float32


def _dot(a, b):
    return jnp.dot(a, b, preferred_element_type=F32)


def _dot_t(a, b):
    return lax.dot_general(a, b, (((1,), (1,)), ((), ())), preferred_element_type=F32)


def _dot_ta(a, b):
    return lax.dot_general(a, b, (((0,), (0,)), ((), ())), preferred_element_type=F32)


def _silu(x):
    return x / (1.0 + jnp.exp(-x))


def _const_spec(shape):
    nd = len(shape)
    return pl.BlockSpec(shape, lambda *_: (0,) * nd, pipeline_mode=pl.Buffered(1))


def _inproj_kernel(x_ref, ng_ref, w_ref, wlr_ref, wgu_ref, bg_ref, gqk_ref, gng_ref,
                   q_ref, k_ref, v_ref, ga_ref, qg_ref, kg_ref, vg_ref, gg_ref, lah_ref, lal_ref,
                   *, sub):
    lo = lax.broadcasted_iota(jnp.int32, (1, LANES), 1) < HEAD_DIM
    for s in range(x_ref.shape[0] // sub):
        rows = slice(s * sub, (s + 1) * sub)
        x = x_ref[rows, :]
        ms = jnp.mean(x * x, axis=-1, keepdims=True)
        xn = (x * lax.rsqrt(ms + EPS) * ng_ref[...]).astype(BF16)

        qk = _dot(xn, w_ref[:, OFF_Q:OFF_V])
        ga_ref[rows, :] = _silu(_dot(xn, w_ref[:, OFF_GA:OFF_QG])).astype(BF16)
        lr = _dot(xn, wlr_ref[...]).astype(BF16)
        qg_ref[rows, :] = (_dot(xn, w_ref[:, OFF_QG:OFF_KG]) * (GLA_DK ** -0.5)).astype(BF16)
        kg_ref[rows, :] = _dot(xn, w_ref[:, OFF_KG:OFF_VG]).astype(BF16)
        z = _dot(lr, wgu_ref[...]) + bg_ref[...]
        gg_ref[rows, :] = (_silu(_dot(xn, w_ref[:, OFF_GG:MAIN_COLS])) * gng_ref[...]).astype(BF16)
        vg_ref[rows, :] = _dot(xn, w_ref[:, OFF_VG:OFF_GG]).astype(BF16)
        v_ref[rows, :] = _dot(xn, w_ref[:, OFF_V:OFF_GA])

        for n in range(QK_WIDTH // LANES):
            cols = slice(n * LANES, (n + 1) * LANES)
            blk = qk[:, cols]
            sq = blk * blk
            ms_lo = jnp.sum(jnp.where(lo, sq, 0.0), axis=-1, keepdims=True) * (1.0 / HEAD_DIM)
            ms_hi = jnp.sum(jnp.where(lo, 0.0, sq), axis=-1, keepdims=True) * (1.0 / HEAD_DIM)
            scale = jnp.where(lo, lax.rsqrt(ms_lo + EPS), lax.rsqrt(ms_hi + EPS))
            normed = blk * scale * gqk_ref[:, cols]
            if n < ATT_WIDTH // LANES:
                q_ref[rows, cols] = normed.astype(BF16)
            else:
                k_ref[rows, n * LANES - ATT_WIDTH:(n + 1) * LANES - ATT_WIDTH] = normed
        log_sig = jnp.minimum(z, 0.0) - jnp.log(1.0 + jnp.exp(-jnp.abs(z)))
        la = log_sig * (LOG2E / GATE_TAU)
        la_hi = la.astype(BF16)
        lah_ref[rows, :] = la_hi
        lal_ref[rows, :] = (la - la_hi.astype(F32)).astype(BF16)


def _inproj(x2d, p, tm):
    m = x2d.shape[0]
    row = lambda w: pl.BlockSpec((tm, w), lambda i: (i, 0))
    outs = [(ATT_WIDTH, BF16), (KV_WIDTH, F32), (KV_WIDTH, F32), (ATT_WIDTH, BF16),
            (GLA_KW, BF16), (GLA_KW, BF16), (GLA_WIDTH, BF16), (GLA_WIDTH, BF16),
            (GLA_KW, BF16), (GLA_KW, BF16)]
    consts = [p["norm_g"], p["w_main"], p["w_lr"], p["w_gu"], p["b_gate"], p["g_qk"], p["g_gla"]]
    return pl.pallas_call(
        functools.partial(_inproj_kernel, sub=min(tm, INPROJ_SUB)),
        grid=(m // tm,),
        in_specs=[row(D_MODEL)] + [_const_spec(c.shape) for c in consts],
        out_specs=[row(w) for w, _ in outs],
        out_shape=[jax.ShapeDtypeStruct((m, w), dt) for w, dt in outs],
        compiler_params=pltpu.CompilerParams(
            dimension_semantics=("arbitrary",), vmem_limit_bytes=VMEM_LIMIT),
        name="inproj",
    )(x2d, *consts)


def _layer_prompt_kernel(q_ref, kp_ref, kc_ref, vp_ref, vc_ref, ga_ref, sink_ref,
                         qg_ref, kg_ref, vg_ref, gg_ref, lah_ref, lal_ref, tril_ref, x_ref, w_ref,
                         y_ref, wk_ref, wv_ref, sout_ref,
                         kw_ref, vw_ref, st_ref, b_ref, qx_ref, mix_ref, *, tq, group, nblk, n_steps):
    n_chunks = tq // CHUNK
    assert n_chunks % group == 0 and group >= WIN_CHUNKS and tq >= WINDOW
    t = pl.program_id(0)
    i = lax.rem(jnp.minimum(t, n_steps - 1), nblk)
    last = nblk - 1
    cur = i % 2
    nxt = 1 - cur
    mix_slot = t % 2
    prev_slot = 1 - mix_slot

    out_rows, out_cols = 512, 512
    out_pieces = [(slice(r, r + out_rows), slice(c, c + out_cols))
                  for r in range(0, tq, out_rows) for c in range(0, D_MODEL, out_cols)]

    def out_piece(n):
        r, c = out_pieces[n]
        y_ref[r, c] = x_ref[r, c] + _dot(mix_ref[prev_slot, r, :], w_ref[:, c])

    def mixers(do_out):
        kw_ref[0:WINDOW, :] = kp_ref[0].astype(BF16)
        kw_ref[WINDOW:, :] = kc_ref[0].astype(BF16)
        vw_ref[0:WINDOW, :] = vp_ref[0].astype(BF16)
        vw_ref[WINDOW:, :] = vc_ref[0].astype(BF16)

        @pl.when(i == 0)
        def _():
            st_ref[0] = jnp.zeros(st_ref.shape[1:], F32)

        @pl.when(i == last)
        def _():
            wk_ref[0] = kc_ref[0, tq - WINDOW:, :]
            wv_ref[0] = vc_ref[0, tq - WINDOW:, :]

        lane = lax.broadcasted_iota(jnp.int32, (1, LANES), 1)
        lo = lane < HEAD_DIM
        col_group = lax.broadcasted_iota(jnp.int32, (1, KV_WIDTH), 1) // HEAD_DIM
        col = lax.broadcasted_iota(jnp.int32, (1, N_KV * WIN_KEYS), 1)
        key_in_seg = col - WIN_KEYS * sum((col >= g * WIN_KEYS).astype(jnp.int32) for g in range(1, N_KV))
        neg_inf = jnp.float32(-jnp.inf)
        rows = [slice(c * CHUNK, (c + 1) * CHUNK) for c in range(n_chunks)]
        a_s, a_v, a_p, a_den, a_ov = {}, {}, {}, {}, {}

        def block_diag(win):
            zero = jnp.zeros_like(win)
            return jnp.concatenate([jnp.where(col_group == g, win, zero) for g in range(N_KV)], axis=0)

        def attn_scores(c):
            k_bd = block_diag(kw_ref[c * CHUNK:c * CHUNK + WIN_KEYS, :])
            a_v[c] = block_diag(vw_ref[c * CHUNK:c * CHUNK + WIN_KEYS, :])
            qs = jnp.concatenate(
                [q_ref[rows[c], r * KV_WIDTH:(r + 1) * KV_WIDTH] for r in range(GQA_REP)], axis=0)
            a_s[c] = _dot_t(qs, k_bd)

        def attn_softmax(c):
            su = a_s.pop(c)
            if c < WIN_CHUNKS:
                first_valid = jnp.where(i == 0, (WIN_CHUNKS - c) * CHUNK, 0)
                su = jnp.where(key_in_seg >= first_valid, su, neg_inf)
            cs = [su[:, LANES * n:LANES * (n + 1)] for n in range(6)]
            seg_max = [
                jnp.maximum(cs[0], jnp.where(lo, cs[1], neg_inf)),
                jnp.maximum(jnp.where(lo, neg_inf, cs[1]), cs[2]),
                jnp.maximum(cs[3], jnp.where(lo, cs[4], neg_inf)),
                jnp.maximum(jnp.where(lo, neg_inf, cs[4]), cs[5]),
            ]
            mu = [jnp.maximum(jnp.max(seg_max[g], axis=-1, keepdims=True), sink_ref[g])
                  for g in range(N_KV)]
            shift = [mu[0], jnp.where(lo, mu[0], mu[1]), mu[1], mu[2], jnp.where(lo, mu[2], mu[3]), mu[3]]
            e = [jnp.exp2(cs[n] - shift[n]) for n in range(6)]
            a_p[c] = jnp.concatenate(e, axis=1).astype(BF16)
            seg_sum = [e[0] + jnp.where(lo, e[1], 0.0), jnp.where(lo, 0.0, e[1]) + e[2],
                       e[3] + jnp.where(lo, e[4], 0.0), jnp.where(lo, 0.0, e[4]) + e[5]]
            a_den[c] = [jnp.sum(seg_sum[g], axis=-1, keepdims=True) + jnp.exp2(sink_ref[g] - mu[g])
                        for g in range(N_KV)]

        def attn_pv(c):
            a_ov[c] = _dot(a_p.pop(c), a_v.pop(c))

        def attn_finish(c):
            ov, den = a_ov.pop(c), a_den.pop(c)
            den = jnp.concatenate([jnp.where(lo, den[0], den[1]), jnp.where(lo, den[2], den[3])], axis=1)
            o = ov / den
            for r in range(GQA_REP):
                cols = slice(r * KV_WIDTH, (r + 1) * KV_WIDTH)
                gate = ga_ref[rows[c], cols].astype(F32)
                mix_ref[mix_slot, rows[c], cols] = (o[r * CHUNK:(r + 1) * CHUNK, :] * gate).astype(BF16)

        rowi = lax.broadcasted_iota(jnp.int32, (CHUNK, CHUNK), 0)
        coli = lax.broadcasted_iota(jnp.int32, (CHUNK, CHUNK), 1)
        causal = rowi >= coli
        eye = (lax.broadcasted_iota(jnp.int32, (GLA_DK, GLA_DK), 0)
               == lax.broadcasted_iota(jnp.int32, (GLA_DK, GLA_DK), 1))
        heads = range(N_GLA)
        kcol = [slice(h * GLA_DK, (h + 1) * GLA_DK) for h in heads]
        vcol = [slice(h * GLA_DV, (h + 1) * GLA_DV) for h in heads]
        mcol = [slice(ATT_WIDTH + h * GLA_DV, ATT_WIDTH + (h + 1) * GLA_DV) for h in heads]
        g_b, g_qd, g_ki, g_ke, g_dc, g_att, g_dst, g_o = {}, {}, {}, {}, {}, {}, {}, {}
        state = {}

        def cum_log2_decay(r):
            return _dot(tril_ref[...], jnp.concatenate([lah_ref[r, :], lal_ref[r, :]], axis=0))

        def decay_column(blast):
            dec_row = jnp.broadcast_to(jnp.exp2(blast), (GLA_DK, GLA_DK))
            return jnp.sum(jnp.where(eye, dec_row, 0.0), axis=-1, keepdims=True)

        def gla_finish_rows(ov, r, h):
            ms = jnp.mean(ov * ov, axis=-1, keepdims=True)
            gate = gg_ref[r, vcol[h]].astype(F32)
            mix_ref[mix_slot, r, mcol[h]] = (ov * lax.rsqrt(ms + EPS) * gate).astype(BF16)

        def gla_cumsum(c):
            g_b[c] = cum_log2_decay(rows[c])

        def gla_prepare(c):
            for h in heads:
                b = g_b[c][:, kcol[h]]
                blast = b[CHUNK - 1:CHUNK, :]
                q = qg_ref[rows[c], kcol[h]].astype(F32)
                k = kg_ref[rows[c], kcol[h]].astype(F32)
                decay = jnp.exp2(b)
                g_qd[c, h] = (q * decay).astype(BF16)
                g_ki[c, h] = (k * (1.0 / decay)).astype(BF16)
                g_ke[c, h] = (k * jnp.exp2(blast - b)).astype(BF16)
                g_dc[c, h] = decay_column(blast)

        def gla_scores(c):
            for h in heads:
                g_att[c, h] = _dot_t(g_qd[c, h], g_ki.pop((c, h)))
                g_dst[c, h] = _dot_ta(g_ke.pop((c, h)), vg_ref[rows[c], vcol[h]])

        def gla_outputs(c):
            for h in heads:
                att = jnp.where(causal, g_att.pop((c, h)), 0.0).astype(BF16)
                lhs = jnp.concatenate([g_qd.pop((c, h)), att], axis=1)
                rhs = jnp.concatenate([state[h].astype(BF16), vg_ref[rows[c], vcol[h]]], axis=0)
                g_o[c, h] = _dot(lhs, rhs)
                state[h] = state[h] * g_dc.pop((c, h)) + g_dst.pop((c, h))

        def gla_finish(c):
            for h in heads:
                gla_finish_rows(g_o.pop((c, h)), rows[c], h)

        def gla_exact_step():
            def chunk(c, carry):
                r = pl.ds(pl.multiple_of(c * CHUNK, CHUNK), CHUNK)
                bcum = cum_log2_decay(r)
                for h in heads:
                    b = bcum[:, kcol[h]]
                    blast = b[CHUNK - 1:CHUNK, :]
                    q = qg_ref[r, kcol[h]].astype(F32)
                    k = kg_ref[r, kcol[h]].astype(F32)
                    v = vg_ref[r, vcol[h]]
                    b_ref[...] = b
                    qx_ref[...] = q

                    def score_row(tok, att_t):
                        w = jnp.where(rowi[:, 0:1] <= tok, jnp.exp2(b_ref[pl.ds(tok, 1), :] - b), 0.0)
                        colv = jnp.sum(qx_ref[pl.ds(tok, 1), :] * k * w, axis=-1, keepdims=True)
                        return att_t + jnp.where(coli == tok, colv, 0.0)

                    att_t = lax.fori_loop(0, CHUNK, score_row, jnp.zeros((CHUNK, CHUNK), F32))
                    q_dec = (q * jnp.exp2(b)).astype(BF16)
                    k_end = (k * jnp.exp2(blast - b)).astype(BF16)
                    st = st_ref[nxt, h]
                    ov = _dot_ta(att_t.astype(BF16), v) + _dot(q_dec, st.astype(BF16))
                    st_ref[nxt, h] = st * decay_column(blast) + _dot_ta(k_end, v)
                    gla_finish_rows(ov, r, h)
                return carry

            st_ref[nxt] = st_ref[cur]
            lax.fori_loop(0, n_chunks, chunk, 0)

        pending = list(range(len(out_pieces))) if do_out else []
        n_slots = 4 * (n_chunks // group)
        slots_seen = [0]

        def out_slot():
            slots_seen[0] += 1
            due = len(out_pieces) * slots_seen[0] // n_slots - (len(out_pieces) - len(pending))
            for _ in range(min(due, len(pending))):
                out_piece(pending.pop(0))

        for h in heads:
            state[h] = st_ref[cur, h]
        for g0 in range(0, n_chunks, group):
            cs = list(range(g0, g0 + group))
            half = group // 2
            for c in cs:
                gla_cumsum(c)
            out_slot()
            for c in cs[:half]:
                attn_scores(c)
            out_slot()
            for c in cs:
                gla_prepare(c)
            for c in cs:
                gla_scores(c)
            out_slot()
            for c in cs[:half]:
                attn_softmax(c)
            for c in cs[:half]:
                attn_pv(c)
            for c in cs[half:]:
                attn_scores(c)
            out_slot()
            for c in cs:
                gla_outputs(c)
            for c in cs[half:]:
                attn_softmax(c)
            for c in cs[half:]:
                attn_pv(c)
            for c in cs:
                gla_finish(c)
            for c in cs:
                attn_finish(c)
        assert not pending
        for h in heads:
            st_ref[nxt, h] = state[h]
        total = functools.reduce(jnp.minimum, [g_b[c][CHUNK - 1:CHUNK, :] for c in range(n_chunks)])

        @pl.when(jnp.min(total) < -LOG2_DECAY_LIMIT)
        def _():
            gla_exact_step()

        @pl.when(i == last)
        def _():
            sout_ref[0] = st_ref[nxt]

    @pl.when(t == 0)
    def _():
        mixers(do_out=False)

    @pl.when((t > 0) & (t < n_steps))
    def _():
        mixers(do_out=True)

    @pl.when(t == n_steps)
    def _():
        for n in range(len(out_pieces)):
            out_piece(n)


def _layer_prompt(x2d, q, k, v, ga, sink_rows, qg, kg, vg, gg, la_hi, la_lo, tril2, w_out, batch, seq,
                  tq, group):
    nblk = seq // tq
    n_steps = batch * nblk
    per_win = tq // WINDOW
    blk_of = lambda t: jnp.minimum(t, n_steps - 1)
    rows = lambda w: pl.BlockSpec((tq, w), lambda t: (blk_of(t), 0))
    late = lambda w: pl.BlockSpec((tq, w), lambda t: (jnp.maximum(t - 1, 0), 0))
    cur = pl.BlockSpec((1, tq, KV_WIDTH), lambda t: (blk_of(t) // nblk, blk_of(t) % nblk, 0))
    prev = pl.BlockSpec((1, WINDOW, KV_WIDTH),
                        lambda t: (blk_of(t) // nblk, jnp.maximum((blk_of(t) % nblk) * per_win - 1, 0), 0))
    win = pl.BlockSpec((1, WINDOW, KV_WIDTH), lambda t: (blk_of(t) // nblk, 0, 0))
    state = pl.BlockSpec((1, N_GLA, GLA_DK, GLA_DV), lambda t: (blk_of(t) // nblk, 0, 0, 0))
    k3 = k.reshape(batch, seq, KV_WIDTH)
    v3 = v.reshape(batch, seq, KV_WIDTH)
    return pl.pallas_call(
        functools.partial(_layer_prompt_kernel, tq=tq, group=group, nblk=nblk, n_steps=n_steps),
        grid=(n_steps + 1,),
        in_specs=[rows(ATT_WIDTH), prev, cur, prev, cur, rows(ATT_WIDTH), _const_spec(sink_rows.shape),
                  rows(GLA_KW), rows(GLA_KW), rows(GLA_WIDTH), rows(GLA_WIDTH), rows(GLA_KW),
                  rows(GLA_KW), _const_spec(tril2.shape), late(D_MODEL), _const_spec(w_out.shape)],
        out_specs=[late(D_MODEL), win, win, state],
        out_shape=[jax.ShapeDtypeStruct((batch * seq, D_MODEL), F32),
                   jax.ShapeDtypeStruct((batch, WINDOW, KV_WIDTH), F32),
                   jax.ShapeDtypeStruct((batch, WINDOW, KV_WIDTH), F32),
                   jax.ShapeDtypeStruct((batch, N_GLA, GLA_DK, GLA_DV), F32)],
        scratch_shapes=[pltpu.VMEM((WINDOW + tq, KV_WIDTH), BF16),
                        pltpu.VMEM((WINDOW + tq, KV_WIDTH), BF16),
                        pltpu.VMEM((2, N_GLA, GLA_DK, GLA_DV), F32),
                        pltpu.VMEM((CHUNK, GLA_DK), F32),
                        pltpu.VMEM((CHUNK, GLA_DK), F32),
                        pltpu.VMEM((2, tq, D_MODEL), BF16)],
        compiler_params=pltpu.CompilerParams(
            dimension_semantics=("arbitrary",), vmem_limit_bytes=VMEM_LIMIT),
        name="layer_prompt",
    )(q, k3, k3, v3, v3, ga, sink_rows, qg, kg, vg, gg, la_hi, la_lo, tril2, x2d, w_out)


def _attn_sample_kernel(q_ref, kn_ref, vn_ref, ck_ref, cv_ref, ga_ref, sink_ref,
                        o_ref, wk_ref, wv_ref, *, t, bb):
    n_cache = ck_ref.shape[1]
    assert n_cache == LANES and t <= LANES
    lane = lax.broadcasted_iota(jnp.int32, (1, LANES), 1)
    lo = lane < HEAD_DIM
    is_new = lane < t
    col_group = lax.broadcasted_iota(jnp.int32, (1, KV_WIDTH), 1) // HEAD_DIM
    seg = 2 * LANES
    ones_bd = jnp.concatenate(
        [jnp.broadcast_to(jnp.where(col_group == g, 1.0, 0.0).astype(BF16), (seg, KV_WIDTH))
         for g in range(N_KV)], axis=0)
    pad = jnp.zeros((LANES - t, KV_WIDTH), BF16)
    neg_inf = jnp.float32(-jnp.inf)

    s, v_all = [], []
    for b in range(bb):
        rows = slice(b * t, (b + 1) * t)
        kn, vn, ck, cv = kn_ref[rows, :], vn_ref[rows, :], ck_ref[b], cv_ref[b]
        wk_ref[b] = jnp.concatenate([ck[t:, :], kn], axis=0)
        wv_ref[b] = jnp.concatenate([cv[t:, :], vn], axis=0)
        kcat = jnp.concatenate([ck.astype(BF16), kn.astype(BF16), pad], axis=0)
        vcat = jnp.concatenate([cv.astype(BF16), vn.astype(BF16), pad], axis=0)
        zero = jnp.zeros_like(kcat)
        k_bd = jnp.concatenate([jnp.where(col_group == g, kcat, zero) for g in range(N_KV)], axis=0)
        v_bd = jnp.concatenate([jnp.where(col_group == g, vcat, zero) for g in range(N_KV)], axis=0)
        v_all.append(jnp.concatenate([v_bd, ones_bd], axis=1))
        qs = jnp.concatenate(
            [q_ref[rows, r * KV_WIDTH:(r + 1) * KV_WIDTH] for r in range(GQA_REP)], axis=0)
        s.append(_dot_t(qs, k_bd))
    p, m = [], []
    for b in range(bb):
        c = [s[b][:, LANES * n:LANES * (n + 1)] for n in range(2 * N_KV)]
        c = [cn if n % 2 == 0 else jnp.where(is_new, cn, neg_inf) for n, cn in enumerate(c)]
        mb = [jnp.maximum(jnp.max(jnp.maximum(c[2 * g], c[2 * g + 1]), axis=-1, keepdims=True),
                          sink_ref[g]) for g in range(N_KV)]
        p.append(jnp.concatenate([jnp.exp2(c[n] - mb[n // 2]) for n in range(2 * N_KV)],
                                 axis=1).astype(BF16))
        m.append(mb)
    ov = [_dot(p[b], v_all[b]) for b in range(bb)]
    for b in range(bb):
        rows = slice(b * t, (b + 1) * t)
        es = [jnp.exp2(sink_ref[g] - m[b][g]) for g in range(N_KV)]
        e_sink = jnp.concatenate([jnp.where(lo, es[0], es[1]), jnp.where(lo, es[2], es[3])], axis=1)
        o = ov[b][:, :KV_WIDTH] / (ov[b][:, KV_WIDTH:] + e_sink)
        for r in range(GQA_REP):
            cols = slice(r * KV_WIDTH, (r + 1) * KV_WIDTH)
            gate = ga_ref[rows, cols].astype(F32)
            o_ref[rows, cols] = (o[r * t:(r + 1) * t, :] * gate).astype(BF16)


def _attn_sample(q, k, v, cache_k, cache_v, ga, sink_rows, batch, t, bb):
    n_cache = cache_k.shape[1]
    rows = lambda w: pl.BlockSpec((bb * t, w), lambda b: (b, 0))
    cache = pl.BlockSpec((bb, n_cache, KV_WIDTH), lambda b: (b, 0, 0))
    return pl.pallas_call(
        functools.partial(_attn_sample_kernel, t=t, bb=bb),
        grid=(batch // bb,),
        in_specs=[rows(ATT_WIDTH), rows(KV_WIDTH), rows(KV_WIDTH), cache, cache, rows(ATT_WIDTH),
                  _const_spec(sink_rows.shape)],
        out_specs=[rows(ATT_WIDTH), cache, cache],
        out_shape=[jax.ShapeDtypeStruct((batch * t, ATT_WIDTH), BF16),
                   jax.ShapeDtypeStruct((batch, n_cache, KV_WIDTH), F32),
                   jax.ShapeDtypeStruct((batch, n_cache, KV_WIDTH), F32)],
        compiler_params=pltpu.CompilerParams(
            dimension_semantics=("arbitrary",), vmem_limit_bytes=VMEM_LIMIT),
        name="attn_sample",
    )(q, k, v, cache_k, cache_v, ga, sink_rows)


def _gla_kernel(qg_ref, kg_ref, vg_ref, gg_ref, lah_ref, lal_ref, s0_ref, tril_ref,
                o_ref, sout_ref, b_ref, q_ref, *, blk, n_chunks, streams):
    rowi = lax.broadcasted_iota(jnp.int32, (blk, blk), 0)
    coli = lax.broadcasted_iota(jnp.int32, (blk, blk), 1)
    causal = rowi >= coli
    eye = (lax.broadcasted_iota(jnp.int32, (GLA_DK, GLA_DK), 0)
           == lax.broadcasted_iota(jnp.int32, (GLA_DK, GLA_DK), 1))

    heads = range(N_GLA)
    kcol = [slice(h * GLA_DK, (h + 1) * GLA_DK) for h in heads]
    vcol = [slice(h * GLA_DV, (h + 1) * GLA_DV) for h in heads]

    def cum_log2_decay(r):
        return _dot(tril_ref[...], jnp.concatenate([lah_ref[r, :], lal_ref[r, :]], axis=0))

    def decay_column(blast):
        dec_row = jnp.broadcast_to(jnp.exp2(blast), (GLA_DK, GLA_DK))
        return jnp.sum(jnp.where(eye, dec_row, 0.0), axis=-1, keepdims=True)

    def finish(ov, r, h):
        ms = jnp.mean(ov * ov, axis=-1, keepdims=True)
        gate = gg_ref[r, vcol[h]].astype(F32)
        o_ref[r, vcol[h]] = (ov * lax.rsqrt(ms + EPS) * gate).astype(BF16)

    units = [(s, u) for s in range(streams) for u in range(n_chunks)]
    rows = {(s, u): slice((s * n_chunks + u) * blk, (s * n_chunks + u + 1) * blk) for s, u in units}

    def exact_step():
        for s in range(streams):
            st = [s0_ref[s, h] for h in heads]
            for u in range(n_chunks):
                r = rows[s, u]
                bcum = cum_log2_decay(r)
                for h in heads:
                    b = bcum[:, kcol[h]]
                    blast = b[blk - 1:blk, :]
                    q = qg_ref[r, kcol[h]].astype(F32)
                    k = kg_ref[r, kcol[h]].astype(F32)
                    v = vg_ref[r, vcol[h]]
                    b_ref[...] = b
                    q_ref[...] = q

                    def score_row(t, att_t):
                        w = jnp.where(rowi[:, 0:1] <= t, jnp.exp2(b_ref[pl.ds(t, 1), :] - b), 0.0)
                        col = jnp.sum(q_ref[pl.ds(t, 1), :] * k * w, axis=-1, keepdims=True)
                        return att_t + jnp.where(coli == t, col, 0.0)

                    att_t = lax.fori_loop(0, blk, score_row, jnp.zeros((blk, blk), F32))
                    q_dec = (q * jnp.exp2(b)).astype(BF16)
                    k_end = (k * jnp.exp2(blast - b)).astype(BF16)
                    ov = _dot_ta(att_t.astype(BF16), v) + _dot(q_dec, st[h].astype(BF16))
                    st[h] = st[h] * decay_column(blast) + _dot_ta(k_end, v)
                    finish(ov, r, h)
            for h in heads:
                sout_ref[s, h] = st[h]

    bcum = {unit: cum_log2_decay(rows[unit]) for unit in units}
    q_dec, k_inv, k_end, dec_col = {}, {}, {}, {}
    for unit in units:
        for h in heads:
            b = bcum[unit][:, kcol[h]]
            blast = b[blk - 1:blk, :]
            q = qg_ref[rows[unit], kcol[h]].astype(F32)
            k = kg_ref[rows[unit], kcol[h]].astype(F32)
            decay = jnp.exp2(b)
            q_dec[unit, h] = (q * decay).astype(BF16)
            k_inv[unit, h] = (k * (1.0 / decay)).astype(BF16)
            k_end[unit, h] = (k * jnp.exp2(blast - b)).astype(BF16)
            dec_col[unit, h] = decay_column(blast)
    att = {(unit, h): _dot_t(q_dec[unit, h], k_inv[unit, h]) for unit in units for h in heads}
    d_st = {(unit, h): _dot_ta(k_end[unit, h], vg_ref[rows[unit], vcol[h]])
            for unit in units for h in heads}
    att = {key: jnp.where(causal, a, 0.0).astype(BF16) for key, a in att.items()}
    st = {(s, h): s0_ref[s, h] for s in range(streams) for h in heads}
    o = {}
    for s, u in units:
        for h in heads:
            o[s, u, h] = (_dot(att[(s, u), h], vg_ref[rows[s, u], vcol[h]])
                          + _dot(q_dec[(s, u), h], st[s, h].astype(BF16)))
            st[s, h] = st[s, h] * dec_col[(s, u), h] + d_st[(s, u), h]
    for (s, h), value in st.items():
        sout_ref[s, h] = value
    for s, u in units:
        for h in heads:
            finish(o[s, u, h], rows[s, u], h)

    total = functools.reduce(jnp.minimum, [bc[blk - 1:blk, :] for bc in bcum.values()])

    @pl.when(jnp.min(total) < -LOG2_DECAY_LIMIT)
    def _():
        exact_step()


def _gla(qg, kg, vg, gg, la_hi, la_lo, s0, tril2, batch, seq, blk, streams):
    rows = lambda w: pl.BlockSpec((streams * seq, w), lambda g: (g, 0))
    state = pl.BlockSpec((streams, N_GLA, GLA_DK, GLA_DV), lambda g: (g, 0, 0, 0))
    return pl.pallas_call(
        functools.partial(_gla_kernel, blk=blk, n_chunks=seq // blk, streams=streams),
        grid=(batch // streams,),
        in_specs=[rows(GLA_KW), rows(GLA_KW), rows(GLA_WIDTH), rows(GLA_WIDTH), rows(GLA_KW),
                  rows(GLA_KW), state, _const_spec(tril2.shape)],
        out_specs=[rows(GLA_WIDTH), state],
        out_shape=[jax.ShapeDtypeStruct((batch * seq, GLA_WIDTH), BF16),
                   jax.ShapeDtypeStruct((batch, N_GLA, GLA_DK, GLA_DV), F32)],
        scratch_shapes=[pltpu.VMEM((blk, GLA_DK), F32),
                        pltpu.VMEM((blk, GLA_DK), F32)],
        compiler_params=pltpu.CompilerParams(
            dimension_semantics=("arbitrary",), vmem_limit_bytes=VMEM_LIMIT),
        name="gla",
    )(qg, kg, vg, gg, la_hi, la_lo, s0, tril2)


def _outproj_kernel(x_ref, ma_ref, mg_ref, w_ref, y_ref):
    y_ref[...] = (x_ref[...] + _dot(ma_ref[...], w_ref[:ATT_WIDTH, :])
                  + _dot(mg_ref[...], w_ref[ATT_WIDTH:, :]))


def _outproj(x2d, mix_a, mix_g, w, tm):
    m = x2d.shape[0]
    row = lambda w: pl.BlockSpec((tm, w), lambda i: (i, 0))
    return pl.pallas_call(
        _outproj_kernel,
        grid=(m // tm,),
        in_specs=[row(D_MODEL), row(ATT_WIDTH), row(GLA_WIDTH), _const_spec(w.shape)],
        out_specs=row(D_MODEL),
        out_shape=jax.ShapeDtypeStruct((m, D_MODEL), F32),
        compiler_params=pltpu.CompilerParams(
            dimension_semantics=("arbitrary",), vmem_limit_bytes=VMEM_LIMIT),
        name="outproj",
    )(x2d, mix_a, mix_g, w)


def _regroup_unit(u):
    return (u % N_KV) * GQA_REP + u // N_KV


def _prep_w_in_kernel(u0_ref, u1_ref, u2_ref, u3_ref, lr_ref, main_ref, wlr_ref):
    units = jnp.concatenate([u0_ref[...], u1_ref[...], u2_ref[...], u3_ref[...]], axis=0)
    main_ref[...] = units.T.astype(BF16)

    @pl.when(pl.program_id(0) == 0)
    def _():
        lr = lr_ref[...]
        padded = jnp.concatenate([lr, jnp.zeros((LANES - lr.shape[0], lr.shape[1]), F32)], axis=0)
        wlr_ref[...] = padded.T.astype(BF16)


def _prep_w_in(w_in_t):
    _, k = w_in_t.shape
    units_per_block = KV_WIDTH // HEAD_DIM
    n_att = ATT_WIDTH // HEAD_DIM

    def src_unit(j, p):
        u = j * units_per_block + p
        q0, ga0 = OFF_Q // HEAD_DIM, OFF_GA // HEAD_DIM
        u = jnp.where((u >= q0) & (u < q0 + n_att), q0 + _regroup_unit(u - q0), u)
        return jnp.where((u >= ga0) & (u < ga0 + n_att), ga0 + _regroup_unit(u - ga0), u)

    unit = lambda p: pl.BlockSpec((HEAD_DIM, k), lambda j: (src_unit(j, p), 0))
    return pl.pallas_call(
        _prep_w_in_kernel,
        grid=(MAIN_COLS // KV_WIDTH,),
        in_specs=[unit(0), unit(1), unit(2), unit(3),
                  pl.BlockSpec((GATE_RANK, k), lambda j: (MAIN_COLS // GATE_RANK, 0))],
        out_specs=[pl.BlockSpec((k, KV_WIDTH), lambda j: (0, j)),
                   pl.BlockSpec((k, LANES), lambda j: (0, 0))],
        out_shape=[jax.ShapeDtypeStruct((k, MAIN_COLS), BF16),
                   jax.ShapeDtypeStruct((k, LANES), BF16)],
        compiler_params=pltpu.CompilerParams(
            dimension_semantics=("arbitrary",), vmem_limit_bytes=VMEM_LIMIT),
        name="prep_w_in",
    )(w_in_t, w_in_t, w_in_t, w_in_t, w_in_t)


def _prep_w_out_kernel(u0_ref, u1_ref, u2_ref, u3_ref, o_ref):
    o_ref[...] = jnp.concatenate(
        [u0_ref[...], u1_ref[...], u2_ref[...], u3_ref[...]], axis=0).astype(BF16)


def _prep_w_out(w_out):
    rows, cols = w_out.shape
    units_per_block = KV_WIDTH // HEAD_DIM
    n_att = ATT_WIDTH // HEAD_DIM

    def src_unit(j, p):
        u = j * units_per_block + p
        return jnp.where(u < n_att, _regroup_unit(u), u)

    unit = lambda p: pl.BlockSpec((HEAD_DIM, cols), lambda j: (src_unit(j, p), 0))
    return pl.pallas_call(
        _prep_w_out_kernel,
        grid=(rows // KV_WIDTH,),
        in_specs=[unit(0), unit(1), unit(2), unit(3)],
        out_specs=pl.BlockSpec((KV_WIDTH, cols), lambda j: (j, 0)),
        out_shape=jax.ShapeDtypeStruct((rows, cols), BF16),
        compiler_params=pltpu.CompilerParams(dimension_semantics=("arbitrary",)),
        name="prep_w_out",
    )(w_out, w_out, w_out, w_out)


def _tril2(blk):
    t = np.tril(np.ones((blk, blk), np.float32))
    return jnp.asarray(np.concatenate([t, t], axis=1), BF16)


def _prep(norm_g, w_in, w_gate_up, b_gate, q_norm_g, k_norm_g, sinks, gla_norm_g, w_out):
    w_main, w_lr = _prep_w_in(w_in.T)
    w_gu = jnp.pad(w_gate_up, ((0, LANES - GATE_RANK), (0, 0))).astype(BF16)
    g_qk = jnp.concatenate([jnp.tile(q_norm_g, N_HEADS) * (ATT_SCALE * LOG2E),
                            jnp.tile(k_norm_g, N_KV)])[None, :]

    def sink_rows(t):
        sk = sinks.astype(F32).reshape(N_KV, GQA_REP) * LOG2E
        return jnp.broadcast_to(sk[:, :, None, None], (N_KV, GQA_REP, t, LANES)).reshape(
            N_KV, GQA_REP * t, LANES)

    return dict(norm_g=norm_g[None, :], w_main=w_main, w_lr=w_lr, w_gu=w_gu, b_gate=b_gate[None, :],
                g_qk=g_qk, g_gla=jnp.tile(gla_norm_g, N_GLA)[None, :],
                w_out=_prep_w_out(w_out), sink_rows=sink_rows)


def kernel(x_prompt, x_sample, cache_k, cache_v, state_gla, norm_g, w_in, w_gate_up, b_gate,
           q_norm_g, k_norm_g, sinks, gla_norm_g, w_out):
    depth = norm_g.shape[0]
    assert depth == 1
    B, S, _ = x_prompt.shape
    DB, T, _ = x_sample.shape
    n_cache = cache_k.shape[2]
    assert n_cache == WINDOW
    p = _prep(norm_g[0], w_in[0], w_gate_up[0], b_gate[0], q_norm_g[0], k_norm_g[0], sinks[0],
              gla_norm_g[0], w_out[0])

    xp = x_prompt.reshape(B * S, D_MODEL)
    q, k, v, ga, qg, kg, vg, gg, la_hi, la_lo = _inproj(xp, p, tm=INPROJ_ROWS)
    y_p, win_k_p, win_v_p, gla_p = _layer_prompt(
        xp, q, k, v, ga, p["sink_rows"](CHUNK), qg, kg, vg, gg, la_hi, la_lo, _tril2(CHUNK), p["w_out"],
        B, S, tq=LAYER_ROWS, group=LAYER_GROUP)
    y_p = y_p.reshape(B, S, D_MODEL)

    xs = x_sample.reshape(DB * T, D_MODEL)
    q, k, v, ga, qg, kg, vg, gg, la_hi, la_lo = _inproj(xs, p, tm=DB * T)
    ck = cache_k[0].reshape(DB, n_cache, KV_WIDTH)
    cv = cache_v[0].reshape(DB, n_cache, KV_WIDTH)
    mix_a, win_k_s, win_v_s = _attn_sample(q, k, v, ck, cv, ga, p["sink_rows"](T), DB, T,
                                           bb=SAMPLE_STREAMS)
    blk = min(CHUNK, T)
    mix_g, gla_s = _gla(qg, kg, vg, gg, la_hi, la_lo, state_gla[0], _tril2(blk), DB, T,
                        blk=blk, streams=SAMPLE_STREAMS)
    y_s = _outproj(xs, mix_a, mix_g, p["w_out"], tm=DB * T).reshape(DB, T, D_MODEL)

    shape5 = lambda a: a.reshape(1, a.shape[0], n_cache, N_KV, HEAD_DIM)
    return (y_p, y_s, shape5(win_k_p), shape5(win_v_p), gla_p[None],
            shape5(win_k_s), shape5(win_v_s), gla_s[None])
```

```python
import functools

import jax
import jax.numpy as jnp
import numpy as np
from jax import lax
from jax.experimental import pallas as pl
from jax.experimental.pallas import tpu as pltpu

D_MODEL = 2048
CHUNK = 64
WINDOW = 128
HEAD_DIM = 64
N_HEADS = 16
N_KV = 4
GQA_REP = N_HEADS // N_KV
ATT_WIDTH = N_HEADS * HEAD_DIM
KV_WIDTH = N_KV * HEAD_DIM
N_GLA = 4
GLA_DK = 128
GLA_DV = 256
GLA_KW = N_GLA * GLA_DK
GLA_WIDTH = N_GLA * GLA_DV
GATE_RANK = 16
GATE_TAU = 16.0
EPS = 1e-6
ATT_SCALE = HEAD_DIM ** -0.5
LOG2E = float(np.log2(np.e))
LOG2_DECAY_LIMIT = 100.0
WIN_CHUNKS = WINDOW // CHUNK
WIN_KEYS = WINDOW + CHUNK

LANES = 128
QK_WIDTH = ATT_WIDTH + KV_WIDTH
OFF_Q, OFF_K, OFF_V = 0, ATT_WIDTH, ATT_WIDTH + KV_WIDTH
OFF_GA = OFF_V + KV_WIDTH
OFF_QG = OFF_GA + ATT_WIDTH
OFF_KG = OFF_QG + GLA_KW
OFF_VG = OFF_KG + GLA_KW
OFF_GG = OFF_VG + GLA_WIDTH
MAIN_COLS = OFF_GG + GLA_WIDTH

VMEM_LIMIT = 56 * 1024 * 1024
INPROJ_ROWS = 512
INPROJ_SUB = 256
LAYER_ROWS = 512
LAYER_GROUP = 4
SAMPLE_STREAMS = 4
PREP_UNITS = 8

BF16 = jnp.bfloat16
F32 = jnp.float32


def _dot(a, b):
    return jnp.dot(a, b, preferred_element_type=F32)


def _dot_t(a, b):
    return lax.dot_general(a, b, (((1,), (1,)), ((), ())), preferred_element_type=F32)


def _dot_ta(a, b):
    return lax.dot_general(a, b, (((0,), (0,)), ((), ())), preferred_element_type=F32)


def _silu(x):
    return x / (1.0 + jnp.exp(-x))


def _const_spec(shape):
    nd = len(shape)
    return pl.BlockSpec(shape, lambda *_: (0,) * nd, pipeline_mode=pl.Buffered(1))


def _inproj_kernel(x_ref, ng_ref, w_ref, wlr_ref, wgu_ref, bg_ref, gqk_ref, gng_ref,
                   q_ref, k_ref, v_ref, ga_ref, qg_ref, kg_ref, vg_ref, gg_ref, lah_ref, lal_ref,
                   *, sub):
    lo = lax.broadcasted_iota(jnp.int32, (1, LANES), 1) < HEAD_DIM
    for s in range(x_ref.shape[0] // sub):
        rows = slice(s * sub, (s + 1) * sub)
        x = x_ref[rows, :]
        ms = jnp.mean(x * x, axis=-1, keepdims=True)
        xn = (x * lax.rsqrt(ms + EPS) * ng_ref[...]).astype(BF16)

        qk = _dot(xn, w_ref[:, OFF_Q:OFF_V])
        ga_ref[rows, :] = _silu(_dot(xn, w_ref[:, OFF_GA:OFF_QG])).astype(BF16)
        lr = _dot(xn, wlr_ref[...]).astype(BF16)
        qg_ref[rows, :] = (_dot(xn, w_ref[:, OFF_QG:OFF_KG]) * (GLA_DK ** -0.5)).astype(BF16)
        kg_ref[rows, :] = _dot(xn, w_ref[:, OFF_KG:OFF_VG]).astype(BF16)
        z = _dot(lr, wgu_ref[...]) + bg_ref[...]
        gg_ref[rows, :] = (_silu(_dot(xn, w_ref[:, OFF_GG:MAIN_COLS])) * gng_ref[...]).astype(BF16)
        vg_ref[rows, :] = _dot(xn, w_ref[:, OFF_VG:OFF_GG]).astype(BF16)
        v_ref[rows, :] = _dot(xn, w_ref[:, OFF_V:OFF_GA])

        for n in range(QK_WIDTH // LANES):
            cols = slice(n * LANES, (n + 1) * LANES)
            blk = qk[:, cols]
            sq = blk * blk
            ms_lo = jnp.sum(jnp.where(lo, sq, 0.0), axis=-1, keepdims=True) * (1.0 / HEAD_DIM)
            ms_hi = jnp.sum(jnp.where(lo, 0.0, sq), axis=-1, keepdims=True) * (1.0 / HEAD_DIM)
            scale = jnp.where(lo, lax.rsqrt(ms_lo + EPS), lax.rsqrt(ms_hi + EPS))
            normed = blk * scale * gqk_ref[:, cols]
            if n < ATT_WIDTH // LANES:
                q_ref[rows, cols] = normed.astype(BF16)
            else:
                k_ref[rows, n * LANES - ATT_WIDTH:(n + 1) * LANES - ATT_WIDTH] = normed
        log_sig = jnp.minimum(z, 0.0) - jnp.log(1.0 + jnp.exp(-jnp.abs(z)))
        la = log_sig * (LOG2E / GATE_TAU)
        la_hi = la.astype(BF16)
        lah_ref[rows, :] = la_hi
        lal_ref[rows, :] = (la - la_hi.astype(F32)).astype(BF16)


def _inproj(x2d, p, tm):
    m = x2d.shape[0]
    row = lambda w: pl.BlockSpec((tm, w), lambda i: (i, 0))
    outs = [(ATT_WIDTH, BF16), (KV_WIDTH, F32), (KV_WIDTH, F32), (ATT_WIDTH, BF16),
            (GLA_KW, BF16), (GLA_KW, BF16), (GLA_WIDTH, BF16), (GLA_WIDTH, BF16),
            (GLA_KW, BF16), (GLA_KW, BF16)]
    consts = [p["norm_g"], p["w_main"], p["w_lr"], p["w_gu"], p["b_gate"], p["g_qk"], p["g_gla"]]
    return pl.pallas_call(
        functools.partial(_inproj_kernel, sub=min(tm, INPROJ_SUB)),
        grid=(m // tm,),
        in_specs=[row(D_MODEL)] + [_const_spec(c.shape) for c in consts],
        out_specs=[row(w) for w, _ in outs],
        out_shape=[jax.ShapeDtypeStruct((m, w), dt) for w, dt in outs],
        compiler_params=pltpu.CompilerParams(
            dimension_semantics=("arbitrary",), vmem_limit_bytes=VMEM_LIMIT),
        name="inproj",
    )(x2d, *consts)


def _layer_prompt_kernel(q_ref, kp_ref, kc_ref, vp_ref, vc_ref, ga_ref, sink_ref,
                         qg_ref, kg_ref, vg_ref, gg_ref, lah_ref, lal_ref, tril_ref, x_ref, w_ref,
                         y_ref, wk_ref, wv_ref, sout_ref,
                         kw_ref, vw_ref, st_ref, b_ref, qx_ref, mix_ref, *, tq, group, nblk, n_steps):
    n_chunks = tq // CHUNK
    assert n_chunks % group == 0 and group >= WIN_CHUNKS and tq >= WINDOW
    t = pl.program_id(0)
    i = lax.rem(jnp.minimum(t, n_steps - 1), nblk)
    last = nblk - 1
    cur = i % 2
    nxt = 1 - cur
    mix_slot = t % 2
    prev_slot = 1 - mix_slot

    out_rows, out_cols = 512, 512
    out_pieces = [(slice(r, r + out_rows), slice(c, c + out_cols))
                  for r in range(0, tq, out_rows) for c in range(0, D_MODEL, out_cols)]

    def out_piece(n):
        r, c = out_pieces[n]
        y_ref[r, c] = x_ref[r, c] + _dot(mix_ref[prev_slot, r, :], w_ref[:, c])

    def mixers(do_out):
        kw_ref[0:WINDOW, :] = kp_ref[0].astype(BF16)
        kw_ref[WINDOW:, :] = kc_ref[0].astype(BF16)
        vw_ref[0:WINDOW, :] = vp_ref[0].astype(BF16)
        vw_ref[WINDOW:, :] = vc_ref[0].astype(BF16)

        @pl.when(i == 0)
        def _():
            st_ref[0] = jnp.zeros(st_ref.shape[1:], F32)

        @pl.when(i == last)
        def _():
            wk_ref[0] = kc_ref[0, tq - WINDOW:, :]
            wv_ref[0] = vc_ref[0, tq - WINDOW:, :]

        lane = lax.broadcasted_iota(jnp.int32, (1, LANES), 1)
        lo = lane < HEAD_DIM
        col_group = lax.broadcasted_iota(jnp.int32, (1, KV_WIDTH), 1) // HEAD_DIM
        col = lax.broadcasted_iota(jnp.int32, (1, N_KV * WIN_KEYS), 1)
        key_in_seg = col - WIN_KEYS * sum((col >= g * WIN_KEYS).astype(jnp.int32) for g in range(1, N_KV))
        neg_inf = jnp.float32(-jnp.inf)
        rows = [slice(c * CHUNK, (c + 1) * CHUNK) for c in range(n_chunks)]
        a_s, a_v, a_p, a_den, a_ov = {}, {}, {}, {}, {}

        def block_diag(win):
            zero = jnp.zeros_like(win)
            return jnp.concatenate([jnp.where(col_group == g, win, zero) for g in range(N_KV)], axis=0)

        def attn_scores(c):
            k_bd = block_diag(kw_ref[c * CHUNK:c * CHUNK + WIN_KEYS, :])
            a_v[c] = block_diag(vw_ref[c * CHUNK:c * CHUNK + WIN_KEYS, :])
            qs = jnp.concatenate(
                [q_ref[rows[c], r * KV_WIDTH:(r + 1) * KV_WIDTH] for r in range(GQA_REP)], axis=0)
            a_s[c] = _dot_t(qs, k_bd)

        def attn_softmax(c):
            su = a_s.pop(c)
            if c < WIN_CHUNKS:
                first_valid = jnp.where(i == 0, (WIN_CHUNKS - c) * CHUNK, 0)
                su = jnp.where(key_in_seg >= first_valid, su, neg_inf)
            cs = [su[:, LANES * n:LANES * (n + 1)] for n in range(6)]
            seg_max = [
                jnp.maximum(cs[0], jnp.where(lo, cs[1], neg_inf)),
                jnp.maximum(jnp.where(lo, neg_inf, cs[1]), cs[2]),
                jnp.maximum(cs[3], jnp.where(lo, cs[4], neg_inf)),
                jnp.maximum(jnp.where(lo, neg_inf, cs[4]), cs[5]),
            ]
            mu = [jnp.maximum(jnp.max(seg_max[g], axis=-1, keepdims=True), sink_ref[g])
                  for g in range(N_KV)]
            shift = [mu[0], jnp.where(lo, mu[0], mu[1]), mu[1], mu[2], jnp.where(lo, mu[2], mu[3]), mu[3]]
            e = [jnp.exp2(cs[n] - shift[n]) for n in range(6)]
            a_p[c] = jnp.concatenate(e, axis=1).astype(BF16)
            seg_sum = [e[0] + jnp.where(lo, e[1], 0.0), jnp.where(lo, 0.0, e[1]) + e[2],
                       e[3] + jnp.where(lo, e[4], 0.0), jnp.where(lo, 0.0, e[4]) + e[5]]
            a_den[c] = [jnp.sum(seg_sum[g], axis=-1, keepdims=True) + jnp.exp2(sink_ref[g] - mu[g])
                        for g in range(N_KV)]

        def attn_pv(c):
            a_ov[c] = _dot(a_p.pop(c), a_v.pop(c))

        def attn_finish(c):
            ov, den = a_ov.pop(c), a_den.pop(c)
            den = jnp.concatenate([jnp.where(lo, den[0], den[1]), jnp.where(lo, den[2], den[3])], axis=1)
            o = ov / den
            for r in range(GQA_REP):
                cols = slice(r * KV_WIDTH, (r + 1) * KV_WIDTH)
                gate = ga_ref[rows[c], cols].astype(F32)
                mix_ref[mix_slot, rows[c], cols] = (o[r * CHUNK:(r + 1) * CHUNK, :] * gate).astype(BF16)

        rowi = lax.broadcasted_iota(jnp.int32, (CHUNK, CHUNK), 0)
        coli = lax.broadcasted_iota(jnp.int32, (CHUNK, CHUNK), 1)
        causal = rowi >= coli
        eye = (lax.broadcasted_iota(jnp.int32, (GLA_DK, GLA_DK), 0)
               == lax.broadcasted_iota(jnp.int32, (GLA_DK, GLA_DK), 1))
        heads = range(N_GLA)
        kcol = [slice(h * GLA_DK, (h + 1) * GLA_DK) for h in heads]
        vcol = [slice(h * GLA_DV, (h + 1) * GLA_DV) for h in heads]
        mcol = [slice(ATT_WIDTH + h * GLA_DV, ATT_WIDTH + (h + 1) * GLA_DV) for h in heads]
        g_b, g_qd, g_ki, g_ke, g_dc, g_att, g_dst, g_o = {}, {}, {}, {}, {}, {}, {}, {}
        state = {}

        def cum_log2_decay(r):
            return _dot(tril_ref[...], jnp.concatenate([lah_ref[r, :], lal_ref[r, :]], axis=0))

        def decay_column(blast):
            dec_row = jnp.broadcast_to(jnp.exp2(blast), (GLA_DK, GLA_DK))
            return jnp.sum(jnp.where(eye, dec_row, 0.0), axis=-1, keepdims=True)

        def gla_finish_rows(ov, r, h):
            ms = jnp.mean(ov * ov, axis=-1, keepdims=True)
            gate = gg_ref[r, vcol[h]].astype(F32)
            mix_ref[mix_slot, r, mcol[h]] = (ov * lax.rsqrt(ms + EPS) * gate).astype(BF16)

        def gla_cumsum(c):
            g_b[c] = cum_log2_decay(rows[c])

        def gla_prepare(c):
            for h in heads:
                b = g_b[c][:, kcol[h]]
                blast = b[CHUNK - 1:CHUNK, :]
                q = qg_ref[rows[c], kcol[h]].astype(F32)
                k = kg_ref[rows[c], kcol[h]].astype(F32)
                decay = jnp.exp2(b)
                g_qd[c, h] = (q * decay).astype(BF16)
                g_ki[c, h] = (k * (1.0 / decay)).astype(BF16)
                g_ke[c, h] = (k * jnp.exp2(blast - b)).astype(BF16)
                g_dc[c, h] = decay_column(blast)

        def gla_scores(c):
            for h in heads:
                g_att[c, h] = _dot_t(g_qd[c, h], g_ki.pop((c, h)))
                g_dst[c, h] = _dot_ta(g_ke.pop((c, h)), vg_ref[rows[c], vcol[h]])

        def gla_outputs(c):
            for h in heads:
                att = jnp.where(causal, g_att.pop((c, h)), 0.0).astype(BF16)
                lhs = jnp.concatenate([g_qd.pop((c, h)), att], axis=1)
                rhs = jnp.concatenate([state[h].astype(BF16), vg_ref[rows[c], vcol[h]]], axis=0)
                g_o[c, h] = _dot(lhs, rhs)
                state[h] = state[h] * g_dc.pop((c, h)) + g_dst.pop((c, h))

        def gla_finish(c):
            for h in heads:
                gla_finish_rows(g_o.pop((c, h)), rows[c], h)

        def gla_exact_step():
            def chunk(c, carry):
                r = pl.ds(pl.multiple_of(c * CHUNK, CHUNK), CHUNK)
                bcum = cum_log2_decay(r)
                for h in heads:
                    b = bcum[:, kcol[h]]
                    blast = b[CHUNK - 1:CHUNK, :]
                    q = qg_ref[r, kcol[h]].astype(F32)
                    k = kg_ref[r, kcol[h]].astype(F32)
                    v = vg_ref[r, vcol[h]]
                    b_ref[...] = b
                    qx_ref[...] = q

                    def score_row(tok, att_t):
                        w = jnp.where(rowi[:, 0:1] <= tok, jnp.exp2(b_ref[pl.ds(tok, 1), :] - b), 0.0)
                        colv = jnp.sum(qx_ref[pl.ds(tok, 1), :] * k * w, axis=-1, keepdims=True)
                        return att_t + jnp.where(coli == tok, colv, 0.0)

                    att_t = lax.fori_loop(0, CHUNK, score_row, jnp.zeros((CHUNK, CHUNK), F32))
                    q_dec = (q * jnp.exp2(b)).astype(BF16)
                    k_end = (k * jnp.exp2(blast - b)).astype(BF16)
                    st = st_ref[nxt, h]
                    ov = _dot_ta(att_t.astype(BF16), v) + _dot(q_dec, st.astype(BF16))
                    st_ref[nxt, h] = st * decay_column(blast) + _dot_ta(k_end, v)
                    gla_finish_rows(ov, r, h)
                return carry

            st_ref[nxt] = st_ref[cur]
            lax.fori_loop(0, n_chunks, chunk, 0)

        pending = list(range(len(out_pieces))) if do_out else []
        n_slots = 4 * (n_chunks // group)
        slots_seen = [0]

        def out_slot():
            slots_seen[0] += 1
            due = len(out_pieces) * slots_seen[0] // n_slots - (len(out_pieces) - len(pending))
            for _ in range(min(due, len(pending))):
                out_piece(pending.pop(0))

        for h in heads:
            state[h] = st_ref[cur, h]
        for g0 in range(0, n_chunks, group):
            cs = list(range(g0, g0 + group))
            half = group // 2
            for c in cs:
                gla_cumsum(c)
            out_slot()
            for c in cs[:half]:
                attn_scores(c)
            out_slot()
            for c in cs:
                gla_prepare(c)
            for c in cs:
                gla_scores(c)
            out_slot()
            for c in cs[:half]:
                attn_softmax(c)
            for c in cs[:half]:
                attn_pv(c)
            for c in cs[half:]:
                attn_scores(c)
            out_slot()
            for c in cs:
                gla_outputs(c)
            for c in cs[half:]:
                attn_softmax(c)
            for c in cs[half:]:
                attn_pv(c)
            for c in cs:
                gla_finish(c)
            for c in cs:
                attn_finish(c)
        assert not pending
        for h in heads:
            st_ref[nxt, h] = state[h]
        total = functools.reduce(jnp.minimum, [g_b[c][CHUNK - 1:CHUNK, :] for c in range(n_chunks)])

        @pl.when(jnp.min(total) < -LOG2_DECAY_LIMIT)
        def _():
            gla_exact_step()

        @pl.when(i == last)
        def _():
            sout_ref[0] = st_ref[nxt]

    @pl.when(t == 0)
    def _():
        mixers(do_out=False)

    @pl.when((t > 0) & (t < n_steps))
    def _():
        mixers(do_out=True)

    @pl.when(t == n_steps)
    def _():
        for n in range(len(out_pieces)):
            out_piece(n)


def _layer_prompt(x2d, q, k, v, ga, sink_rows, qg, kg, vg, gg, la_hi, la_lo, tril2, w_out, batch, seq,
                  tq, group):
    nblk = seq // tq
    n_steps = batch * nblk
    per_win = tq // WINDOW
    blk_of = lambda t: jnp.minimum(t, n_steps - 1)
    rows = lambda w: pl.BlockSpec((tq, w), lambda t: (blk_of(t), 0))
    late = lambda w: pl.BlockSpec((tq, w), lambda t: (jnp.maximum(t - 1, 0), 0))
    cur = pl.BlockSpec((1, tq, KV_WIDTH), lambda t: (blk_of(t) // nblk, blk_of(t) % nblk, 0))
    prev = pl.BlockSpec((1, WINDOW, KV_WIDTH),
                        lambda t: (blk_of(t) // nblk, jnp.maximum((blk_of(t) % nblk) * per_win - 1, 0), 0))
    win = pl.BlockSpec((1, WINDOW, KV_WIDTH), lambda t: (blk_of(t) // nblk, 0, 0))
    state = pl.BlockSpec((1, N_GLA, GLA_DK, GLA_DV), lambda t: (blk_of(t) // nblk, 0, 0, 0))
    k3 = k.reshape(batch, seq, KV_WIDTH)
    v3 = v.reshape(batch, seq, KV_WIDTH)
    return pl.pallas_call(
        functools.partial(_layer_prompt_kernel, tq=tq, group=group, nblk=nblk, n_steps=n_steps),
        grid=(n_steps + 1,),
        in_specs=[rows(ATT_WIDTH), prev, cur, prev, cur, rows(ATT_WIDTH), _const_spec(sink_rows.shape),
                  rows(GLA_KW), rows(GLA_KW), rows(GLA_WIDTH), rows(GLA_WIDTH), rows(GLA_KW),
                  rows(GLA_KW), _const_spec(tril2.shape), late(D_MODEL), _const_spec(w_out.shape)],
        out_specs=[late(D_MODEL), win, win, state],
        out_shape=[jax.ShapeDtypeStruct((batch * seq, D_MODEL), F32),
                   jax.ShapeDtypeStruct((batch, WINDOW, KV_WIDTH), F32),
                   jax.ShapeDtypeStruct((batch, WINDOW, KV_WIDTH), F32),
                   jax.ShapeDtypeStruct((batch, N_GLA, GLA_DK, GLA_DV), F32)],
        scratch_shapes=[pltpu.VMEM((WINDOW + tq, KV_WIDTH), BF16),
                        pltpu.VMEM((WINDOW + tq, KV_WIDTH), BF16),
                        pltpu.VMEM((2, N_GLA, GLA_DK, GLA_DV), F32),
                        pltpu.VMEM((CHUNK, GLA_DK), F32),
                        pltpu.VMEM((CHUNK, GLA_DK), F32),
                        pltpu.VMEM((2, tq, D_MODEL), BF16)],
        compiler_params=pltpu.CompilerParams(
            dimension_semantics=("arbitrary",), vmem_limit_bytes=VMEM_LIMIT),
        name="layer_prompt",
    )(q, k3, k3, v3, v3, ga, sink_rows, qg, kg, vg, gg, la_hi, la_lo, tril2, x2d, w_out)


def _attn_sample_kernel(q_ref, kn_ref, vn_ref, ck_ref, cv_ref, ga_ref, sink_ref,
                        o_ref, wk_ref, wv_ref, *, t, bb):
    n_cache = ck_ref.shape[1]
    assert n_cache == LANES and t <= LANES
    lane = lax.broadcasted_iota(jnp.int32, (1, LANES), 1)
    lo = lane < HEAD_DIM
    is_new = lane < t
    col_group = lax.broadcasted_iota(jnp.int32, (1, KV_WIDTH), 1) // HEAD_DIM
    seg = 2 * LANES
    ones_bd = jnp.concatenate(
        [jnp.broadcast_to(jnp.where(col_group == g, 1.0, 0.0).astype(BF16), (seg, KV_WIDTH))
         for g in range(N_KV)], axis=0)
    pad = jnp.zeros((LANES - t, KV_WIDTH), BF16)
    neg_inf = jnp.float32(-jnp.inf)

    s, v_all = [], []
    for b in range(bb):
        rows = slice(b * t, (b + 1) * t)
        kn, vn, ck, cv = kn_ref[rows, :], vn_ref[rows, :], ck_ref[b], cv_ref[b]
        wk_ref[b] = jnp.concatenate([ck[t:, :], kn], axis=0)
        wv_ref[b] = jnp.concatenate([cv[t:, :], vn], axis=0)
        kcat = jnp.concatenate([ck.astype(BF16), kn.astype(BF16), pad], axis=0)
        vcat = jnp.concatenate([cv.astype(BF16), vn.astype(BF16), pad], axis=0)
        zero = jnp.zeros_like(kcat)
        k_bd = jnp.concatenate([jnp.where(col_group == g, kcat, zero) for g in range(N_KV)], axis=0)
        v_bd = jnp.concatenate([jnp.where(col_group == g, vcat, zero) for g in range(N_KV)], axis=0)
        v_all.append(jnp.concatenate([v_bd, ones_bd], axis=1))
        qs = jnp.concatenate(
            [q_ref[rows, r * KV_WIDTH:(r + 1) * KV_WIDTH] for r in range(GQA_REP)], axis=0)
        s.append(_dot_t(qs, k_bd))
    p, m = [], []
    for b in range(bb):
        c = [s[b][:, LANES * n:LANES * (n + 1)] for n in range(2 * N_KV)]
        c = [cn if n % 2 == 0 else jnp.where(is_new, cn, neg_inf) for n, cn in enumerate(c)]
        mb = [jnp.maximum(jnp.max(jnp.maximum(c[2 * g], c[2 * g + 1]), axis=-1, keepdims=True),
                          sink_ref[g]) for g in range(N_KV)]
        p.append(jnp.concatenate([jnp.exp2(c[n] - mb[n // 2]) for n in range(2 * N_KV)],
                                 axis=1).astype(BF16))
        m.append(mb)
    ov = [_dot(p[b], v_all[b]) for b in range(bb)]
    for b in range(bb):
        rows = slice(b * t, (b + 1) * t)
        es = [jnp.exp2(sink_ref[g] - m[b][g]) for g in range(N_KV)]
        e_sink = jnp.concatenate([jnp.where(lo, es[0], es[1]), jnp.where(lo, es[2], es[3])], axis=1)
        o = ov[b][:, :KV_WIDTH] / (ov[b][:, KV_WIDTH:] + e_sink)
        for r in range(GQA_REP):
            cols = slice(r * KV_WIDTH, (r + 1) * KV_WIDTH)
            gate = ga_ref[rows, cols].astype(F32)
            o_ref[rows, cols] = (o[r * t:(r + 1) * t, :] * gate).astype(BF16)


def _attn_sample(q, k, v, cache_k, cache_v, ga, sink_rows, batch, t, bb):
    n_cache = cache_k.shape[1]
    rows = lambda w: pl.BlockSpec((bb * t, w), lambda b: (b, 0))
    cache = pl.BlockSpec((bb, n_cache, KV_WIDTH), lambda b: (b, 0, 0))
    return pl.pallas_call(
        functools.partial(_attn_sample_kernel, t=t, bb=bb),
        grid=(batch // bb,),
        in_specs=[rows(ATT_WIDTH), rows(KV_WIDTH), rows(KV_WIDTH), cache, cache, rows(ATT_WIDTH),
                  _const_spec(sink_rows.shape)],
        out_specs=[rows(ATT_WIDTH), cache, cache],
        out_shape=[jax.ShapeDtypeStruct((batch * t, ATT_WIDTH), BF16),
                   jax.ShapeDtypeStruct((batch, n_cache, KV_WIDTH), F32),
                   jax.ShapeDtypeStruct((batch, n_cache, KV_WIDTH), F32)],
        compiler_params=pltpu.CompilerParams(
            dimension_semantics=("arbitrary",), vmem_limit_bytes=VMEM_LIMIT),
        name="attn_sample",
    )(q, k, v, cache_k, cache_v, ga, sink_rows)


def _gla_kernel(qg_ref, kg_ref, vg_ref, gg_ref, lah_ref, lal_ref, s0_ref, tril_ref,
                o_ref, sout_ref, b_ref, q_ref, *, blk, n_chunks, streams):
    rowi = lax.broadcasted_iota(jnp.int32, (blk, blk), 0)
    coli = lax.broadcasted_iota(jnp.int32, (blk, blk), 1)
    causal = rowi >= coli
    eye = (lax.broadcasted_iota(jnp.int32, (GLA_DK, GLA_DK), 0)
           == lax.broadcasted_iota(jnp.int32, (GLA_DK, GLA_DK), 1))

    heads = range(N_GLA)
    kcol = [slice(h * GLA_DK, (h + 1) * GLA_DK) for h in heads]
    vcol = [slice(h * GLA_DV, (h + 1) * GLA_DV) for h in heads]

    def cum_log2_decay(r):
        return _dot(tril_ref[...], jnp.concatenate([lah_ref[r, :], lal_ref[r, :]], axis=0))

    def decay_column(blast):
        dec_row = jnp.broadcast_to(jnp.exp2(blast), (GLA_DK, GLA_DK))
        return jnp.sum(jnp.where(eye, dec_row, 0.0), axis=-1, keepdims=True)

    def finish(ov, r, h):
        ms = jnp.mean(ov * ov, axis=-1, keepdims=True)
        gate = gg_ref[r, vcol[h]].astype(F32)
        o_ref[r, vcol[h]] = (ov * lax.rsqrt(ms + EPS) * gate).astype(BF16)

    units = [(s, u) for s in range(streams) for u in range(n_chunks)]
    rows = {(s, u): slice((s * n_chunks + u) * blk, (s * n_chunks + u + 1) * blk) for s, u in units}

    def exact_step():
        for s in range(streams):
            st = [s0_ref[s, h] for h in heads]
            for u in range(n_chunks):
                r = rows[s, u]
                bcum = cum_log2_decay(r)
                for h in heads:
                    b = bcum[:, kcol[h]]
                    blast = b[blk - 1:blk, :]
                    q = qg_ref[r, kcol[h]].astype(F32)
                    k = kg_ref[r, kcol[h]].astype(F32)
                    v = vg_ref[r, vcol[h]]
                    b_ref[...] = b
                    q_ref[...] = q

                    def score_row(t, att_t):
                        w = jnp.where(rowi[:, 0:1] <= t, jnp.exp2(b_ref[pl.ds(t, 1), :] - b), 0.0)
                        col = jnp.sum(q_ref[pl.ds(t, 1), :] * k * w, axis=-1, keepdims=True)
                        return att_t + jnp.where(coli == t, col, 0.0)

                    att_t = lax.fori_loop(0, blk, score_row, jnp.zeros((blk, blk), F32))
                    q_dec = (q * jnp.exp2(b)).astype(BF16)
                    k_end = (k * jnp.exp2(blast - b)).astype(BF16)
                    ov = _dot_ta(att_t.astype(BF16), v) + _dot(q_dec, st[h].astype(BF16))
                    st[h] = st[h] * decay_column(blast) + _dot_ta(k_end, v)
                    finish(ov, r, h)
            for h in heads:
                sout_ref[s, h] = st[h]

    bcum = {unit: cum_log2_decay(rows[unit]) for unit in units}
    q_dec, k_inv, k_end, dec_col = {}, {}, {}, {}
    for unit in units:
        for h in heads:
            b = bcum[unit][:, kcol[h]]
            blast = b[blk - 1:blk, :]
            q = qg_ref[rows[unit], kcol[h]].astype(F32)
            k = kg_ref[rows[unit], kcol[h]].astype(F32)
            decay = jnp.exp2(b)
            q_dec[unit, h] = (q * decay).astype(BF16)
            k_inv[unit, h] = (k * (1.0 / decay)).astype(BF16)
            k_end[unit, h] = (k * jnp.exp2(blast - b)).astype(BF16)
            dec_col[unit, h] = decay_column(blast)
    att = {(unit, h): _dot_t(q_dec[unit, h], k_inv[unit, h]) for unit in units for h in heads}
    d_st = {(unit, h): _dot_ta(k_end[unit, h], vg_ref[rows[unit], vcol[h]])
            for unit in units for h in heads}
    att = {key: jnp.where(causal, a, 0.0).astype(BF16) for key, a in att.items()}
    st = {(s, h): s0_ref[s, h] for s in range(streams) for h in heads}
    o = {}
    for s, u in units:
        for h in heads:
            o[s, u, h] = (_dot(att[(s, u), h], vg_ref[rows[s, u], vcol[h]])
                          + _dot(q_dec[(s, u), h], st[s, h].astype(BF16)))
            st[s, h] = st[s, h] * dec_col[(s, u), h] + d_st[(s, u), h]
    for (s, h), value in st.items():
        sout_ref[s, h] = value
    for s, u in units:
        for h in heads:
            finish(o[s, u, h], rows[s, u], h)

    total = functools.reduce(jnp.minimum, [bc[blk - 1:blk, :] for bc in bcum.values()])

    @pl.when(jnp.min(total) < -LOG2_DECAY_LIMIT)
    def _():
        exact_step()


def _gla(qg, kg, vg, gg, la_hi, la_lo, s0, tril2, batch, seq, blk, streams):
    rows = lambda w: pl.BlockSpec((streams * seq, w), lambda g: (g, 0))
    state = pl.BlockSpec((streams, N_GLA, GLA_DK, GLA_DV), lambda g: (g, 0, 0, 0))
    return pl.pallas_call(
        functools.partial(_gla_kernel, blk=blk, n_chunks=seq // blk, streams=streams),
        grid=(batch // streams,),
        in_specs=[rows(GLA_KW), rows(GLA_KW), rows(GLA_WIDTH), rows(GLA_WIDTH), rows(GLA_KW),
                  rows(GLA_KW), state, _const_spec(tril2.shape)],
        out_specs=[rows(GLA_WIDTH), state],
        out_shape=[jax.ShapeDtypeStruct((batch * seq, GLA_WIDTH), BF16),
                   jax.ShapeDtypeStruct((batch, N_GLA, GLA_DK, GLA_DV), F32)],
        scratch_shapes=[pltpu.VMEM((blk, GLA_DK), F32),
                        pltpu.VMEM((blk, GLA_DK), F32)],
        compiler_params=pltpu.CompilerParams(
            dimension_semantics=("arbitrary",), vmem_limit_bytes=VMEM_LIMIT),
        name="gla",
    )(qg, kg, vg, gg, la_hi, la_lo, s0, tril2)


def _outproj_kernel(x_ref, ma_ref, mg_ref, w_ref, y_ref):
    y_ref[...] = (x_ref[...] + _dot(ma_ref[...], w_ref[:ATT_WIDTH, :])
                  + _dot(mg_ref[...], w_ref[ATT_WIDTH:, :]))


def _outproj(x2d, mix_a, mix_g, w, tm):
    m = x2d.shape[0]
    row = lambda w: pl.BlockSpec((tm, w), lambda i: (i, 0))
    return pl.pallas_call(
        _outproj_kernel,
        grid=(m // tm,),
        in_specs=[row(D_MODEL), row(ATT_WIDTH), row(GLA_WIDTH), _const_spec(w.shape)],
        out_specs=row(D_MODEL),
        out_shape=jax.ShapeDtypeStruct((m, D_MODEL), F32),
        compiler_params=pltpu.CompilerParams(
            dimension_semantics=("arbitrary",), vmem_limit_bytes=VMEM_LIMIT),
        name="outproj",
    )(x2d, mix_a, mix_g, w)


def _regroup_unit(u):
    return (u % N_KV) * GQA_REP + u // N_KV


def _prep_w_in_kernel(*refs):
    *unit_refs, lr_ref, main_ref, wlr_ref = refs
    units = jnp.concatenate([u[...] for u in unit_refs], axis=0)
    main_ref[...] = units.T.astype(BF16)

    @pl.when(pl.program_id(0) == 0)
    def _():
        lr = lr_ref[...]
        padded = jnp.concatenate([lr, jnp.zeros((LANES - lr.shape[0], lr.shape[1]), F32)], axis=0)
        wlr_ref[...] = padded.T.astype(BF16)


def _prep_w_in(w_in_t):
    _, k = w_in_t.shape
    units_per_block = PREP_UNITS
    block_cols = units_per_block * HEAD_DIM
    n_att = ATT_WIDTH // HEAD_DIM
    assert MAIN_COLS % block_cols == 0

    def src_unit(j, p):
        u = j * units_per_block + p
        q0, ga0 = OFF_Q // HEAD_DIM, OFF_GA // HEAD_DIM
        u = jnp.where((u >= q0) & (u < q0 + n_att), q0 + _regroup_unit(u - q0), u)
        return jnp.where((u >= ga0) & (u < ga0 + n_att), ga0 + _regroup_unit(u - ga0), u)

    unit = lambda p: pl.BlockSpec((HEAD_DIM, k), lambda j: (src_unit(j, p), 0))
    return pl.pallas_call(
        _prep_w_in_kernel,
        grid=(MAIN_COLS // block_cols,),
        in_specs=[unit(p) for p in range(units_per_block)]
        + [pl.BlockSpec((GATE_RANK, k), lambda j: (MAIN_COLS // GATE_RANK, 0))],
        out_specs=[pl.BlockSpec((k, block_cols), lambda j: (0, j)),
                   pl.BlockSpec((k, LANES), lambda j: (0, 0))],
        out_shape=[jax.ShapeDtypeStruct((k, MAIN_COLS), BF16),
                   jax.ShapeDtypeStruct((k, LANES), BF16)],
        compiler_params=pltpu.CompilerParams(
            dimension_semantics=("arbitrary",), vmem_limit_bytes=VMEM_LIMIT),
        name="prep_w_in",
    )(*([w_in_t] * (units_per_block + 1)))


def _prep_w_out_kernel(*refs):
    *unit_refs, o_ref = refs
    o_ref[...] = jnp.concatenate([u[...] for u in unit_refs], axis=0).astype(BF16)


def _prep_w_out(w_out):
    rows, cols = w_out.shape
    units_per_block = PREP_UNITS
    block_rows = units_per_block * HEAD_DIM
    n_att = ATT_WIDTH // HEAD_DIM
    assert rows % block_rows == 0

    def src_unit(j, p):
        u = j * units_per_block + p
        return jnp.where(u < n_att, _regroup_unit(u), u)

    unit = lambda p: pl.BlockSpec((HEAD_DIM, cols), lambda j: (src_unit(j, p), 0))
    return pl.pallas_call(
        _prep_w_out_kernel,
        grid=(rows // block_rows,),
        in_specs=[unit(p) for p in range(units_per_block)],
        out_specs=pl.BlockSpec((block_rows, cols), lambda j: (j, 0)),
        out_shape=jax.ShapeDtypeStruct((rows, cols), BF16),
        compiler_params=pltpu.CompilerParams(
            dimension_semantics=("arbitrary",), vmem_limit_bytes=VMEM_LIMIT),
        name="prep_w_out",
    )(*([w_out] * units_per_block))


def _tril2(blk):
    t = np.tril(np.ones((blk, blk), np.float32))
    return jnp.asarray(np.concatenate([t, t], axis=1), BF16)


def _prep(norm_g, w_in, w_gate_up, b_gate, q_norm_g, k_norm_g, sinks, gla_norm_g, w_out):
    w_main, w_lr = _prep_w_in(w_in.T)
    w_gu = jnp.pad(w_gate_up, ((0, LANES - GATE_RANK), (0, 0))).astype(BF16)
    g_qk = jnp.concatenate([jnp.tile(q_norm_g, N_HEADS) * (ATT_SCALE * LOG2E),
                            jnp.tile(k_norm_g, N_KV)])[None, :]

    def sink_rows(t):
        sk = sinks.astype(F32).reshape(N_KV, GQA_REP) * LOG2E
        return jnp.broadcast_to(sk[:, :, None, None], (N_KV, GQA_REP, t, LANES)).reshape(
            N_KV, GQA_REP * t, LANES)

    return dict(norm_g=norm_g[None, :], w_main=w_main, w_lr=w_lr, w_gu=w_gu, b_gate=b_gate[None, :],
                g_qk=g_qk, g_gla=jnp.tile(gla_norm_g, N_GLA)[None, :],
                w_out=_prep_w_out(w_out), sink_rows=sink_rows)


def kernel(x_prompt, x_sample, cache_k, cache_v, state_gla, norm_g, w_in, w_gate_up, b_gate,
           q_norm_g, k_norm_g, sinks, gla_norm_g, w_out):
    depth = norm_g.shape[0]
    assert depth == 1
    B, S, _ = x_prompt.shape
    DB, T, _ = x_sample.shape
    n_cache = cache_k.shape[2]
    assert n_cache == WINDOW
    p = _prep(norm_g[0], w_in[0], w_gate_up[0], b_gate[0], q_norm_g[0], k_norm_g[0], sinks[0],
              gla_norm_g[0], w_out[0])

    xp = x_prompt.reshape(B * S, D_MODEL)
    q, k, v, ga, qg, kg, vg, gg, la_hi, la_lo = _inproj(xp, p, tm=INPROJ_ROWS)
    y_p, win_k_p, win_v_p, gla_p = _layer_prompt(
        xp, q, k, v, ga, p["sink_rows"](CHUNK), qg, kg, vg, gg, la_hi, la_lo, _tril2(CHUNK), p["w_out"],
        B, S, tq=LAYER_ROWS, group=LAYER_GROUP)
    y_p = y_p.reshape(B, S, D_MODEL)

    xs = x_sample.reshape(DB * T, D_MODEL)
    q, k, v, ga, qg, kg, vg, gg, la_hi, la_lo = _inproj(xs, p, tm=DB * T)
    ck = cache_k[0].reshape(DB, n_cache, KV_WIDTH)
    cv = cache_v[0].reshape(DB, n_cache, KV_WIDTH)
    mix_a, win_k_s, win_v_s = _attn_sample(q, k, v, ck, cv, ga, p["sink_rows"](T), DB, T,
                                           bb=SAMPLE_STREAMS)
    blk = min(CHUNK, T)
    mix_g, gla_s = _gla(qg, kg, vg, gg, la_hi, la_lo, state_gla[0], _tril2(blk), DB, T,
                        blk=blk, streams=SAMPLE_STREAMS)
    y_s = _outproj(xs, mix_a, mix_g, p["w_out"], tm=DB * T).reshape(DB, T, D_MODEL)

    shape5 = lambda a: a.reshape(1, a.shape[0], n_cache, N_KV, HEAD_DIM)
    return (y_p, y_s, shape5(win_k_p), shape5(win_v_p), gla_p[None],
            shape5(win_k_s), shape5(win_v_s), gla_s[None])
```

```python
import functools

import jax
import jax.numpy as jnp
import numpy as np
from jax import lax
from jax.experimental import pallas as pl
from jax.experimental.pallas import tpu as pltpu

D_MODEL = 2048
CHUNK = 64
WINDOW = 128
HEAD_DIM = 64
N_HEADS = 16
N_KV = 4
GQA_REP = N_HEADS // N_KV
ATT_WIDTH = N_HEADS * HEAD_DIM
KV_WIDTH = N_KV * HEAD_DIM
N_GLA = 4
GLA_DK = 128
GLA_DV = 256
GLA_KW = N_GLA * GLA_DK
GLA_WIDTH = N_GLA * GLA_DV
GATE_RANK = 16
GATE_TAU = 16.0
EPS = 1e-6
ATT_SCALE = HEAD_DIM ** -0.5
LOG2E = float(np.log2(np.e))
LOG2_DECAY_LIMIT = 100.0
WIN_CHUNKS = WINDOW // CHUNK
WIN_KEYS = WINDOW + CHUNK

LANES = 128
QK_WIDTH = ATT_WIDTH + KV_WIDTH
OFF_Q, OFF_K, OFF_V = 0, ATT_WIDTH, ATT_WIDTH + KV_WIDTH
OFF_GA = OFF_V + KV_WIDTH
OFF_QG = OFF_GA + ATT_WIDTH
OFF_KG = OFF_QG + GLA_KW
OFF_VG = OFF_KG + GLA_KW
OFF_GG = OFF_VG + GLA_WIDTH
MAIN_COLS = OFF_GG + GLA_WIDTH

VMEM_LIMIT = 56 * 1024 * 1024
INPROJ_ROWS = 512
INPROJ_SUB = 256
LAYER_ROWS = 512
LAYER_GROUP = 4
OUT_PIECE_COLS = 512
SAMPLE_STREAMS = 4
PREP_UNITS = 8

BF16 = jnp.bfloat16
F32 = jnp.float32


def _dot(a, b):
    return jnp.dot(a, b, preferred_element_type=F32)


def _dot_t(a, b):
    return lax.dot_general(a, b, (((1,), (1,)), ((), ())), preferred_element_type=F32)


def _dot_ta(a, b):
    return lax.dot_general(a, b, (((0,), (0,)), ((), ())), preferred_element_type=F32)


def _silu(x):
    return x / (1.0 + jnp.exp(-x))


def _const_spec(shape):
    nd = len(shape)
    return pl.BlockSpec(shape, lambda *_: (0,) * nd, pipeline_mode=pl.Buffered(1))


def _inproj_kernel(x_ref, ng_ref, w_ref, wlr_ref, wgu_ref, bg_ref, gqk_ref, gng_ref,
                   q_ref, k_ref, v_ref, ga_ref, qg_ref, kg_ref, vg_ref, gg_ref, lah_ref, lal_ref,
                   *, sub):
    lo = lax.broadcasted_iota(jnp.int32, (1, LANES), 1) < HEAD_DIM
    for s in range(x_ref.shape[0] // sub):
        rows = slice(s * sub, (s + 1) * sub)
        x = x_ref[rows, :]
        ms = jnp.mean(x * x, axis=-1, keepdims=True)
        xn = (x * lax.rsqrt(ms + EPS) * ng_ref[...]).astype(BF16)

        qk = _dot(xn, w_ref[:, OFF_Q:OFF_V])
        ga_ref[rows, :] = _silu(_dot(xn, w_ref[:, OFF_GA:OFF_QG])).astype(BF16)
        lr = _dot(xn, wlr_ref[...]).astype(BF16)
        qg_ref[rows, :] = (_dot(xn, w_ref[:, OFF_QG:OFF_KG]) * (GLA_DK ** -0.5)).astype(BF16)
        kg_ref[rows, :] = _dot(xn, w_ref[:, OFF_KG:OFF_VG]).astype(BF16)
        z = _dot(lr, wgu_ref[...]) + bg_ref[...]
        gg_ref[rows, :] = (_silu(_dot(xn, w_ref[:, OFF_GG:MAIN_COLS])) * gng_ref[...]).astype(BF16)
        vg_ref[rows, :] = _dot(xn, w_ref[:, OFF_VG:OFF_GG]).astype(BF16)
        v_ref[rows, :] = _dot(xn, w_ref[:, OFF_V:OFF_GA])

        for n in range(QK_WIDTH // LANES):
            cols = slice(n * LANES, (n + 1) * LANES)
            blk = qk[:, cols]
            sq = blk * blk
            ms_lo = jnp.sum(jnp.where(lo, sq, 0.0), axis=-1, keepdims=True) * (1.0 / HEAD_DIM)
            ms_hi = jnp.sum(jnp.where(lo, 0.0, sq), axis=-1, keepdims=True) * (1.0 / HEAD_DIM)
            scale = jnp.where(lo, lax.rsqrt(ms_lo + EPS), lax.rsqrt(ms_hi + EPS))
            normed = blk * scale * gqk_ref[:, cols]
            if n < ATT_WIDTH // LANES:
                q_ref[rows, cols] = normed.astype(BF16)
            else:
                k_ref[rows, n * LANES - ATT_WIDTH:(n + 1) * LANES - ATT_WIDTH] = normed
        log_sig = jnp.minimum(z, 0.0) - jnp.log(1.0 + jnp.exp(-jnp.abs(z)))
        la = log_sig * (LOG2E / GATE_TAU)
        la_hi = la.astype(BF16)
        lah_ref[rows, :] = la_hi
        lal_ref[rows, :] = (la - la_hi.astype(F32)).astype(BF16)


def _inproj(x2d, p, tm):
    m = x2d.shape[0]
    row = lambda w: pl.BlockSpec((tm, w), lambda i: (i, 0))
    outs = [(ATT_WIDTH, BF16), (KV_WIDTH, F32), (KV_WIDTH, F32), (ATT_WIDTH, BF16),
            (GLA_KW, BF16), (GLA_KW, BF16), (GLA_WIDTH, BF16), (GLA_WIDTH, BF16),
            (GLA_KW, BF16), (GLA_KW, BF16)]
    consts = [p["norm_g"], p["w_main"], p["w_lr"], p["w_gu"], p["b_gate"], p["g_qk"], p["g_gla"]]
    return pl.pallas_call(
        functools.partial(_inproj_kernel, sub=min(tm, INPROJ_SUB)),
        grid=(m // tm,),
        in_specs=[row(D_MODEL)] + [_const_spec(c.shape) for c in consts],
        out_specs=[row(w) for w, _ in outs],
        out_shape=[jax.ShapeDtypeStruct((m, w), dt) for w, dt in outs],
        compiler_params=pltpu.CompilerParams(
            dimension_semantics=("arbitrary",), vmem_limit_bytes=VMEM_LIMIT),
        name="inproj",
    )(x2d, *consts)


def _layer_prompt_kernel(q_ref, kp_ref, kc_ref, vp_ref, vc_ref, ga_ref, sink_ref,
                         qg_ref, kg_ref, vg_ref, gg_ref, lah_ref, lal_ref, tril_ref, x_ref, w_ref,
                         y_ref, wk_ref, wv_ref, sout_ref,
                         kw_ref, vw_ref, st_ref, b_ref, qx_ref, mix_ref, *, tq, group, nblk, n_steps):
    n_chunks = tq // CHUNK
    assert n_chunks % group == 0 and group >= WIN_CHUNKS and tq >= WINDOW
    t = pl.program_id(0)
    i = lax.rem(jnp.minimum(t, n_steps - 1), nblk)
    last = nblk - 1
    cur = i % 2
    nxt = 1 - cur
    mix_slot = t % 2
    prev_slot = 1 - mix_slot

    out_pieces = [(slice(0, tq), slice(c, c + OUT_PIECE_COLS)) for c in range(0, D_MODEL, OUT_PIECE_COLS)]

    def out_piece(n):
        r, c = out_pieces[n]
        y_ref[r, c] = x_ref[r, c] + _dot(mix_ref[prev_slot, r, :], w_ref[:, c])

    def mixers(do_out):
        kw_ref[0:WINDOW, :] = kp_ref[0].astype(BF16)
        kw_ref[WINDOW:, :] = kc_ref[0].astype(BF16)
        vw_ref[0:WINDOW, :] = vp_ref[0].astype(BF16)
        vw_ref[WINDOW:, :] = vc_ref[0].astype(BF16)

        @pl.when(i == 0)
        def _():
            st_ref[0] = jnp.zeros(st_ref.shape[1:], F32)

        @pl.when(i == last)
        def _():
            wk_ref[0] = kc_ref[0, tq - WINDOW:, :]
            wv_ref[0] = vc_ref[0, tq - WINDOW:, :]

        lane = lax.broadcasted_iota(jnp.int32, (1, LANES), 1)
        lo = lane < HEAD_DIM
        col_group = lax.broadcasted_iota(jnp.int32, (1, KV_WIDTH), 1) // HEAD_DIM
        col = lax.broadcasted_iota(jnp.int32, (1, N_KV * WIN_KEYS), 1)
        key_in_seg = col - WIN_KEYS * sum((col >= g * WIN_KEYS).astype(jnp.int32) for g in range(1, N_KV))
        neg_inf = jnp.float32(-jnp.inf)
        rows = [slice(c * CHUNK, (c + 1) * CHUNK) for c in range(n_chunks)]
        a_s, a_v, a_p, a_den, a_ov = {}, {}, {}, {}, {}

        def block_diag(win):
            zero = jnp.zeros_like(win)
            return jnp.concatenate([jnp.where(col_group == g, win, zero) for g in range(N_KV)], axis=0)

        def attn_scores(c):
            k_bd = block_diag(kw_ref[c * CHUNK:c * CHUNK + WIN_KEYS, :])
            a_v[c] = block_diag(vw_ref[c * CHUNK:c * CHUNK + WIN_KEYS, :])
            qs = jnp.concatenate(
                [q_ref[rows[c], r * KV_WIDTH:(r + 1) * KV_WIDTH] for r in range(GQA_REP)], axis=0)
            a_s[c] = _dot_t(qs, k_bd)

        def attn_softmax(c):
            su = a_s.pop(c)
            if c < WIN_CHUNKS:
                first_valid = jnp.where(i == 0, (WIN_CHUNKS - c) * CHUNK, 0)
                su = jnp.where(key_in_seg >= first_valid, su, neg_inf)
            cs = [su[:, LANES * n:LANES * (n + 1)] for n in range(6)]
            seg_max = [
                jnp.maximum(cs[0], jnp.where(lo, cs[1], neg_inf)),
                jnp.maximum(jnp.where(lo, neg_inf, cs[1]), cs[2]),
                jnp.maximum(cs[3], jnp.where(lo, cs[4], neg_inf)),
                jnp.maximum(jnp.where(lo, neg_inf, cs[4]), cs[5]),
            ]
            mu = [jnp.maximum(jnp.max(seg_max[g], axis=-1, keepdims=True), sink_ref[g])
                  for g in range(N_KV)]
            shift = [mu[0], jnp.where(lo, mu[0], mu[1]), mu[1], mu[2], jnp.where(lo, mu[2], mu[3]), mu[3]]
            e = [jnp.exp2(cs[n] - shift[n]) for n in range(6)]
            a_p[c] = jnp.concatenate(e, axis=1).astype(BF16)
            seg_sum = [e[0] + jnp.where(lo, e[1], 0.0), jnp.where(lo, 0.0, e[1]) + e[2],
                       e[3] + jnp.where(lo, e[4], 0.0), jnp.where(lo, 0.0, e[4]) + e[5]]
            a_den[c] = [jnp.sum(seg_sum[g], axis=-1, keepdims=True) + jnp.exp2(sink_ref[g] - mu[g])
                        for g in range(N_KV)]

        def attn_pv(c):
            a_ov[c] = _dot(a_p.pop(c), a_v.pop(c))

        def attn_finish(c):
            ov, den = a_ov.pop(c), a_den.pop(c)
            den = jnp.concatenate([jnp.where(lo, den[0], den[1]), jnp.where(lo, den[2], den[3])], axis=1)
            o = ov / den
            for r in range(GQA_REP):
                cols = slice(r * KV_WIDTH, (r + 1) * KV_WIDTH)
                gate = ga_ref[rows[c], cols].astype(F32)
                mix_ref[mix_slot, rows[c], cols] = (o[r * CHUNK:(r + 1) * CHUNK, :] * gate).astype(BF16)

        rowi = lax.broadcasted_iota(jnp.int32, (CHUNK, CHUNK), 0)
        coli = lax.broadcasted_iota(jnp.int32, (CHUNK, CHUNK), 1)
        causal = rowi >= coli
        eye = (lax.broadcasted_iota(jnp.int32, (GLA_DK, GLA_DK), 0)
               == lax.broadcasted_iota(jnp.int32, (GLA_DK, GLA_DK), 1))
        heads = range(N_GLA)
        kcol = [slice(h * GLA_DK, (h + 1) * GLA_DK) for h in heads]
        vcol = [slice(h * GLA_DV, (h + 1) * GLA_DV) for h in heads]
        mcol = [slice(ATT_WIDTH + h * GLA_DV, ATT_WIDTH + (h + 1) * GLA_DV) for h in heads]
        g_b, g_qd, g_ki, g_ke, g_dc, g_att, g_dst, g_o = {}, {}, {}, {}, {}, {}, {}, {}
        state = {}

        def cum_log2_decay(r):
            return _dot(tril_ref[...], jnp.concatenate([lah_ref[r, :], lal_ref[r, :]], axis=0))

        def decay_column(blast):
            dec_row = jnp.broadcast_to(jnp.exp2(blast), (GLA_DK, GLA_DK))
            return jnp.sum(jnp.where(eye, dec_row, 0.0), axis=-1, keepdims=True)

        def gla_finish_rows(ov, r, h):
            ms = jnp.mean(ov * ov, axis=-1, keepdims=True)
            gate = gg_ref[r, vcol[h]].astype(F32)
            mix_ref[mix_slot, r, mcol[h]] = (ov * lax.rsqrt(ms + EPS) * gate).astype(BF16)

        def gla_cumsum(c):
            g_b[c] = cum_log2_decay(rows[c])

        def gla_prepare(c):
            for h in heads:
                b = g_b[c][:, kcol[h]]
                blast = b[CHUNK - 1:CHUNK, :]
                q = qg_ref[rows[c], kcol[h]].astype(F32)
                k = kg_ref[rows[c], kcol[h]].astype(F32)
                decay = jnp.exp2(b)
                g_qd[c, h] = (q * decay).astype(BF16)
                g_ki[c, h] = (k * (1.0 / decay)).astype(BF16)
                g_ke[c, h] = (k * jnp.exp2(blast - b)).astype(BF16)
                g_dc[c, h] = decay_column(blast)

        def gla_scores(c):
            for h in heads:
                g_att[c, h] = _dot_t(g_qd[c, h], g_ki.pop((c, h)))
                g_dst[c, h] = _dot_ta(g_ke.pop((c, h)), vg_ref[rows[c], vcol[h]])

        def gla_outputs(c):
            for h in heads:
                att = jnp.where(causal, g_att.pop((c, h)), 0.0).astype(BF16)
                lhs = jnp.concatenate([g_qd.pop((c, h)), att], axis=1)
                rhs = jnp.concatenate([state[h].astype(BF16), vg_ref[rows[c], vcol[h]]], axis=0)
                g_o[c, h] = _dot(lhs, rhs)
                state[h] = state[h] * g_dc.pop((c, h)) + g_dst.pop((c, h))

        def gla_finish(c):
            for h in heads:
                gla_finish_rows(g_o.pop((c, h)), rows[c], h)

        def gla_exact_step():
            def chunk(c, carry):
                r = pl.ds(pl.multiple_of(c * CHUNK, CHUNK), CHUNK)
                bcum = cum_log2_decay(r)
                for h in heads:
                    b = bcum[:, kcol[h]]
                    blast = b[CHUNK - 1:CHUNK, :]
                    q = qg_ref[r, kcol[h]].astype(F32)
                    k = kg_ref[r, kcol[h]].astype(F32)
                    v = vg_ref[r, vcol[h]]
                    b_ref[...] = b
                    qx_ref[...] = q

                    def score_row(tok, att_t):
                        w = jnp.where(rowi[:, 0:1] <= tok, jnp.exp2(b_ref[pl.ds(tok, 1), :] - b), 0.0)
                        colv = jnp.sum(qx_ref[pl.ds(tok, 1), :] * k * w, axis=-1, keepdims=True)
                        return att_t + jnp.where(coli == tok, colv, 0.0)

                    att_t = lax.fori_loop(0, CHUNK, score_row, jnp.zeros((CHUNK, CHUNK), F32))
                    q_dec = (q * jnp.exp2(b)).astype(BF16)
                    k_end = (k * jnp.exp2(blast - b)).astype(BF16)
                    st = st_ref[nxt, h]
                    ov = _dot_ta(att_t.astype(BF16), v) + _dot(q_dec, st.astype(BF16))
                    st_ref[nxt, h] = st * decay_column(blast) + _dot_ta(k_end, v)
                    gla_finish_rows(ov, r, h)
                return carry

            st_ref[nxt] = st_ref[cur]
            lax.fori_loop(0, n_chunks, chunk, 0)

        pending = list(range(len(out_pieces))) if do_out else []
        n_slots = 4 * (n_chunks // group)
        slots_seen = [0]

        def out_slot():
            slots_seen[0] += 1
            due = len(out_pieces) * slots_seen[0] // n_slots - (len(out_pieces) - len(pending))
            for _ in range(min(due, len(pending))):
                out_piece(pending.pop(0))

        for h in heads:
            state[h] = st_ref[cur, h]
        for g0 in range(0, n_chunks, group):
            cs = list(range(g0, g0 + group))
            half = group // 2
            for c in cs:
                gla_cumsum(c)
            out_slot()
            for c in cs[:half]:
                attn_scores(c)
            out_slot()
            for c in cs:
                gla_prepare(c)
            for c in cs:
                gla_scores(c)
            out_slot()
            for c in cs[:half]:
                attn_softmax(c)
            for c in cs[:half]:
                attn_pv(c)
            for c in cs[half:]:
                attn_scores(c)
            out_slot()
            for c in cs:
                gla_outputs(c)
            for c in cs[half:]:
                attn_softmax(c)
            for c in cs[half:]:
                attn_pv(c)
            for c in cs:
                gla_finish(c)
            for c in cs:
                attn_finish(c)
        assert not pending
        for h in heads:
            st_ref[nxt, h] = state[h]
        total = functools.reduce(jnp.minimum, [g_b[c][CHUNK - 1:CHUNK, :] for c in range(n_chunks)])

        @pl.when(jnp.min(total) < -LOG2_DECAY_LIMIT)
        def _():
            gla_exact_step()

        @pl.when(i == last)
        def _():
            sout_ref[0] = st_ref[nxt]

    @pl.when(t == 0)
    def _():
        mixers(do_out=False)

    @pl.when((t > 0) & (t < n_steps))
    def _():
        mixers(do_out=True)

    @pl.when(t == n_steps)
    def _():
        for n in range(len(out_pieces)):
            out_piece(n)


def _layer_prompt(x2d, q, k, v, ga, sink_rows, qg, kg, vg, gg, la_hi, la_lo, tril2, w_out, batch, seq,
                  tq, group):
    nblk = seq // tq
    n_steps = batch * nblk
    per_win = tq // WINDOW
    blk_of = lambda t: jnp.minimum(t, n_steps - 1)
    rows = lambda w: pl.BlockSpec((tq, w), lambda t: (blk_of(t), 0))
    late = lambda w: pl.BlockSpec((tq, w), lambda t: (jnp.maximum(t - 1, 0), 0))
    cur = pl.BlockSpec((1, tq, KV_WIDTH), lambda t: (blk_of(t) // nblk, blk_of(t) % nblk, 0))
    prev = pl.BlockSpec((1, WINDOW, KV_WIDTH),
                        lambda t: (blk_of(t) // nblk, jnp.maximum((blk_of(t) % nblk) * per_win - 1, 0), 0))
    win = pl.BlockSpec((1, WINDOW, KV_WIDTH), lambda t: (blk_of(t) // nblk, 0, 0))
    state = pl.BlockSpec((1, N_GLA, GLA_DK, GLA_DV), lambda t: (blk_of(t) // nblk, 0, 0, 0))
    k3 = k.reshape(batch, seq, KV_WIDTH)
    v3 = v.reshape(batch, seq, KV_WIDTH)
    return pl.pallas_call(
        functools.partial(_layer_prompt_kernel, tq=tq, group=group, nblk=nblk, n_steps=n_steps),
        grid=(n_steps + 1,),
        in_specs=[rows(ATT_WIDTH), prev, cur, prev, cur, rows(ATT_WIDTH), _const_spec(sink_rows.shape),
                  rows(GLA_KW), rows(GLA_KW), rows(GLA_WIDTH), rows(GLA_WIDTH), rows(GLA_KW),
                  rows(GLA_KW), _const_spec(tril2.shape), late(D_MODEL), _const_spec(w_out.shape)],
        out_specs=[late(D_MODEL), win, win, state],
        out_shape=[jax.ShapeDtypeStruct((batch * seq, D_MODEL), F32),
                   jax.ShapeDtypeStruct((batch, WINDOW, KV_WIDTH), F32),
                   jax.ShapeDtypeStruct((batch, WINDOW, KV_WIDTH), F32),
                   jax.ShapeDtypeStruct((batch, N_GLA, GLA_DK, GLA_DV), F32)],
        scratch_shapes=[pltpu.VMEM((WINDOW + tq, KV_WIDTH), BF16),
                        pltpu.VMEM((WINDOW + tq, KV_WIDTH), BF16),
                        pltpu.VMEM((2, N_GLA, GLA_DK, GLA_DV), F32),
                        pltpu.VMEM((CHUNK, GLA_DK), F32),
                        pltpu.VMEM((CHUNK, GLA_DK), F32),
                        pltpu.VMEM((2, tq, D_MODEL), BF16)],
        compiler_params=pltpu.CompilerParams(
            dimension_semantics=("arbitrary",), vmem_limit_bytes=VMEM_LIMIT),
        name="layer_prompt",
    )(q, k3, k3, v3, v3, ga, sink_rows, qg, kg, vg, gg, la_hi, la_lo, tril2, x2d, w_out)


def _attn_sample_kernel(q_ref, kn_ref, vn_ref, ck_ref, cv_ref, ga_ref, sink_ref,
                        o_ref, wk_ref, wv_ref, *, t, bb):
    n_cache = ck_ref.shape[1]
    assert n_cache == LANES and t <= LANES
    lane = lax.broadcasted_iota(jnp.int32, (1, LANES), 1)
    lo = lane < HEAD_DIM
    is_new = lane < t
    col_group = lax.broadcasted_iota(jnp.int32, (1, KV_WIDTH), 1) // HEAD_DIM
    seg = 2 * LANES
    ones_bd = jnp.concatenate(
        [jnp.broadcast_to(jnp.where(col_group == g, 1.0, 0.0).astype(BF16), (seg, KV_WIDTH))
         for g in range(N_KV)], axis=0)
    pad = jnp.zeros((LANES - t, KV_WIDTH), BF16)
    neg_inf = jnp.float32(-jnp.inf)

    s, v_all = [], []
    for b in range(bb):
        rows = slice(b * t, (b + 1) * t)
        kn, vn, ck, cv = kn_ref[rows, :], vn_ref[rows, :], ck_ref[b], cv_ref[b]
        wk_ref[b] = jnp.concatenate([ck[t:, :], kn], axis=0)
        wv_ref[b] = jnp.concatenate([cv[t:, :], vn], axis=0)
        kcat = jnp.concatenate([ck.astype(BF16), kn.astype(BF16), pad], axis=0)
        vcat = jnp.concatenate([cv.astype(BF16), vn.astype(BF16), pad], axis=0)
        zero = jnp.zeros_like(kcat)
        k_bd = jnp.concatenate([jnp.where(col_group == g, kcat, zero) for g in range(N_KV)], axis=0)
        v_bd = jnp.concatenate([jnp.where(col_group == g, vcat, zero) for g in range(N_KV)], axis=0)
        v_all.append(jnp.concatenate([v_bd, ones_bd], axis=1))
        qs = jnp.concatenate(
            [q_ref[rows, r * KV_WIDTH:(r + 1) * KV_WIDTH] for r in range(GQA_REP)], axis=0)
        s.append(_dot_t(qs, k_bd))
    p, m = [], []
    for b in range(bb):
        c = [s[b][:, LANES * n:LANES * (n + 1)] for n in range(2 * N_KV)]
        c = [cn if n % 2 == 0 else jnp.where(is_new, cn, neg_inf) for n, cn in enumerate(c)]
        mb = [jnp.maximum(jnp.max(jnp.maximum(c[2 * g], c[2 * g + 1]), axis=-1, keepdims=True),
                          sink_ref[g]) for g in range(N_KV)]
        p.append(jnp.concatenate([jnp.exp2(c[n] - mb[n // 2]) for n in range(2 * N_KV)],
                                 axis=1).astype(BF16))
        m.append(mb)
    ov = [_dot(p[b], v_all[b]) for b in range(bb)]
    for b in range(bb):
        rows = slice(b * t, (b + 1) * t)
        es = [jnp.exp2(sink_ref[g] - m[b][g]) for g in range(N_KV)]
        e_sink = jnp.concatenate([jnp.where(lo, es[0], es[1]), jnp.where(lo, es[2], es[3])], axis=1)
        o = ov[b][:, :KV_WIDTH] / (ov[b][:, KV_WIDTH:] + e_sink)
        for r in range(GQA_REP):
            cols = slice(r * KV_WIDTH, (r + 1) * KV_WIDTH)
            gate = ga_ref[rows, cols].astype(F32)
            o_ref[rows, cols] = (o[r * t:(r + 1) * t, :] * gate).astype(BF16)


def _attn_sample(q, k, v, cache_k, cache_v, ga, sink_rows, batch, t, bb):
    n_cache = cache_k.shape[1]
    rows = lambda w: pl.BlockSpec((bb * t, w), lambda b: (b, 0))
    cache = pl.BlockSpec((bb, n_cache, KV_WIDTH), lambda b: (b, 0, 0))
    return pl.pallas_call(
        functools.partial(_attn_sample_kernel, t=t, bb=bb),
        grid=(batch // bb,),
        in_specs=[rows(ATT_WIDTH), rows(KV_WIDTH), rows(KV_WIDTH), cache, cache, rows(ATT_WIDTH),
                  _const_spec(sink_rows.shape)],
        out_specs=[rows(ATT_WIDTH), cache, cache],
        out_shape=[jax.ShapeDtypeStruct((batch * t, ATT_WIDTH), BF16),
                   jax.ShapeDtypeStruct((batch, n_cache, KV_WIDTH), F32),
                   jax.ShapeDtypeStruct((batch, n_cache, KV_WIDTH), F32)],
        compiler_params=pltpu.CompilerParams(
            dimension_semantics=("arbitrary",), vmem_limit_bytes=VMEM_LIMIT),
        name="attn_sample",
    )(q, k, v, cache_k, cache_v, ga, sink_rows)


def _gla_kernel(qg_ref, kg_ref, vg_ref, gg_ref, lah_ref, lal_ref, s0_ref, tril_ref,
                o_ref, sout_ref, b_ref, q_ref, *, blk, n_chunks, streams):
    rowi = lax.broadcasted_iota(jnp.int32, (blk, blk), 0)
    coli = lax.broadcasted_iota(jnp.int32, (blk, blk), 1)
    causal = rowi >= coli
    eye = (lax.broadcasted_iota(jnp.int32, (GLA_DK, GLA_DK), 0)
           == lax.broadcasted_iota(jnp.int32, (GLA_DK, GLA_DK), 1))

    heads = range(N_GLA)
    kcol = [slice(h * GLA_DK, (h + 1) * GLA_DK) for h in heads]
    vcol = [slice(h * GLA_DV, (h + 1) * GLA_DV) for h in heads]

    def cum_log2_decay(r):
        return _dot(tril_ref[...], jnp.concatenate([lah_ref[r, :], lal_ref[r, :]], axis=0))

    def decay_column(blast):
        dec_row = jnp.broadcast_to(jnp.exp2(blast), (GLA_DK, GLA_DK))
        return jnp.sum(jnp.where(eye, dec_row, 0.0), axis=-1, keepdims=True)

    def finish(ov, r, h):
        ms = jnp.mean(ov * ov, axis=-1, keepdims=True)
        gate = gg_ref[r, vcol[h]].astype(F32)
        o_ref[r, vcol[h]] = (ov * lax.rsqrt(ms + EPS) * gate).astype(BF16)

    units = [(s, u) for s in range(streams) for u in range(n_chunks)]
    rows = {(s, u): slice((s * n_chunks + u) * blk, (s * n_chunks + u + 1) * blk) for s, u in units}

    def exact_step():
        for s in range(streams):
            st = [s0_ref[s, h] for h in heads]
            for u in range(n_chunks):
                r = rows[s, u]
                bcum = cum_log2_decay(r)
                for h in heads:
                    b = bcum[:, kcol[h]]
                    blast = b[blk - 1:blk, :]
                    q = qg_ref[r, kcol[h]].astype(F32)
                    k = kg_ref[r, kcol[h]].astype(F32)
                    v = vg_ref[r, vcol[h]]
                    b_ref[...] = b
                    q_ref[...] = q

                    def score_row(t, att_t):
                        w = jnp.where(rowi[:, 0:1] <= t, jnp.exp2(b_ref[pl.ds(t, 1), :] - b), 0.0)
                        col = jnp.sum(q_ref[pl.ds(t, 1), :] * k * w, axis=-1, keepdims=True)
                        return att_t + jnp.where(coli == t, col, 0.0)

                    att_t = lax.fori_loop(0, blk, score_row, jnp.zeros((blk, blk), F32))
                    q_dec = (q * jnp.exp2(b)).astype(BF16)
                    k_end = (k * jnp.exp2(blast - b)).astype(BF16)
                    ov = _dot_ta(att_t.astype(BF16), v) + _dot(q_dec, st[h].astype(BF16))
                    st[h] = st[h] * decay_column(blast) + _dot_ta(k_end, v)
                    finish(ov, r, h)
            for h in heads:
                sout_ref[s, h] = st[h]

    bcum = {unit: cum_log2_decay(rows[unit]) for unit in units}
    q_dec, k_inv, k_end, dec_col = {}, {}, {}, {}
    for unit in units:
        for h in heads:
            b = bcum[unit][:, kcol[h]]
            blast = b[blk - 1:blk, :]
            q = qg_ref[rows[unit], kcol[h]].astype(F32)
            k = kg_ref[rows[unit], kcol[h]].astype(F32)
            decay = jnp.exp2(b)
            q_dec[unit, h] = (q * decay).astype(BF16)
            k_inv[unit, h] = (k * (1.0 / decay)).astype(BF16)
            k_end[unit, h] = (k * jnp.exp2(blast - b)).astype(BF16)
            dec_col[unit, h] = decay_column(blast)
    att = {(unit, h): _dot_t(q_dec[unit, h], k_inv[unit, h]) for unit in units for h in heads}
    d_st = {(unit, h): _dot_ta(k_end[unit, h], vg_ref[rows[unit], vcol[h]])
            for unit in units for h in heads}
    att = {key: jnp.where(causal, a, 0.0).astype(BF16) for key, a in att.items()}
    st = {(s, h): s0_ref[s, h] for s in range(streams) for h in heads}
    o = {}
    for s, u in units:
        for h in heads:
            o[s, u, h] = (_dot(att[(s, u), h], vg_ref[rows[s, u], vcol[h]])
                          + _dot(q_dec[(s, u), h], st[s, h].astype(BF16)))
            st[s, h] = st[s, h] * dec_col[(s, u), h] + d_st[(s, u), h]
    for (s, h), value in st.items():
        sout_ref[s, h] = value
    for s, u in units:
        for h in heads:
            finish(o[s, u, h], rows[s, u], h)

    total = functools.reduce(jnp.minimum, [bc[blk - 1:blk, :] for bc in bcum.values()])

    @pl.when(jnp.min(total) < -LOG2_DECAY_LIMIT)
    def _():
        exact_step()


def _gla(qg, kg, vg, gg, la_hi, la_lo, s0, tril2, batch, seq, blk, streams):
    rows = lambda w: pl.BlockSpec((streams * seq, w), lambda g: (g, 0))
    state = pl.BlockSpec((streams, N_GLA, GLA_DK, GLA_DV), lambda g: (g, 0, 0, 0))
    return pl.pallas_call(
        functools.partial(_gla_kernel, blk=blk, n_chunks=seq // blk, streams=streams),
        grid=(batch // streams,),
        in_specs=[rows(GLA_KW), rows(GLA_KW), rows(GLA_WIDTH), rows(GLA_WIDTH), rows(GLA_KW),
                  rows(GLA_KW), state, _const_spec(tril2.shape)],
        out_specs=[rows(GLA_WIDTH), state],
        out_shape=[jax.ShapeDtypeStruct((batch * seq, GLA_WIDTH), BF16),
                   jax.ShapeDtypeStruct((batch, N_GLA, GLA_DK, GLA_DV), F32)],
        scratch_shapes=[pltpu.VMEM((blk, GLA_DK), F32),
                        pltpu.VMEM((blk, GLA_DK), F32)],
        compiler_params=pltpu.CompilerParams(
            dimension_semantics=("arbitrary",), vmem_limit_bytes=VMEM_LIMIT),
        name="gla",
    )(qg, kg, vg, gg, la_hi, la_lo, s0, tril2)


def _outproj_kernel(x_ref, ma_ref, mg_ref, w_ref, y_ref):
    y_ref[...] = (x_ref[...] + _dot(ma_ref[...], w_ref[:ATT_WIDTH, :])
                  + _dot(mg_ref[...], w_ref[ATT_WIDTH:, :]))


def _outproj(x2d, mix_a, mix_g, w, tm):
    m = x2d.shape[0]
    row = lambda w: pl.BlockSpec((tm, w), lambda i: (i, 0))
    return pl.pallas_call(
        _outproj_kernel,
        grid=(m // tm,),
        in_specs=[row(D_MODEL), row(ATT_WIDTH), row(GLA_WIDTH), _const_spec(w.shape)],
        out_specs=row(D_MODEL),
        out_shape=jax.ShapeDtypeStruct((m, D_MODEL), F32),
        compiler_params=pltpu.CompilerParams(
            dimension_semantics=("arbitrary",), vmem_limit_bytes=VMEM_LIMIT),
        name="outproj",
    )(x2d, mix_a, mix_g, w)


def _regroup_unit(u):
    return (u % N_KV) * GQA_REP + u // N_KV


def _prep_w_in_kernel(*refs):
    *unit_refs, lr_ref, main_ref, wlr_ref = refs
    units = jnp.concatenate([u[...] for u in unit_refs], axis=0)
    main_ref[...] = units.T.astype(BF16)

    @pl.when(pl.program_id(0) == 0)
    def _():
        lr = lr_ref[...]
        padded = jnp.concatenate([lr, jnp.zeros((LANES - lr.shape[0], lr.shape[1]), F32)], axis=0)
        wlr_ref[...] = padded.T.astype(BF16)


def _prep_w_in(w_in_t):
    _, k = w_in_t.shape
    units_per_block = PREP_UNITS
    block_cols = units_per_block * HEAD_DIM
    n_att = ATT_WIDTH // HEAD_DIM
    assert MAIN_COLS % block_cols == 0

    def src_unit(j, p):
        u = j * units_per_block + p
        q0, ga0 = OFF_Q // HEAD_DIM, OFF_GA // HEAD_DIM
        u = jnp.where((u >= q0) & (u < q0 + n_att), q0 + _regroup_unit(u - q0), u)
        return jnp.where((u >= ga0) & (u < ga0 + n_att), ga0 + _regroup_unit(u - ga0), u)

    unit = lambda p: pl.BlockSpec((HEAD_DIM, k), lambda j: (src_unit(j, p), 0))
    return pl.pallas_call(
        _prep_w_in_kernel,
        grid=(MAIN_COLS // block_cols,),
        in_specs=[unit(p) for p in range(units_per_block)]
        + [pl.BlockSpec((GATE_RANK, k), lambda j: (MAIN_COLS // GATE_RANK, 0))],
        out_specs=[pl.BlockSpec((k, block_cols), lambda j: (0, j)),
                   pl.BlockSpec((k, LANES), lambda j: (0, 0))],
        out_shape=[jax.ShapeDtypeStruct((k, MAIN_COLS), BF16),
                   jax.ShapeDtypeStruct((k, LANES), BF16)],
        compiler_params=pltpu.CompilerParams(
            dimension_semantics=("arbitrary",), vmem_limit_bytes=VMEM_LIMIT),
        name="prep_w_in",
    )(*([w_in_t] * (units_per_block + 1)))


def _prep_w_out_kernel(*refs):
    *unit_refs, o_ref = refs
    o_ref[...] = jnp.concatenate([u[...] for u in unit_refs], axis=0).astype(BF16)


def _prep_w_out(w_out):
    rows, cols = w_out.shape
    units_per_block = PREP_UNITS
    block_rows = units_per_block * HEAD_DIM
    n_att = ATT_WIDTH // HEAD_DIM
    assert rows % block_rows == 0

    def src_unit(j, p):
        u = j * units_per_block + p
        return jnp.where(u < n_att, _regroup_unit(u), u)

    unit = lambda p: pl.BlockSpec((HEAD_DIM, cols), lambda j: (src_unit(j, p), 0))
    return pl.pallas_call(
        _prep_w_out_kernel,
        grid=(rows // block_rows,),
        in_specs=[unit(p) for p in range(units_per_block)],
        out_specs=pl.BlockSpec((block_rows, cols), lambda j: (j, 0)),
        out_shape=jax.ShapeDtypeStruct((rows, cols), BF16),
        compiler_params=pltpu.CompilerParams(
            dimension_semantics=("arbitrary",), vmem_limit_bytes=VMEM_LIMIT),
        name="prep_w_out",
    )(*([w_out] * units_per_block))


def _tril2(blk):
    t = np.tril(np.ones((blk, blk), np.float32))
    return jnp.asarray(np.concatenate([t, t], axis=1), BF16)


def _prep(norm_g, w_in, w_gate_up, b_gate, q_norm_g, k_norm_g, sinks, gla_norm_g, w_out):
    w_main, w_lr = _prep_w_in(w_in.T)
    w_gu = jnp.pad(w_gate_up, ((0, LANES - GATE_RANK), (0, 0))).astype(BF16)
    g_qk = jnp.concatenate([jnp.tile(q_norm_g, N_HEADS) * (ATT_SCALE * LOG2E),
                            jnp.tile(k_norm_g, N_KV)])[None, :]

    def sink_rows(t):
        sk = sinks.astype(F32).reshape(N_KV, GQA_REP) * LOG2E
        return jnp.broadcast_to(sk[:, :, None, None], (N_KV, GQA_REP, t, LANES)).reshape(
            N_KV, GQA_REP * t, LANES)

    return dict(norm_g=norm_g[None, :], w_main=w_main, w_lr=w_lr, w_gu=w_gu, b_gate=b_gate[None, :],
                g_qk=g_qk, g_gla=jnp.tile(gla_norm_g, N_GLA)[None, :],
                w_out=_prep_w_out(w_out), sink_rows=sink_rows)


def kernel(x_prompt, x_sample, cache_k, cache_v, state_gla, norm_g, w_in, w_gate_up, b_gate,
           q_norm_g, k_norm_g, sinks, gla_norm_g, w_out):
    depth = norm_g.shape[0]
    assert depth == 1
    B, S, _ = x_prompt.shape
    DB, T, _ = x_sample.shape
    n_cache = cache_k.shape[2]
    assert n_cache == WINDOW
    p = _prep(norm_g[0], w_in[0], w_gate_up[0], b_gate[0], q_norm_g[0], k_norm_g[0], sinks[0],
              gla_norm_g[0], w_out[0])

    xp = x_prompt.reshape(B * S, D_MODEL)
    q, k, v, ga, qg, kg, vg, gg, la_hi, la_lo = _inproj(xp, p, tm=INPROJ_ROWS)
    y_p, win_k_p, win_v_p, gla_p = _layer_prompt(
        xp, q, k, v, ga, p["sink_rows"](CHUNK), qg, kg, vg, gg, la_hi, la_lo, _tril2(CHUNK), p["w_out"],
        B, S, tq=LAYER_ROWS, group=LAYER_GROUP)
    y_p = y_p.reshape(B, S, D_MODEL)

    xs = x_sample.reshape(DB * T, D_MODEL)
    q, k, v, ga, qg, kg, vg, gg, la_hi, la_lo = _inproj(xs, p, tm=DB * T)
    ck = cache_k[0].reshape(DB, n_cache, KV_WIDTH)
    cv = cache_v[0].reshape(DB, n_cache, KV_WIDTH)
    mix_a, win_k_s, win_v_s = _attn_sample(q, k, v, ck, cv, ga, p["sink_rows"](T), DB, T,
                                           bb=SAMPLE_STREAMS)
    blk = min(CHUNK, T)
    mix_g, gla_s = _gla(qg, kg, vg, gg, la_hi, la_lo, state_gla[0], _tril2(blk), DB, T,
                        blk=blk, streams=SAMPLE_STREAMS)
    y_s = _outproj(xs, mix_a, mix_g, p["w_out"], tm=DB * T).reshape(DB, T, D_MODEL)

    shape5 = lambda a: a.reshape(1, a.shape[0], n_cache, N_KV, HEAD_DIM)
    return (y_p, y_s, shape5(win_k_p), shape5(win_v_p), gla_p[None],
            shape5(win_k_s), shape5(win_v_s), gla_s[None])
```

```python
import functools

import jax
import jax.numpy as jnp
import numpy as np
from jax import lax
from jax.experimental import pallas as pl
from jax.experimental.pallas import tpu as pltpu

D_MODEL = 2048
CHUNK = 64
WINDOW = 128
HEAD_DIM = 64
N_HEADS = 16
N_KV = 4
GQA_REP = N_HEADS // N_KV
ATT_WIDTH = N_HEADS * HEAD_DIM
KV_WIDTH = N_KV * HEAD_DIM
N_GLA = 4
GLA_DK = 128
GLA_DV = 256
GLA_KW = N_GLA * GLA_DK
GLA_WIDTH = N_GLA * GLA_DV
GATE_RANK = 16
GATE_TAU = 16.0
EPS = 1e-6
ATT_SCALE = HEAD_DIM ** -0.5
LOG2E = float(np.log2(np.e))
LOG2_DECAY_LIMIT = 100.0
WIN_CHUNKS = WINDOW // CHUNK
WIN_KEYS = WINDOW + CHUNK

LANES = 128
QK_WIDTH = ATT_WIDTH + KV_WIDTH
OFF_Q, OFF_K, OFF_V = 0, ATT_WIDTH, ATT_WIDTH + KV_WIDTH
OFF_GA = OFF_V + KV_WIDTH
OFF_QG = OFF_GA + ATT_WIDTH
OFF_KG = OFF_QG + GLA_KW
OFF_VG = OFF_KG + GLA_KW
OFF_GG = OFF_VG + GLA_WIDTH
MAIN_COLS = OFF_GG + GLA_WIDTH

VMEM_LIMIT = 56 * 1024 * 1024
INPROJ_ROWS = 512
INPROJ_SUB = 256
LAYER_ROWS = 512
LAYER_GROUP = 4
OUT_PIECE_COLS = 512
SAMPLE_STREAMS = 4
PREP_UNITS = 8

BF16 = jnp.bfloat16
F32 = jnp.float32


def _dot(a, b):
    return jnp.dot(a, b, preferred_element_type=F32)


def _dot_t(a, b):
    return lax.dot_general(a, b, (((1,), (1,)), ((), ())), preferred_element_type=F32)


def _dot_ta(a, b):
    return lax.dot_general(a, b, (((0,), (0,)), ((), ())), preferred_element_type=F32)


def _silu(x):
    return x / (1.0 + jnp.exp(-x))


def _const_spec(shape):
    nd = len(shape)
    return pl.BlockSpec(shape, lambda *_: (0,) * nd, pipeline_mode=pl.Buffered(1))


def _inproj_kernel(x_ref, xs_ref, ng_ref, w_ref, wlr_ref, wgu_ref, bg_ref, gqk_ref, gng_ref,
                   q_ref, k_ref, v_ref, ga_ref, qg_ref, kg_ref, vg_ref, gg_ref, lah_ref, lal_ref,
                   *, sub, n_main):
    lo = lax.broadcasted_iota(jnp.int32, (1, LANES), 1) < HEAD_DIM
    out_refs = (q_ref, k_ref, v_ref, ga_ref, qg_ref, kg_ref, vg_ref, gg_ref, lah_ref, lal_ref)

    def project(src_ref, rows):
        x = src_ref[rows, :]
        ms = jnp.mean(x * x, axis=-1, keepdims=True)
        xn = (x * lax.rsqrt(ms + EPS) * ng_ref[...]).astype(BF16)

        qk = _dot(xn, w_ref[:, OFF_Q:OFF_V])
        ga_ref[rows, :] = _silu(_dot(xn, w_ref[:, OFF_GA:OFF_QG])).astype(BF16)
        lr = _dot(xn, wlr_ref[...]).astype(BF16)
        qg_ref[rows, :] = (_dot(xn, w_ref[:, OFF_QG:OFF_KG]) * (GLA_DK ** -0.5)).astype(BF16)
        kg_ref[rows, :] = _dot(xn, w_ref[:, OFF_KG:OFF_VG]).astype(BF16)
        z = _dot(lr, wgu_ref[...]) + bg_ref[...]
        gg_ref[rows, :] = (_silu(_dot(xn, w_ref[:, OFF_GG:MAIN_COLS])) * gng_ref[...]).astype(BF16)
        vg_ref[rows, :] = _dot(xn, w_ref[:, OFF_VG:OFF_GG]).astype(BF16)
        v_ref[rows, :] = _dot(xn, w_ref[:, OFF_V:OFF_GA])

        for n in range(QK_WIDTH // LANES):
            cols = slice(n * LANES, (n + 1) * LANES)
            blk = qk[:, cols]
            sq = blk * blk
            ms_lo = jnp.sum(jnp.where(lo, sq, 0.0), axis=-1, keepdims=True) * (1.0 / HEAD_DIM)
            ms_hi = jnp.sum(jnp.where(lo, 0.0, sq), axis=-1, keepdims=True) * (1.0 / HEAD_DIM)
            scale = jnp.where(lo, lax.rsqrt(ms_lo + EPS), lax.rsqrt(ms_hi + EPS))
            normed = blk * scale * gqk_ref[:, cols]
            if n < ATT_WIDTH // LANES:
                q_ref[rows, cols] = normed.astype(BF16)
            else:
                k_ref[rows, n * LANES - ATT_WIDTH:(n + 1) * LANES - ATT_WIDTH] = normed
        log_sig = jnp.minimum(z, 0.0) - jnp.log(1.0 + jnp.exp(-jnp.abs(z)))
        la = log_sig * (LOG2E / GATE_TAU)
        la_hi = la.astype(BF16)
        lah_ref[rows, :] = la_hi
        lal_ref[rows, :] = (la - la_hi.astype(F32)).astype(BF16)

    @pl.when(pl.program_id(0) < n_main)
    def _():
        for s in range(x_ref.shape[0] // sub):
            project(x_ref, slice(s * sub, (s + 1) * sub))

    @pl.when(pl.program_id(0) == n_main)
    def _():
        n_rows = xs_ref.shape[0]
        sub_s = min(sub, n_rows)
        for s in range(n_rows // sub_s):
            project(xs_ref, slice(s * sub_s, (s + 1) * sub_s))
        for o in out_refs if n_rows < x_ref.shape[0] else ():
            o[n_rows:, :] = jnp.zeros((o.shape[0] - n_rows, o.shape[1]), o.dtype)


def _inproj(x2d, xs2d, p, tm):
    m, ms = x2d.shape[0], xs2d.shape[0]
    sub = min(tm, INPROJ_SUB)
    n_main = m // tm
    assert m % tm == 0 and ms <= tm and ms % min(sub, ms) == 0
    row = lambda w: pl.BlockSpec((tm, w), lambda i: (i, 0))
    outs = [(ATT_WIDTH, BF16), (KV_WIDTH, F32), (KV_WIDTH, F32), (ATT_WIDTH, BF16),
            (GLA_KW, BF16), (GLA_KW, BF16), (GLA_WIDTH, BF16), (GLA_WIDTH, BF16),
            (GLA_KW, BF16), (GLA_KW, BF16)]
    consts = [p["norm_g"], p["w_main"], p["w_lr"], p["w_gu"], p["b_gate"], p["g_qk"], p["g_gla"]]
    x_spec = pl.BlockSpec((tm, D_MODEL), lambda i: (jnp.minimum(i, n_main - 1), 0))
    return pl.pallas_call(
        functools.partial(_inproj_kernel, sub=sub, n_main=n_main),
        grid=(n_main + 1,),
        in_specs=[x_spec, _const_spec(xs2d.shape)] + [_const_spec(c.shape) for c in consts],
        out_specs=[row(w) for w, _ in outs],
        out_shape=[jax.ShapeDtypeStruct((m + tm, w), dt) for w, dt in outs],
        compiler_params=pltpu.CompilerParams(
            dimension_semantics=("arbitrary",), vmem_limit_bytes=VMEM_LIMIT),
        name="inproj",
    )(x2d, xs2d, *consts)


def _layer_prompt_kernel(q_ref, kp_ref, kc_ref, vp_ref, vc_ref, ga_ref, sink_ref,
                         qg_ref, kg_ref, vg_ref, gg_ref, lah_ref, lal_ref, tril_ref, x_ref, w_ref,
                         y_ref, wk_ref, wv_ref, sout_ref,
                         kw_ref, vw_ref, st_ref, b_ref, qx_ref, mix_ref, *, tq, group, nblk, n_steps):
    n_chunks = tq // CHUNK
    assert n_chunks % group == 0 and group >= WIN_CHUNKS and tq >= WINDOW
    t = pl.program_id(0)
    i = lax.rem(jnp.minimum(t, n_steps - 1), nblk)
    last = nblk - 1
    cur = i % 2
    nxt = 1 - cur
    mix_slot = t % 2
    prev_slot = 1 - mix_slot

    out_pieces = [(slice(0, tq), slice(c, c + OUT_PIECE_COLS)) for c in range(0, D_MODEL, OUT_PIECE_COLS)]

    def out_piece(n):
        r, c = out_pieces[n]
        y_ref[r, c] = x_ref[r, c] + _dot(mix_ref[prev_slot, r, :], w_ref[:, c])

    def mixers(do_out):
        kw_ref[0:WINDOW, :] = kp_ref[...].astype(BF16)
        kw_ref[WINDOW:, :] = kc_ref[...].astype(BF16)
        vw_ref[0:WINDOW, :] = vp_ref[...].astype(BF16)
        vw_ref[WINDOW:, :] = vc_ref[...].astype(BF16)

        @pl.when(i == 0)
        def _():
            st_ref[0] = jnp.zeros(st_ref.shape[1:], F32)

        @pl.when(i == last)
        def _():
            wk_ref[0] = kc_ref[tq - WINDOW:, :]
            wv_ref[0] = vc_ref[tq - WINDOW:, :]

        lane = lax.broadcasted_iota(jnp.int32, (1, LANES), 1)
        lo = lane < HEAD_DIM
        col_group = lax.broadcasted_iota(jnp.int32, (1, KV_WIDTH), 1) // HEAD_DIM
        col = lax.broadcasted_iota(jnp.int32, (1, N_KV * WIN_KEYS), 1)
        key_in_seg = col - WIN_KEYS * sum((col >= g * WIN_KEYS).astype(jnp.int32) for g in range(1, N_KV))
        neg_inf = jnp.float32(-jnp.inf)
        rows = [slice(c * CHUNK, (c + 1) * CHUNK) for c in range(n_chunks)]
        a_s, a_v, a_p, a_den, a_ov = {}, {}, {}, {}, {}

        def block_diag(win):
            zero = jnp.zeros_like(win)
            return jnp.concatenate([jnp.where(col_group == g, win, zero) for g in range(N_KV)], axis=0)

        def attn_scores(c):
            k_bd = block_diag(kw_ref[c * CHUNK:c * CHUNK + WIN_KEYS, :])
            a_v[c] = block_diag(vw_ref[c * CHUNK:c * CHUNK + WIN_KEYS, :])
            qs = jnp.concatenate(
                [q_ref[rows[c], r * KV_WIDTH:(r + 1) * KV_WIDTH] for r in range(GQA_REP)], axis=0)
            a_s[c] = _dot_t(qs, k_bd)

        def attn_softmax(c):
            su = a_s.pop(c)
            if c < WIN_CHUNKS:
                first_valid = jnp.where(i == 0, (WIN_CHUNKS - c) * CHUNK, 0)
                su = jnp.where(key_in_seg >= first_valid, su, neg_inf)
            cs = [su[:, LANES * n:LANES * (n + 1)] for n in range(6)]
            seg_max = [
                jnp.maximum(cs[0], jnp.where(lo, cs[1], neg_inf)),
                jnp.maximum(jnp.where(lo, neg_inf, cs[1]), cs[2]),
                jnp.maximum(cs[3], jnp.where(lo, cs[4], neg_inf)),
                jnp.maximum(jnp.where(lo, neg_inf, cs[4]), cs[5]),
            ]
            mu = [jnp.maximum(jnp.max(seg_max[g], axis=-1, keepdims=True), sink_ref[g])
                  for g in range(N_KV)]
            shift = [mu[0], jnp.where(lo, mu[0], mu[1]), mu[1], mu[2], jnp.where(lo, mu[2], mu[3]), mu[3]]
            e = [jnp.exp2(cs[n] - shift[n]) for n in range(6)]
            a_p[c] = jnp.concatenate(e, axis=1).astype(BF16)
            seg_sum = [e[0] + jnp.where(lo, e[1], 0.0), jnp.where(lo, 0.0, e[1]) + e[2],
                       e[3] + jnp.where(lo, e[4], 0.0), jnp.where(lo, 0.0, e[4]) + e[5]]
            a_den[c] = [jnp.sum(seg_sum[g], axis=-1, keepdims=True) + jnp.exp2(sink_ref[g] - mu[g])
                        for g in range(N_KV)]

        def attn_pv(c):
            a_ov[c] = _dot(a_p.pop(c), a_v.pop(c))

        def attn_finish(c):
            ov, den = a_ov.pop(c), a_den.pop(c)
            den = jnp.concatenate([jnp.where(lo, den[0], den[1]), jnp.where(lo, den[2], den[3])], axis=1)
            o = ov / den
            for r in range(GQA_REP):
                cols = slice(r * KV_WIDTH, (r + 1) * KV_WIDTH)
                gate = ga_ref[rows[c], cols].astype(F32)
                mix_ref[mix_slot, rows[c], cols] = (o[r * CHUNK:(r + 1) * CHUNK, :] * gate).astype(BF16)

        rowi = lax.broadcasted_iota(jnp.int32, (CHUNK, CHUNK), 0)
        coli = lax.broadcasted_iota(jnp.int32, (CHUNK, CHUNK), 1)
        causal = rowi >= coli
        eye = (lax.broadcasted_iota(jnp.int32, (GLA_DK, GLA_DK), 0)
               == lax.broadcasted_iota(jnp.int32, (GLA_DK, GLA_DK), 1))
        heads = range(N_GLA)
        kcol = [slice(h * GLA_DK, (h + 1) * GLA_DK) for h in heads]
        vcol = [slice(h * GLA_DV, (h + 1) * GLA_DV) for h in heads]
        mcol = [slice(ATT_WIDTH + h * GLA_DV, ATT_WIDTH + (h + 1) * GLA_DV) for h in heads]
        g_b, g_qd, g_ki, g_ke, g_dc, g_att, g_dst, g_o = {}, {}, {}, {}, {}, {}, {}, {}
        state = {}

        def cum_log2_decay(r):
            return _dot(tril_ref[...], jnp.concatenate([lah_ref[r, :], lal_ref[r, :]], axis=0))

        def decay_column(blast):
            dec_row = jnp.broadcast_to(jnp.exp2(blast), (GLA_DK, GLA_DK))
            return jnp.sum(jnp.where(eye, dec_row, 0.0), axis=-1, keepdims=True)

        def gla_finish_rows(ov, r, h):
            ms = jnp.mean(ov * ov, axis=-1, keepdims=True)
            gate = gg_ref[r, vcol[h]].astype(F32)
            mix_ref[mix_slot, r, mcol[h]] = (ov * lax.rsqrt(ms + EPS) * gate).astype(BF16)

        def gla_cumsum(c):
            g_b[c] = cum_log2_decay(rows[c])

        def gla_prepare(c):
            for h in heads:
                b = g_b[c][:, kcol[h]]
                blast = b[CHUNK - 1:CHUNK, :]
                q = qg_ref[rows[c], kcol[h]].astype(F32)
                k = kg_ref[rows[c], kcol[h]].astype(F32)
                decay = jnp.exp2(b)
                g_qd[c, h] = (q * decay).astype(BF16)
                g_ki[c, h] = (k * (1.0 / decay)).astype(BF16)
                g_ke[c, h] = (k * jnp.exp2(blast - b)).astype(BF16)
                g_dc[c, h] = decay_column(blast)

        def gla_scores(c):
            for h in heads:
                g_att[c, h] = _dot_t(g_qd[c, h], g_ki.pop((c, h)))
                g_dst[c, h] = _dot_ta(g_ke.pop((c, h)), vg_ref[rows[c], vcol[h]])

        def gla_outputs(c):
            for h in heads:
                att = jnp.where(causal, g_att.pop((c, h)), 0.0).astype(BF16)
                lhs = jnp.concatenate([g_qd.pop((c, h)), att], axis=1)
                rhs = jnp.concatenate([state[h].astype(BF16), vg_ref[rows[c], vcol[h]]], axis=0)
                g_o[c, h] = _dot(lhs, rhs)
                state[h] = state[h] * g_dc.pop((c, h)) + g_dst.pop((c, h))

        def gla_finish(c):
            for h in heads:
                gla_finish_rows(g_o.pop((c, h)), rows[c], h)

        def gla_exact_step():
            def chunk(c, carry):
                r = pl.ds(pl.multiple_of(c * CHUNK, CHUNK), CHUNK)
                bcum = cum_log2_decay(r)
                for h in heads:
                    b = bcum[:, kcol[h]]
                    blast = b[CHUNK - 1:CHUNK, :]
                    q = qg_ref[r, kcol[h]].astype(F32)
                    k = kg_ref[r, kcol[h]].astype(F32)
                    v = vg_ref[r, vcol[h]]
                    b_ref[...] = b
                    qx_ref[...] = q

                    def score_row(tok, att_t):
                        w = jnp.where(rowi[:, 0:1] <= tok, jnp.exp2(b_ref[pl.ds(tok, 1), :] - b), 0.0)
                        colv = jnp.sum(qx_ref[pl.ds(tok, 1), :] * k * w, axis=-1, keepdims=True)
                        return att_t + jnp.where(coli == tok, colv, 0.0)

                    att_t = lax.fori_loop(0, CHUNK, score_row, jnp.zeros((CHUNK, CHUNK), F32))
                    q_dec = (q * jnp.exp2(b)).astype(BF16)
                    k_end = (k * jnp.exp2(blast - b)).astype(BF16)
                    st = st_ref[nxt, h]
                    ov = _dot_ta(att_t.astype(BF16), v) + _dot(q_dec, st.astype(BF16))
                    st_ref[nxt, h] = st * decay_column(blast) + _dot_ta(k_end, v)
                    gla_finish_rows(ov, r, h)
                return carry

            st_ref[nxt] = st_ref[cur]
            lax.fori_loop(0, n_chunks, chunk, 0)

        pending = list(range(len(out_pieces))) if do_out else []
        n_slots = 4 * (n_chunks // group)
        slots_seen = [0]

        def out_slot():
            slots_seen[0] += 1
            due = len(out_pieces) * slots_seen[0] // n_slots - (len(out_pieces) - len(pending))
            for _ in range(min(due, len(pending))):
                out_piece(pending.pop(0))

        for h in heads:
            state[h] = st_ref[cur, h]
        for g0 in range(0, n_chunks, group):
            cs = list(range(g0, g0 + group))
            half = group // 2
            for c in cs:
                gla_cumsum(c)
            out_slot()
            for c in cs[:half]:
                attn_scores(c)
            out_slot()
            for c in cs:
                gla_prepare(c)
            for c in cs:
                gla_scores(c)
            out_slot()
            for c in cs[:half]:
                attn_softmax(c)
            for c in cs[:half]:
                attn_pv(c)
            for c in cs[half:]:
                attn_scores(c)
            out_slot()
            for c in cs:
                gla_outputs(c)
            for c in cs[half:]:
                attn_softmax(c)
            for c in cs[half:]:
                attn_pv(c)
            for c in cs:
                gla_finish(c)
            for c in cs:
                attn_finish(c)
        assert not pending
        for h in heads:
            st_ref[nxt, h] = state[h]
        total = functools.reduce(jnp.minimum, [g_b[c][CHUNK - 1:CHUNK, :] for c in range(n_chunks)])

        @pl.when(jnp.min(total) < -LOG2_DECAY_LIMIT)
        def _():
            gla_exact_step()

        @pl.when(i == last)
        def _():
            sout_ref[0] = st_ref[nxt]

    @pl.when(t == 0)
    def _():
        mixers(do_out=False)

    @pl.when((t > 0) & (t < n_steps))
    def _():
        mixers(do_out=True)

    @pl.when(t == n_steps)
    def _():
        for n in range(len(out_pieces)):
            out_piece(n)


def _layer_prompt(x2d, q, k, v, ga, sink_rows, qg, kg, vg, gg, la_hi, la_lo, tril2, w_out, batch, seq,
                  tq, group):
    nblk = seq // tq
    n_steps = batch * nblk
    per_win = tq // WINDOW
    blk_of = lambda t: jnp.minimum(t, n_steps - 1)
    rows = lambda w: pl.BlockSpec((tq, w), lambda t: (blk_of(t), 0))
    late = lambda w: pl.BlockSpec((tq, w), lambda t: (jnp.maximum(t - 1, 0), 0))
    cur = pl.BlockSpec((tq, KV_WIDTH), lambda t: (blk_of(t), 0))
    prev = pl.BlockSpec((WINDOW, KV_WIDTH), lambda t: (jnp.maximum(blk_of(t) * per_win - 1, 0), 0))
    win = pl.BlockSpec((1, WINDOW, KV_WIDTH), lambda t: (blk_of(t) // nblk, 0, 0))
    state = pl.BlockSpec((1, N_GLA, GLA_DK, GLA_DV), lambda t: (blk_of(t) // nblk, 0, 0, 0))
    return pl.pallas_call(
        functools.partial(_layer_prompt_kernel, tq=tq, group=group, nblk=nblk, n_steps=n_steps),
        grid=(n_steps + 1,),
        in_specs=[rows(ATT_WIDTH), prev, cur, prev, cur, rows(ATT_WIDTH), _const_spec(sink_rows.shape),
                  rows(GLA_KW), rows(GLA_KW), rows(GLA_WIDTH), rows(GLA_WIDTH), rows(GLA_KW),
                  rows(GLA_KW), _const_spec(tril2.shape), late(D_MODEL), _const_spec(w_out.shape)],
        out_specs=[late(D_MODEL), win, win, state],
        out_shape=[jax.ShapeDtypeStruct((batch * seq, D_MODEL), F32),
                   jax.ShapeDtypeStruct((batch, WINDOW, KV_WIDTH), F32),
                   jax.ShapeDtypeStruct((batch, WINDOW, KV_WIDTH), F32),
                   jax.ShapeDtypeStruct((batch, N_GLA, GLA_DK, GLA_DV), F32)],
        scratch_shapes=[pltpu.VMEM((WINDOW + tq, KV_WIDTH), BF16),
                        pltpu.VMEM((WINDOW + tq, KV_WIDTH), BF16),
                        pltpu.VMEM((2, N_GLA, GLA_DK, GLA_DV), F32),
                        pltpu.VMEM((CHUNK, GLA_DK), F32),
                        pltpu.VMEM((CHUNK, GLA_DK), F32),
                        pltpu.VMEM((2, tq, D_MODEL), BF16)],
        compiler_params=pltpu.CompilerParams(
            dimension_semantics=("arbitrary",), vmem_limit_bytes=VMEM_LIMIT),
        name="layer_prompt",
    )(q, k, k, v, v, ga, sink_rows, qg, kg, vg, gg, la_hi, la_lo, tril2, x2d, w_out)


def _attn_sample_kernel(q_ref, kn_ref, vn_ref, ck_ref, cv_ref, ga_ref, sink_ref,
                        o_ref, wk_ref, wv_ref, *, t, bb):
    n_cache = ck_ref.shape[1]
    assert n_cache == LANES and t <= LANES
    lane = lax.broadcasted_iota(jnp.int32, (1, LANES), 1)
    lo = lane < HEAD_DIM
    is_new = lane < t
    col_group = lax.broadcasted_iota(jnp.int32, (1, KV_WIDTH), 1) // HEAD_DIM
    seg = 2 * LANES
    ones_bd = jnp.concatenate(
        [jnp.broadcast_to(jnp.where(col_group == g, 1.0, 0.0).astype(BF16), (seg, KV_WIDTH))
         for g in range(N_KV)], axis=0)
    pad = jnp.zeros((LANES - t, KV_WIDTH), BF16)
    neg_inf = jnp.float32(-jnp.inf)

    s, v_all = [], []
    for b in range(bb):
        rows = slice(b * t, (b + 1) * t)
        kn, vn, ck, cv = kn_ref[rows, :], vn_ref[rows, :], ck_ref[b], cv_ref[b]
        wk_ref[b] = jnp.concatenate([ck[t:, :], kn], axis=0)
        wv_ref[b] = jnp.concatenate([cv[t:, :], vn], axis=0)
        kcat = jnp.concatenate([ck.astype(BF16), kn.astype(BF16), pad], axis=0)
        vcat = jnp.concatenate([cv.astype(BF16), vn.astype(BF16), pad], axis=0)
        zero = jnp.zeros_like(kcat)
        k_bd = jnp.concatenate([jnp.where(col_group == g, kcat, zero) for g in range(N_KV)], axis=0)
        v_bd = jnp.concatenate([jnp.where(col_group == g, vcat, zero) for g in range(N_KV)], axis=0)
        v_all.append(jnp.concatenate([v_bd, ones_bd], axis=1))
        qs = jnp.concatenate(
            [q_ref[rows, r * KV_WIDTH:(r + 1) * KV_WIDTH] for r in range(GQA_REP)], axis=0)
        s.append(_dot_t(qs, k_bd))
    p, m = [], []
    for b in range(bb):
        c = [s[b][:, LANES * n:LANES * (n + 1)] for n in range(2 * N_KV)]
        c = [cn if n % 2 == 0 else jnp.where(is_new, cn, neg_inf) for n, cn in enumerate(c)]
        mb = [jnp.maximum(jnp.max(jnp.maximum(c[2 * g], c[2 * g + 1]), axis=-1, keepdims=True),
                          sink_ref[g]) for g in range(N_KV)]
        p.append(jnp.concatenate([jnp.exp2(c[n] - mb[n // 2]) for n in range(2 * N_KV)],
                                 axis=1).astype(BF16))
        m.append(mb)
    ov = [_dot(p[b], v_all[b]) for b in range(bb)]
    for b in range(bb):
        rows = slice(b * t, (b + 1) * t)
        es = [jnp.exp2(sink_ref[g] - m[b][g]) for g in range(N_KV)]
        e_sink = jnp.concatenate([jnp.where(lo, es[0], es[1]), jnp.where(lo, es[2], es[3])], axis=1)
        o = ov[b][:, :KV_WIDTH] / (ov[b][:, KV_WIDTH:] + e_sink)
        for r in range(GQA_REP):
            cols = slice(r * KV_WIDTH, (r + 1) * KV_WIDTH)
            gate = ga_ref[rows, cols].astype(F32)
            o_ref[rows, cols] = (o[r * t:(r + 1) * t, :] * gate).astype(BF16)


def _attn_sample(q, k, v, cache_k, cache_v, ga, sink_rows, batch, t, bb, row0):
    n_cache = cache_k.shape[1]
    assert row0 % (bb * t) == 0
    first = row0 // (bb * t)
    rows = lambda w: pl.BlockSpec((bb * t, w), lambda b: (first + b, 0))
    out_rows = pl.BlockSpec((bb * t, ATT_WIDTH), lambda b: (b, 0))
    cache = pl.BlockSpec((bb, n_cache, KV_WIDTH), lambda b: (b, 0, 0))
    return pl.pallas_call(
        functools.partial(_attn_sample_kernel, t=t, bb=bb),
        grid=(batch // bb,),
        in_specs=[rows(ATT_WIDTH), rows(KV_WIDTH), rows(KV_WIDTH), cache, cache, rows(ATT_WIDTH),
                  _const_spec(sink_rows.shape)],
        out_specs=[out_rows, cache, cache],
        out_shape=[jax.ShapeDtypeStruct((batch * t, ATT_WIDTH), BF16),
                   jax.ShapeDtypeStruct((batch, n_cache, KV_WIDTH), F32),
                   jax.ShapeDtypeStruct((batch, n_cache, KV_WIDTH), F32)],
        compiler_params=pltpu.CompilerParams(
            dimension_semantics=("arbitrary",), vmem_limit_bytes=VMEM_LIMIT),
        name="attn_sample",
    )(q, k, v, cache_k, cache_v, ga, sink_rows)


def _gla_kernel(qg_ref, kg_ref, vg_ref, gg_ref, lah_ref, lal_ref, s0_ref, tril_ref,
                o_ref, sout_ref, b_ref, q_ref, *, blk, n_chunks, streams):
    rowi = lax.broadcasted_iota(jnp.int32, (blk, blk), 0)
    coli = lax.broadcasted_iota(jnp.int32, (blk, blk), 1)
    causal = rowi >= coli
    eye = (lax.broadcasted_iota(jnp.int32, (GLA_DK, GLA_DK), 0)
           == lax.broadcasted_iota(jnp.int32, (GLA_DK, GLA_DK), 1))

    heads = range(N_GLA)
    kcol = [slice(h * GLA_DK, (h + 1) * GLA_DK) for h in heads]
    vcol = [slice(h * GLA_DV, (h + 1) * GLA_DV) for h in heads]

    def cum_log2_decay(r):
        return _dot(tril_ref[...], jnp.concatenate([lah_ref[r, :], lal_ref[r, :]], axis=0))

    def decay_column(blast):
        dec_row = jnp.broadcast_to(jnp.exp2(blast), (GLA_DK, GLA_DK))
        return jnp.sum(jnp.where(eye, dec_row, 0.0), axis=-1, keepdims=True)

    def finish(ov, r, h):
        ms = jnp.mean(ov * ov, axis=-1, keepdims=True)
        gate = gg_ref[r, vcol[h]].astype(F32)
        o_ref[r, vcol[h]] = (ov * lax.rsqrt(ms + EPS) * gate).astype(BF16)

    units = [(s, u) for s in range(streams) for u in range(n_chunks)]
    rows = {(s, u): slice((s * n_chunks + u) * blk, (s * n_chunks + u + 1) * blk) for s, u in units}

    def exact_step():
        for s in range(streams):
            st = [s0_ref[s, h] for h in heads]
            for u in range(n_chunks):
                r = rows[s, u]
                bcum = cum_log2_decay(r)
                for h in heads:
                    b = bcum[:, kcol[h]]
                    blast = b[blk - 1:blk, :]
                    q = qg_ref[r, kcol[h]].astype(F32)
                    k = kg_ref[r, kcol[h]].astype(F32)
                    v = vg_ref[r, vcol[h]]
                    b_ref[...] = b
                    q_ref[...] = q

                    def score_row(t, att_t):
                        w = jnp.where(rowi[:, 0:1] <= t, jnp.exp2(b_ref[pl.ds(t, 1), :] - b), 0.0)
                        col = jnp.sum(q_ref[pl.ds(t, 1), :] * k * w, axis=-1, keepdims=True)
                        return att_t + jnp.where(coli == t, col, 0.0)

                    att_t = lax.fori_loop(0, blk, score_row, jnp.zeros((blk, blk), F32))
                    q_dec = (q * jnp.exp2(b)).astype(BF16)
                    k_end = (k * jnp.exp2(blast - b)).astype(BF16)
                    ov = _dot_ta(att_t.astype(BF16), v) + _dot(q_dec, st[h].astype(BF16))
                    st[h] = st[h] * decay_column(blast) + _dot_ta(k_end, v)
                    finish(ov, r, h)
            for h in heads:
                sout_ref[s, h] = st[h]

    bcum = {unit: cum_log2_decay(rows[unit]) for unit in units}
    q_dec, k_inv, k_end, dec_col = {}, {}, {}, {}
    for unit in units:
        for h in heads:
            b = bcum[unit][:, kcol[h]]
            blast = b[blk - 1:blk, :]
            q = qg_ref[rows[unit], kcol[h]].astype(F32)
            k = kg_ref[rows[unit], kcol[h]].astype(F32)
            decay = jnp.exp2(b)
            q_dec[unit, h] = (q * decay).astype(BF16)
            k_inv[unit, h] = (k * (1.0 / decay)).astype(BF16)
            k_end[unit, h] = (k * jnp.exp2(blast - b)).astype(BF16)
            dec_col[unit, h] = decay_column(blast)
    att = {(unit, h): _dot_t(q_dec[unit, h], k_inv[unit, h]) for unit in units for h in heads}
    d_st = {(unit, h): _dot_ta(k_end[unit, h], vg_ref[rows[unit], vcol[h]])
            for unit in units for h in heads}
    att = {key: jnp.where(causal, a, 0.0).astype(BF16) for key, a in att.items()}
    st = {(s, h): s0_ref[s, h] for s in range(streams) for h in heads}
    o = {}
    for s, u in units:
        for h in heads:
            o[s, u, h] = (_dot(att[(s, u), h], vg_ref[rows[s, u], vcol[h]])
                          + _dot(q_dec[(s, u), h], st[s, h].astype(BF16)))
            st[s, h] = st[s, h] * dec_col[(s, u), h] + d_st[(s, u), h]
    for (s, h), value in st.items():
        sout_ref[s, h] = value
    for s, u in units:
        for h in heads:
            finish(o[s, u, h], rows[s, u], h)

    total = functools.reduce(jnp.minimum, [bc[blk - 1:blk, :] for bc in bcum.values()])

    @pl.when(jnp.min(total) < -LOG2_DECAY_LIMIT)
    def _():
        exact_step()


def _gla(qg, kg, vg, gg, la_hi, la_lo, s0, tril2, batch, seq, blk, streams, row0):
    assert row0 % (streams * seq) == 0
    first = row0 // (streams * seq)
    rows = lambda w: pl.BlockSpec((streams * seq, w), lambda g: (first + g, 0))
    out_rows = pl.BlockSpec((streams * seq, GLA_WIDTH), lambda g: (g, 0))
    state = pl.BlockSpec((streams, N_GLA, GLA_DK, GLA_DV), lambda g: (g, 0, 0, 0))
    return pl.pallas_call(
        functools.partial(_gla_kernel, blk=blk, n_chunks=seq // blk, streams=streams),
        grid=(batch // streams,),
        in_specs=[rows(GLA_KW), rows(GLA_KW), rows(GLA_WIDTH), rows(GLA_WIDTH), rows(GLA_KW),
                  rows(GLA_KW), state, _const_spec(tril2.shape)],
        out_specs=[out_rows, state],
        out_shape=[jax.ShapeDtypeStruct((batch * seq, GLA_WIDTH), BF16),
                   jax.ShapeDtypeStruct((batch, N_GLA, GLA_DK, GLA_DV), F32)],
        scratch_shapes=[pltpu.VMEM((blk, GLA_DK), F32),
                        pltpu.VMEM((blk, GLA_DK), F32)],
        compiler_params=pltpu.CompilerParams(
            dimension_semantics=("arbitrary",), vmem_limit_bytes=VMEM_LIMIT),
        name="gla",
    )(qg, kg, vg, gg, la_hi, la_lo, s0, tril2)


def _outproj_kernel(x_ref, ma_ref, mg_ref, w_ref, y_ref):
    y_ref[...] = (x_ref[...] + _dot(ma_ref[...], w_ref[:ATT_WIDTH, :])
                  + _dot(mg_ref[...], w_ref[ATT_WIDTH:, :]))


def _outproj(x2d, mix_a, mix_g, w, tm):
    m = x2d.shape[0]
    row = lambda w: pl.BlockSpec((tm, w), lambda i: (i, 0))
    return pl.pallas_call(
        _outproj_kernel,
        grid=(m // tm,),
        in_specs=[row(D_MODEL), row(ATT_WIDTH), row(GLA_WIDTH), _const_spec(w.shape)],
        out_specs=row(D_MODEL),
        out_shape=jax.ShapeDtypeStruct((m, D_MODEL), F32),
        compiler_params=pltpu.CompilerParams(
            dimension_semantics=("arbitrary",), vmem_limit_bytes=VMEM_LIMIT),
        name="outproj",
    )(x2d, mix_a, mix_g, w)


def _regroup_unit(u):
    return (u % N_KV) * GQA_REP + u // N_KV


def _prep_w_in_kernel(*refs):
    *unit_refs, lr_ref, main_ref, wlr_ref = refs
    units = jnp.concatenate([u[...] for u in unit_refs], axis=0)
    main_ref[...] = units.T.astype(BF16)

    @pl.when(pl.program_id(0) == 0)
    def _():
        lr = lr_ref[...]
        padded = jnp.concatenate([lr, jnp.zeros((LANES - lr.shape[0], lr.shape[1]), F32)], axis=0)
        wlr_ref[...] = padded.T.astype(BF16)


def _prep_w_in(w_in_t):
    _, k = w_in_t.shape
    units_per_block = PREP_UNITS
    block_cols = units_per_block * HEAD_DIM
    n_att = ATT_WIDTH // HEAD_DIM
    assert MAIN_COLS % block_cols == 0

    def src_unit(j, p):
        u = j * units_per_block + p
        q0, ga0 = OFF_Q // HEAD_DIM, OFF_GA // HEAD_DIM
        u = jnp.where((u >= q0) & (u < q0 + n_att), q0 + _regroup_unit(u - q0), u)
        return jnp.where((u >= ga0) & (u < ga0 + n_att), ga0 + _regroup_unit(u - ga0), u)

    unit = lambda p: pl.BlockSpec((HEAD_DIM, k), lambda j: (src_unit(j, p), 0))
    return pl.pallas_call(
        _prep_w_in_kernel,
        grid=(MAIN_COLS // block_cols,),
        in_specs=[unit(p) for p in range(units_per_block)]
        + [pl.BlockSpec((GATE_RANK, k), lambda j: (MAIN_COLS // GATE_RANK, 0))],
        out_specs=[pl.BlockSpec((k, block_cols), lambda j: (0, j)),
                   pl.BlockSpec((k, LANES), lambda j: (0, 0))],
        out_shape=[jax.ShapeDtypeStruct((k, MAIN_COLS), BF16),
                   jax.ShapeDtypeStruct((k, LANES), BF16)],
        compiler_params=pltpu.CompilerParams(
            dimension_semantics=("arbitrary",), vmem_limit_bytes=VMEM_LIMIT),
        name="prep_w_in",
    )(*([w_in_t] * (units_per_block + 1)))


def _prep_w_out_kernel(*refs):
    *unit_refs, o_ref = refs
    o_ref[...] = jnp.concatenate([u[...] for u in unit_refs], axis=0).astype(BF16)


def _prep_w_out(w_out):
    rows, cols = w_out.shape
    units_per_block = PREP_UNITS
    block_rows = units_per_block * HEAD_DIM
    n_att = ATT_WIDTH // HEAD_DIM
    assert rows % block_rows == 0

    def src_unit(j, p):
        u = j * units_per_block + p
        return jnp.where(u < n_att, _regroup_unit(u), u)

    unit = lambda p: pl.BlockSpec((HEAD_DIM, cols), lambda j: (src_unit(j, p), 0))
    return pl.pallas_call(
        _prep_w_out_kernel,
        grid=(rows // block_rows,),
        in_specs=[unit(p) for p in range(units_per_block)],
        out_specs=pl.BlockSpec((block_rows, cols), lambda j: (j, 0)),
        out_shape=jax.ShapeDtypeStruct((rows, cols), BF16),
        compiler_params=pltpu.CompilerParams(
            dimension_semantics=("arbitrary",), vmem_limit_bytes=VMEM_LIMIT),
        name="prep_w_out",
    )(*([w_out] * units_per_block))


def _tril2(blk):
    t = np.tril(np.ones((blk, blk), np.float32))
    return jnp.asarray(np.concatenate([t, t], axis=1), BF16)


def _prep(norm_g, w_in, w_gate_up, b_gate, q_norm_g, k_norm_g, sinks, gla_norm_g, w_out):
    w_main, w_lr = _prep_w_in(w_in.T)
    w_gu = jnp.pad(w_gate_up, ((0, LANES - GATE_RANK), (0, 0))).astype(BF16)
    g_qk = jnp.concatenate([jnp.tile(q_norm_g, N_HEADS) * (ATT_SCALE * LOG2E),
                            jnp.tile(k_norm_g, N_KV)])[None, :]

    def sink_rows(t):
        sk = sinks.astype(F32).reshape(N_KV, GQA_REP) * LOG2E
        return jnp.broadcast_to(sk[:, :, None, None], (N_KV, GQA_REP, t, LANES)).reshape(
            N_KV, GQA_REP * t, LANES)

    return dict(norm_g=norm_g[None, :], w_main=w_main, w_lr=w_lr, w_gu=w_gu, b_gate=b_gate[None, :],
                g_qk=g_qk, g_gla=jnp.tile(gla_norm_g, N_GLA)[None, :],
                w_out=_prep_w_out(w_out), sink_rows=sink_rows)


def kernel(x_prompt, x_sample, cache_k, cache_v, state_gla, norm_g, w_in, w_gate_up, b_gate,
           q_norm_g, k_norm_g, sinks, gla_norm_g, w_out):
    depth = norm_g.shape[0]
    assert depth == 1
    B, S, _ = x_prompt.shape
    DB, T, _ = x_sample.shape
    n_cache = cache_k.shape[2]
    assert n_cache == WINDOW
    p = _prep(norm_g[0], w_in[0], w_gate_up[0], b_gate[0], q_norm_g[0], k_norm_g[0], sinks[0],
              gla_norm_g[0], w_out[0])

    xp = x_prompt.reshape(B * S, D_MODEL)
    xs = x_sample.reshape(DB * T, D_MODEL)
    q, k, v, ga, qg, kg, vg, gg, la_hi, la_lo = _inproj(xp, xs, p, tm=INPROJ_ROWS)

    y_p, win_k_p, win_v_p, gla_p = _layer_prompt(
        xp, q, k, v, ga, p["sink_rows"](CHUNK), qg, kg, vg, gg, la_hi, la_lo, _tril2(CHUNK), p["w_out"],
        B, S, tq=LAYER_ROWS, group=LAYER_GROUP)
    y_p = y_p.reshape(B, S, D_MODEL)

    ck = cache_k[0].reshape(DB, n_cache, KV_WIDTH)
    cv = cache_v[0].reshape(DB, n_cache, KV_WIDTH)
    mix_a, win_k_s, win_v_s = _attn_sample(q, k, v, ck, cv, ga, p["sink_rows"](T), DB, T,
                                           bb=SAMPLE_STREAMS, row0=B * S)
    blk = min(CHUNK, T)
    mix_g, gla_s = _gla(qg, kg, vg, gg, la_hi, la_lo, state_gla[0], _tril2(blk), DB, T,
                        blk=blk, streams=SAMPLE_STREAMS, row0=B * S)
    y_s = _outproj(xs, mix_a, mix_g, p["w_out"], tm=DB * T).reshape(DB, T, D_MODEL)

    shape5 = lambda a: a.reshape(1, a.shape[0], n_cache, N_KV, HEAD_DIM)
    return (y_p, y_s, shape5(win_k_p), shape5(win_v_p), gla_p[None],
            shape5(win_k_s), shape5(win_v_s), gla_s[None])
```

```python
import functools

import jax
import jax.numpy as jnp
import numpy as np
from jax import lax
from jax.experimental import pallas as pl
from jax.experimental.pallas import tpu as pltpu

D_MODEL = 2048
CHUNK = 64
WINDOW = 128
HEAD_DIM = 64
N_HEADS = 16
N_KV = 4
GQA_REP = N_HEADS // N_KV
ATT_WIDTH = N_HEADS * HEAD_DIM
KV_WIDTH = N_KV * HEAD_DIM
N_GLA = 4
GLA_DK = 128
GLA_DV = 256
GLA_KW = N_GLA * GLA_DK
GLA_WIDTH = N_GLA * GLA_DV
GATE_RANK = 16
GATE_TAU = 16.0
EPS = 1e-6
ATT_SCALE = HEAD_DIM ** -0.5
LOG2E = float(np.log2(np.e))
LOG2_DECAY_LIMIT = 100.0
WIN_CHUNKS = WINDOW // CHUNK
WIN_KEYS = WINDOW + CHUNK

LANES = 128
QK_WIDTH = ATT_WIDTH + KV_WIDTH
OFF_Q, OFF_K, OFF_V = 0, ATT_WIDTH, ATT_WIDTH + KV_WIDTH
OFF_GA = OFF_V + KV_WIDTH
OFF_QG = OFF_GA + ATT_WIDTH
OFF_KG = OFF_QG + GLA_KW
OFF_VG = OFF_KG + GLA_KW
OFF_GG = OFF_VG + GLA_WIDTH
MAIN_COLS = OFF_GG + GLA_WIDTH

VMEM_LIMIT = 56 * 1024 * 1024
INPROJ_ROWS = 512
INPROJ_SUB = 256
LAYER_ROWS = 512
LAYER_GROUP = 4
OUT_PIECE_COLS = 512
SAMPLE_STREAMS = 8
PREP_UNITS = 8

BF16 = jnp.bfloat16
F32 = jnp.float32


def _dot(a, b):
    return jnp.dot(a, b, preferred_element_type=F32)


def _dot_t(a, b):
    return lax.dot_general(a, b, (((1,), (1,)), ((), ())), preferred_element_type=F32)


def _dot_ta(a, b):
    return lax.dot_general(a, b, (((0,), (0,)), ((), ())), preferred_element_type=F32)


def _silu(x):
    return x / (1.0 + jnp.exp(-x))


def _const_spec(shape):
    nd = len(shape)
    return pl.BlockSpec(shape, lambda *_: (0,) * nd, pipeline_mode=pl.Buffered(1))


def _inproj_kernel(x_ref, ng_ref, w_ref, wlr_ref, wgu_ref, bg_ref, gqk_ref, gng_ref,
                   q_ref, k_ref, v_ref, ga_ref, qg_ref, kg_ref, vg_ref, gg_ref, lah_ref, lal_ref,
                   *, sub):
    lo = lax.broadcasted_iota(jnp.int32, (1, LANES), 1) < HEAD_DIM
    for s in range(x_ref.shape[0] // sub):
        rows = slice(s * sub, (s + 1) * sub)
        x = x_ref[rows, :]
        ms = jnp.mean(x * x, axis=-1, keepdims=True)
        xn = (x * lax.rsqrt(ms + EPS) * ng_ref[...]).astype(BF16)

        qk = _dot(xn, w_ref[:, OFF_Q:OFF_V])
        ga_ref[rows, :] = _silu(_dot(xn, w_ref[:, OFF_GA:OFF_QG])).astype(BF16)
        lr = _dot(xn, wlr_ref[...]).astype(BF16)
        qg_ref[rows, :] = (_dot(xn, w_ref[:, OFF_QG:OFF_KG]) * (GLA_DK ** -0.5)).astype(BF16)
        kg_ref[rows, :] = _dot(xn, w_ref[:, OFF_KG:OFF_VG]).astype(BF16)
        z = _dot(lr, wgu_ref[...]) + bg_ref[...]
        gg_ref[rows, :] = (_silu(_dot(xn, w_ref[:, OFF_GG:MAIN_COLS])) * gng_ref[...]).astype(BF16)
        vg_ref[rows, :] = _dot(xn, w_ref[:, OFF_VG:OFF_GG]).astype(BF16)
        v_ref[rows, :] = _dot(xn, w_ref[:, OFF_V:OFF_GA])

        for n in range(QK_WIDTH // LANES):
            cols = slice(n * LANES, (n + 1) * LANES)
            blk = qk[:, cols]
            sq = blk * blk
            ms_lo = jnp.sum(jnp.where(lo, sq, 0.0), axis=-1, keepdims=True) * (1.0 / HEAD_DIM)
            ms_hi = jnp.sum(jnp.where(lo, 0.0, sq), axis=-1, keepdims=True) * (1.0 / HEAD_DIM)
            scale = jnp.where(lo, lax.rsqrt(ms_lo + EPS), lax.rsqrt(ms_hi + EPS))
            normed = blk * scale * gqk_ref[:, cols]
            if n < ATT_WIDTH // LANES:
                q_ref[rows, cols] = normed.astype(BF16)
            else:
                k_ref[rows, n * LANES - ATT_WIDTH:(n + 1) * LANES - ATT_WIDTH] = normed
        log_sig = jnp.minimum(z, 0.0) - jnp.log(1.0 + jnp.exp(-jnp.abs(z)))
        la = log_sig * (LOG2E / GATE_TAU)
        la_hi = la.astype(BF16)
        lah_ref[rows, :] = la_hi
        lal_ref[rows, :] = (la - la_hi.astype(F32)).astype(BF16)


def _inproj(x2d, p, tm):
    m = x2d.shape[0]
    row = lambda w: pl.BlockSpec((tm, w), lambda i: (i, 0))
    outs = [(ATT_WIDTH, BF16), (KV_WIDTH, F32), (KV_WIDTH, F32), (ATT_WIDTH, BF16),
            (GLA_KW, BF16), (GLA_KW, BF16), (GLA_WIDTH, BF16), (GLA_WIDTH, BF16),
            (GLA_KW, BF16), (GLA_KW, BF16)]
    consts = [p["norm_g"], p["w_main"], p["w_lr"], p["w_gu"], p["b_gate"], p["g_qk"], p["g_gla"]]
    return pl.pallas_call(
        functools.partial(_inproj_kernel, sub=min(tm, INPROJ_SUB)),
        grid=(m // tm,),
        in_specs=[row(D_MODEL)] + [_const_spec(c.shape) for c in consts],
        out_specs=[row(w) for w, _ in outs],
        out_shape=[jax.ShapeDtypeStruct((m, w), dt) for w, dt in outs],
        compiler_params=pltpu.CompilerParams(
            dimension_semantics=("arbitrary",), vmem_limit_bytes=VMEM_LIMIT),
        name="inproj",
    )(x2d, *consts)


def _layer_prompt_kernel(q_ref, kp_ref, kc_ref, vp_ref, vc_ref, ga_ref, sink_ref,
                         qg_ref, kg_ref, vg_ref, gg_ref, lah_ref, lal_ref, tril_ref, x_ref, w_ref,
                         y_ref, wk_ref, wv_ref, sout_ref,
                         kw_ref, vw_ref, st_ref, b_ref, qx_ref, mix_ref, *, tq, group, nblk, n_steps):
    n_chunks = tq // CHUNK
    assert n_chunks % group == 0 and group >= WIN_CHUNKS and tq >= WINDOW
    t = pl.program_id(0)
    i = lax.rem(jnp.minimum(t, n_steps - 1), nblk)
    last = nblk - 1
    cur = i % 2
    nxt = 1 - cur
    mix_slot = t % 2
    prev_slot = 1 - mix_slot

    out_pieces = [(slice(0, tq), slice(c, c + OUT_PIECE_COLS)) for c in range(0, D_MODEL, OUT_PIECE_COLS)]

    def out_piece(n):
        r, c = out_pieces[n]
        y_ref[r, c] = x_ref[r, c] + _dot(mix_ref[prev_slot, r, :], w_ref[:, c])

    def mixers(do_out):
        kw_ref[0:WINDOW, :] = kp_ref[0].astype(BF16)
        kw_ref[WINDOW:, :] = kc_ref[0].astype(BF16)
        vw_ref[0:WINDOW, :] = vp_ref[0].astype(BF16)
        vw_ref[WINDOW:, :] = vc_ref[0].astype(BF16)

        @pl.when(i == 0)
        def _():
            st_ref[0] = jnp.zeros(st_ref.shape[1:], F32)

        @pl.when(i == last)
        def _():
            wk_ref[0] = kc_ref[0, tq - WINDOW:, :]
            wv_ref[0] = vc_ref[0, tq - WINDOW:, :]

        lane = lax.broadcasted_iota(jnp.int32, (1, LANES), 1)
        lo = lane < HEAD_DIM
        col_group = lax.broadcasted_iota(jnp.int32, (1, KV_WIDTH), 1) // HEAD_DIM
        col = lax.broadcasted_iota(jnp.int32, (1, N_KV * WIN_KEYS), 1)
        key_in_seg = col - WIN_KEYS * sum((col >= g * WIN_KEYS).astype(jnp.int32) for g in range(1, N_KV))
        neg_inf = jnp.float32(-jnp.inf)
        rows = [slice(c * CHUNK, (c + 1) * CHUNK) for c in range(n_chunks)]
        a_s, a_v, a_p, a_den, a_ov = {}, {}, {}, {}, {}

        def block_diag(win):
            zero = jnp.zeros_like(win)
            return jnp.concatenate([jnp.where(col_group == g, win, zero) for g in range(N_KV)], axis=0)

        def attn_scores(c):
            k_bd = block_diag(kw_ref[c * CHUNK:c * CHUNK + WIN_KEYS, :])
            a_v[c] = block_diag(vw_ref[c * CHUNK:c * CHUNK + WIN_KEYS, :])
            qs = jnp.concatenate(
                [q_ref[rows[c], r * KV_WIDTH:(r + 1) * KV_WIDTH] for r in range(GQA_REP)], axis=0)
            a_s[c] = _dot_t(qs, k_bd)

        def attn_softmax(c):
            su = a_s.pop(c)
            if c < WIN_CHUNKS:
                first_valid = jnp.where(i == 0, (WIN_CHUNKS - c) * CHUNK, 0)
                su = jnp.where(key_in_seg >= first_valid, su, neg_inf)
            cs = [su[:, LANES * n:LANES * (n + 1)] for n in range(6)]
            seg_max = [
                jnp.maximum(cs[0], jnp.where(lo, cs[1], neg_inf)),
                jnp.maximum(jnp.where(lo, neg_inf, cs[1]), cs[2]),
                jnp.maximum(cs[3], jnp.where(lo, cs[4], neg_inf)),
                jnp.maximum(jnp.where(lo, neg_inf, cs[4]), cs[5]),
            ]
            mu = [jnp.maximum(jnp.max(seg_max[g], axis=-1, keepdims=True), sink_ref[g])
                  for g in range(N_KV)]
            shift = [mu[0], jnp.where(lo, mu[0], mu[1]), mu[1], mu[2], jnp.where(lo, mu[2], mu[3]), mu[3]]
            e = [jnp.exp2(cs[n] - shift[n]) for n in range(6)]
            a_p[c] = jnp.concatenate(e, axis=1).astype(BF16)
            seg_sum = [e[0] + jnp.where(lo, e[1], 0.0), jnp.where(lo, 0.0, e[1]) + e[2],
                       e[3] + jnp.where(lo, e[4], 0.0), jnp.where(lo, 0.0, e[4]) + e[5]]
            a_den[c] = [jnp.sum(seg_sum[g], axis=-1, keepdims=True) + jnp.exp2(sink_ref[g] - mu[g])
                        for g in range(N_KV)]

        def attn_pv(c):
            a_ov[c] = _dot(a_p.pop(c), a_v.pop(c))

        def attn_finish(c):
            ov, den = a_ov.pop(c), a_den.pop(c)
            den = jnp.concatenate([jnp.where(lo, den[0], den[1]), jnp.where(lo, den[2], den[3])], axis=1)
            o = ov / den
            for r in range(GQA_REP):
                cols = slice(r * KV_WIDTH, (r + 1) * KV_WIDTH)
                gate = ga_ref[rows[c], cols].astype(F32)
                mix_ref[mix_slot, rows[c], cols] = (o[r * CHUNK:(r + 1) * CHUNK, :] * gate).astype(BF16)

        rowi = lax.broadcasted_iota(jnp.int32, (CHUNK, CHUNK), 0)
        coli = lax.broadcasted_iota(jnp.int32, (CHUNK, CHUNK), 1)
        causal = rowi >= coli
        eye = (lax.broadcasted_iota(jnp.int32, (GLA_DK, GLA_DK), 0)
               == lax.broadcasted_iota(jnp.int32, (GLA_DK, GLA_DK), 1))
        heads = range(N_GLA)
        kcol = [slice(h * GLA_DK, (h + 1) * GLA_DK) for h in heads]
        vcol = [slice(h * GLA_DV, (h + 1) * GLA_DV) for h in heads]
        mcol = [slice(ATT_WIDTH + h * GLA_DV, ATT_WIDTH + (h + 1) * GLA_DV) for h in heads]
        g_b, g_qd, g_ki, g_ke, g_dc, g_att, g_dst, g_o = {}, {}, {}, {}, {}, {}, {}, {}
        state = {}

        def cum_log2_decay(r):
            return _dot(tril_ref[...], jnp.concatenate([lah_ref[r, :], lal_ref[r, :]], axis=0))

        def decay_column(blast):
            dec_row = jnp.broadcast_to(jnp.exp2(blast), (GLA_DK, GLA_DK))
            return jnp.sum(jnp.where(eye, dec_row, 0.0), axis=-1, keepdims=True)

        def gla_finish_rows(ov, r, h):
            ms = jnp.mean(ov * ov, axis=-1, keepdims=True)
            gate = gg_ref[r, vcol[h]].astype(F32)
            mix_ref[mix_slot, r, mcol[h]] = (ov * lax.rsqrt(ms + EPS) * gate).astype(BF16)

        def gla_cumsum(c):
            g_b[c] = cum_log2_decay(rows[c])

        def gla_prepare(c):
            for h in heads:
                b = g_b[c][:, kcol[h]]
                blast = b[CHUNK - 1:CHUNK, :]
                q = qg_ref[rows[c], kcol[h]].astype(F32)
                k = kg_ref[rows[c], kcol[h]].astype(F32)
                decay = jnp.exp2(b)
                g_qd[c, h] = (q * decay).astype(BF16)
                g_ki[c, h] = (k * (1.0 / decay)).astype(BF16)
                g_ke[c, h] = (k * jnp.exp2(blast - b)).astype(BF16)
                g_dc[c, h] = decay_column(blast)

        def gla_scores(c):
            for h in heads:
                g_att[c, h] = _dot_t(g_qd[c, h], g_ki.pop((c, h)))
                g_dst[c, h] = _dot_ta(g_ke.pop((c, h)), vg_ref[rows[c], vcol[h]])

        def gla_outputs(c):
            for h in heads:
                att = jnp.where(causal, g_att.pop((c, h)), 0.0).astype(BF16)
                lhs = jnp.concatenate([g_qd.pop((c, h)), att], axis=1)
                rhs = jnp.concatenate([state[h].astype(BF16), vg_ref[rows[c], vcol[h]]], axis=0)
                g_o[c, h] = _dot(lhs, rhs)
                state[h] = state[h] * g_dc.pop((c, h)) + g_dst.pop((c, h))

        def gla_finish(c):
            for h in heads:
                gla_finish_rows(g_o.pop((c, h)), rows[c], h)

        def gla_exact_step():
            def chunk(c, carry):
                r = pl.ds(pl.multiple_of(c * CHUNK, CHUNK), CHUNK)
                bcum = cum_log2_decay(r)
                for h in heads:
                    b = bcum[:, kcol[h]]
                    blast = b[CHUNK - 1:CHUNK, :]
                    q = qg_ref[r, kcol[h]].astype(F32)
                    k = kg_ref[r, kcol[h]].astype(F32)
                    v = vg_ref[r, vcol[h]]
                    b_ref[...] = b
                    qx_ref[...] = q

                    def score_row(tok, att_t):
                        w = jnp.where(rowi[:, 0:1] <= tok, jnp.exp2(b_ref[pl.ds(tok, 1), :] - b), 0.0)
                        colv = jnp.sum(qx_ref[pl.ds(tok, 1), :] * k * w, axis=-1, keepdims=True)
                        return att_t + jnp.where(coli == tok, colv, 0.0)

                    att_t = lax.fori_loop(0, CHUNK, score_row, jnp.zeros((CHUNK, CHUNK), F32))
                    q_dec = (q * jnp.exp2(b)).astype(BF16)
                    k_end = (k * jnp.exp2(blast - b)).astype(BF16)
                    st = st_ref[nxt, h]
                    ov = _dot_ta(att_t.astype(BF16), v) + _dot(q_dec, st.astype(BF16))
                    st_ref[nxt, h] = st * decay_column(blast) + _dot_ta(k_end, v)
                    gla_finish_rows(ov, r, h)
                return carry

            st_ref[nxt] = st_ref[cur]
            lax.fori_loop(0, n_chunks, chunk, 0)

        pending = list(range(len(out_pieces))) if do_out else []
        n_slots = 4 * (n_chunks // group)
        slots_seen = [0]

        def out_slot():
            slots_seen[0] += 1
            due = len(out_pieces) * slots_seen[0] // n_slots - (len(out_pieces) - len(pending))
            for _ in range(min(due, len(pending))):
                out_piece(pending.pop(0))

        for h in heads:
            state[h] = st_ref[cur, h]
        for g0 in range(0, n_chunks, group):
            cs = list(range(g0, g0 + group))
            half = group // 2
            for c in cs:
                gla_cumsum(c)
            out_slot()
            for c in cs[:half]:
                attn_scores(c)
            out_slot()
            for c in cs:
                gla_prepare(c)
            for c in cs:
                gla_scores(c)
            out_slot()
            for c in cs[:half]:
                attn_softmax(c)
            for c in cs[:half]:
                attn_pv(c)
            for c in cs[half:]:
                attn_scores(c)
            out_slot()
            for c in cs:
                gla_outputs(c)
            for c in cs[half:]:
                attn_softmax(c)
            for c in cs[half:]:
                attn_pv(c)
            for c in cs:
                gla_finish(c)
            for c in cs:
                attn_finish(c)
        assert not pending
        for h in heads:
            st_ref[nxt, h] = state[h]
        total = functools.reduce(jnp.minimum, [g_b[c][CHUNK - 1:CHUNK, :] for c in range(n_chunks)])

        @pl.when(jnp.min(total) < -LOG2_DECAY_LIMIT)
        def _():
            gla_exact_step()

        @pl.when(i == last)
        def _():
            sout_ref[0] = st_ref[nxt]

    @pl.when(t == 0)
    def _():
        mixers(do_out=False)

    @pl.when((t > 0) & (t < n_steps))
    def _():
        mixers(do_out=True)

    @pl.when(t == n_steps)
    def _():
        for n in range(len(out_pieces)):
            out_piece(n)


def _layer_prompt(x2d, q, k, v, ga, sink_rows, qg, kg, vg, gg, la_hi, la_lo, tril2, w_out, batch, seq,
                  tq, group):
    nblk = seq // tq
    n_steps = batch * nblk
    per_win = tq // WINDOW
    blk_of = lambda t: jnp.minimum(t, n_steps - 1)
    rows = lambda w: pl.BlockSpec((tq, w), lambda t: (blk_of(t), 0))
    late = lambda w: pl.BlockSpec((tq, w), lambda t: (jnp.maximum(t - 1, 0), 0))
    cur = pl.BlockSpec((1, tq, KV_WIDTH), lambda t: (blk_of(t) // nblk, blk_of(t) % nblk, 0))
    prev = pl.BlockSpec((1, WINDOW, KV_WIDTH),
                        lambda t: (blk_of(t) // nblk, jnp.maximum((blk_of(t) % nblk) * per_win - 1, 0), 0))
    win = pl.BlockSpec((1, WINDOW, KV_WIDTH), lambda t: (blk_of(t) // nblk, 0, 0))
    state = pl.BlockSpec((1, N_GLA, GLA_DK, GLA_DV), lambda t: (blk_of(t) // nblk, 0, 0, 0))
    k3 = k.reshape(batch, seq, KV_WIDTH)
    v3 = v.reshape(batch, seq, KV_WIDTH)
    return pl.pallas_call(
        functools.partial(_layer_prompt_kernel, tq=tq, group=group, nblk=nblk, n_steps=n_steps),
        grid=(n_steps + 1,),
        in_specs=[rows(ATT_WIDTH), prev, cur, prev, cur, rows(ATT_WIDTH), _const_spec(sink_rows.shape),
                  rows(GLA_KW), rows(GLA_KW), rows(GLA_WIDTH), rows(GLA_WIDTH), rows(GLA_KW),
                  rows(GLA_KW), _const_spec(tril2.shape), late(D_MODEL), _const_spec(w_out.shape)],
        out_specs=[late(D_MODEL), win, win, state],
        out_shape=[jax.ShapeDtypeStruct((batch * seq, D_MODEL), F32),
                   jax.ShapeDtypeStruct((batch, WINDOW, KV_WIDTH), F32),
                   jax.ShapeDtypeStruct((batch, WINDOW, KV_WIDTH), F32),
                   jax.ShapeDtypeStruct((batch, N_GLA, GLA_DK, GLA_DV), F32)],
        scratch_shapes=[pltpu.VMEM((WINDOW + tq, KV_WIDTH), BF16),
                        pltpu.VMEM((WINDOW + tq, KV_WIDTH), BF16),
                        pltpu.VMEM((2, N_GLA, GLA_DK, GLA_DV), F32),
                        pltpu.VMEM((CHUNK, GLA_DK), F32),
                        pltpu.VMEM((CHUNK, GLA_DK), F32),
                        pltpu.VMEM((2, tq, D_MODEL), BF16)],
        compiler_params=pltpu.CompilerParams(
            dimension_semantics=("arbitrary",), vmem_limit_bytes=VMEM_LIMIT),
        name="layer_prompt",
    )(q, k3, k3, v3, v3, ga, sink_rows, qg, kg, vg, gg, la_hi, la_lo, tril2, x2d, w_out)


def _attn_sample_kernel(q_ref, kn_ref, vn_ref, ck_ref, cv_ref, ga_ref, sink_ref,
                        o_ref, wk_ref, wv_ref, *, t, bb):
    n_cache = ck_ref.shape[1]
    assert n_cache == LANES and t <= LANES
    lane = lax.broadcasted_iota(jnp.int32, (1, LANES), 1)
    lo = lane < HEAD_DIM
    is_new = lane < t
    col_group = lax.broadcasted_iota(jnp.int32, (1, KV_WIDTH), 1) // HEAD_DIM
    seg = 2 * LANES
    ones_bd = jnp.concatenate(
        [jnp.broadcast_to(jnp.where(col_group == g, 1.0, 0.0).astype(BF16), (seg, KV_WIDTH))
         for g in range(N_KV)], axis=0)
    pad = jnp.zeros((LANES - t, KV_WIDTH), BF16)
    neg_inf = jnp.float32(-jnp.inf)

    s, v_all = [], []
    for b in range(bb):
        rows = slice(b * t, (b + 1) * t)
        kn, vn, ck, cv = kn_ref[rows, :], vn_ref[rows, :], ck_ref[b], cv_ref[b]
        wk_ref[b] = jnp.concatenate([ck[t:, :], kn], axis=0)
        wv_ref[b] = jnp.concatenate([cv[t:, :], vn], axis=0)
        kcat = jnp.concatenate([ck.astype(BF16), kn.astype(BF16), pad], axis=0)
        vcat = jnp.concatenate([cv.astype(BF16), vn.astype(BF16), pad], axis=0)
        zero = jnp.zeros_like(kcat)
        k_bd = jnp.concatenate([jnp.where(col_group == g, kcat, zero) for g in range(N_KV)], axis=0)
        v_bd = jnp.concatenate([jnp.where(col_group == g, vcat, zero) for g in range(N_KV)], axis=0)
        v_all.append(jnp.concatenate([v_bd, ones_bd], axis=1))
        qs = jnp.concatenate(
            [q_ref[rows, r * KV_WIDTH:(r + 1) * KV_WIDTH] for r in range(GQA_REP)], axis=0)
        s.append(_dot_t(qs, k_bd))
    p, m = [], []
    for b in range(bb):
        c = [s[b][:, LANES * n:LANES * (n + 1)] for n in range(2 * N_KV)]
        c = [cn if n % 2 == 0 else jnp.where(is_new, cn, neg_inf) for n, cn in enumerate(c)]
        mb = [jnp.maximum(jnp.max(jnp.maximum(c[2 * g], c[2 * g + 1]), axis=-1, keepdims=True),
                          sink_ref[g]) for g in range(N_KV)]
        p.append(jnp.concatenate([jnp.exp2(c[n] - mb[n // 2]) for n in range(2 * N_KV)],
                                 axis=1).astype(BF16))
        m.append(mb)
    ov = [_dot(p[b], v_all[b]) for b in range(bb)]
    for b in range(bb):
        rows = slice(b * t, (b + 1) * t)
        es = [jnp.exp2(sink_ref[g] - m[b][g]) for g in range(N_KV)]
        e_sink = jnp.concatenate([jnp.where(lo, es[0], es[1]), jnp.where(lo, es[2], es[3])], axis=1)
        o = ov[b][:, :KV_WIDTH] / (ov[b][:, KV_WIDTH:] + e_sink)
        for r in range(GQA_REP):
            cols = slice(r * KV_WIDTH, (r + 1) * KV_WIDTH)
            gate = ga_ref[rows, cols].astype(F32)
            o_ref[rows, cols] = (o[r * t:(r + 1) * t, :] * gate).astype(BF16)


def _attn_sample(q, k, v, cache_k, cache_v, ga, sink_rows, batch, t, bb):
    n_cache = cache_k.shape[1]
    rows = lambda w: pl.BlockSpec((bb * t, w), lambda b: (b, 0))
    cache = pl.BlockSpec((bb, n_cache, KV_WIDTH), lambda b: (b, 0, 0))
    return pl.pallas_call(
        functools.partial(_attn_sample_kernel, t=t, bb=bb),
        grid=(batch // bb,),
        in_specs=[rows(ATT_WIDTH), rows(KV_WIDTH), rows(KV_WIDTH), cache, cache, rows(ATT_WIDTH),
                  _const_spec(sink_rows.shape)],
        out_specs=[rows(ATT_WIDTH), cache, cache],
        out_shape=[jax.ShapeDtypeStruct((batch * t, ATT_WIDTH), BF16),
                   jax.ShapeDtypeStruct((batch, n_cache, KV_WIDTH), F32),
                   jax.ShapeDtypeStruct((batch, n_cache, KV_WIDTH), F32)],
        compiler_params=pltpu.CompilerParams(
            dimension_semantics=("arbitrary",), vmem_limit_bytes=VMEM_LIMIT),
        name="attn_sample",
    )(q, k, v, cache_k, cache_v, ga, sink_rows)


def _gla_kernel(qg_ref, kg_ref, vg_ref, gg_ref, lah_ref, lal_ref, s0_ref, tril_ref,
                o_ref, sout_ref, b_ref, q_ref, *, blk, n_chunks, streams):
    rowi = lax.broadcasted_iota(jnp.int32, (blk, blk), 0)
    coli = lax.broadcasted_iota(jnp.int32, (blk, blk), 1)
    causal = rowi >= coli
    eye = (lax.broadcasted_iota(jnp.int32, (GLA_DK, GLA_DK), 0)
           == lax.broadcasted_iota(jnp.int32, (GLA_DK, GLA_DK), 1))

    heads = range(N_GLA)
    kcol = [slice(h * GLA_DK, (h + 1) * GLA_DK) for h in heads]
    vcol = [slice(h * GLA_DV, (h + 1) * GLA_DV) for h in heads]

    def cum_log2_decay(r):
        return _dot(tril_ref[...], jnp.concatenate([lah_ref[r, :], lal_ref[r, :]], axis=0))

    def decay_column(blast):
        dec_row = jnp.broadcast_to(jnp.exp2(blast), (GLA_DK, GLA_DK))
        return jnp.sum(jnp.where(eye, dec_row, 0.0), axis=-1, keepdims=True)

    def finish(ov, r, h):
        ms = jnp.mean(ov * ov, axis=-1, keepdims=True)
        gate = gg_ref[r, vcol[h]].astype(F32)
        o_ref[r, vcol[h]] = (ov * lax.rsqrt(ms + EPS) * gate).astype(BF16)

    units = [(s, u) for s in range(streams) for u in range(n_chunks)]
    rows = {(s, u): slice((s * n_chunks + u) * blk, (s * n_chunks + u + 1) * blk) for s, u in units}

    def exact_step():
        for s in range(streams):
            st = [s0_ref[s, h] for h in heads]
            for u in range(n_chunks):
                r = rows[s, u]
                bcum = cum_log2_decay(r)
                for h in heads:
                    b = bcum[:, kcol[h]]
                    blast = b[blk - 1:blk, :]
                    q = qg_ref[r, kcol[h]].astype(F32)
                    k = kg_ref[r, kcol[h]].astype(F32)
                    v = vg_ref[r, vcol[h]]
                    b_ref[...] = b
                    q_ref[...] = q

                    def score_row(t, att_t):
                        w = jnp.where(rowi[:, 0:1] <= t, jnp.exp2(b_ref[pl.ds(t, 1), :] - b), 0.0)
                        col = jnp.sum(q_ref[pl.ds(t, 1), :] * k * w, axis=-1, keepdims=True)
                        return att_t + jnp.where(coli == t, col, 0.0)

                    att_t = lax.fori_loop(0, blk, score_row, jnp.zeros((blk, blk), F32))
                    q_dec = (q * jnp.exp2(b)).astype(BF16)
                    k_end = (k * jnp.exp2(blast - b)).astype(BF16)
                    ov = _dot_ta(att_t.astype(BF16), v) + _dot(q_dec, st[h].astype(BF16))
                    st[h] = st[h] * decay_column(blast) + _dot_ta(k_end, v)
                    finish(ov, r, h)
            for h in heads:
                sout_ref[s, h] = st[h]

    bcum = {unit: cum_log2_decay(rows[unit]) for unit in units}
    q_dec, k_inv, k_end, dec_col = {}, {}, {}, {}
    for unit in units:
        for h in heads:
            b = bcum[unit][:, kcol[h]]
            blast = b[blk - 1:blk, :]
            q = qg_ref[rows[unit], kcol[h]].astype(F32)
            k = kg_ref[rows[unit], kcol[h]].astype(F32)
            decay = jnp.exp2(b)
            q_dec[unit, h] = (q * decay).astype(BF16)
            k_inv[unit, h] = (k * (1.0 / decay)).astype(BF16)
            k_end[unit, h] = (k * jnp.exp2(blast - b)).astype(BF16)
            dec_col[unit, h] = decay_column(blast)
    att = {(unit, h): _dot_t(q_dec[unit, h], k_inv[unit, h]) for unit in units for h in heads}
    d_st = {(unit, h): _dot_ta(k_end[unit, h], vg_ref[rows[unit], vcol[h]])
            for unit in units for h in heads}
    att = {key: jnp.where(causal, a, 0.0).astype(BF16) for key, a in att.items()}
    st = {(s, h): s0_ref[s, h] for s in range(streams) for h in heads}
    o = {}
    for s, u in units:
        for h in heads:
            o[s, u, h] = (_dot(att[(s, u), h], vg_ref[rows[s, u], vcol[h]])
                          + _dot(q_dec[(s, u), h], st[s, h].astype(BF16)))
            st[s, h] = st[s, h] * dec_col[(s, u), h] + d_st[(s, u), h]
    for (s, h), value in st.items():
        sout_ref[s, h] = value
    for s, u in units:
        for h in heads:
            finish(o[s, u, h], rows[s, u], h)

    total = functools.reduce(jnp.minimum, [bc[blk - 1:blk, :] for bc in bcum.values()])

    @pl.when(jnp.min(total) < -LOG2_DECAY_LIMIT)
    def _():
        exact_step()


def _gla(qg, kg, vg, gg, la_hi, la_lo, s0, tril2, batch, seq, blk, streams):
    rows = lambda w: pl.BlockSpec((streams * seq, w), lambda g: (g, 0))
    state = pl.BlockSpec((streams, N_GLA, GLA_DK, GLA_DV), lambda g: (g, 0, 0, 0))
    return pl.pallas_call(
        functools.partial(_gla_kernel, blk=blk, n_chunks=seq // blk, streams=streams),
        grid=(batch // streams,),
        in_specs=[rows(GLA_KW), rows(GLA_KW), rows(GLA_WIDTH), rows(GLA_WIDTH), rows(GLA_KW),
                  rows(GLA_KW), state, _const_spec(tril2.shape)],
        out_specs=[rows(GLA_WIDTH), state],
        out_shape=[jax.ShapeDtypeStruct((batch * seq, GLA_WIDTH), BF16),
                   jax.ShapeDtypeStruct((batch, N_GLA, GLA_DK, GLA_DV), F32)],
        scratch_shapes=[pltpu.VMEM((blk, GLA_DK), F32),
                        pltpu.VMEM((blk, GLA_DK), F32)],
        compiler_params=pltpu.CompilerParams(
            dimension_semantics=("arbitrary",), vmem_limit_bytes=VMEM_LIMIT),
        name="gla",
    )(qg, kg, vg, gg, la_hi, la_lo, s0, tril2)


def _outproj_kernel(x_ref, ma_ref, mg_ref, w_ref, y_ref):
    y_ref[...] = (x_ref[...] + _dot(ma_ref[...], w_ref[:ATT_WIDTH, :])
                  + _dot(mg_ref[...], w_ref[ATT_WIDTH:, :]))


def _outproj(x2d, mix_a, mix_g, w, tm):
    m = x2d.shape[0]
    row = lambda w: pl.BlockSpec((tm, w), lambda i: (i, 0))
    return pl.pallas_call(
        _outproj_kernel,
        grid=(m // tm,),
        in_specs=[row(D_MODEL), row(ATT_WIDTH), row(GLA_WIDTH), _const_spec(w.shape)],
        out_specs=row(D_MODEL),
        out_shape=jax.ShapeDtypeStruct((m, D_MODEL), F32),
        compiler_params=pltpu.CompilerParams(
            dimension_semantics=("arbitrary",), vmem_limit_bytes=VMEM_LIMIT),
        name="outproj",
    )(x2d, mix_a, mix_g, w)


def _regroup_unit(u):
    return (u % N_KV) * GQA_REP + u // N_KV


def _prep_w_in_kernel(*refs):
    *unit_refs, lr_ref, main_ref, wlr_ref = refs
    units = jnp.concatenate([u[...] for u in unit_refs], axis=0)
    main_ref[...] = units.T.astype(BF16)

    @pl.when(pl.program_id(0) == 0)
    def _():
        lr = lr_ref[...]
        padded = jnp.concatenate([lr, jnp.zeros((LANES - lr.shape[0], lr.shape[1]), F32)], axis=0)
        wlr_ref[...] = padded.T.astype(BF16)


def _prep_w_in(w_in_t):
    _, k = w_in_t.shape
    units_per_block = PREP_UNITS
    block_cols = units_per_block * HEAD_DIM
    n_att = ATT_WIDTH // HEAD_DIM
    assert MAIN_COLS % block_cols == 0

    def src_unit(j, p):
        u = j * units_per_block + p
        q0, ga0 = OFF_Q // HEAD_DIM, OFF_GA // HEAD_DIM
        u = jnp.where((u >= q0) & (u < q0 + n_att), q0 + _regroup_unit(u - q0), u)
        return jnp.where((u >= ga0) & (u < ga0 + n_att), ga0 + _regroup_unit(u - ga0), u)

    unit = lambda p: pl.BlockSpec((HEAD_DIM, k), lambda j: (src_unit(j, p), 0))
    return pl.pallas_call(
        _prep_w_in_kernel,
        grid=(MAIN_COLS // block_cols,),
        in_specs=[unit(p) for p in range(units_per_block)]
        + [pl.BlockSpec((GATE_RANK, k), lambda j: (MAIN_COLS // GATE_RANK, 0))],
        out_specs=[pl.BlockSpec((k, block_cols), lambda j: (0, j)),
                   pl.BlockSpec((k, LANES), lambda j: (0, 0))],
        out_shape=[jax.ShapeDtypeStruct((k, MAIN_COLS), BF16),
                   jax.ShapeDtypeStruct((k, LANES), BF16)],
        compiler_params=pltpu.CompilerParams(
            dimension_semantics=("arbitrary",), vmem_limit_bytes=VMEM_LIMIT),
        name="prep_w_in",
    )(*([w_in_t] * (units_per_block + 1)))


def _prep_w_out_kernel(*refs):
    *unit_refs, o_ref = refs
    o_ref[...] = jnp.concatenate([u[...] for u in unit_refs], axis=0).astype(BF16)


def _prep_w_out(w_out):
    rows, cols = w_out.shape
    units_per_block = PREP_UNITS
    block_rows = units_per_block * HEAD_DIM
    n_att = ATT_WIDTH // HEAD_DIM
    assert rows % block_rows == 0

    def src_unit(j, p):
        u = j * units_per_block + p
        return jnp.where(u < n_att, _regroup_unit(u), u)

    unit = lambda p: pl.BlockSpec((HEAD_DIM, cols), lambda j: (src_unit(j, p), 0))
    return pl.pallas_call(
        _prep_w_out_kernel,
        grid=(rows // block_rows,),
        in_specs=[unit(p) for p in range(units_per_block)],
        out_specs=pl.BlockSpec((block_rows, cols), lambda j: (j, 0)),
        out_shape=jax.ShapeDtypeStruct((rows, cols), BF16),
        compiler_params=pltpu.CompilerParams(
            dimension_semantics=("arbitrary",), vmem_limit_bytes=VMEM_LIMIT),
        name="prep_w_out",
    )(*([w_out] * units_per_block))


def _tril2(blk):
    t = np.tril(np.ones((blk, blk), np.float32))
    return jnp.asarray(np.concatenate([t, t], axis=1), BF16)


def _prep(norm_g, w_in, w_gate_up, b_gate, q_norm_g, k_norm_g, sinks, gla_norm_g, w_out):
    w_main, w_lr = _prep_w_in(w_in.T)
    w_gu = jnp.pad(w_gate_up, ((0, LANES - GATE_RANK), (0, 0))).astype(BF16)
    g_qk = jnp.concatenate([jnp.tile(q_norm_g, N_HEADS) * (ATT_SCALE * LOG2E),
                            jnp.tile(k_norm_g, N_KV)])[None, :]

    def sink_rows(t):
        sk = sinks.astype(F32).reshape(N_KV, GQA_REP) * LOG2E
        return jnp.broadcast_to(sk[:, :, None, None], (N_KV, GQA_REP, t, LANES)).reshape(
            N_KV, GQA_REP * t, LANES)

    return dict(norm_g=norm_g[None, :], w_main=w_main, w_lr=w_lr, w_gu=w_gu, b_gate=b_gate[None, :],
                g_qk=g_qk, g_gla=jnp.tile(gla_norm_g, N_GLA)[None, :],
                w_out=_prep_w_out(w_out), sink_rows=sink_rows)


def kernel(x_prompt, x_sample, cache_k, cache_v, state_gla, norm_g, w_in, w_gate_up, b_gate,
           q_norm_g, k_norm_g, sinks, gla_norm_g, w_out):
    depth = norm_g.shape[0]
    assert depth == 1
    B, S, _ = x_prompt.shape
    DB, T, _ = x_sample.shape
    n_cache = cache_k.shape[2]
    assert n_cache == WINDOW
    p = _prep(norm_g[0], w_in[0], w_gate_up[0], b_gate[0], q_norm_g[0], k_norm_g[0], sinks[0],
              gla_norm_g[0], w_out[0])

    xp = x_prompt.reshape(B * S, D_MODEL)
    q, k, v, ga, qg, kg, vg, gg, la_hi, la_lo = _inproj(xp, p, tm=INPROJ_ROWS)
    y_p, win_k_p, win_v_p, gla_p = _layer_prompt(
        xp, q, k, v, ga, p["sink_rows"](CHUNK), qg, kg, vg, gg, la_hi, la_lo, _tril2(CHUNK), p["w_out"],
        B, S, tq=LAYER_ROWS, group=LAYER_GROUP)
    y_p = y_p.reshape(B, S, D_MODEL)

    xs = x_sample.reshape(DB * T, D_MODEL)
    q, k, v, ga, qg, kg, vg, gg, la_hi, la_lo = _inproj(xs, p, tm=DB * T)
    ck = cache_k[0].reshape(DB, n_cache, KV_WIDTH)
    cv = cache_v[0].reshape(DB, n_cache, KV_WIDTH)
    mix_a, win_k_s, win_v_s = _attn_sample(q, k, v, ck, cv, ga, p["sink_rows"](T), DB, T,
                                           bb=SAMPLE_STREAMS)
    blk = min(CHUNK, T)
    mix_g, gla_s = _gla(qg, kg, vg, gg, la_hi, la_lo, state_gla[0], _tril2(blk), DB, T,
                        blk=blk, streams=SAMPLE_STREAMS)
    y_s = _outproj(xs, mix_a, mix_g, p["w_out"], tm=DB * T).reshape(DB, T, D_MODEL)

    shape5 = lambda a: a.reshape(1, a.shape[0], n_cache, N_KV, HEAD_DIM)
    return (y_p, y_s, shape5(win_k_p), shape5(win_v_p), gla_p[None],
            shape5(win_k_s), shape5(win_v_s), gla_s[None])
```
